```python
import math, functools
import jax, jax.numpy as jnp
from jax import lax
import numpy as np

D_MODEL = 1024
BATCH = 4
SEQ = 8192
DEPTH = 1
DEC_BATCH = 32
DEC_SEQ = 16
PAST_LEN = 4096

CHUNK = 64
A_GROUPS = 4
A_GROUP_DIM = 128
A_WIDTH = A_GROUPS * A_GROUP_DIM
A_CHUNK = 128
N_HEADS = 8
HEAD_DIM = 64
ATT_WIDTH = N_HEADS * HEAD_DIM
N_IDX_HEADS = 8
IDX_DIM = 64
TOPK_MAX = 256
Q_BLOCK = 128
ATTN_SCALE = HEAD_DIM ** -0.5
IDX_W_SCALE = (N_IDX_HEADS ** -0.5) * (IDX_DIM ** -0.5)
NEG_INF = -1e30
OFF_AU = 0
OFF_AV = OFF_AU + A_WIDTH
OFF_Q = OFF_AV + A_WIDTH
OFF_K = OFF_Q + ATT_WIDTH
OFF_V = OFF_K + ATT_WIDTH
OFF_IQ = OFF_V + ATT_WIDTH
OFF_IK = OFF_IQ + N_IDX_HEADS * IDX_DIM
OFF_IW = OFF_IK + IDX_DIM
IN_COLS = OFF_IW + N_IDX_HEADS
MIX_WIDTH = A_WIDTH + ATT_WIDTH
N_EXPERTS = 32
TOP_K = 4
D_FF = D_MODEL
SWIGLU_LIMIT = 7.0
SWIGLU_ALPHA = 1.702
MOE_BLOCK = 256
DEEPNORM_ALPHA = (2.0 * DEPTH) ** 0.25
DEEPNORM_BETA = (8.0 * DEPTH) ** -0.25
LN_EPS = 1e-5

kernel_name = "hybrid_gmlp_dsa_moe_stream_step"


def layer_norm(x, g, b):
    xf = x.astype(jnp.float32)
    mu = jnp.mean(xf, axis=-1, keepdims=True)
    var = jnp.mean(jnp.square(xf - mu), axis=-1, keepdims=True)
    return ((xf - mu) * lax.rsqrt(var + LN_EPS) * g.astype(jnp.float32) + b.astype(jnp.float32)).astype(x.dtype)


def group_layer_norm(x, g, b):
    xr = x.reshape(x.shape[:-1] + (A_GROUPS, A_GROUP_DIM))
    return layer_norm(xr, g.reshape(A_GROUPS, A_GROUP_DIM), b.reshape(A_GROUPS, A_GROUP_DIM))


def alibi_slopes():
    return jnp.exp2(-8.0 * jnp.arange(1, N_HEADS + 1, dtype=jnp.float32) / N_HEADS)


def cond_mods(c, w_c, b_c):
    m = jax.nn.silu(c) @ w_c + b_c
    m = m.reshape(c.shape[0], 1, 6, D_MODEL)
    return tuple(m[:, :, i] for i in range(6))


def split_in(z):
    return jnp.split(z, [OFF_AV, OFF_Q, OFF_K, OFF_V, OFF_IQ, OFF_IK, OFF_IW], axis=-1)


def chunk_mix(u, v, ws, bs):
    B, T = u.shape[0], u.shape[1]
    tc = min(T, A_CHUNK)
    n = T // tc
    ur = u.reshape(B, n, tc, A_GROUPS, A_GROUP_DIM)
    vr = v.reshape(B, n, tc, A_GROUPS, A_GROUP_DIM)
    wm = jnp.tril(ws)[:, :tc, :tc]
    bias = jnp.transpose(bs[:, :tc])[None, None, :, :, None]
    mixed = jnp.einsum('gts,bnsgc->bntgc', wm, vr) + bias
    return (ur * mixed).reshape(B, T, A_WIDTH)


def dsa_attend(q, iq, iw, q_pos, k_all, v_all, ik_all, k_pos, topk, slopes):
    B = q.shape[0]
    adm = (k_pos[None, :] // CHUNK) <= (q_pos[:, None] // CHUNK)
    rel = jax.nn.relu(jnp.einsum('bthd,bsd->bths', iq, ik_all))
    score = jnp.einsum('bths,bth->bts', rel, iw)
    score = jnp.where(adm[None], score, -jnp.inf)
    _, idx = lax.top_k(score, topk)
    bidx = jnp.arange(B)[:, None, None]
    k_sel = k_all[bidx, idx]
    v_sel = v_all[bidx, idx]
    s_pos = k_pos[idx]
    dist = jnp.abs(q_pos[None, :, None] - s_pos).astype(jnp.float32)
    logits = (jnp.einsum('bthd,btkhd->bthk', q, k_sel).astype(jnp.float32) * ATTN_SCALE
              - slopes[None, None, :, None] * dist[:, :, None, :])
    ok = (s_pos // CHUNK) <= (q_pos[None, :, None] // CHUNK)
    logits = jnp.where(ok[:, :, None, :], logits, NEG_INF)
    p = jax.nn.softmax(logits, axis=-1).astype(v_all.dtype)
    return jnp.einsum('bthk,btkhd->bthd', p, v_sel)


def moe(h, w_router, b_router, w_gate, b_gate, w_up, b_up, w_down, b_down):
    N, D = h.shape
    logits = (h @ w_router + b_router).astype(jnp.float32)
    top_v, top_e = lax.top_k(logits, TOP_K)
    gates = jax.nn.softmax(top_v, axis=-1).astype(h.dtype)
    flat_e = top_e.reshape(-1)
    flat_t = jnp.arange(N * TOP_K) // TOP_K
    flat_g = gates.reshape(-1)
    order = jnp.argsort(flat_e)
    se, st, sg = flat_e[order], flat_t[order], flat_g[order]
    counts = jnp.bincount(flat_e, length=N_EXPERTS)
    start = jnp.cumsum(counts) - counts
    pcounts = (counts + MOE_BLOCK - 1) // MOE_BLOCK * MOE_BLOCK
    pend = jnp.cumsum(pcounts)
    pstart = pend - pcounts
    dest = pstart[se] + (jnp.arange(N * TOP_K) - start[se])
    nb = (N * TOP_K + N_EXPERTS * (MOE_BLOCK - 1) + MOE_BLOCK - 1) // MOE_BLOCK
    xp = jnp.zeros((nb * MOE_BLOCK, D), h.dtype).at[dest].set(h[st])
    block_e = jnp.minimum(jnp.searchsorted(pend, jnp.arange(nb) * MOE_BLOCK, side='right'), N_EXPERTS - 1)

    def expert_rows(args):
        xb, e = args
        g = jnp.minimum(xb @ w_gate[e] + b_gate[e], SWIGLU_LIMIT)
        u = jnp.clip(xb @ w_up[e] + b_up[e], -SWIGLU_LIMIT, SWIGLU_LIMIT)
        a = g * jax.nn.sigmoid(SWIGLU_ALPHA * g)
        return ((u + 1.0) * a) @ w_down[e] + b_down[e]

    yp = lax.map(expert_rows, (xp.reshape(nb, MOE_BLOCK, D), block_e)).reshape(nb * MOE_BLOCK, D)
    y = yp[dest] * sg[:, None]
    return jnp.zeros((N, D), h.dtype).at[st].add(y)


def trunk_layer(x, c, token_mix, w_c, b_c, w_o, b_o, ln1_g, ln1_b, ln2_g, ln2_b, moe_params):
    B, T, D = x.shape
    sh_a, sc_a, g_a, sh_f, sc_f, g_f = cond_mods(c, w_c, b_c)
    h = x * (1.0 + sc_a) + sh_a
    mixed, states = token_mix(h)
    y = mixed @ w_o + b_o
    x = layer_norm(DEEPNORM_ALPHA * x + (1.0 + g_a) * y, ln1_g, ln1_b)
    h = x * (1.0 + sc_f) + sh_f
    f = moe(h.reshape(B * T, D), *moe_params).reshape(B, T, D)
    x = layer_norm(DEEPNORM_ALPHA * x + (1.0 + g_f) * f, ln2_g, ln2_b)
    return x, states


def project_heads(h, w_in, a_ln_g, a_ln_b):
    B, T, _ = h.shape
    a_u, a_v, q, k, v, iq, ik, iw = split_in(h @ w_in)
    u = jax.nn.gelu(a_u, approximate=False).reshape(B, T, A_GROUPS, A_GROUP_DIM)
    va = group_layer_norm(jax.nn.gelu(a_v, approximate=False), a_ln_g, a_ln_b)
    q = q.reshape(B, T, N_HEADS, HEAD_DIM)
    k = k.reshape(B, T, N_HEADS, HEAD_DIM)
    v = v.reshape(B, T, N_HEADS, HEAD_DIM)
    iq = iq.reshape(B, T, N_IDX_HEADS, IDX_DIM)
    iw = iw * IDX_W_SCALE
    return u, va, q, k, v, iq, ik, iw


def token_mix_prompt(h, w_in, a_ln_g, a_ln_b, a_ws, a_bs, slopes):
    B, S, _ = h.shape
    u, va, q, k, v, iq, ik, iw = project_heads(h, w_in, a_ln_g, a_ln_b)
    ya = chunk_mix(u, va, a_ws, a_bs)
    pos = jnp.arange(S, dtype=jnp.int32)
    topk = min(TOPK_MAX, S // 4)
    nbq = S // Q_BLOCK

    def to_blocks(a):
        return jnp.moveaxis(a.reshape((B, nbq, Q_BLOCK) + a.shape[2:]), 1, 0)

    def q_block(args):
        qb, iqb, iwb, pb = args
        return dsa_attend(qb, iqb, iwb, pb, k, v, ik, pos, topk, slopes)

    ob = lax.map(q_block, (to_blocks(q), to_blocks(iq), to_blocks(iw), pos.reshape(nbq, Q_BLOCK)))
    yb = jnp.moveaxis(ob, 0, 1).reshape(B, S, ATT_WIDTH)
    return jnp.concatenate([ya, yb], axis=-1), (k, v, ik)


def token_mix_sample(h, ck, cv, cki, w_in, a_ln_g, a_ln_b, a_ws, a_bs, slopes):
    B, T, _ = h.shape
    past = ck.shape[1]
    u, va, q, k, v, iq, ik, iw = project_heads(h, w_in, a_ln_g, a_ln_b)
    ya = chunk_mix(u, va, a_ws, a_bs)
    k_all = jnp.concatenate([ck, k], axis=1)
    v_all = jnp.concatenate([cv, v], axis=1)
    ik_all = jnp.concatenate([cki, ik], axis=1)
    L = past + T
    k_pos = jnp.arange(L, dtype=jnp.int32)
    q_pos = past + jnp.arange(T, dtype=jnp.int32)
    topk = min(TOPK_MAX, L // 4)
    yb = dsa_attend(q, iq, iw, q_pos, k_all, v_all, ik_all, k_pos, topk, slopes).reshape(B, T, ATT_WIDTH)
    return jnp.concatenate([ya, yb], axis=-1), (k, v, ik, va.reshape(B, T, A_WIDTH))


def setup_inputs(seed: int = 0) -> dict:
    key = jax.random.key(seed)
    ks = jax.random.split(key, 32)
    f32 = jnp.float32
    nrm = lambda k, shp, s: jax.random.normal(k, shp, f32) * s
    col_scale = jnp.ones((IN_COLS,), f32).at[OFF_V:OFF_V + ATT_WIDTH].set(DEEPNORM_BETA)
    return {
        "x_prompt": nrm(ks[0], (BATCH, SEQ, D_MODEL), 1.0),
        "x_sample": nrm(ks[1], (DEC_BATCH, DEC_SEQ, D_MODEL), 1.0),
        "c_prompt": nrm(ks[2], (BATCH, D_MODEL), 1.0),
        "c_sample": nrm(ks[3], (DEC_BATCH, D_MODEL), 1.0),
        "cache_k": nrm(ks[4], (DEPTH, DEC_BATCH, PAST_LEN, N_HEADS, HEAD_DIM), 1.0),
        "cache_v": nrm(ks[5], (DEPTH, DEC_BATCH, PAST_LEN, N_HEADS, HEAD_DIM), DEEPNORM_BETA),
        "cache_kidx": nrm(ks[6], (DEPTH, DEC_BATCH, PAST_LEN, IDX_DIM), 1.0),
        "w_in": nrm(ks[7], (DEPTH, D_MODEL, IN_COLS), D_MODEL ** -0.5) * col_scale,
        "a_ln_g": 1.0 + nrm(ks[8], (DEPTH, A_WIDTH), 0.02),
        "a_ln_b": nrm(ks[9], (DEPTH, A_WIDTH), 0.02),
        "a_ws": nrm(ks[10], (DEPTH, A_GROUPS, A_CHUNK, A_CHUNK), A_CHUNK ** -0.5),
        "a_bs": 1.0 + nrm(ks[11], (DEPTH, A_GROUPS, A_CHUNK), 0.02),
        "w_o": nrm(ks[12], (DEPTH, MIX_WIDTH, D_MODEL), MIX_WIDTH ** -0.5 * DEEPNORM_BETA),
        "b_o": nrm(ks[13], (DEPTH, D_MODEL), 0.01),
        "w_c": nrm(ks[14], (DEPTH, D_MODEL, 6 * D_MODEL), 0.1 * D_MODEL ** -0.5),
        "b_c": nrm(ks[15], (DEPTH, 6 * D_MODEL), 0.01),
        "ln1_g": 1.0 + nrm(ks[16], (DEPTH, D_MODEL), 0.02),
        "ln1_b": nrm(ks[17], (DEPTH, D_MODEL), 0.02),
        "ln2_g": 1.0 + nrm(ks[18], (DEPTH, D_MODEL), 0.02),
        "ln2_b": nrm(ks[19], (DEPTH, D_MODEL), 0.02),
        "w_router": nrm(ks[20], (DEPTH, D_MODEL, N_EXPERTS), D_MODEL ** -0.5),
        "b_router": nrm(ks[21], (DEPTH, N_EXPERTS), 0.01),
        "w_gate": nrm(ks[22], (DEPTH, N_EXPERTS, D_MODEL, D_FF), D_MODEL ** -0.5),
        "b_gate": nrm(ks[23], (DEPTH, N_EXPERTS, D_FF), 0.01),
        "w_up": nrm(ks[24], (DEPTH, N_EXPERTS, D_MODEL, D_FF), D_MODEL ** -0.5),
        "b_up": nrm(ks[25], (DEPTH, N_EXPERTS, D_FF), 0.01),
        "w_down": nrm(ks[26], (DEPTH, N_EXPERTS, D_FF, D_MODEL), D_FF ** -0.5 * DEEPNORM_BETA),
        "b_down": nrm(ks[27], (DEPTH, N_EXPERTS, D_MODEL), 0.01),
    }


def reference(x_prompt, x_sample, c_prompt, c_sample, cache_k, cache_v, cache_kidx,
              w_in, a_ln_g, a_ln_b, a_ws, a_bs, w_o, b_o, w_c, b_c,
              ln1_g, ln1_b, ln2_g, ln2_b, w_router, b_router,
              w_gate, b_gate, w_up, b_up, w_down, b_down):
    slopes = alibi_slopes()
    xp, xs = x_prompt, x_sample
    kp_l, vp_l, kip_l, ks_l, vs_l, kis_l, cvs_l = [], [], [], [], [], [], []
    for l in range(DEPTH):
        moe_params = (w_router[l], b_router[l], w_gate[l], b_gate[l], w_up[l], b_up[l], w_down[l], b_down[l])
        mix_p = functools.partial(token_mix_prompt, w_in=w_in[l], a_ln_g=a_ln_g[l], a_ln_b=a_ln_b[l],
                                  a_ws=a_ws[l], a_bs=a_bs[l], slopes=slopes)
        mix_s = functools.partial(token_mix_sample, ck=cache_k[l], cv=cache_v[l], cki=cache_kidx[l],
                                  w_in=w_in[l], a_ln_g=a_ln_g[l], a_ln_b=a_ln_b[l],
                                  a_ws=a_ws[l], a_bs=a_bs[l], slopes=slopes)
        xp, (kp, vp, kip) = trunk_layer(xp, c_prompt, mix_p, w_c[l], b_c[l], w_o[l], b_o[l],
                                        ln1_g[l], ln1_b[l], ln2_g[l], ln2_b[l], moe_params)
        xs, (ks, vs, kis, cvs) = trunk_layer(xs, c_sample, mix_s, w_c[l], b_c[l], w_o[l], b_o[l],
                                             ln1_g[l], ln1_b[l], ln2_g[l], ln2_b[l], moe_params)
        kp_l.append(kp); vp_l.append(vp); kip_l.append(kip)
        ks_l.append(ks); vs_l.append(vs); kis_l.append(kis); cvs_l.append(cvs)
    new_k_prompt = jnp.stack(kp_l)
    new_v_prompt = jnp.stack(vp_l)
    new_kidx_prompt = jnp.stack(kip_l)
    new_k_sample = jnp.stack(ks_l)
    new_v_sample = jnp.stack(vs_l)
    new_kidx_sample = jnp.stack(kis_l)
    new_chunkv_sample = jnp.stack(cvs_l)
    return (xp, xs, new_k_prompt, new_v_prompt, new_kidx_prompt,
            new_k_sample, new_v_sample, new_kidx_sample, new_chunkv_sample)
```

```python
import functools

import jax
import jax.numpy as jnp
from jax import lax
from jax.experimental import pallas as pl
from jax.experimental.pallas import tpu as pltpu

F32 = jnp.float32
BF16 = jnp.bfloat16
I32 = jnp.int32

D_MODEL = 1024
CHUNK_SHIFT = 6
A_GROUPS = 4
A_GROUP_DIM = 128
A_WIDTH = A_GROUPS * A_GROUP_DIM
A_CHUNK = 128
N_HEADS = 8
HEAD_DIM = 64
ATT_WIDTH = N_HEADS * HEAD_DIM
N_IDX_HEADS = 8
IDX_DIM = 64
TOPK_MAX = 256
ATTN_SCALE = HEAD_DIM ** -0.5
IDX_W_SCALE = (N_IDX_HEADS ** -0.5) * (IDX_DIM ** -0.5)
N_EXPERTS = 32
TOP_K = 4
SWIGLU_LIMIT = 7.0
SWIGLU_ALPHA = 1.702
DEEPNORM_ALPHA = 2.0 ** 0.25
LN_EPS = 1e-5

LANES = 128
SUBLANES = 8
VMEM_LIMIT_BYTES = 56 * 1024 * 1024

ROW_TILE = 256
MOE_BLOCK = 256
ATT_TILE = 256

OFF_AU, OFF_AV, OFF_Q, OFF_K, OFF_V, OFF_IQ = 0, 512, 1024, 1536, 2048, 2560
OFF_IK = 3072
OFF_IW = 3200
IN_COLS_PAD = 3328

MASKED_DIST = 3.0e32
F32_MAX = 3.4028234663852886e38
INT32_MIN = -(2 ** 31)
KEY_NEG_INF = INT32_MIN + 0x7FFFFF

_NT = (((1,), (1,)), ((), ()))


def _cparams(*sem):
    return pltpu.CompilerParams(dimension_semantics=sem, vmem_limit_bytes=VMEM_LIMIT_BYTES)


def _mods_kernel(c_ref, w_ref, b_ref, o_ref):
    c = c_ref[...]
    s = c * jax.nn.sigmoid(c)
    o_ref[...] = jnp.dot(s.astype(BF16), w_ref[...].astype(BF16), preferred_element_type=F32) + b_ref[...]


def _cond_mods(c, w_c, b_c):
    nb, d = c.shape
    n_out = w_c.shape[1]
    return pl.pallas_call(
        _mods_kernel,
        out_shape=jax.ShapeDtypeStruct((nb, n_out), F32),
        grid=(n_out // d,),
        in_specs=[pl.BlockSpec((nb, d), lambda j: (0, 0)),
                  pl.BlockSpec((d, d), lambda j: (0, j)),
                  pl.BlockSpec((1, d), lambda j: (0, j))],
        out_specs=pl.BlockSpec((nb, d), lambda j: (0, j)),
        compiler_params=_cparams("arbitrary"),
        name="mods",
    )(c, w_c, b_c.reshape(1, n_out))


def _gelu(x):
    return 0.5 * x * (1.0 + lax.erf(x * 0.7071067811865476))


def _project_kernel(x_ref, sc_ref, sh_ref, w_ref, lng_ref, lnb_ref, wm_ref, bm_ref,
                    ya_ref, q_ref, kf_ref, kb_ref, vf_ref, vt_ref, iq_ref, ikf_ref, ikb_ref, iwt_ref, va_ref):
    t = x_ref.shape[0]
    h = (x_ref[...] * (1.0 + sc_ref[...]) + sh_ref[...]).astype(BF16)

    def proj(c0, n):
        return jnp.dot(h, w_ref[:, c0:c0 + n], preferred_element_type=F32)

    u = _gelu(proj(OFF_AU, A_WIDTH))
    gv = _gelu(proj(OFF_AV, A_WIDTH))
    for g in range(A_GROUPS):
        lo, hi = g * A_GROUP_DIM, (g + 1) * A_GROUP_DIM
        xg = gv[:, lo:hi]
        mu = jnp.mean(xg, axis=-1, keepdims=True)
        xc = xg - mu
        var = jnp.mean(xc * xc, axis=-1, keepdims=True)
        vg = xc * lax.rsqrt(var + LN_EPS) * lng_ref[:, lo:hi] + lnb_ref[:, lo:hi]
        va_ref[:, lo:hi] = vg
        vgb = vg.astype(BF16)
        for c in range(t // A_CHUNK):
            r0, r1 = c * A_CHUNK, (c + 1) * A_CHUNK
            mixed = jnp.dot(wm_ref[g], vgb[r0:r1, :], preferred_element_type=F32) + bm_ref[g]
            ya_ref[r0:r1, lo:hi] = (u[r0:r1, lo:hi] * mixed).astype(BF16)

    q_ref[...] = proj(OFF_Q, ATT_WIDTH).astype(BF16)
    k = proj(OFF_K, ATT_WIDTH)
    kf_ref[...] = k
    kb_ref[...] = k.astype(BF16)
    v = proj(OFF_V, ATT_WIDTH)
    vf_ref[...] = v
    vt_ref[0] = v.T.astype(BF16)
    iq_ref[...] = proj(OFF_IQ, N_IDX_HEADS * IDX_DIM).astype(BF16)
    ik2 = proj(OFF_IK, LANES)
    ikf_ref[...] = ik2[:, :IDX_DIM]
    ikb_ref[...] = ik2.astype(BF16)
    iw = proj(OFF_IW, LANES) * IDX_W_SCALE
    iwt_ref[...] = iw.T[:N_IDX_HEADS, :]


def _project(x, sc, sh, w_in_p, lng, lnb, wm, bm):
    n, d = x.shape
    t = ROW_TILE
    nt = n // t
    if sc.shape[0] == n:
        mod_spec = pl.BlockSpec((t, d), lambda i: (i, 0))
    else:
        tiles_per_seq = nt // sc.shape[0]
        sc = sc.reshape(sc.shape[0], 1, d)
        sh = sh.reshape(sh.shape[0], 1, d)
        mod_spec = pl.BlockSpec((None, 1, d), lambda i: (i // tiles_per_seq, 0, 0))
    row = lambda w: pl.BlockSpec((t, w), lambda i: (i, 0))
    const2 = lambda a: pl.BlockSpec(a.shape, lambda i: (0, 0))
    const3 = lambda a: pl.BlockSpec(a.shape, lambda i: (0, 0, 0))
    out_shape = (
        jax.ShapeDtypeStruct((n, A_WIDTH), BF16),
        jax.ShapeDtypeStruct((n, ATT_WIDTH), BF16),
        jax.ShapeDtypeStruct((n, ATT_WIDTH), F32),
        jax.ShapeDtypeStruct((n, ATT_WIDTH), BF16),
        jax.ShapeDtypeStruct((n, ATT_WIDTH), F32),
        jax.ShapeDtypeStruct((nt, ATT_WIDTH, t), BF16),
        jax.ShapeDtypeStruct((n, ATT_WIDTH), BF16),
        jax.ShapeDtypeStruct((n, IDX_DIM), F32),
        jax.ShapeDtypeStruct((n, LANES), BF16),
        jax.ShapeDtypeStruct((N_IDX_HEADS, n), F32),
        jax.ShapeDtypeStruct((n, A_WIDTH), F32),
    )
    out_specs = (row(A_WIDTH), row(ATT_WIDTH), row(ATT_WIDTH), row(ATT_WIDTH), row(ATT_WIDTH),
                 pl.BlockSpec((1, ATT_WIDTH, t), lambda i: (i, 0, 0)),
                 row(ATT_WIDTH), row(IDX_DIM), row(LANES),
                 pl.BlockSpec((N_IDX_HEADS, t), lambda i: (0, i)),
                 row(A_WIDTH))
    return pl.pallas_call(
        _project_kernel,
        out_shape=out_shape,
        grid=(nt,),
        in_specs=[row(d), mod_spec, mod_spec, const2(w_in_p), const2(lng), const2(lnb), const3(wm), const3(bm)],
        out_specs=out_specs,
        compiler_params=_cparams("arbitrary"),
        name="project",
    )(x, sc, sh, w_in_p, lng, lnb, wm, bm)


def _key_to_f32(key):
    bits = jnp.where(key >= 0, key, key ^ jnp.int32(0x7FFFFFFF))
    return lax.bitcast_convert_type(bits, F32)


def _count(scores_ref, nkt, cand, strict):
    _, tk, w = scores_ref.shape

    def body(kt, acc):
        x = scores_ref[kt]
        hit = (x > cand) if strict else (x >= cand)
        m = jnp.where(hit, 1.0, 0.0)
        return acc + jnp.sum(m.reshape(tk // SUBLANES, SUBLANES, w), axis=0)

    acc = lax.fori_loop(0, nkt, body, jnp.zeros((SUBLANES, w), F32))
    return jnp.sum(acc, axis=0, keepdims=True)


def _select_threshold(scores_ref, nkt, topk):
    w = scores_ref.shape[2]

    def bit_body(i, tkey):
        cand_key = tkey + lax.shift_left(jnp.int32(1), jnp.int32(31) - i)
        cand = _key_to_f32(jnp.maximum(cand_key, jnp.int32(KEY_NEG_INF)))
        cnt = _count(scores_ref, nkt, cand, strict=False)
        return jnp.where(cnt >= float(topk), cand_key, tkey)

    tkey = lax.fori_loop(0, 32, bit_body, jnp.full((1, w), INT32_MIN, I32))
    return jnp.maximum(_key_to_f32(tkey), -F32_MAX)


def _break_ties(scores_ref, nkt, thr, topk):
    _, tk, w = scores_ref.shape
    cnt_ge = _count(scores_ref, nkt, thr, strict=False)

    @pl.when(jnp.max(cnt_ge) > float(topk))
    def _():
        need = float(topk) - _count(scores_ref, nkt, thr, strict=True)
        r = lax.broadcasted_iota(I32, (tk, tk), 0)
        c = lax.broadcasted_iota(I32, (tk, tk), 1)
        before = jnp.where(c < r, 1.0, 0.0).astype(BF16)

        def body(kt, seen):
            x = scores_ref[kt]
            eq = x == thr
            eqf = jnp.where(eq, 1.0, 0.0)
            prior = jnp.dot(before, eqf.astype(BF16), preferred_element_type=F32) + seen
            scores_ref[kt] = jnp.where(eq & (prior >= need), -jnp.inf, x)
            return seen + jnp.sum(eqf.reshape(tk // SUBLANES, SUBLANES, w), axis=0).sum(axis=0, keepdims=True)

        lax.fori_loop(0, nkt, body, jnp.zeros((1, w), F32))


def _half_mask(x_pair, head):
    lane = lax.broadcasted_iota(I32, x_pair.shape, 1)
    keep = (lane >= HEAD_DIM) if head % 2 else (lane < HEAD_DIM)
    return jnp.where(keep, x_pair, jnp.zeros_like(x_pair))


def _attend_prompt_kernel(q_ref, iq_ref, iwt_ref, k_ref, vt_ref, ik_ref, o_ref, scores_ref, *, topk):
    tq = q_ref.shape[0]
    tk = scores_ref.shape[1]
    j = pl.program_id(1)
    nkt = j + 1
    q0 = j * tq
    row = lax.broadcasted_iota(I32, (tk, tq), 0)
    lane = lax.broadcasted_iota(I32, (tk, tq), 1)
    qpos = q0 + lane

    iq = iq_ref[...]
    iqm = [_half_mask(iq[:, (h // 2) * LANES:(h // 2 + 1) * LANES], h) for h in range(N_IDX_HEADS)]
    iw = iwt_ref[...]

    def score_body(kt, _):
        k0 = pl.multiple_of(kt * tk, tk)
        ikt = ik_ref[pl.ds(k0, tk), :]
        s = jnp.zeros((tk, tq), F32)
        for h in range(N_IDX_HEADS):
            r = lax.dot_general(ikt, iqm[h], _NT, preferred_element_type=F32)
            s = s + jnp.maximum(r, 0.0) * iw[h:h + 1, :]
        kpos = k0 + row
        adm = lax.shift_right_logical(kpos, CHUNK_SHIFT) <= lax.shift_right_logical(qpos, CHUNK_SHIFT)
        scores_ref[kt] = jnp.where(adm, s, -jnp.inf)
        return 0

    lax.fori_loop(0, nkt, score_body, 0)

    thr = _select_threshold(scores_ref, nkt, topk)
    _break_ties(scores_ref, nkt, thr, topk)

    def dist_body(kt, _):
        kpos = kt * tk + row
        dist = jnp.abs(qpos - kpos).astype(F32)
        scores_ref[kt] = jnp.where(scores_ref[kt] >= thr, dist, MASKED_DIST)
        return 0

    lax.fori_loop(0, nkt, dist_body, 0)

    qfull = q_ref[...]
    outs = []
    for h in range(N_HEADS):
        pair = h // 2
        qh = _half_mask(qfull[:, pair * LANES:(pair + 1) * LANES], h) * jnp.asarray(ATTN_SCALE, BF16)
        slope = 2.0 ** (-8.0 * (h + 1) / N_HEADS)

        def att_body(kt, carry, pair=pair, qh=qh, slope=slope, h=h):
            m, l, acc = carry
            k0 = pl.multiple_of(kt * tk, tk)
            kp = k_ref[pl.ds(k0, tk), pair * LANES:(pair + 1) * LANES]
            lt = lax.dot_general(kp, qh, _NT, preferred_element_type=F32) - slope * scores_ref[kt]
            m_new = jnp.maximum(m, jnp.max(lt, axis=0, keepdims=True))
            alpha = jnp.exp(m - m_new)
            p = jnp.exp(lt - m_new)
            l = alpha * l + jnp.sum(p, axis=0, keepdims=True)
            vt = vt_ref[kt, h * HEAD_DIM:(h + 1) * HEAD_DIM, :]
            acc = alpha * acc + jnp.dot(vt, p.astype(BF16), preferred_element_type=F32)
            return m_new, l, acc

        init = (jnp.full((1, tq), -jnp.inf, F32), jnp.zeros((1, tq), F32), jnp.zeros((HEAD_DIM, tq), F32))
        m, l, acc = lax.fori_loop(0, nkt, att_body, init)
        outs.append(acc / l)
    o_ref[...] = jnp.concatenate(outs, axis=0).T.astype(BF16)


def _attend_prompt(q, iq, iwt, kb, vt, ikb, n_seq, seq_len, topk):
    n = q.shape[0]
    t = ATT_TILE
    nq = seq_len // t
    once = pl.Buffered(1)
    return pl.pallas_call(
        functools.partial(_attend_prompt_kernel, topk=topk),
        out_shape=jax.ShapeDtypeStruct((n, ATT_WIDTH), BF16),
        grid=(n_seq, nq),
        in_specs=[pl.BlockSpec((t, ATT_WIDTH), lambda b, j: (b * nq + j, 0)),
                  pl.BlockSpec((t, ATT_WIDTH), lambda b, j: (b * nq + j, 0)),
                  pl.BlockSpec((N_IDX_HEADS, t), lambda b, j: (0, b * nq + j)),
                  pl.BlockSpec((seq_len, ATT_WIDTH), lambda b, j: (b, 0), pipeline_mode=once),
                  pl.BlockSpec((nq, ATT_WIDTH, t), lambda b, j: (b, 0, 0), pipeline_mode=once),
                  pl.BlockSpec((seq_len, LANES), lambda b, j: (b, 0), pipeline_mode=once)],
        out_specs=pl.BlockSpec((t, ATT_WIDTH), lambda b, j: (b * nq + j, 0)),
        scratch_shapes=[pltpu.VMEM((nq, t, t), F32)],
        compiler_params=_cparams("arbitrary", "arbitrary"),
        name="attend_prompt",
    )(q, iq, iwt, kb, vt, ikb)


def _attend_sample_kernel(q_ref, iq_ref, iwl_ref, kn_ref, vn_ref, ikn_ref, ck_ref, cv_ref, ci_ref,
                          o_ref, scores_ref, acc_ref, *, topk, past):
    tq = q_ref.shape[0]
    nkt, tk, w = scores_ref.shape
    nct = nkt - 1
    row = lax.broadcasted_iota(I32, (tk, w), 0)
    lane = lax.broadcasted_iota(I32, (tk, w), 1)
    qpos = past + (lane & (tq - 1))
    lane_head = lax.shift_right_logical(lax.broadcasted_iota(I32, (1, w), 1), tq.bit_length() - 1)
    slope = lax.bitcast_convert_type(lax.shift_left(126 - lane_head, 23), F32)

    iq = iq_ref[...]
    q = q_ref[...]
    iq_rows = jnp.concatenate([iq[:, h * IDX_DIM:(h + 1) * IDX_DIM] for h in range(N_IDX_HEADS)], axis=0)
    q_rows = jnp.concatenate(
        [_half_mask_wide(q, h) for h in range(N_HEADS)], axis=0) * jnp.asarray(ATTN_SCALE, BF16)
    iwl = iwl_ref[...]
    pad = tk - tq
    kn = jnp.concatenate([kn_ref[...], jnp.zeros((pad, ATT_WIDTH), BF16)], axis=0)
    vn = jnp.concatenate([vn_ref[...].astype(BF16), jnp.zeros((pad, ATT_WIDTH), BF16)], axis=0)
    ikn = jnp.concatenate([ikn_ref[:, :IDX_DIM], jnp.zeros((pad, IDX_DIM), BF16)], axis=0)

    def score_tile(ik_tile, kpos, valid):
        r = lax.dot_general(ik_tile, iq_rows, _NT, preferred_element_type=F32)
        s = jnp.maximum(r, 0.0) * iwl
        s = s + pltpu.roll(s, w // 2, 1)
        s = s + pltpu.roll(s, w // 4, 1)
        s = s + pltpu.roll(s, w // 8, 1)
        adm = lax.shift_right_logical(kpos, CHUNK_SHIFT) <= lax.shift_right_logical(qpos, CHUNK_SHIFT)
        return jnp.where(adm & valid, s, -jnp.inf)

    def score_body(kt, _):
        k0 = pl.multiple_of(kt * tk, tk)
        scores_ref[kt] = score_tile(ci_ref[pl.ds(k0, tk), :].astype(BF16), k0 + row, True)
        return 0

    lax.fori_loop(0, nct, score_body, 0)
    scores_ref[nct] = score_tile(ikn, past + row, row < tq)

    thr = _select_threshold(scores_ref, nkt, topk)
    _break_ties(scores_ref, nkt, thr, topk)

    def logit_tile(kt, k_tile, kpos):
        dist = jnp.where(scores_ref[kt] >= thr, jnp.abs(qpos - kpos).astype(F32), MASKED_DIST)
        lt = lax.dot_general(k_tile, q_rows, _NT, preferred_element_type=F32) - slope * dist
        scores_ref[kt] = lt
        return jnp.max(lt, axis=0, keepdims=True)

    def logit_body(kt, m):
        k0 = pl.multiple_of(kt * tk, tk)
        return jnp.maximum(m, logit_tile(kt, ck_ref[pl.ds(k0, tk), :].astype(BF16), k0 + row))

    m = lax.fori_loop(0, nct, logit_body, jnp.full((1, w), -jnp.inf, F32))
    m = jnp.maximum(m, logit_tile(nct, kn, past + row))

    acc_ref[...] = jnp.zeros_like(acc_ref)

    def pv_tile(kt, v_tile):
        p = jnp.exp(scores_ref[kt] - m)
        acc_ref[...] += jnp.dot(p.T.astype(BF16), v_tile, preferred_element_type=F32)
        return jnp.sum(p, axis=0, keepdims=True)

    def pv_body(kt, l):
        k0 = pl.multiple_of(kt * tk, tk)
        return l + pv_tile(kt, cv_ref[pl.ds(k0, tk), :].astype(BF16))

    l = lax.fori_loop(0, nct, pv_body, jnp.zeros((1, w), F32))
    l = l + pv_tile(nct, vn)

    l_col = jnp.broadcast_to(l, (w, w)).T
    out_lane_head = lax.shift_right_logical(lax.broadcasted_iota(I32, (tq, ATT_WIDTH), 1), 6)
    y = jnp.zeros((tq, ATT_WIDTH), F32)
    for h in range(N_HEADS):
        blk = acc_ref[h * tq:(h + 1) * tq, :] / jnp.concatenate([l_col[h * tq:(h + 1) * tq, :]] * (ATT_WIDTH // w), axis=1)
        y = y + jnp.where(out_lane_head == h, blk, 0.0)
    o_ref[...] = y.astype(BF16)


def _half_mask_wide(x, head):
    lane = lax.broadcasted_iota(I32, x.shape, 1)
    keep = lax.shift_right_logical(lane, 6) == head
    return jnp.where(keep, x, jnp.zeros_like(x))


def _attend_sample(q, iq, iwl, kb, vf, ikb, cache_k, cache_v, cache_i, topk):
    n_seq, past, _ = cache_k.shape
    tq = q.shape[0] // n_seq
    tk = ATT_TILE
    w = N_HEADS * tq
    assert w == LANES and past % tk == 0
    new = lambda width: pl.BlockSpec((tq, width), lambda b: (b, 0))
    cache = lambda width: pl.BlockSpec((None, past, width), lambda b: (b, 0, 0))
    return pl.pallas_call(
        functools.partial(_attend_sample_kernel, topk=topk, past=past),
        out_shape=jax.ShapeDtypeStruct((n_seq * tq, ATT_WIDTH), BF16),
        grid=(n_seq,),
        in_specs=[new(ATT_WIDTH), new(ATT_WIDTH), pl.BlockSpec((None, 1, w), lambda b: (b, 0, 0)),
                  new(ATT_WIDTH), new(ATT_WIDTH), new(LANES),
                  cache(ATT_WIDTH), cache(ATT_WIDTH), cache(IDX_DIM)],
        out_specs=new(ATT_WIDTH),
        scratch_shapes=[pltpu.VMEM((past // tk + 1, tk, w), F32), pltpu.VMEM((w, ATT_WIDTH), F32)],
        compiler_params=_cparams("arbitrary"),
        name="attend_sample",
    )(q, iq, iwl, kb, vf, ikb, cache_k, cache_v, cache_i)


TOK_ROWS = D_MODEL // LANES


def _store_token_tiles(ref, x):
    t = x.shape[0]
    for c in range(TOK_ROWS):
        ref[pl.ds(c, t, stride=TOK_ROWS), :] = x[:, c * LANES:(c + 1) * LANES]


def _load_token_tiles(ref):
    t = ref.shape[0] // TOK_ROWS
    return jnp.concatenate([ref[pl.ds(c, t, stride=TOK_ROWS), :] for c in range(TOK_ROWS)], axis=1)


def _layer_norm(x, g, b):
    mu = jnp.mean(x, axis=-1, keepdims=True)
    xc = x - mu
    var = jnp.mean(xc * xc, axis=-1, keepdims=True)
    return xc * lax.rsqrt(var + LN_EPS) * g + b


def _post_kernel(x_ref, ya_ref, yb_ref, ga_ref, scf_ref, shf_ref, wo_ref, bo_ref, g1_ref, b1_ref,
                 wrh_ref, wrl_ref, br_ref,
                 x1_ref, h2_ref, tope_ref, gate_ref, rank_ref, cnt_ref):
    t = x_ref.shape[0]
    y = (jnp.dot(ya_ref[...], wo_ref[:A_WIDTH, :], preferred_element_type=F32)
         + jnp.dot(yb_ref[...], wo_ref[A_WIDTH:, :], preferred_element_type=F32) + bo_ref[...])
    x1 = _layer_norm(DEEPNORM_ALPHA * x_ref[...] + (1.0 + ga_ref[...]) * y, g1_ref[...], b1_ref[...])
    x1_ref[...] = x1
    h2 = x1 * (1.0 + scf_ref[...]) + shf_ref[...]
    _store_token_tiles(h2_ref, h2)

    hh = h2.astype(BF16)
    hl = (h2 - hh.astype(F32)).astype(BF16)
    logits = (lax.dot_general(wrh_ref[...], hh, _NT, preferred_element_type=F32)
              + lax.dot_general(wrh_ref[...], hl, _NT, preferred_element_type=F32)
              + lax.dot_general(wrl_ref[...], hh, _NT, preferred_element_type=F32) + br_ref[...])
    erow = lax.broadcasted_iota(I32, (N_EXPERTS, t), 0)
    vals, idxs = [], []
    for _ in range(TOP_K):
        v = jnp.max(logits, axis=0, keepdims=True)
        i = jnp.min(jnp.where(logits == v, erow, N_EXPERTS), axis=0, keepdims=True)
        vals.append(v)
        idxs.append(i)
        logits = jnp.where(erow == i, -jnp.inf, logits)
    ex = [jnp.exp(v - vals[0]) for v in vals]
    den = ex[0] + ex[1] + ex[2] + ex[3]
    gate_ref[...] = jnp.concatenate([e / den for e in ex], axis=0)
    tope_ref[...] = jnp.concatenate(idxs, axis=0)

    @pl.when(pl.program_id(0) == 0)
    def _():
        cnt_ref[...] = jnp.zeros_like(cnt_ref)

    hit = jnp.zeros((N_EXPERTS, t), F32)
    for i in idxs:
        hit = hit + jnp.where(erow == i, 1.0, 0.0)
    hitb = hit.astype(BF16)
    r = lax.broadcasted_iota(I32, (t, t), 0)
    c = lax.broadcasted_iota(I32, (t, t), 1)
    earlier = jnp.where(r < c, 1.0, 0.0).astype(BF16)
    before = jnp.dot(hitb, earlier, preferred_element_type=F32) + cnt_ref[...]
    total = jnp.dot(hitb, jnp.ones((t, t), BF16), preferred_element_type=F32)
    rank_ref[...] = jnp.concatenate(
        [jnp.sum(jnp.where(erow == i, before, 0.0), axis=0, keepdims=True) for i in idxs], axis=0).astype(I32)
    cnt_ref[...] += total


def _post(x, ya, yb, ga, scf, shf, w_o, b_o, g1, b1, wrh, wrl, br):
    n, d = x.shape
    t = ROW_TILE
    nt = n // t
    if ga.shape[0] == n:
        mod_spec = pl.BlockSpec((t, d), lambda i: (i, 0))
    else:
        tiles_per_seq = nt // ga.shape[0]
        ga, scf, shf = (a.reshape(a.shape[0], 1, d) for a in (ga, scf, shf))
        mod_spec = pl.BlockSpec((None, 1, d), lambda i: (i // tiles_per_seq, 0, 0))
    row = lambda w: pl.BlockSpec((t, w), lambda i: (i, 0))
    col = lambda r: pl.BlockSpec((r, t), lambda i: (0, i))
    const = lambda a: pl.BlockSpec(a.shape, lambda i: (0, 0))
    return pl.pallas_call(
        _post_kernel,
        out_shape=(jax.ShapeDtypeStruct((n, d), F32), jax.ShapeDtypeStruct((n * TOK_ROWS, LANES), F32),
                   jax.ShapeDtypeStruct((TOP_K, n), I32), jax.ShapeDtypeStruct((TOP_K, n), F32),
                   jax.ShapeDtypeStruct((TOP_K, n), I32), jax.ShapeDtypeStruct((N_EXPERTS, t), F32)),
        grid=(nt,),
        in_specs=[row(d), row(A_WIDTH), row(ATT_WIDTH), mod_spec, mod_spec, mod_spec,
                  const(w_o), const(b_o), const(g1), const(b1), const(wrh), const(wrl), const(br)],
        out_specs=(row(d), pl.BlockSpec((t * TOK_ROWS, LANES), lambda i: (i, 0)), col(TOP_K), col(TOP_K), col(TOP_K),
                   pl.BlockSpec((N_EXPERTS, t), lambda i: (0, 0))),
        compiler_params=_cparams("arbitrary"),
        name="post",
    )(x, ya, yb, ga, scf, shf, w_o, b_o, g1, b1, wrh, wrl, br)


def _token_rows(ref, r):
    return ref.at[pl.ds(pl.multiple_of(r * TOK_ROWS, TOK_ROWS), TOK_ROWS)]


def _token_copies_wait(hbm_ref, vmem_ref, sem, n_tokens):
    rows = n_tokens * TOK_ROWS
    pltpu.make_async_copy(hbm_ref.at[pl.ds(0, rows)], vmem_ref.at[pl.ds(0, rows)], sem).wait()


def _dispatch_kernel(zrow_ref, dest_ref, h_ref, xp_ref, zero_ref, sem):
    t = h_ref.shape[0] // TOK_ROWS

    block_rows = MOE_BLOCK * TOK_ROWS

    def zero_block(slot0):
        z0 = pl.multiple_of(slot0 * TOK_ROWS, block_rows)
        cp = pltpu.make_async_copy(zero_ref, xp_ref.at[pl.ds(z0, block_rows)], sem)
        cp.start()
        cp.wait()

    @pl.when(pl.program_id(0) == 0)
    def _():
        zero_ref[...] = jnp.zeros_like(zero_ref)
        for e in range(N_EXPERTS):
            @pl.when(zrow_ref[e] >= 0)
            def _():
                zero_block(zrow_ref[e])

        def unused(b, _):
            zero_block(b * MOE_BLOCK)
            return 0

        lax.fori_loop(zrow_ref[N_EXPERTS], xp_ref.shape[0] // block_rows, unused, 0)

    def body(i, _):
        for k in range(TOP_K):
            pltpu.make_async_copy(_token_rows(h_ref, i), _token_rows(xp_ref, dest_ref[k, i]), sem).start()
        return 0

    lax.fori_loop(0, t, body, 0)
    for k in range(TOP_K):
        _token_copies_wait(xp_ref, h_ref, sem, t)


def _dispatch(zrow, dest, h2, n_slots):
    t = ROW_TILE
    n = h2.shape[0] // TOK_ROWS
    grid_spec = pltpu.PrefetchScalarGridSpec(
        num_scalar_prefetch=1,
        grid=(n // t,),
        in_specs=[pl.BlockSpec((TOP_K, t), lambda i, z: (0, i), memory_space=pltpu.SMEM),
                  pl.BlockSpec((t * TOK_ROWS, LANES), lambda i, z: (i, 0))],
        out_specs=pl.BlockSpec(memory_space=pl.ANY),
        scratch_shapes=[pltpu.VMEM((MOE_BLOCK * TOK_ROWS, LANES), F32), pltpu.SemaphoreType.DMA],
    )
    return pl.pallas_call(
        _dispatch_kernel,
        out_shape=jax.ShapeDtypeStruct((n_slots * TOK_ROWS, LANES), F32),
        grid_spec=grid_spec,
        compiler_params=_cparams("arbitrary"),
        name="dispatch",
    )(zrow, dest, h2)


def _experts_kernel(be_ref, bi_ref, nu_ref, x_ref, wg_ref, bg_ref, wu_ref, bu_ref, wd_ref, bd_ref, y_ref):
    @pl.when(pl.program_id(0) < nu_ref[0])
    def _():
        x = _load_token_tiles(x_ref).astype(BF16)
        g = jnp.minimum(jnp.dot(x, wg_ref[...], preferred_element_type=F32) + bg_ref[...], SWIGLU_LIMIT)
        u = jnp.clip(jnp.dot(x, wu_ref[...], preferred_element_type=F32) + bu_ref[...], -SWIGLU_LIMIT, SWIGLU_LIMIT)
        a = g * jax.nn.sigmoid(SWIGLU_ALPHA * g)
        mid = ((u + 1.0) * a).astype(BF16)
        _store_token_tiles(y_ref, jnp.dot(mid, wd_ref[...], preferred_element_type=F32) + bd_ref[...])

    @pl.when(pl.program_id(0) >= nu_ref[0])
    def _():
        y_ref[...] = jnp.zeros_like(y_ref)


def _experts(block_e, block_i, n_used, xp, wg, bg, wu, bu, wd, bd):
    d, f = wg.shape[1], wg.shape[2]
    nb = xp.shape[0] // (MOE_BLOCK * TOK_ROWS)
    wspec = lambda a, b: pl.BlockSpec((None, a, b), lambda i, be, bi, nu: (be[i], 0, 0))
    slots = pl.BlockSpec((MOE_BLOCK * TOK_ROWS, LANES), lambda i, be, bi, nu: (bi[i], 0))
    grid_spec = pltpu.PrefetchScalarGridSpec(
        num_scalar_prefetch=3,
        grid=(nb,),
        in_specs=[slots, wspec(d, f), wspec(1, f), wspec(d, f), wspec(1, f), wspec(f, d), wspec(1, d)],
        out_specs=pl.BlockSpec((MOE_BLOCK * TOK_ROWS, LANES), lambda i, be, bi, nu: (i, 0)),
    )
    return pl.pallas_call(
        _experts_kernel,
        out_shape=jax.ShapeDtypeStruct(xp.shape, F32),
        grid_spec=grid_spec,
        compiler_params=_cparams("arbitrary"),
        name="experts",
    )(block_e, block_i, n_used, xp, wg, bg, wu, bu, wd, bd)


def _combine_kernel(dest_ref, gate_ref, x1_ref, gf_ref, g2_ref, b2_ref, yp_ref, o_ref, buf_ref, sem):
    t = x1_ref.shape[0]

    def body(i, _):
        for k in range(TOP_K):
            pltpu.make_async_copy(_token_rows(yp_ref, dest_ref[k, i]), _token_rows(buf_ref.at[k], i), sem).start()
        return 0

    lax.fori_loop(0, t, body, 0)
    for k in range(TOP_K):
        _token_copies_wait(yp_ref, buf_ref.at[k], sem, t)

    gates = jnp.concatenate([gate_ref[...], jnp.zeros((LANES - TOP_K, t), F32)], axis=0).T
    f = gates[:, 0:1] * _load_token_tiles(buf_ref.at[0])
    for k in range(1, TOP_K):
        f = f + gates[:, k:k + 1] * _load_token_tiles(buf_ref.at[k])
    o_ref[...] = _layer_norm(DEEPNORM_ALPHA * x1_ref[...] + (1.0 + gf_ref[...]) * f, g2_ref[...], b2_ref[...])


def _combine(dest, gates, x1, gf, g2, b2, yp):
    n, d = x1.shape
    t = ROW_TILE
    nt = n // t
    if gf.shape[0] == n:
        mod_spec = pl.BlockSpec((t, d), lambda i: (i, 0))
    else:
        tiles_per_seq = nt // gf.shape[0]
        gf = gf.reshape(gf.shape[0], 1, d)
        mod_spec = pl.BlockSpec((None, 1, d), lambda i: (i // tiles_per_seq, 0, 0))
    return pl.pallas_call(
        _combine_kernel,
        out_shape=jax.ShapeDtypeStruct((n, d), F32),
        grid=(nt,),
        in_specs=[pl.BlockSpec((TOP_K, t), lambda i: (0, i), memory_space=pltpu.SMEM),
                  pl.BlockSpec((TOP_K, t), lambda i: (0, i)),
                  pl.BlockSpec((t, d), lambda i: (i, 0)), mod_spec,
                  pl.BlockSpec((1, d), lambda i: (0, 0)), pl.BlockSpec((1, d), lambda i: (0, 0)),
                  pl.BlockSpec(memory_space=pl.ANY)],
        out_specs=pl.BlockSpec((t, d), lambda i: (i, 0)),
        scratch_shapes=[pltpu.VMEM((TOP_K, t * TOK_ROWS, LANES), F32), pltpu.SemaphoreType.DMA],
        compiler_params=_cparams("arbitrary"),
        name="combine",
    )(dest, gates, x1, gf, g2, b2, yp)


def _moe(h2, tope, rank, counts, gates, x1, gf, g2, b2, experts_w):
    n = tope.shape[1]
    nb = (n * TOP_K + N_EXPERTS * (MOE_BLOCK - 1) + MOE_BLOCK - 1) // MOE_BLOCK
    pcounts = (counts + MOE_BLOCK - 1) // MOE_BLOCK * MOE_BLOCK
    pend = jnp.cumsum(pcounts)
    pstart = pend - pcounts
    dest = (pstart[tope] + rank).astype(I32)
    n_used = (pend[-1] // MOE_BLOCK).astype(I32)
    zrow = jnp.concatenate([jnp.where(counts > 0, pend - MOE_BLOCK, -1), n_used.reshape(1)]).astype(I32)
    blk = jnp.arange(nb, dtype=I32)
    block_i = jnp.minimum(blk, n_used - 1)
    block_e = jnp.minimum(jnp.searchsorted(pend, block_i * MOE_BLOCK, side="right"), N_EXPERTS - 1).astype(I32)
    xp = _dispatch(zrow, dest, h2, nb * MOE_BLOCK)
    yp = _experts(block_e, block_i, n_used.reshape(1), xp, *experts_w)
    return _combine(dest, gates, x1, gf, g2, b2, yp)


def kernel(x_prompt, x_sample, c_prompt, c_sample, cache_k, cache_v, cache_kidx, w_in, a_ln_g, a_ln_b, a_ws, a_bs,
           w_o, b_o, w_c, b_c, ln1_g, ln1_b, ln2_g, ln2_b, w_router, b_router, w_gate, b_gate, w_up, b_up,
           w_down, b_down):
    bp, s, d = x_prompt.shape
    bs, ts, _ = x_sample.shape
    past = cache_k.shape[2]
    np_, ns = bp * s, bs * ts

    wi = w_in[0]
    w_in_p = jnp.zeros((d, IN_COLS_PAD), F32)
    w_in_p = w_in_p.at[:, :OFF_IK].set(wi[:, :OFF_IK])
    w_in_p = w_in_p.at[:, OFF_IK:OFF_IK + IDX_DIM].set(wi[:, OFF_IK:OFF_IK + IDX_DIM])
    w_in_p = w_in_p.at[:, OFF_IK + IDX_DIM:OFF_IK + 2 * IDX_DIM].set(wi[:, OFF_IK:OFF_IK + IDX_DIM])
    w_in_p = w_in_p.at[:, OFF_IW:OFF_IW + N_IDX_HEADS].set(wi[:, OFF_IK + IDX_DIM:OFF_IK + IDX_DIM + N_IDX_HEADS])
    w_in_p = w_in_p.astype(BF16)
    lng, lnb = a_ln_g[0].reshape(1, A_WIDTH), a_ln_b[0].reshape(1, A_WIDTH)
    wtril = jnp.tril(a_ws[0])
    wm_p = wtril.astype(BF16)
    bm_p = jnp.broadcast_to(a_bs[0][:, :, None], (A_GROUPS, A_CHUNK, A_GROUP_DIM)).astype(F32)
    rep = A_CHUNK // ts
    wm_s = jnp.einsum("ab,gij->gaibj", jnp.eye(rep, dtype=F32), wtril[:, :ts, :ts]).reshape(
        A_GROUPS, A_CHUNK, A_CHUNK).astype(BF16)
    bm_s = jnp.broadcast_to(jnp.tile(a_bs[0][:, :ts], (1, rep))[:, :, None], (A_GROUPS, A_CHUNK, A_GROUP_DIM)).astype(F32)
    w_o_b = w_o[0].astype(BF16)
    b_o_r = b_o[0].reshape(1, d)
    wr_t = w_router[0].T
    wrh = wr_t.astype(BF16)
    wrl = (wr_t - wrh.astype(F32)).astype(BF16)
    br = jnp.broadcast_to(b_router[0][:, None], (N_EXPERTS, ROW_TILE)).astype(F32)
    experts_w = (w_gate[0].astype(BF16), b_gate[0][:, None, :], w_up[0].astype(BF16), b_up[0][:, None, :],
                 w_down[0].astype(BF16), b_down[0][:, None, :])
    g1, b1 = ln1_g[0].reshape(1, d), ln1_b[0].reshape(1, d)
    g2, b2 = ln2_g[0].reshape(1, d), ln2_b[0].reshape(1, d)

    mods = _cond_mods(jnp.concatenate([c_prompt, c_sample], axis=0), w_c[0], b_c[0]).reshape(bp + bs, 6, d)
    mods_p = [mods[:bp, i] for i in range(6)]
    mods_s = [jnp.repeat(mods[bp:, i], ts, axis=0) for i in range(6)]

    xp2 = x_prompt.reshape(np_, d)
    ya, q, kf, kb, vf, vt, iq, ikf, ikb, iwt, _ = _project(xp2, mods_p[1], mods_p[0], w_in_p, lng, lnb, wm_p, bm_p)
    yb = _attend_prompt(q, iq, iwt, kb, vt, ikb, bp, s, min(TOPK_MAX, s // 4))
    x1, h2, tope, gates, rank, cnt = _post(xp2, ya, yb, mods_p[2], mods_p[4], mods_p[3], w_o_b, b_o_r, g1, b1, wrh, wrl, br)
    y_p = _moe(h2, tope, rank, cnt[:, 0].astype(I32), gates, x1, mods_p[5], g2, b2, experts_w)
    out_p = (y_p.reshape(bp, s, d), kf.reshape(1, bp, s, N_HEADS, HEAD_DIM), vf.reshape(1, bp, s, N_HEADS, HEAD_DIM),
             ikf.reshape(1, bp, s, IDX_DIM))

    xs2 = x_sample.reshape(ns, d)
    ya, q, kf, kb, vf, vt, iq, ikf, ikb, iwt, va = _project(xs2, mods_s[1], mods_s[0], w_in_p, lng, lnb, wm_s, bm_s)
    iwl = jnp.transpose(iwt.reshape(N_IDX_HEADS, bs, ts), (1, 0, 2)).reshape(bs, 1, N_IDX_HEADS * ts)
    yb = _attend_sample(q, iq, iwl, kb, vf, ikb, cache_k[0].reshape(bs, past, ATT_WIDTH),
                        cache_v[0].reshape(bs, past, ATT_WIDTH), cache_kidx[0], min(TOPK_MAX, (past + ts) // 4))
    x1, h2, tope, gates, rank, cnt = _post(xs2, ya, yb, mods_s[2], mods_s[4], mods_s[3], w_o_b, b_o_r, g1, b1, wrh, wrl, br)
    y_s = _moe(h2, tope, rank, cnt[:, 0].astype(I32), gates, x1, mods_s[5], g2, b2, experts_w)

    return (out_p[0], y_s.reshape(bs, ts, d), out_p[1], out_p[2], out_p[3],
            kf.reshape(1, bs, ts, N_HEADS, HEAD_DIM), vf.reshape(1, bs, ts, N_HEADS, HEAD_DIM),
            ikf.reshape(1, bs, ts, IDX_DIM), va.reshape(1, bs, ts, A_WIDTH))
```

```python
import functools

import jax
import jax.numpy as jnp
from jax import lax
from jax.experimental import pallas as pl
from jax.experimental.pallas import tpu as pltpu

F32 = jnp.float32
BF16 = jnp.bfloat16
I32 = jnp.int32

D_MODEL = 1024
CHUNK_SHIFT = 6
A_GROUPS = 4
A_GROUP_DIM = 128
A_WIDTH = A_GROUPS * A_GROUP_DIM
A_CHUNK = 128
N_HEADS = 8
HEAD_DIM = 64
ATT_WIDTH = N_HEADS * HEAD_DIM
N_IDX_HEADS = 8
IDX_DIM = 64
TOPK_MAX = 256
ATTN_SCALE = HEAD_DIM ** -0.5
IDX_W_SCALE = (N_IDX_HEADS ** -0.5) * (IDX_DIM ** -0.5)
N_EXPERTS = 32
TOP_K = 4
SWIGLU_LIMIT = 7.0
SWIGLU_ALPHA = 1.702
DEEPNORM_ALPHA = 2.0 ** 0.25
LN_EPS = 1e-5

LANES = 128
SUBLANES = 8
VMEM_LIMIT_BYTES = 56 * 1024 * 1024

ROW_TILE = 256
MOE_BLOCK = 256
ATT_TILE = 256

OFF_AU, OFF_AV, OFF_Q, OFF_K, OFF_V, OFF_IQ = 0, 512, 1024, 1536, 2048, 2560
OFF_IK = 3072
OFF_IW = 3200
IN_COLS_PAD = 3328

MASKED_DIST = 3.0e32
F32_MAX = 3.4028234663852886e38
INT32_MIN = -(2 ** 31)
KEY_NEG_INF = INT32_MIN + 0x7FFFFF

_NT = (((1,), (1,)), ((), ()))


def _cparams(*sem):
    return pltpu.CompilerParams(dimension_semantics=sem, vmem_limit_bytes=VMEM_LIMIT_BYTES)


def _mods_kernel(c_ref, w_ref, b_ref, o_ref):
    c = c_ref[...]
    s = c * jax.nn.sigmoid(c)
    o_ref[...] = jnp.dot(s.astype(BF16), w_ref[...].astype(BF16), preferred_element_type=F32) + b_ref[...]


def _cond_mods(c, w_c, b_c):
    nb, d = c.shape
    n_out = w_c.shape[1]
    return pl.pallas_call(
        _mods_kernel,
        out_shape=jax.ShapeDtypeStruct((nb, n_out), F32),
        grid=(n_out // d,),
        in_specs=[pl.BlockSpec((nb, d), lambda j: (0, 0)),
                  pl.BlockSpec((d, d), lambda j: (0, j)),
                  pl.BlockSpec((1, d), lambda j: (0, j))],
        out_specs=pl.BlockSpec((nb, d), lambda j: (0, j)),
        compiler_params=_cparams("arbitrary"),
        name="mods",
    )(c, w_c, b_c.reshape(1, n_out))


def _gelu(x):
    return 0.5 * x * (1.0 + lax.erf(x * 0.7071067811865476))


def _project_kernel(x_ref, sc_ref, sh_ref, w_ref, lng_ref, lnb_ref, wm_ref, bm_ref,
                    ya_ref, q_ref, kf_ref, kb_ref, vf_ref, vt_ref, iq_ref, ikf_ref, ikb_ref, iwt_ref, va_ref):
    t = x_ref.shape[0]
    h = (x_ref[...] * (1.0 + sc_ref[...]) + sh_ref[...]).astype(BF16)

    def proj(c0, n):
        return jnp.dot(h, w_ref[:, c0:c0 + n], preferred_element_type=F32)

    u = _gelu(proj(OFF_AU, A_WIDTH))
    gv = _gelu(proj(OFF_AV, A_WIDTH))
    for g in range(A_GROUPS):
        lo, hi = g * A_GROUP_DIM, (g + 1) * A_GROUP_DIM
        xg = gv[:, lo:hi]
        mu = jnp.mean(xg, axis=-1, keepdims=True)
        xc = xg - mu
        var = jnp.mean(xc * xc, axis=-1, keepdims=True)
        vg = xc * lax.rsqrt(var + LN_EPS) * lng_ref[:, lo:hi] + lnb_ref[:, lo:hi]
        va_ref[:, lo:hi] = vg
        vgb = vg.astype(BF16)
        for c in range(t // A_CHUNK):
            r0, r1 = c * A_CHUNK, (c + 1) * A_CHUNK
            mixed = jnp.dot(wm_ref[g], vgb[r0:r1, :], preferred_element_type=F32) + bm_ref[g]
            ya_ref[r0:r1, lo:hi] = (u[r0:r1, lo:hi] * mixed).astype(BF16)

    q_ref[...] = proj(OFF_Q, ATT_WIDTH).astype(BF16)
    k = proj(OFF_K, ATT_WIDTH)
    kf_ref[...] = k
    kb_ref[...] = k.astype(BF16)
    v = proj(OFF_V, ATT_WIDTH)
    vf_ref[...] = v
    vt_ref[0] = v.T.astype(BF16)
    iq_ref[...] = proj(OFF_IQ, N_IDX_HEADS * IDX_DIM).astype(BF16)
    ik2 = proj(OFF_IK, LANES)
    ikf_ref[...] = ik2[:, :IDX_DIM]
    ikb_ref[...] = ik2.astype(BF16)
    iw = proj(OFF_IW, LANES) * IDX_W_SCALE
    iwt_ref[...] = iw.T[:N_IDX_HEADS, :]


def _project(x, sc, sh, w_in_p, lng, lnb, wm, bm):
    n, d = x.shape
    t = ROW_TILE
    nt = n // t
    if sc.shape[0] == n:
        mod_spec = pl.BlockSpec((t, d), lambda i: (i, 0))
    else:
        tiles_per_seq = nt // sc.shape[0]
        sc = sc.reshape(sc.shape[0], 1, d)
        sh = sh.reshape(sh.shape[0], 1, d)
        mod_spec = pl.BlockSpec((None, 1, d), lambda i: (i // tiles_per_seq, 0, 0))
    row = lambda w: pl.BlockSpec((t, w), lambda i: (i, 0))
    const2 = lambda a: pl.BlockSpec(a.shape, lambda i: (0, 0))
    const3 = lambda a: pl.BlockSpec(a.shape, lambda i: (0, 0, 0))
    out_shape = (
        jax.ShapeDtypeStruct((n, A_WIDTH), BF16),
        jax.ShapeDtypeStruct((n, ATT_WIDTH), BF16),
        jax.ShapeDtypeStruct((n, ATT_WIDTH), F32),
        jax.ShapeDtypeStruct((n, ATT_WIDTH), BF16),
        jax.ShapeDtypeStruct((n, ATT_WIDTH), F32),
        jax.ShapeDtypeStruct((nt, ATT_WIDTH, t), BF16),
        jax.ShapeDtypeStruct((n, ATT_WIDTH), BF16),
        jax.ShapeDtypeStruct((n, IDX_DIM), F32),
        jax.ShapeDtypeStruct((n, LANES), BF16),
        jax.ShapeDtypeStruct((N_IDX_HEADS, n), F32),
        jax.ShapeDtypeStruct((n, A_WIDTH), F32),
    )
    out_specs = (row(A_WIDTH), row(ATT_WIDTH), row(ATT_WIDTH), row(ATT_WIDTH), row(ATT_WIDTH),
                 pl.BlockSpec((1, ATT_WIDTH, t), lambda i: (i, 0, 0)),
                 row(ATT_WIDTH), row(IDX_DIM), row(LANES),
                 pl.BlockSpec((N_IDX_HEADS, t), lambda i: (0, i)),
                 row(A_WIDTH))
    return pl.pallas_call(
        _project_kernel,
        out_shape=out_shape,
        grid=(nt,),
        in_specs=[row(d), mod_spec, mod_spec, const2(w_in_p), const2(lng), const2(lnb), const3(wm), const3(bm)],
        out_specs=out_specs,
        compiler_params=_cparams("arbitrary"),
        name="project",
    )(x, sc, sh, w_in_p, lng, lnb, wm, bm)


def _key_to_f32(key):
    bits = jnp.where(key >= 0, key, key ^ jnp.int32(0x7FFFFFFF))
    return lax.bitcast_convert_type(bits, F32)


def _count(scores_ref, nkt, cand, strict):
    _, tk, w = scores_ref.shape

    def body(kt, acc):
        x = scores_ref[kt]
        hit = (x > cand) if strict else (x >= cand)
        m = jnp.where(hit, 1.0, 0.0)
        return acc + jnp.sum(m.reshape(tk // SUBLANES, SUBLANES, w), axis=0)

    acc = lax.fori_loop(0, nkt, body, jnp.zeros((SUBLANES, w), F32))
    return jnp.sum(acc, axis=0, keepdims=True)


def _select_threshold(scores_ref, nkt, topk):
    w = scores_ref.shape[2]

    def bit_body(i, tkey):
        cand_key = tkey + lax.shift_left(jnp.int32(1), jnp.int32(31) - i)
        cand = _key_to_f32(jnp.maximum(cand_key, jnp.int32(KEY_NEG_INF)))
        cnt = _count(scores_ref, nkt, cand, strict=False)
        return jnp.where(cnt >= float(topk), cand_key, tkey)

    tkey = lax.fori_loop(0, 32, bit_body, jnp.full((1, w), INT32_MIN, I32))
    return jnp.maximum(_key_to_f32(tkey), -F32_MAX)


def _break_ties(scores_ref, nkt, thr, topk):
    _, tk, w = scores_ref.shape
    cnt_ge = _count(scores_ref, nkt, thr, strict=False)

    @pl.when(jnp.max(cnt_ge) > float(topk))
    def _():
        need = float(topk) - _count(scores_ref, nkt, thr, strict=True)
        r = lax.broadcasted_iota(I32, (tk, tk), 0)
        c = lax.broadcasted_iota(I32, (tk, tk), 1)
        before = jnp.where(c < r, 1.0, 0.0).astype(BF16)

        def body(kt, seen):
            x = scores_ref[kt]
            eq = x == thr
            eqf = jnp.where(eq, 1.0, 0.0)
            prior = jnp.dot(before, eqf.astype(BF16), preferred_element_type=F32) + seen
            scores_ref[kt] = jnp.where(eq & (prior >= need), -jnp.inf, x)
            return seen + jnp.sum(eqf.reshape(tk // SUBLANES, SUBLANES, w), axis=0).sum(axis=0, keepdims=True)

        lax.fori_loop(0, nkt, body, jnp.zeros((1, w), F32))


def _half_mask(x_pair, head):
    lane = lax.broadcasted_iota(I32, x_pair.shape, 1)
    keep = (lane >= HEAD_DIM) if head % 2 else (lane < HEAD_DIM)
    return jnp.where(keep, x_pair, jnp.zeros_like(x_pair))


def _attend_prompt_kernel(q_ref, iq_ref, iwt_ref, k_ref, vt_ref, ik_ref, o_ref, scores_ref, *head_refs, topk):
    qh_refs, acc_refs, lt_refs = (head_refs[i * N_HEADS:(i + 1) * N_HEADS] for i in range(3))
    tq = q_ref.shape[0]
    tk = scores_ref.shape[1]
    j = pl.program_id(1)
    nkt = j + 1
    q0 = j * tq
    row = lax.broadcasted_iota(I32, (tk, tq), 0)
    lane = lax.broadcasted_iota(I32, (tk, tq), 1)
    qpos = q0 + lane

    iq = iq_ref[...]
    iqm = [_half_mask(iq[:, (h // 2) * LANES:(h // 2 + 1) * LANES], h) for h in range(N_IDX_HEADS)]
    iw = iwt_ref[...]

    def score_body(kt, _):
        k0 = pl.multiple_of(kt * tk, tk)
        ikt = ik_ref[pl.ds(k0, tk), :]
        s = jnp.zeros((tk, tq), F32)
        for h in range(N_IDX_HEADS):
            r = lax.dot_general(ikt, iqm[h], _NT, preferred_element_type=F32)
            s = s + jnp.maximum(r, 0.0) * iw[h:h + 1, :]
        kpos = k0 + row
        adm = lax.shift_right_logical(kpos, CHUNK_SHIFT) <= lax.shift_right_logical(qpos, CHUNK_SHIFT)
        scores_ref[kt] = jnp.where(adm, s, -jnp.inf)
        return 0

    lax.fori_loop(0, nkt, score_body, 0)

    thr = _select_threshold(scores_ref, nkt, topk)
    _break_ties(scores_ref, nkt, thr, topk)

    def dist_body(kt, _):
        kpos = kt * tk + row
        dist = jnp.abs(qpos - kpos).astype(F32)
        scores_ref[kt] = jnp.where(scores_ref[kt] >= thr, dist, MASKED_DIST)
        return 0

    lax.fori_loop(0, nkt, dist_body, 0)

    qfull = q_ref[...]
    for h in range(N_HEADS):
        pair = h // 2
        qh_refs[h][...] = _half_mask(qfull[:, pair * LANES:(pair + 1) * LANES], h) * jnp.asarray(ATTN_SCALE, BF16)
        acc_refs[h][...] = jnp.zeros_like(acc_refs[h])

    def att_body(kt, carry):
        m_all, l_all = carry
        k0 = pl.multiple_of(kt * tk, tk)
        dist = scores_ref[kt]
        ms = []
        for h in range(N_HEADS):
            pair = h // 2
            slope = 2.0 ** (-8.0 * (h + 1) / N_HEADS)
            kp = k_ref[pl.ds(k0, tk), pair * LANES:(pair + 1) * LANES]
            lt = lax.dot_general(kp, qh_refs[h][...], _NT, preferred_element_type=F32) - slope * dist
            lt_refs[h][...] = lt
            ms.append(jnp.maximum(m_all[h:h + 1, :], jnp.max(lt, axis=0, keepdims=True)))
        ls = []
        for h in range(N_HEADS):
            rows = slice(h * HEAD_DIM, (h + 1) * HEAD_DIM)
            alpha = jnp.exp(m_all[h:h + 1, :] - ms[h])
            p = jnp.exp(lt_refs[h][...] - ms[h])
            ls.append(alpha * l_all[h:h + 1, :] + jnp.sum(p, axis=0, keepdims=True))
            pv = jnp.dot(vt_ref[kt, rows, :], p.astype(BF16), preferred_element_type=F32)
            acc_refs[h][...] = alpha * acc_refs[h][...] + pv
        return jnp.concatenate(ms, axis=0), jnp.concatenate(ls, axis=0)

    init = (jnp.full((N_HEADS, tq), -jnp.inf, F32), jnp.zeros((N_HEADS, tq), F32))
    _, l_all = lax.fori_loop(0, nkt, att_body, init)
    out_t = jnp.concatenate([acc_refs[h][...] / l_all[h:h + 1, :] for h in range(N_HEADS)], axis=0)
    o_ref[...] = out_t.T.astype(BF16)


def _attend_prompt(q, iq, iwt, kb, vt, ikb, n_seq, seq_len, topk):
    n = q.shape[0]
    t = ATT_TILE
    nq = seq_len // t
    once = pl.Buffered(1)
    return pl.pallas_call(
        functools.partial(_attend_prompt_kernel, topk=topk),
        out_shape=jax.ShapeDtypeStruct((n, ATT_WIDTH), BF16),
        grid=(n_seq, nq),
        in_specs=[pl.BlockSpec((t, ATT_WIDTH), lambda b, j: (b * nq + j, 0)),
                  pl.BlockSpec((t, ATT_WIDTH), lambda b, j: (b * nq + j, 0)),
                  pl.BlockSpec((N_IDX_HEADS, t), lambda b, j: (0, b * nq + j)),
                  pl.BlockSpec((seq_len, ATT_WIDTH), lambda b, j: (b, 0), pipeline_mode=once),
                  pl.BlockSpec((nq, ATT_WIDTH, t), lambda b, j: (b, 0, 0), pipeline_mode=once),
                  pl.BlockSpec((seq_len, LANES), lambda b, j: (b, 0), pipeline_mode=once)],
        out_specs=pl.BlockSpec((t, ATT_WIDTH), lambda b, j: (b * nq + j, 0)),
        scratch_shapes=([pltpu.VMEM((nq, t, t), F32)] + [pltpu.VMEM((t, LANES), BF16)] * N_HEADS
                        + [pltpu.VMEM((HEAD_DIM, t), F32)] * N_HEADS + [pltpu.VMEM((t, t), F32)] * N_HEADS),
        compiler_params=_cparams("arbitrary", "arbitrary"),
        name="attend_prompt",
    )(q, iq, iwt, kb, vt, ikb)


def _attend_sample_kernel(q_ref, iq_ref, iwl_ref, kn_ref, vn_ref, ikn_ref, ck_ref, cv_ref, ci_ref,
                          o_ref, scores_ref, acc_ref, *, topk, past):
    tq = q_ref.shape[0]
    nkt, tk, w = scores_ref.shape
    nct = nkt - 1
    row = lax.broadcasted_iota(I32, (tk, w), 0)
    lane = lax.broadcasted_iota(I32, (tk, w), 1)
    qpos = past + (lane & (tq - 1))
    lane_head = lax.shift_right_logical(lax.broadcasted_iota(I32, (1, w), 1), tq.bit_length() - 1)
    slope = lax.bitcast_convert_type(lax.shift_left(126 - lane_head, 23), F32)

    iq = iq_ref[...]
    q = q_ref[...]
    iq_rows = jnp.concatenate([iq[:, h * IDX_DIM:(h + 1) * IDX_DIM] for h in range(N_IDX_HEADS)], axis=0)
    q_rows = jnp.concatenate(
        [_half_mask_wide(q, h) for h in range(N_HEADS)], axis=0) * jnp.asarray(ATTN_SCALE, BF16)
    iwl = iwl_ref[...]
    pad = tk - tq
    kn = jnp.concatenate([kn_ref[...], jnp.zeros((pad, ATT_WIDTH), BF16)], axis=0)
    vn = jnp.concatenate([vn_ref[...].astype(BF16), jnp.zeros((pad, ATT_WIDTH), BF16)], axis=0)
    ikn = jnp.concatenate([ikn_ref[:, :IDX_DIM], jnp.zeros((pad, IDX_DIM), BF16)], axis=0)

    def score_tile(ik_tile, kpos, valid):
        r = lax.dot_general(ik_tile, iq_rows, _NT, preferred_element_type=F32)
        s = jnp.maximum(r, 0.0) * iwl
        s = s + pltpu.roll(s, w // 2, 1)
        s = s + pltpu.roll(s, w // 4, 1)
        s = s + pltpu.roll(s, w // 8, 1)
        adm = lax.shift_right_logical(kpos, CHUNK_SHIFT) <= lax.shift_right_logical(qpos, CHUNK_SHIFT)
        return jnp.where(adm & valid, s, -jnp.inf)

    def score_body(kt, _):
        k0 = pl.multiple_of(kt * tk, tk)
        scores_ref[kt] = score_tile(ci_ref[pl.ds(k0, tk), :].astype(BF16), k0 + row, True)
        return 0

    lax.fori_loop(0, nct, score_body, 0)
    scores_ref[nct] = score_tile(ikn, past + row, row < tq)

    thr = _select_threshold(scores_ref, nkt, topk)
    _break_ties(scores_ref, nkt, thr, topk)

    def logit_tile(kt, k_tile, kpos):
        dist = jnp.where(scores_ref[kt] >= thr, jnp.abs(qpos - kpos).astype(F32), MASKED_DIST)
        lt = lax.dot_general(k_tile, q_rows, _NT, preferred_element_type=F32) - slope * dist
        scores_ref[kt] = lt
        return jnp.max(lt, axis=0, keepdims=True)

    def logit_body(kt, m):
        k0 = pl.multiple_of(kt * tk, tk)
        return jnp.maximum(m, logit_tile(kt, ck_ref[pl.ds(k0, tk), :].astype(BF16), k0 + row))

    m = lax.fori_loop(0, nct, logit_body, jnp.full((1, w), -jnp.inf, F32))
    m = jnp.maximum(m, logit_tile(nct, kn, past + row))

    acc_ref[...] = jnp.zeros_like(acc_ref)

    def pv_tile(kt, v_tile):
        p = jnp.exp(scores_ref[kt] - m)
        acc_ref[...] += jnp.dot(p.T.astype(BF16), v_tile, preferred_element_type=F32)
        return jnp.sum(p, axis=0, keepdims=True)

    def pv_body(kt, l):
        k0 = pl.multiple_of(kt * tk, tk)
        return l + pv_tile(kt, cv_ref[pl.ds(k0, tk), :].astype(BF16))

    l = lax.fori_loop(0, nct, pv_body, jnp.zeros((1, w), F32))
    l = l + pv_tile(nct, vn)

    l_col = jnp.broadcast_to(l, (w, w)).T
    out_lane_head = lax.shift_right_logical(lax.broadcasted_iota(I32, (tq, ATT_WIDTH), 1), 6)
    y = jnp.zeros((tq, ATT_WIDTH), F32)
    for h in range(N_HEADS):
        blk = acc_ref[h * tq:(h + 1) * tq, :] / jnp.concatenate([l_col[h * tq:(h + 1) * tq, :]] * (ATT_WIDTH // w), axis=1)
        y = y + jnp.where(out_lane_head == h, blk, 0.0)
    o_ref[...] = y.astype(BF16)


def _half_mask_wide(x, head):
    lane = lax.broadcasted_iota(I32, x.shape, 1)
    keep = lax.shift_right_logical(lane, 6) == head
    return jnp.where(keep, x, jnp.zeros_like(x))


def _attend_sample(q, iq, iwl, kb, vf, ikb, cache_k, cache_v, cache_i, topk):
    n_seq, past, _ = cache_k.shape
    tq = q.shape[0] // n_seq
    tk = ATT_TILE
    w = N_HEADS * tq
    assert w == LANES and past % tk == 0
    new = lambda width: pl.BlockSpec((tq, width), lambda b: (b, 0))
    cache = lambda width: pl.BlockSpec((None, past, width), lambda b: (b, 0, 0))
    return pl.pallas_call(
        functools.partial(_attend_sample_kernel, topk=topk, past=past),
        out_shape=jax.ShapeDtypeStruct((n_seq * tq, ATT_WIDTH), BF16),
        grid=(n_seq,),
        in_specs=[new(ATT_WIDTH), new(ATT_WIDTH), pl.BlockSpec((None, 1, w), lambda b: (b, 0, 0)),
                  new(ATT_WIDTH), new(ATT_WIDTH), new(LANES),
                  cache(ATT_WIDTH), cache(ATT_WIDTH), cache(IDX_DIM)],
        out_specs=new(ATT_WIDTH),
        scratch_shapes=[pltpu.VMEM((past // tk + 1, tk, w), F32), pltpu.VMEM((w, ATT_WIDTH), F32)],
        compiler_params=_cparams("arbitrary"),
        name="attend_sample",
    )(q, iq, iwl, kb, vf, ikb, cache_k, cache_v, cache_i)


TOK_ROWS = D_MODEL // LANES


def _store_token_tiles(ref, x):
    t = x.shape[0]
    for c in range(TOK_ROWS):
        ref[pl.ds(c, t, stride=TOK_ROWS), :] = x[:, c * LANES:(c + 1) * LANES]


def _load_token_tiles(ref):
    t = ref.shape[0] // TOK_ROWS
    return jnp.concatenate([ref[pl.ds(c, t, stride=TOK_ROWS), :] for c in range(TOK_ROWS)], axis=1)


def _layer_norm(x, g, b):
    mu = jnp.mean(x, axis=-1, keepdims=True)
    xc = x - mu
    var = jnp.mean(xc * xc, axis=-1, keepdims=True)
    return xc * lax.rsqrt(var + LN_EPS) * g + b


def _post_kernel(x_ref, ya_ref, yb_ref, ga_ref, scf_ref, shf_ref, wo_ref, bo_ref, g1_ref, b1_ref,
                 wrh_ref, wrl_ref, br_ref,
                 x1_ref, h2_ref, tope_ref, gate_ref, rank_ref, cnt_ref):
    t = x_ref.shape[0]
    y = (jnp.dot(ya_ref[...], wo_ref[:A_WIDTH, :], preferred_element_type=F32)
         + jnp.dot(yb_ref[...], wo_ref[A_WIDTH:, :], preferred_element_type=F32) + bo_ref[...])
    x1 = _layer_norm(DEEPNORM_ALPHA * x_ref[...] + (1.0 + ga_ref[...]) * y, g1_ref[...], b1_ref[...])
    x1_ref[...] = x1
    h2 = x1 * (1.0 + scf_ref[...]) + shf_ref[...]
    _store_token_tiles(h2_ref, h2)

    hh = h2.astype(BF16)
    hl = (h2 - hh.astype(F32)).astype(BF16)
    logits = (lax.dot_general(wrh_ref[...], hh, _NT, preferred_element_type=F32)
              + lax.dot_general(wrh_ref[...], hl, _NT, preferred_element_type=F32)
              + lax.dot_general(wrl_ref[...], hh, _NT, preferred_element_type=F32) + br_ref[...])
    erow = lax.broadcasted_iota(I32, (N_EXPERTS, t), 0)
    vals, idxs = [], []
    for _ in range(TOP_K):
        v = jnp.max(logits, axis=0, keepdims=True)
        i = jnp.min(jnp.where(logits == v, erow, N_EXPERTS), axis=0, keepdims=True)
        vals.append(v)
        idxs.append(i)
        logits = jnp.where(erow == i, -jnp.inf, logits)
    ex = [jnp.exp(v - vals[0]) for v in vals]
    den = ex[0] + ex[1] + ex[2] + ex[3]
    gate_ref[...] = jnp.concatenate([e / den for e in ex], axis=0)
    tope_ref[...] = jnp.concatenate(idxs, axis=0)

    @pl.when(pl.program_id(0) == 0)
    def _():
        cnt_ref[...] = jnp.zeros_like(cnt_ref)

    hit = jnp.zeros((N_EXPERTS, t), F32)
    for i in idxs:
        hit = hit + jnp.where(erow == i, 1.0, 0.0)
    hitb = hit.astype(BF16)
    r = lax.broadcasted_iota(I32, (t, t), 0)
    c = lax.broadcasted_iota(I32, (t, t), 1)
    earlier = jnp.where(r < c, 1.0, 0.0).astype(BF16)
    before = jnp.dot(hitb, earlier, preferred_element_type=F32) + cnt_ref[...]
    total = jnp.dot(hitb, jnp.ones((t, t), BF16), preferred_element_type=F32)
    rank_ref[...] = jnp.concatenate(
        [jnp.sum(jnp.where(erow == i, before, 0.0), axis=0, keepdims=True) for i in idxs], axis=0).astype(I32)
    cnt_ref[...] += total


def _post(x, ya, yb, ga, scf, shf, w_o, b_o, g1, b1, wrh, wrl, br):
    n, d = x.shape
    t = ROW_TILE
    nt = n // t
    if ga.shape[0] == n:
        mod_spec = pl.BlockSpec((t, d), lambda i: (i, 0))
    else:
        tiles_per_seq = nt // ga.shape[0]
        ga, scf, shf = (a.reshape(a.shape[0], 1, d) for a in (ga, scf, shf))
        mod_spec = pl.BlockSpec((None, 1, d), lambda i: (i // tiles_per_seq, 0, 0))
    row = lambda w: pl.BlockSpec((t, w), lambda i: (i, 0))
    col = lambda r: pl.BlockSpec((r, t), lambda i: (0, i))
    const = lambda a: pl.BlockSpec(a.shape, lambda i: (0, 0))
    return pl.pallas_call(
        _post_kernel,
        out_shape=(jax.ShapeDtypeStruct((n, d), F32), jax.ShapeDtypeStruct((n * TOK_ROWS, LANES), F32),
                   jax.ShapeDtypeStruct((TOP_K, n), I32), jax.ShapeDtypeStruct((TOP_K, n), F32),
                   jax.ShapeDtypeStruct((TOP_K, n), I32), jax.ShapeDtypeStruct((N_EXPERTS, t), F32)),
        grid=(nt,),
        in_specs=[row(d), row(A_WIDTH), row(ATT_WIDTH), mod_spec, mod_spec, mod_spec,
                  const(w_o), const(b_o), const(g1), const(b1), const(wrh), const(wrl), const(br)],
        out_specs=(row(d), pl.BlockSpec((t * TOK_ROWS, LANES), lambda i: (i, 0)), col(TOP_K), col(TOP_K), col(TOP_K),
                   pl.BlockSpec((N_EXPERTS, t), lambda i: (0, 0))),
        compiler_params=_cparams("arbitrary"),
        name="post",
    )(x, ya, yb, ga, scf, shf, w_o, b_o, g1, b1, wrh, wrl, br)


def _token_rows(ref, r):
    return ref.at[pl.ds(pl.multiple_of(r * TOK_ROWS, TOK_ROWS), TOK_ROWS)]


def _token_copies_wait(hbm_ref, vmem_ref, sem, n_tokens):
    rows = n_tokens * TOK_ROWS
    pltpu.make_async_copy(hbm_ref.at[pl.ds(0, rows)], vmem_ref.at[pl.ds(0, rows)], sem).wait()


def _dispatch_kernel(zrow_ref, dest_ref, h_ref, xp_ref, zero_ref, sem):
    t = h_ref.shape[0] // TOK_ROWS

    block_rows = MOE_BLOCK * TOK_ROWS

    def zero_block(slot0):
        z0 = pl.multiple_of(slot0 * TOK_ROWS, block_rows)
        cp = pltpu.make_async_copy(zero_ref, xp_ref.at[pl.ds(z0, block_rows)], sem)
        cp.start()
        cp.wait()

    @pl.when(pl.program_id(0) == 0)
    def _():
        zero_ref[...] = jnp.zeros_like(zero_ref)
        for e in range(N_EXPERTS):
            @pl.when(zrow_ref[e] >= 0)
            def _():
                zero_block(zrow_ref[e])

        def unused(b, _):
            zero_block(b * MOE_BLOCK)
            return 0

        lax.fori_loop(zrow_ref[N_EXPERTS], xp_ref.shape[0] // block_rows, unused, 0)

    def body(i, _):
        for k in range(TOP_K):
            pltpu.make_async_copy(_token_rows(h_ref, i), _token_rows(xp_ref, dest_ref[k, i]), sem).start()
        return 0

    lax.fori_loop(0, t, body, 0)
    for k in range(TOP_K):
        _token_copies_wait(xp_ref, h_ref, sem, t)


def _dispatch(zrow, dest, h2, n_slots):
    t = ROW_TILE
    n = h2.shape[0] // TOK_ROWS
    grid_spec = pltpu.PrefetchScalarGridSpec(
        num_scalar_prefetch=1,
        grid=(n // t,),
        in_specs=[pl.BlockSpec((TOP_K, t), lambda i, z: (0, i), memory_space=pltpu.SMEM),
                  pl.BlockSpec((t * TOK_ROWS, LANES), lambda i, z: (i, 0))],
        out_specs=pl.BlockSpec(memory_space=pl.ANY),
        scratch_shapes=[pltpu.VMEM((MOE_BLOCK * TOK_ROWS, LANES), F32), pltpu.SemaphoreType.DMA],
    )
    return pl.pallas_call(
        _dispatch_kernel,
        out_shape=jax.ShapeDtypeStruct((n_slots * TOK_ROWS, LANES), F32),
        grid_spec=grid_spec,
        compiler_params=_cparams("arbitrary"),
        name="dispatch",
    )(zrow, dest, h2)


def _experts_kernel(be_ref, bi_ref, nu_ref, x_ref, wg_ref, bg_ref, wu_ref, bu_ref, wd_ref, bd_ref, y_ref):
    @pl.when(pl.program_id(0) < nu_ref[0])
    def _():
        x = _load_token_tiles(x_ref).astype(BF16)
        g = jnp.minimum(jnp.dot(x, wg_ref[...], preferred_element_type=F32) + bg_ref[...], SWIGLU_LIMIT)
        u = jnp.clip(jnp.dot(x, wu_ref[...], preferred_element_type=F32) + bu_ref[...], -SWIGLU_LIMIT, SWIGLU_LIMIT)
        a = g * jax.nn.sigmoid(SWIGLU_ALPHA * g)
        mid = ((u + 1.0) * a).astype(BF16)
        _store_token_tiles(y_ref, jnp.dot(mid, wd_ref[...], preferred_element_type=F32) + bd_ref[...])

    @pl.when(pl.program_id(0) >= nu_ref[0])
    def _():
        y_ref[...] = jnp.zeros_like(y_ref)


def _experts(block_e, block_i, n_used, xp, wg, bg, wu, bu, wd, bd):
    d, f = wg.shape[1], wg.shape[2]
    nb = xp.shape[0] // (MOE_BLOCK * TOK_ROWS)
    wspec = lambda a, b: pl.BlockSpec((None, a, b), lambda i, be, bi, nu: (be[i], 0, 0))
    slots = pl.BlockSpec((MOE_BLOCK * TOK_ROWS, LANES), lambda i, be, bi, nu: (bi[i], 0))
    grid_spec = pltpu.PrefetchScalarGridSpec(
        num_scalar_prefetch=3,
        grid=(nb,),
        in_specs=[slots, wspec(d, f), wspec(1, f), wspec(d, f), wspec(1, f), wspec(f, d), wspec(1, d)],
        out_specs=pl.BlockSpec((MOE_BLOCK * TOK_ROWS, LANES), lambda i, be, bi, nu: (i, 0)),
    )
    return pl.pallas_call(
        _experts_kernel,
        out_shape=jax.ShapeDtypeStruct(xp.shape, F32),
        grid_spec=grid_spec,
        compiler_params=_cparams("arbitrary"),
        name="experts",
    )(block_e, block_i, n_used, xp, wg, bg, wu, bu, wd, bd)


def _combine_kernel(dest_ref, gate_ref, x1_ref, gf_ref, g2_ref, b2_ref, yp_ref, o_ref, buf_ref, sem):
    t = x1_ref.shape[0]

    def body(i, _):
        for k in range(TOP_K):
            pltpu.make_async_copy(_token_rows(yp_ref, dest_ref[k, i]), _token_rows(buf_ref.at[k], i), sem).start()
        return 0

    lax.fori_loop(0, t, body, 0)
    for k in range(TOP_K):
        _token_copies_wait(yp_ref, buf_ref.at[k], sem, t)

    gates = jnp.concatenate([gate_ref[...], jnp.zeros((LANES - TOP_K, t), F32)], axis=0).T
    f = gates[:, 0:1] * _load_token_tiles(buf_ref.at[0])
    for k in range(1, TOP_K):
        f = f + gates[:, k:k + 1] * _load_token_tiles(buf_ref.at[k])
    o_ref[...] = _layer_norm(DEEPNORM_ALPHA * x1_ref[...] + (1.0 + gf_ref[...]) * f, g2_ref[...], b2_ref[...])


def _combine(dest, gates, x1, gf, g2, b2, yp):
    n, d = x1.shape
    t = ROW_TILE
    nt = n // t
    if gf.shape[0] == n:
        mod_spec = pl.BlockSpec((t, d), lambda i: (i, 0))
    else:
        tiles_per_seq = nt // gf.shape[0]
        gf = gf.reshape(gf.shape[0], 1, d)
        mod_spec = pl.BlockSpec((None, 1, d), lambda i: (i // tiles_per_seq, 0, 0))
    return pl.pallas_call(
        _combine_kernel,
        out_shape=jax.ShapeDtypeStruct((n, d), F32),
        grid=(nt,),
        in_specs=[pl.BlockSpec((TOP_K, t), lambda i: (0, i), memory_space=pltpu.SMEM),
                  pl.BlockSpec((TOP_K, t), lambda i: (0, i)),
                  pl.BlockSpec((t, d), lambda i: (i, 0)), mod_spec,
                  pl.BlockSpec((1, d), lambda i: (0, 0)), pl.BlockSpec((1, d), lambda i: (0, 0)),
                  pl.BlockSpec(memory_space=pl.ANY)],
        out_specs=pl.BlockSpec((t, d), lambda i: (i, 0)),
        scratch_shapes=[pltpu.VMEM((TOP_K, t * TOK_ROWS, LANES), F32), pltpu.SemaphoreType.DMA],
        compiler_params=_cparams("arbitrary"),
        name="combine",
    )(dest, gates, x1, gf, g2, b2, yp)


def _moe(h2, tope, rank, counts, gates, x1, gf, g2, b2, experts_w):
    n = tope.shape[1]
    nb = (n * TOP_K + N_EXPERTS * (MOE_BLOCK - 1) + MOE_BLOCK - 1) // MOE_BLOCK
    pcounts = (counts + MOE_BLOCK - 1) // MOE_BLOCK * MOE_BLOCK
    pend = jnp.cumsum(pcounts)
    pstart = pend - pcounts
    eids = jnp.arange(N_EXPERTS, dtype=I32).reshape(N_EXPERTS, 1, 1)
    dest = rank + jnp.sum(jnp.where(tope[None] == eids, pstart.reshape(N_EXPERTS, 1, 1), 0), axis=0).astype(I32)
    n_used = (pend[-1] // MOE_BLOCK).astype(I32)
    zrow = jnp.concatenate([jnp.where(counts > 0, pend - MOE_BLOCK, -1), n_used.reshape(1)]).astype(I32)
    blk = jnp.arange(nb, dtype=I32)
    block_i = jnp.minimum(blk, n_used - 1)
    block_e = jnp.minimum(jnp.sum(pend[None, :] <= (block_i * MOE_BLOCK)[:, None], axis=1), N_EXPERTS - 1).astype(I32)
    xp = _dispatch(zrow, dest, h2, nb * MOE_BLOCK)
    yp = _experts(block_e, block_i, n_used.reshape(1), xp, *experts_w)
    return _combine(dest, gates, x1, gf, g2, b2, yp)


def kernel(x_prompt, x_sample, c_prompt, c_sample, cache_k, cache_v, cache_kidx, w_in, a_ln_g, a_ln_b, a_ws, a_bs,
           w_o, b_o, w_c, b_c, ln1_g, ln1_b, ln2_g, ln2_b, w_router, b_router, w_gate, b_gate, w_up, b_up,
           w_down, b_down):
    bp, s, d = x_prompt.shape
    bs, ts, _ = x_sample.shape
    past = cache_k.shape[2]
    np_, ns = bp * s, bs * ts

    wi = w_in[0]
    w_in_p = jnp.zeros((d, IN_COLS_PAD), F32)
    w_in_p = w_in_p.at[:, :OFF_IK].set(wi[:, :OFF_IK])
    w_in_p = w_in_p.at[:, OFF_IK:OFF_IK + IDX_DIM].set(wi[:, OFF_IK:OFF_IK + IDX_DIM])
    w_in_p = w_in_p.at[:, OFF_IK + IDX_DIM:OFF_IK + 2 * IDX_DIM].set(wi[:, OFF_IK:OFF_IK + IDX_DIM])
    w_in_p = w_in_p.at[:, OFF_IW:OFF_IW + N_IDX_HEADS].set(wi[:, OFF_IK + IDX_DIM:OFF_IK + IDX_DIM + N_IDX_HEADS])
    w_in_p = w_in_p.astype(BF16)
    lng, lnb = a_ln_g[0].reshape(1, A_WIDTH), a_ln_b[0].reshape(1, A_WIDTH)
    wtril = jnp.tril(a_ws[0])
    wm_p = wtril.astype(BF16)
    bm_p = jnp.broadcast_to(a_bs[0][:, :, None], (A_GROUPS, A_CHUNK, A_GROUP_DIM)).astype(F32)
    rep = A_CHUNK // ts
    wm_s = jnp.einsum("ab,gij->gaibj", jnp.eye(rep, dtype=F32), wtril[:, :ts, :ts]).reshape(
        A_GROUPS, A_CHUNK, A_CHUNK).astype(BF16)
    bm_s = jnp.broadcast_to(jnp.tile(a_bs[0][:, :ts], (1, rep))[:, :, None], (A_GROUPS, A_CHUNK, A_GROUP_DIM)).astype(F32)
    w_o_b = w_o[0].astype(BF16)
    b_o_r = b_o[0].reshape(1, d)
    wr_t = w_router[0].T
    wrh = wr_t.astype(BF16)
    wrl = (wr_t - wrh.astype(F32)).astype(BF16)
    br = jnp.broadcast_to(b_router[0][:, None], (N_EXPERTS, ROW_TILE)).astype(F32)
    experts_w = (w_gate[0].astype(BF16), b_gate[0][:, None, :], w_up[0].astype(BF16), b_up[0][:, None, :],
                 w_down[0].astype(BF16), b_down[0][:, None, :])
    g1, b1 = ln1_g[0].reshape(1, d), ln1_b[0].reshape(1, d)
    g2, b2 = ln2_g[0].reshape(1, d), ln2_b[0].reshape(1, d)

    mods = _cond_mods(jnp.concatenate([c_prompt, c_sample], axis=0), w_c[0], b_c[0]).reshape(bp + bs, 6, d)
    mods_p = [mods[:bp, i] for i in range(6)]
    mods_s = [jnp.repeat(mods[bp:, i], ts, axis=0) for i in range(6)]

    xp2 = x_prompt.reshape(np_, d)
    ya, q, kf, kb, vf, vt, iq, ikf, ikb, iwt, _ = _project(xp2, mods_p[1], mods_p[0], w_in_p, lng, lnb, wm_p, bm_p)
    yb = _attend_prompt(q, iq, iwt, kb, vt, ikb, bp, s, min(TOPK_MAX, s // 4))
    x1, h2, tope, gates, rank, cnt = _post(xp2, ya, yb, mods_p[2], mods_p[4], mods_p[3], w_o_b, b_o_r, g1, b1, wrh, wrl, br)
    y_p = _moe(h2, tope, rank, cnt[:, 0].astype(I32), gates, x1, mods_p[5], g2, b2, experts_w)
    out_p = (y_p.reshape(bp, s, d), kf.reshape(1, bp, s, N_HEADS, HEAD_DIM), vf.reshape(1, bp, s, N_HEADS, HEAD_DIM),
             ikf.reshape(1, bp, s, IDX_DIM))

    xs2 = x_sample.reshape(ns, d)
    ya, q, kf, kb, vf, vt, iq, ikf, ikb, iwt, va = _project(xs2, mods_s[1], mods_s[0], w_in_p, lng, lnb, wm_s, bm_s)
    iwl = jnp.transpose(iwt.reshape(N_IDX_HEADS, bs, ts), (1, 0, 2)).reshape(bs, 1, N_IDX_HEADS * ts)
    yb = _attend_sample(q, iq, iwl, kb, vf, ikb, cache_k[0].reshape(bs, past, ATT_WIDTH),
                        cache_v[0].reshape(bs, past, ATT_WIDTH), cache_kidx[0], min(TOPK_MAX, (past + ts) // 4))
    x1, h2, tope, gates, rank, cnt = _post(xs2, ya, yb, mods_s[2], mods_s[4], mods_s[3], w_o_b, b_o_r, g1, b1, wrh, wrl, br)
    y_s = _moe(h2, tope, rank, cnt[:, 0].astype(I32), gates, x1, mods_s[5], g2, b2, experts_w)

    return (out_p[0], y_s.reshape(bs, ts, d), out_p[1], out_p[2], out_p[3],
            kf.reshape(1, bs, ts, N_HEADS, HEAD_DIM), vf.reshape(1, bs, ts, N_HEADS, HEAD_DIM),
            ikf.reshape(1, bs, ts, IDX_DIM), va.reshape(1, bs, ts, A_WIDTH))
```

```python
import functools

import jax
import jax.numpy as jnp
from jax import lax
from jax.experimental import pallas as pl
from jax.experimental.pallas import tpu as pltpu

F32 = jnp.float32
BF16 = jnp.bfloat16
I32 = jnp.int32

D_MODEL = 1024
CHUNK_SHIFT = 6
A_GROUPS = 4
A_GROUP_DIM = 128
A_WIDTH = A_GROUPS * A_GROUP_DIM
A_CHUNK = 128
N_HEADS = 8
HEAD_DIM = 64
ATT_WIDTH = N_HEADS * HEAD_DIM
N_IDX_HEADS = 8
IDX_DIM = 64
TOPK_MAX = 256
ATTN_SCALE = HEAD_DIM ** -0.5
VT_HEAD_ROWS = HEAD_DIM + 16
VT_ROWS = N_HEADS * VT_HEAD_ROWS
IDX_W_SCALE = (N_IDX_HEADS ** -0.5) * (IDX_DIM ** -0.5)
N_EXPERTS = 32
TOP_K = 4
SWIGLU_LIMIT = 7.0
SWIGLU_ALPHA = 1.702
DEEPNORM_ALPHA = 2.0 ** 0.25
LN_EPS = 1e-5

LANES = 128
SUBLANES = 8
VMEM_LIMIT_BYTES = 56 * 1024 * 1024

ROW_TILE = 256
MOE_BLOCK = 256
ATT_TILE = 256

OFF_AU, OFF_AV, OFF_Q, OFF_K, OFF_V, OFF_IQ = 0, 512, 1024, 1536, 2048, 2560
OFF_IK = 3072
OFF_IW = 3200
IN_COLS_PAD = 3328

MASKED_DIST = 3.0e32
F32_MAX = 3.4028234663852886e38
INT32_MIN = -(2 ** 31)
KEY_NEG_INF = INT32_MIN + 0x7FFFFF

_NT = (((1,), (1,)), ((), ()))


def _cparams(*sem):
    return pltpu.CompilerParams(dimension_semantics=sem, vmem_limit_bytes=VMEM_LIMIT_BYTES)


def _mods_kernel(c_ref, w_ref, b_ref, o_ref):
    c = c_ref[...]
    s = c * jax.nn.sigmoid(c)
    o_ref[...] = jnp.dot(s.astype(BF16), w_ref[...].astype(BF16), preferred_element_type=F32) + b_ref[...]


def _cond_mods(c, w_c, b_c):
    nb, d = c.shape
    n_out = w_c.shape[1]
    return pl.pallas_call(
        _mods_kernel,
        out_shape=jax.ShapeDtypeStruct((nb, n_out), F32),
        grid=(n_out // d,),
        in_specs=[pl.BlockSpec((nb, d), lambda j: (0, 0)),
                  pl.BlockSpec((d, d), lambda j: (0, j)),
                  pl.BlockSpec((1, d), lambda j: (0, j))],
        out_specs=pl.BlockSpec((nb, d), lambda j: (0, j)),
        compiler_params=_cparams("arbitrary"),
        name="mods",
    )(c, w_c, b_c.reshape(1, n_out))


def _gelu(x):
    return 0.5 * x * (1.0 + lax.erf(x * 0.7071067811865476))


def _project_kernel(x_ref, sc_ref, sh_ref, w_ref, lng_ref, lnb_ref, wm_ref, bm_ref,
                    ya_ref, q_ref, kf_ref, kb_ref, vf_ref, vt_ref, iq_ref, ikf_ref, ikb_ref, iwt_ref, va_ref):
    t = x_ref.shape[0]
    h = (x_ref[...] * (1.0 + sc_ref[...]) + sh_ref[...]).astype(BF16)

    def proj(c0, n):
        return jnp.dot(h, w_ref[:, c0:c0 + n], preferred_element_type=F32)

    u = _gelu(proj(OFF_AU, A_WIDTH))
    gv = _gelu(proj(OFF_AV, A_WIDTH))
    for g in range(A_GROUPS):
        lo, hi = g * A_GROUP_DIM, (g + 1) * A_GROUP_DIM
        xg = gv[:, lo:hi]
        mu = jnp.mean(xg, axis=-1, keepdims=True)
        xc = xg - mu
        var = jnp.mean(xc * xc, axis=-1, keepdims=True)
        vg = xc * lax.rsqrt(var + LN_EPS) * lng_ref[:, lo:hi] + lnb_ref[:, lo:hi]
        va_ref[:, lo:hi] = vg
        vgb = vg.astype(BF16)
        for c in range(t // A_CHUNK):
            r0, r1 = c * A_CHUNK, (c + 1) * A_CHUNK
            mixed = jnp.dot(wm_ref[g], vgb[r0:r1, :], preferred_element_type=F32) + bm_ref[g]
            ya_ref[r0:r1, lo:hi] = (u[r0:r1, lo:hi] * mixed).astype(BF16)

    q_ref[...] = proj(OFF_Q, ATT_WIDTH).astype(BF16)
    k = proj(OFF_K, ATT_WIDTH)
    kf_ref[...] = k
    kb_ref[...] = k.astype(BF16)
    v = proj(OFF_V, ATT_WIDTH)
    vf_ref[...] = v
    v_t = v.T.astype(BF16)
    ones = jnp.ones((VT_HEAD_ROWS - HEAD_DIM, t), BF16)
    vt_ref[0] = jnp.concatenate(
        [blk for h in range(N_HEADS) for blk in (v_t[h * HEAD_DIM:(h + 1) * HEAD_DIM, :], ones)], axis=0)
    iq_ref[...] = proj(OFF_IQ, N_IDX_HEADS * IDX_DIM).astype(BF16)
    ik2 = proj(OFF_IK, LANES)
    ikf_ref[...] = ik2[:, :IDX_DIM]
    ikb_ref[...] = ik2.astype(BF16)
    iw = proj(OFF_IW, LANES) * IDX_W_SCALE
    iwt_ref[...] = iw.T[:N_IDX_HEADS, :]


def _project(x, sc, sh, w_in_p, lng, lnb, wm, bm):
    n, d = x.shape
    t = ROW_TILE
    nt = n // t
    if sc.shape[0] == n:
        mod_spec = pl.BlockSpec((t, d), lambda i: (i, 0))
    else:
        tiles_per_seq = nt // sc.shape[0]
        sc = sc.reshape(sc.shape[0], 1, d)
        sh = sh.reshape(sh.shape[0], 1, d)
        mod_spec = pl.BlockSpec((None, 1, d), lambda i: (i // tiles_per_seq, 0, 0))
    row = lambda w: pl.BlockSpec((t, w), lambda i: (i, 0))
    const2 = lambda a: pl.BlockSpec(a.shape, lambda i: (0, 0))
    const3 = lambda a: pl.BlockSpec(a.shape, lambda i: (0, 0, 0))
    out_shape = (
        jax.ShapeDtypeStruct((n, A_WIDTH), BF16),
        jax.ShapeDtypeStruct((n, ATT_WIDTH), BF16),
        jax.ShapeDtypeStruct((n, ATT_WIDTH), F32),
        jax.ShapeDtypeStruct((n, ATT_WIDTH), BF16),
        jax.ShapeDtypeStruct((n, ATT_WIDTH), F32),
        jax.ShapeDtypeStruct((nt, VT_ROWS, t), BF16),
        jax.ShapeDtypeStruct((n, ATT_WIDTH), BF16),
        jax.ShapeDtypeStruct((n, IDX_DIM), F32),
        jax.ShapeDtypeStruct((n, LANES), BF16),
        jax.ShapeDtypeStruct((N_IDX_HEADS, n), F32),
        jax.ShapeDtypeStruct((n, A_WIDTH), F32),
    )
    out_specs = (row(A_WIDTH), row(ATT_WIDTH), row(ATT_WIDTH), row(ATT_WIDTH), row(ATT_WIDTH),
                 pl.BlockSpec((1, VT_ROWS, t), lambda i: (i, 0, 0)),
                 row(ATT_WIDTH), row(IDX_DIM), row(LANES),
                 pl.BlockSpec((N_IDX_HEADS, t), lambda i: (0, i)),
                 row(A_WIDTH))
    return pl.pallas_call(
        _project_kernel,
        out_shape=out_shape,
        grid=(nt,),
        in_specs=[row(d), mod_spec, mod_spec, const2(w_in_p), const2(lng), const2(lnb), const3(wm), const3(bm)],
        out_specs=out_specs,
        compiler_params=_cparams("arbitrary"),
        name="project",
    )(x, sc, sh, w_in_p, lng, lnb, wm, bm)


DIGIT_BITS = 8
N_DIGITS = 32 // DIGIT_BITS
DIGIT_ABOVE = 512.0
DIGIT_BELOW = -1.0
PACKED_ROWS = 16


def _mono_key(x):
    b = lax.bitcast_convert_type(x, I32)
    return jnp.where(b >= 0, b, b ^ jnp.int32(0x7FFFFFFF))


def _digit_plane(key, phase):
    shift = 32 - DIGIT_BITS * (phase + 1)
    d = lax.shift_right_arithmetic(key, jnp.int32(shift)) if shift else key
    d = d + (1 << (DIGIT_BITS - 1)) if phase == 0 else d & ((1 << DIGIT_BITS) - 1)
    return d.astype(F32).astype(BF16)


def _count_plane(plane_ref, nkt, cand, strict):
    _, tk, w = plane_ref.shape
    cb = cand.astype(BF16)
    one, zero = jnp.ones((), BF16), jnp.zeros((), BF16)

    def body(kt, cnt):
        e = plane_ref[kt]
        accs = [jnp.zeros((PACKED_ROWS, w), BF16) for _ in range(4)]
        for r in range(tk // PACKED_ROWS):
            blk = e[r * PACKED_ROWS:(r + 1) * PACKED_ROWS, :]
            accs[r % 4] = accs[r % 4] + jnp.where((blk > cb) if strict else (blk >= cb), one, zero)
        return cnt + ((accs[0] + accs[1]) + (accs[2] + accs[3])).astype(F32)

    cnt = lax.fori_loop(0, nkt, body, jnp.zeros((PACKED_ROWS, w), F32))
    return jnp.sum(cnt, axis=0, keepdims=True)


def _search_digit(plane_ref, nkt, topk):
    w = plane_ref.shape[2]

    def bit_body(i, d):
        cand = d + lax.shift_left(jnp.int32(1), jnp.int32(DIGIT_BITS - 1) - i).astype(F32)
        cnt = _count_plane(plane_ref, nkt, cand, strict=False)
        return jnp.where(cnt >= float(topk), cand, d)

    return lax.fori_loop(0, DIGIT_BITS, bit_body, jnp.zeros((1, w), F32))


def _topk_select(keys_ref, plane_ref, nkt, topk):
    _, tk, w = plane_ref.shape
    d = _search_digit(plane_ref, nkt, topk)
    for phase in range(1, N_DIGITS):
        db = d.astype(BF16)

        def refine(kt, _, phase=phase, db=db):
            e = plane_ref[kt]
            decided = jnp.where(e > db, jnp.asarray(DIGIT_ABOVE, BF16), jnp.asarray(DIGIT_BELOW, BF16))
            plane_ref[kt] = jnp.where(e == db, _digit_plane(keys_ref[kt], phase), decided)
            return 0

        lax.fori_loop(0, nkt, refine, 0)
        d = _search_digit(plane_ref, nkt, topk)

    cnt_ge = _count_plane(plane_ref, nkt, d, strict=False)

    @pl.when(jnp.max(cnt_ge) > float(topk))
    def _():
        need = float(topk) - _count_plane(plane_ref, nkt, d, strict=True)
        r = lax.broadcasted_iota(I32, (tk, tk), 0)
        c = lax.broadcasted_iota(I32, (tk, tk), 1)
        before = jnp.where(c < r, 1.0, 0.0).astype(BF16)

        def body(kt, seen):
            e = plane_ref[kt].astype(F32)
            eq = e == d
            eqf = jnp.where(eq, 1.0, 0.0)
            prior = jnp.dot(before, eqf.astype(BF16), preferred_element_type=F32) + seen
            plane_ref[kt] = jnp.where(eq & (prior >= need), DIGIT_BELOW, e).astype(BF16)
            return seen + jnp.sum(eqf.reshape(tk // SUBLANES, SUBLANES, w), axis=0).sum(axis=0, keepdims=True)

        lax.fori_loop(0, nkt, body, jnp.zeros((1, w), F32))

    return d


def _selected(keys_ref, plane_ref, kt, d):
    return (plane_ref[kt].astype(F32) >= d) & (keys_ref[kt] > jnp.int32(KEY_NEG_INF))


def _half_mask(x_pair, head):
    lane = lax.broadcasted_iota(I32, x_pair.shape, 1)
    keep = (lane >= HEAD_DIM) if head % 2 else (lane < HEAD_DIM)
    return jnp.where(keep, x_pair, jnp.zeros_like(x_pair))


def _attend_prompt_kernel(q_ref, iq_ref, iwt_ref, k_ref, vt_ref, ik_ref, o_ref, keys_ref, plane_ref, *head_refs, topk):
    qh_refs, acc_refs, lt_refs = (head_refs[i * N_HEADS:(i + 1) * N_HEADS] for i in range(3))
    tq = q_ref.shape[0]
    tk = keys_ref.shape[1]
    j = pl.program_id(1)
    nkt = j + 1
    q0 = j * tq
    row = lax.broadcasted_iota(I32, (tk, tq), 0)
    lane = lax.broadcasted_iota(I32, (tk, tq), 1)
    qpos = q0 + lane

    iq = iq_ref[...]
    iqm = [_half_mask(iq[:, (h // 2) * LANES:(h // 2 + 1) * LANES], h) for h in range(N_IDX_HEADS)]
    iw = iwt_ref[...]

    def score_body(kt, _):
        k0 = pl.multiple_of(kt * tk, tk)
        ikt = ik_ref[pl.ds(k0, tk), :]
        s = jnp.zeros((tk, tq), F32)
        for h in range(N_IDX_HEADS):
            r = lax.dot_general(ikt, iqm[h], _NT, preferred_element_type=F32)
            s = s + jnp.maximum(r, 0.0) * iw[h:h + 1, :]
        kpos = k0 + row
        adm = lax.shift_right_logical(kpos, CHUNK_SHIFT) <= lax.shift_right_logical(qpos, CHUNK_SHIFT)
        key = _mono_key(jnp.where(adm, s, -jnp.inf))
        keys_ref[kt] = key
        plane_ref[kt] = _digit_plane(key, 0)
        return 0

    lax.fori_loop(0, nkt, score_body, 0)

    d_last = _topk_select(keys_ref, plane_ref, nkt, topk)

    def dist_body(kt, _):
        kpos = kt * tk + row
        dist = jnp.abs(qpos - kpos).astype(F32)
        masked = jnp.where(_selected(keys_ref, plane_ref, kt, d_last), dist, MASKED_DIST)
        keys_ref[kt] = lax.bitcast_convert_type(masked, I32)
        return 0

    lax.fori_loop(0, nkt, dist_body, 0)

    qfull = q_ref[...]
    for h in range(N_HEADS):
        pair = h // 2
        qh_refs[h][...] = _half_mask(qfull[:, pair * LANES:(pair + 1) * LANES], h) * jnp.asarray(ATTN_SCALE, BF16)
        acc_refs[h][...] = jnp.zeros_like(acc_refs[h])

    def logits(kt, slot, m_all):
        k0 = pl.multiple_of(kt * tk, tk)
        dist = lax.bitcast_convert_type(keys_ref[kt], F32)
        ms = []
        for h in range(N_HEADS):
            pair = h // 2
            slope = 2.0 ** (-8.0 * (h + 1) / N_HEADS)
            kp = k_ref[pl.ds(k0, tk), pair * LANES:(pair + 1) * LANES]
            lt = lax.dot_general(kp, qh_refs[h][...], _NT, preferred_element_type=F32) - slope * dist
            lt_refs[h][slot] = lt.astype(BF16)
            tile_max = jnp.max(lt, axis=0, keepdims=True).astype(BF16).astype(F32)
            ms.append(jnp.maximum(m_all[h:h + 1, :], tile_max))
        return jnp.concatenate(ms, axis=0)

    def att_body(kt, carry):
        m_prev, m_cur, l_all = carry
        slot = kt % 2
        m_next = logits(jnp.minimum(kt + 1, nkt - 1), 1 - slot, m_cur)
        ls = []
        for h in range(N_HEADS):
            alpha = jnp.exp(m_prev[h:h + 1, :] - m_cur[h:h + 1, :])
            p = jnp.exp(lt_refs[h][slot] - m_cur[h:h + 1, :].astype(BF16))
            pv = jnp.dot(vt_ref[kt, h * VT_HEAD_ROWS:(h + 1) * VT_HEAD_ROWS, :], p, preferred_element_type=F32)
            ls.append(alpha * l_all[h:h + 1, :] + pv[HEAD_DIM:HEAD_DIM + 1, :])
            acc_refs[h][...] = alpha * acc_refs[h][...] + pv[:HEAD_DIM, :]
        return m_cur, m_next, jnp.concatenate(ls, axis=0)

    m_init = jnp.full((N_HEADS, tq), -jnp.inf, F32)
    _, _, l_all = lax.fori_loop(0, nkt, att_body, (m_init, logits(0, 0, m_init), jnp.zeros((N_HEADS, tq), F32)))
    out_t = jnp.concatenate([acc_refs[h][...] / l_all[h:h + 1, :] for h in range(N_HEADS)], axis=0)
    o_ref[...] = out_t.T.astype(BF16)


def _attend_prompt(q, iq, iwt, kb, vt, ikb, n_seq, seq_len, topk):
    n = q.shape[0]
    t = ATT_TILE
    nq = seq_len // t
    once = pl.Buffered(1)
    return pl.pallas_call(
        functools.partial(_attend_prompt_kernel, topk=topk),
        out_shape=jax.ShapeDtypeStruct((n, ATT_WIDTH), BF16),
        grid=(n_seq, nq),
        in_specs=[pl.BlockSpec((t, ATT_WIDTH), lambda b, j: (b * nq + j, 0)),
                  pl.BlockSpec((t, ATT_WIDTH), lambda b, j: (b * nq + j, 0)),
                  pl.BlockSpec((N_IDX_HEADS, t), lambda b, j: (0, b * nq + j)),
                  pl.BlockSpec((seq_len, ATT_WIDTH), lambda b, j: (b, 0), pipeline_mode=once),
                  pl.BlockSpec((nq, VT_ROWS, t), lambda b, j: (b, 0, 0), pipeline_mode=once),
                  pl.BlockSpec((seq_len, LANES), lambda b, j: (b, 0), pipeline_mode=once)],
        out_specs=pl.BlockSpec((t, ATT_WIDTH), lambda b, j: (b * nq + j, 0)),
        scratch_shapes=([pltpu.VMEM((nq, t, t), I32), pltpu.VMEM((nq, t, t), BF16)]
                        + [pltpu.VMEM((t, LANES), BF16)] * N_HEADS
                        + [pltpu.VMEM((HEAD_DIM, t), F32)] * N_HEADS + [pltpu.VMEM((2, t, t), BF16)] * N_HEADS),
        compiler_params=_cparams("arbitrary", "arbitrary"),
        name="attend_prompt",
    )(q, iq, iwt, kb, vt, ikb)


def _attend_sample_kernel(q_ref, iq_ref, iwl_ref, kn_ref, vn_ref, ikn_ref, ck_ref, cv_ref, ci_ref,
                          o_ref, keys_ref, plane_ref, acc_ref, *, topk, past):
    tq = q_ref.shape[0]
    nkt, tk, w = keys_ref.shape
    nct = nkt - 1
    row = lax.broadcasted_iota(I32, (tk, w), 0)
    lane = lax.broadcasted_iota(I32, (tk, w), 1)
    qpos = past + (lane & (tq - 1))
    lane_head = lax.shift_right_logical(lax.broadcasted_iota(I32, (1, w), 1), tq.bit_length() - 1)
    slope = lax.bitcast_convert_type(lax.shift_left(126 - lane_head, 23), F32)

    iq = iq_ref[...]
    q = q_ref[...]
    iq_rows = jnp.concatenate([iq[:, h * IDX_DIM:(h + 1) * IDX_DIM] for h in range(N_IDX_HEADS)], axis=0)
    q_rows = jnp.concatenate(
        [_half_mask_wide(q, h) for h in range(N_HEADS)], axis=0) * jnp.asarray(ATTN_SCALE, BF16)
    iwl = iwl_ref[...]
    pad = tk - tq
    kn = jnp.concatenate([kn_ref[...], jnp.zeros((pad, ATT_WIDTH), BF16)], axis=0)
    vn = jnp.concatenate([vn_ref[...].astype(BF16), jnp.zeros((pad, ATT_WIDTH), BF16)], axis=0)
    ikn = jnp.concatenate([ikn_ref[:, :IDX_DIM], jnp.zeros((pad, IDX_DIM), BF16)], axis=0)

    def score_tile(ik_tile, kpos, valid):
        r = lax.dot_general(ik_tile, iq_rows, _NT, preferred_element_type=F32)
        s = jnp.maximum(r, 0.0) * iwl
        s = s + pltpu.roll(s, w // 2, 1)
        s = s + pltpu.roll(s, w // 4, 1)
        s = s + pltpu.roll(s, w // 8, 1)
        adm = lax.shift_right_logical(kpos, CHUNK_SHIFT) <= lax.shift_right_logical(qpos, CHUNK_SHIFT)
        return jnp.where(adm & valid, s, -jnp.inf)

    def put_scores(kt, s):
        key = _mono_key(s)
        keys_ref[kt] = key
        plane_ref[kt] = _digit_plane(key, 0)

    def score_body(kt, _):
        k0 = pl.multiple_of(kt * tk, tk)
        put_scores(kt, score_tile(ci_ref[pl.ds(k0, tk), :].astype(BF16), k0 + row, True))
        return 0

    lax.fori_loop(0, nct, score_body, 0)
    put_scores(nct, score_tile(ikn, past + row, row < tq))

    d_last = _topk_select(keys_ref, plane_ref, nkt, topk)

    def logit_tile(kt, k_tile, kpos):
        dist = jnp.where(_selected(keys_ref, plane_ref, kt, d_last), jnp.abs(qpos - kpos).astype(F32), MASKED_DIST)
        lt = lax.dot_general(k_tile, q_rows, _NT, preferred_element_type=F32) - slope * dist
        keys_ref[kt] = lax.bitcast_convert_type(lt, I32)
        return jnp.max(lt, axis=0, keepdims=True)

    def logit_body(kt, m):
        k0 = pl.multiple_of(kt * tk, tk)
        return jnp.maximum(m, logit_tile(kt, ck_ref[pl.ds(k0, tk), :].astype(BF16), k0 + row))

    m = lax.fori_loop(0, nct, logit_body, jnp.full((1, w), -jnp.inf, F32))
    m = jnp.maximum(m, logit_tile(nct, kn, past + row))

    acc_ref[...] = jnp.zeros_like(acc_ref)

    def pv_tile(kt, v_tile):
        p = jnp.exp(lax.bitcast_convert_type(keys_ref[kt], F32) - m)
        acc_ref[...] += jnp.dot(p.T.astype(BF16), v_tile, preferred_element_type=F32)
        return jnp.sum(p, axis=0, keepdims=True)

    def pv_body(kt, l):
        k0 = pl.multiple_of(kt * tk, tk)
        return l + pv_tile(kt, cv_ref[pl.ds(k0, tk), :].astype(BF16))

    l = lax.fori_loop(0, nct, pv_body, jnp.zeros((1, w), F32))
    l = l + pv_tile(nct, vn)

    l_col = jnp.broadcast_to(l, (w, w)).T
    out_lane_head = lax.shift_right_logical(lax.broadcasted_iota(I32, (tq, ATT_WIDTH), 1), 6)
    y = jnp.zeros((tq, ATT_WIDTH), F32)
    for h in range(N_HEADS):
        blk = acc_ref[h * tq:(h + 1) * tq, :] / jnp.concatenate([l_col[h * tq:(h + 1) * tq, :]] * (ATT_WIDTH // w), axis=1)
        y = y + jnp.where(out_lane_head == h, blk, 0.0)
    o_ref[...] = y.astype(BF16)


def _half_mask_wide(x, head):
    lane = lax.broadcasted_iota(I32, x.shape, 1)
    keep = lax.shift_right_logical(lane, 6) == head
    return jnp.where(keep, x, jnp.zeros_like(x))


def _attend_sample(q, iq, iwl, kb, vf, ikb, cache_k, cache_v, cache_i, topk):
    n_seq, past, _ = cache_k.shape
    tq = q.shape[0] // n_seq
    tk = ATT_TILE
    w = N_HEADS * tq
    assert w == LANES and past % tk == 0
    new = lambda width: pl.BlockSpec((tq, width), lambda b: (b, 0))
    cache = lambda width: pl.BlockSpec((None, past, width), lambda b: (b, 0, 0))
    return pl.pallas_call(
        functools.partial(_attend_sample_kernel, topk=topk, past=past),
        out_shape=jax.ShapeDtypeStruct((n_seq * tq, ATT_WIDTH), BF16),
        grid=(n_seq,),
        in_specs=[new(ATT_WIDTH), new(ATT_WIDTH), pl.BlockSpec((None, 1, w), lambda b: (b, 0, 0)),
                  new(ATT_WIDTH), new(ATT_WIDTH), new(LANES),
                  cache(ATT_WIDTH), cache(ATT_WIDTH), cache(IDX_DIM)],
        out_specs=new(ATT_WIDTH),
        scratch_shapes=[pltpu.VMEM((past // tk + 1, tk, w), I32), pltpu.VMEM((past // tk + 1, tk, w), BF16),
                        pltpu.VMEM((w, ATT_WIDTH), F32)],
        compiler_params=_cparams("arbitrary"),
        name="attend_sample",
    )(q, iq, iwl, kb, vf, ikb, cache_k, cache_v, cache_i)


TOK_ROWS = D_MODEL // LANES


def _store_token_tiles(ref, x):
    t = x.shape[0]
    for c in range(TOK_ROWS):
        ref[pl.ds(c, t, stride=TOK_ROWS), :] = x[:, c * LANES:(c + 1) * LANES]


def _load_token_tiles(ref):
    t = ref.shape[0] // TOK_ROWS
    return jnp.concatenate([ref[pl.ds(c, t, stride=TOK_ROWS), :] for c in range(TOK_ROWS)], axis=1)


def _layer_norm(x, g, b):
    mu = jnp.mean(x, axis=-1, keepdims=True)
    xc = x - mu
    var = jnp.mean(xc * xc, axis=-1, keepdims=True)
    return xc * lax.rsqrt(var + LN_EPS) * g + b


def _post_kernel(x_ref, ya_ref, yb_ref, ga_ref, scf_ref, shf_ref, wo_ref, bo_ref, g1_ref, b1_ref,
                 wrh_ref, wrl_ref, br_ref,
                 x1_ref, h2_ref, tope_ref, gate_ref, rank_ref, cnt_ref):
    t = x_ref.shape[0]
    y = (jnp.dot(ya_ref[...], wo_ref[:A_WIDTH, :], preferred_element_type=F32)
         + jnp.dot(yb_ref[...], wo_ref[A_WIDTH:, :], preferred_element_type=F32) + bo_ref[...])
    x1 = _layer_norm(DEEPNORM_ALPHA * x_ref[...] + (1.0 + ga_ref[...]) * y, g1_ref[...], b1_ref[...])
    x1_ref[...] = x1
    h2 = x1 * (1.0 + scf_ref[...]) + shf_ref[...]
    _store_token_tiles(h2_ref, h2)

    hh = h2.astype(BF16)
    hl = (h2 - hh.astype(F32)).astype(BF16)
    logits = (lax.dot_general(wrh_ref[...], hh, _NT, preferred_element_type=F32)
              + lax.dot_general(wrh_ref[...], hl, _NT, preferred_element_type=F32)
              + lax.dot_general(wrl_ref[...], hh, _NT, preferred_element_type=F32) + br_ref[...])
    erow = lax.broadcasted_iota(I32, (N_EXPERTS, t), 0)
    vals, idxs = [], []
    for _ in range(TOP_K):
        v = jnp.max(logits, axis=0, keepdims=True)
        i = jnp.min(jnp.where(logits == v, erow, N_EXPERTS), axis=0, keepdims=True)
        vals.append(v)
        idxs.append(i)
        logits = jnp.where(erow == i, -jnp.inf, logits)
    ex = [jnp.exp(v - vals[0]) for v in vals]
    den = ex[0] + ex[1] + ex[2] + ex[3]
    gate_ref[...] = jnp.concatenate([e / den for e in ex], axis=0)
    tope_ref[...] = jnp.concatenate(idxs, axis=0)

    @pl.when(pl.program_id(0) == 0)
    def _():
        cnt_ref[...] = jnp.zeros_like(cnt_ref)

    hit = jnp.zeros((N_EXPERTS, t), F32)
    for i in idxs:
        hit = hit + jnp.where(erow == i, 1.0, 0.0)
    hitb = hit.astype(BF16)
    r = lax.broadcasted_iota(I32, (t, t), 0)
    c = lax.broadcasted_iota(I32, (t, t), 1)
    earlier = jnp.where(r < c, 1.0, 0.0).astype(BF16)
    before = jnp.dot(hitb, earlier, preferred_element_type=F32) + cnt_ref[...]
    total = jnp.dot(hitb, jnp.ones((t, t), BF16), preferred_element_type=F32)
    rank_ref[...] = jnp.concatenate(
        [jnp.sum(jnp.where(erow == i, before, 0.0), axis=0, keepdims=True) for i in idxs], axis=0).astype(I32)
    cnt_ref[...] += total


def _post(x, ya, yb, ga, scf, shf, w_o, b_o, g1, b1, wrh, wrl, br):
    n, d = x.shape
    t = ROW_TILE
    nt = n // t
    if ga.shape[0] == n:
        mod_spec = pl.BlockSpec((t, d), lambda i: (i, 0))
    else:
        tiles_per_seq = nt // ga.shape[0]
        ga, scf, shf = (a.reshape(a.shape[0], 1, d) for a in (ga, scf, shf))
        mod_spec = pl.BlockSpec((None, 1, d), lambda i: (i // tiles_per_seq, 0, 0))
    row = lambda w: pl.BlockSpec((t, w), lambda i: (i, 0))
    col = lambda r: pl.BlockSpec((r, t), lambda i: (0, i))
    const = lambda a: pl.BlockSpec(a.shape, lambda i: (0, 0))
    return pl.pallas_call(
        _post_kernel,
        out_shape=(jax.ShapeDtypeStruct((n, d), F32), jax.ShapeDtypeStruct((n * TOK_ROWS, LANES), F32),
                   jax.ShapeDtypeStruct((TOP_K, n), I32), jax.ShapeDtypeStruct((TOP_K, n), F32),
                   jax.ShapeDtypeStruct((TOP_K, n), I32), jax.ShapeDtypeStruct((N_EXPERTS, t), F32)),
        grid=(nt,),
        in_specs=[row(d), row(A_WIDTH), row(ATT_WIDTH), mod_spec, mod_spec, mod_spec,
                  const(w_o), const(b_o), const(g1), const(b1), const(wrh), const(wrl), const(br)],
        out_specs=(row(d), pl.BlockSpec((t * TOK_ROWS, LANES), lambda i: (i, 0)), col(TOP_K), col(TOP_K), col(TOP_K),
                   pl.BlockSpec((N_EXPERTS, t), lambda i: (0, 0))),
        compiler_params=_cparams("arbitrary"),
        name="post",
    )(x, ya, yb, ga, scf, shf, w_o, b_o, g1, b1, wrh, wrl, br)


def _token_rows(ref, r):
    return ref.at[pl.ds(pl.multiple_of(r * TOK_ROWS, TOK_ROWS), TOK_ROWS)]


def _token_copies_wait(hbm_ref, vmem_ref, sem, n_tokens):
    rows = n_tokens * TOK_ROWS
    pltpu.make_async_copy(hbm_ref.at[pl.ds(0, rows)], vmem_ref.at[pl.ds(0, rows)], sem).wait()


def _dispatch_kernel(zrow_ref, dest_ref, h_ref, xp_ref, zero_ref, sem):
    t = h_ref.shape[0] // TOK_ROWS

    block_rows = MOE_BLOCK * TOK_ROWS

    def zero_block(slot0):
        z0 = pl.multiple_of(slot0 * TOK_ROWS, block_rows)
        cp = pltpu.make_async_copy(zero_ref, xp_ref.at[pl.ds(z0, block_rows)], sem)
        cp.start()
        cp.wait()

    @pl.when(pl.program_id(0) == 0)
    def _():
        zero_ref[...] = jnp.zeros_like(zero_ref)
        for e in range(N_EXPERTS):
            @pl.when(zrow_ref[e] >= 0)
            def _():
                zero_block(zrow_ref[e])

        def unused(b, _):
            zero_block(b * MOE_BLOCK)
            return 0

        lax.fori_loop(zrow_ref[N_EXPERTS], xp_ref.shape[0] // block_rows, unused, 0)

    def body(i, _):
        for k in range(TOP_K):
            pltpu.make_async_copy(_token_rows(h_ref, i), _token_rows(xp_ref, dest_ref[k, i]), sem).start()
        return 0

    lax.fori_loop(0, t, body, 0)
    for k in range(TOP_K):
        _token_copies_wait(xp_ref, h_ref, sem, t)


def _dispatch(zrow, dest, h2, n_slots):
    t = ROW_TILE
    n = h2.shape[0] // TOK_ROWS
    grid_spec = pltpu.PrefetchScalarGridSpec(
        num_scalar_prefetch=1,
        grid=(n // t,),
        in_specs=[pl.BlockSpec((TOP_K, t), lambda i, z: (0, i), memory_space=pltpu.SMEM),
                  pl.BlockSpec((t * TOK_ROWS, LANES), lambda i, z: (i, 0))],
        out_specs=pl.BlockSpec(memory_space=pl.ANY),
        scratch_shapes=[pltpu.VMEM((MOE_BLOCK * TOK_ROWS, LANES), F32), pltpu.SemaphoreType.DMA],
    )
    return pl.pallas_call(
        _dispatch_kernel,
        out_shape=jax.ShapeDtypeStruct((n_slots * TOK_ROWS, LANES), F32),
        grid_spec=grid_spec,
        compiler_params=_cparams("arbitrary"),
        name="dispatch",
    )(zrow, dest, h2)


def _experts_kernel(be_ref, bi_ref, nu_ref, x_ref, wg_ref, bg_ref, wu_ref, bu_ref, wd_ref, bd_ref, y_ref):
    @pl.when(pl.program_id(0) < nu_ref[0])
    def _():
        x = _load_token_tiles(x_ref).astype(BF16)
        g = jnp.minimum(jnp.dot(x, wg_ref[...], preferred_element_type=F32) + bg_ref[...], SWIGLU_LIMIT)
        u = jnp.clip(jnp.dot(x, wu_ref[...], preferred_element_type=F32) + bu_ref[...], -SWIGLU_LIMIT, SWIGLU_LIMIT)
        a = g * jax.nn.sigmoid(SWIGLU_ALPHA * g)
        mid = ((u + 1.0) * a).astype(BF16)
        _store_token_tiles(y_ref, jnp.dot(mid, wd_ref[...], preferred_element_type=F32) + bd_ref[...])

    @pl.when(pl.program_id(0) >= nu_ref[0])
    def _():
        y_ref[...] = jnp.zeros_like(y_ref)


def _experts(block_e, block_i, n_used, xp, wg, bg, wu, bu, wd, bd):
    d, f = wg.shape[1], wg.shape[2]
    nb = xp.shape[0] // (MOE_BLOCK * TOK_ROWS)
    wspec = lambda a, b: pl.BlockSpec((None, a, b), lambda i, be, bi, nu: (be[i], 0, 0))
    slots = pl.BlockSpec((MOE_BLOCK * TOK_ROWS, LANES), lambda i, be, bi, nu: (bi[i], 0))
    grid_spec = pltpu.PrefetchScalarGridSpec(
        num_scalar_prefetch=3,
        grid=(nb,),
        in_specs=[slots, wspec(d, f), wspec(1, f), wspec(d, f), wspec(1, f), wspec(f, d), wspec(1, d)],
        out_specs=pl.BlockSpec((MOE_BLOCK * TOK_ROWS, LANES), lambda i, be, bi, nu: (i, 0)),
    )
    return pl.pallas_call(
        _experts_kernel,
        out_shape=jax.ShapeDtypeStruct(xp.shape, F32),
        grid_spec=grid_spec,
        compiler_params=_cparams("arbitrary"),
        name="experts",
    )(block_e, block_i, n_used, xp, wg, bg, wu, bu, wd, bd)


def _combine_kernel(dest_ref, gate_ref, x1_ref, gf_ref, g2_ref, b2_ref, yp_ref, o_ref, buf_ref, sem):
    t = x1_ref.shape[0]

    def body(i, _):
        for k in range(TOP_K):
            pltpu.make_async_copy(_token_rows(yp_ref, dest_ref[k, i]), _token_rows(buf_ref.at[k], i), sem).start()
        return 0

    lax.fori_loop(0, t, body, 0)
    for k in range(TOP_K):
        _token_copies_wait(yp_ref, buf_ref.at[k], sem, t)

    gates = jnp.concatenate([gate_ref[...], jnp.zeros((LANES - TOP_K, t), F32)], axis=0).T
    f = gates[:, 0:1] * _load_token_tiles(buf_ref.at[0])
    for k in range(1, TOP_K):
        f = f + gates[:, k:k + 1] * _load_token_tiles(buf_ref.at[k])
    o_ref[...] = _layer_norm(DEEPNORM_ALPHA * x1_ref[...] + (1.0 + gf_ref[...]) * f, g2_ref[...], b2_ref[...])


def _combine(dest, gates, x1, gf, g2, b2, yp):
    n, d = x1.shape
    t = ROW_TILE
    nt = n // t
    if gf.shape[0] == n:
        mod_spec = pl.BlockSpec((t, d), lambda i: (i, 0))
    else:
        tiles_per_seq = nt // gf.shape[0]
        gf = gf.reshape(gf.shape[0], 1, d)
        mod_spec = pl.BlockSpec((None, 1, d), lambda i: (i // tiles_per_seq, 0, 0))
    return pl.pallas_call(
        _combine_kernel,
        out_shape=jax.ShapeDtypeStruct((n, d), F32),
        grid=(nt,),
        in_specs=[pl.BlockSpec((TOP_K, t), lambda i: (0, i), memory_space=pltpu.SMEM),
                  pl.BlockSpec((TOP_K, t), lambda i: (0, i)),
                  pl.BlockSpec((t, d), lambda i: (i, 0)), mod_spec,
                  pl.BlockSpec((1, d), lambda i: (0, 0)), pl.BlockSpec((1, d), lambda i: (0, 0)),
                  pl.BlockSpec(memory_space=pl.ANY)],
        out_specs=pl.BlockSpec((t, d), lambda i: (i, 0)),
        scratch_shapes=[pltpu.VMEM((TOP_K, t * TOK_ROWS, LANES), F32), pltpu.SemaphoreType.DMA],
        compiler_params=_cparams("arbitrary"),
        name="combine",
    )(dest, gates, x1, gf, g2, b2, yp)


def _moe(h2, tope, rank, counts, gates, x1, gf, g2, b2, experts_w):
    n = tope.shape[1]
    nb = (n * TOP_K + N_EXPERTS * (MOE_BLOCK - 1) + MOE_BLOCK - 1) // MOE_BLOCK
    pcounts = (counts + MOE_BLOCK - 1) // MOE_BLOCK * MOE_BLOCK
    pend = jnp.cumsum(pcounts)
    pstart = pend - pcounts
    eids = jnp.arange(N_EXPERTS, dtype=I32).reshape(N_EXPERTS, 1, 1)
    dest = rank + jnp.sum(jnp.where(tope[None] == eids, pstart.reshape(N_EXPERTS, 1, 1), 0), axis=0).astype(I32)
    n_used = (pend[-1] // MOE_BLOCK).astype(I32)
    zrow = jnp.concatenate([jnp.where(counts > 0, pend - MOE_BLOCK, -1), n_used.reshape(1)]).astype(I32)
    blk = jnp.arange(nb, dtype=I32)
    block_i = jnp.minimum(blk, n_used - 1)
    block_e = jnp.minimum(jnp.sum(pend[None, :] <= (block_i * MOE_BLOCK)[:, None], axis=1), N_EXPERTS - 1).astype(I32)
    xp = _dispatch(zrow, dest, h2, nb * MOE_BLOCK)
    yp = _experts(block_e, block_i, n_used.reshape(1), xp, *experts_w)
    return _combine(dest, gates, x1, gf, g2, b2, yp)


def kernel(x_prompt, x_sample, c_prompt, c_sample, cache_k, cache_v, cache_kidx, w_in, a_ln_g, a_ln_b, a_ws, a_bs,
           w_o, b_o, w_c, b_c, ln1_g, ln1_b, ln2_g, ln2_b, w_router, b_router, w_gate, b_gate, w_up, b_up,
           w_down, b_down):
    bp, s, d = x_prompt.shape
    bs, ts, _ = x_sample.shape
    past = cache_k.shape[2]
    np_, ns = bp * s, bs * ts

    wi = w_in[0]
    w_in_p = jnp.zeros((d, IN_COLS_PAD), F32)
    w_in_p = w_in_p.at[:, :OFF_IK].set(wi[:, :OFF_IK])
    w_in_p = w_in_p.at[:, OFF_IK:OFF_IK + IDX_DIM].set(wi[:, OFF_IK:OFF_IK + IDX_DIM])
    w_in_p = w_in_p.at[:, OFF_IK + IDX_DIM:OFF_IK + 2 * IDX_DIM].set(wi[:, OFF_IK:OFF_IK + IDX_DIM])
    w_in_p = w_in_p.at[:, OFF_IW:OFF_IW + N_IDX_HEADS].set(wi[:, OFF_IK + IDX_DIM:OFF_IK + IDX_DIM + N_IDX_HEADS])
    w_in_p = w_in_p.astype(BF16)
    lng, lnb = a_ln_g[0].reshape(1, A_WIDTH), a_ln_b[0].reshape(1, A_WIDTH)
    wtril = jnp.tril(a_ws[0])
    wm_p = wtril.astype(BF16)
    bm_p = jnp.broadcast_to(a_bs[0][:, :, None], (A_GROUPS, A_CHUNK, A_GROUP_DIM)).astype(F32)
    rep = A_CHUNK // ts
    wm_s = jnp.einsum("ab,gij->gaibj", jnp.eye(rep, dtype=F32), wtril[:, :ts, :ts]).reshape(
        A_GROUPS, A_CHUNK, A_CHUNK).astype(BF16)
    bm_s = jnp.broadcast_to(jnp.tile(a_bs[0][:, :ts], (1, rep))[:, :, None], (A_GROUPS, A_CHUNK, A_GROUP_DIM)).astype(F32)
    w_o_b = w_o[0].astype(BF16)
    b_o_r = b_o[0].reshape(1, d)
    wr_t = w_router[0].T
    wrh = wr_t.astype(BF16)
    wrl = (wr_t - wrh.astype(F32)).astype(BF16)
    br = jnp.broadcast_to(b_router[0][:, None], (N_EXPERTS, ROW_TILE)).astype(F32)
    experts_w = (w_gate[0].astype(BF16), b_gate[0][:, None, :], w_up[0].astype(BF16), b_up[0][:, None, :],
                 w_down[0].astype(BF16), b_down[0][:, None, :])
    g1, b1 = ln1_g[0].reshape(1, d), ln1_b[0].reshape(1, d)
    g2, b2 = ln2_g[0].reshape(1, d), ln2_b[0].reshape(1, d)

    mods = _cond_mods(jnp.concatenate([c_prompt, c_sample], axis=0), w_c[0], b_c[0]).reshape(bp + bs, 6, d)
    mods_p = [mods[:bp, i] for i in range(6)]
    mods_s = [jnp.repeat(mods[bp:, i], ts, axis=0) for i in range(6)]

    xp2 = x_prompt.reshape(np_, d)
    ya, q, kf, kb, vf, vt, iq, ikf, ikb, iwt, _ = _project(xp2, mods_p[1], mods_p[0], w_in_p, lng, lnb, wm_p, bm_p)
    yb = _attend_prompt(q, iq, iwt, kb, vt, ikb, bp, s, min(TOPK_MAX, s // 4))
    x1, h2, tope, gates, rank, cnt = _post(xp2, ya, yb, mods_p[2], mods_p[4], mods_p[3], w_o_b, b_o_r, g1, b1, wrh, wrl, br)
    y_p = _moe(h2, tope, rank, cnt[:, 0].astype(I32), gates, x1, mods_p[5], g2, b2, experts_w)
    out_p = (y_p.reshape(bp, s, d), kf.reshape(1, bp, s, N_HEADS, HEAD_DIM), vf.reshape(1, bp, s, N_HEADS, HEAD_DIM),
             ikf.reshape(1, bp, s, IDX_DIM))

    xs2 = x_sample.reshape(ns, d)
    ya, q, kf, kb, vf, vt, iq, ikf, ikb, iwt, va = _project(xs2, mods_s[1], mods_s[0], w_in_p, lng, lnb, wm_s, bm_s)
    iwl = jnp.transpose(iwt.reshape(N_IDX_HEADS, bs, ts), (1, 0, 2)).reshape(bs, 1, N_IDX_HEADS * ts)
    yb = _attend_sample(q, iq, iwl, kb, vf, ikb, cache_k[0].reshape(bs, past, ATT_WIDTH),
                        cache_v[0].reshape(bs, past, ATT_WIDTH), cache_kidx[0], min(TOPK_MAX, (past + ts) // 4))
    x1, h2, tope, gates, rank, cnt = _post(xs2, ya, yb, mods_s[2], mods_s[4], mods_s[3], w_o_b, b_o_r, g1, b1, wrh, wrl, br)
    y_s = _moe(h2, tope, rank, cnt[:, 0].astype(I32), gates, x1, mods_s[5], g2, b2, experts_w)

    return (out_p[0], y_s.reshape(bs, ts, d), out_p[1], out_p[2], out_p[3],
            kf.reshape(1, bs, ts, N_HEADS, HEAD_DIM), vf.reshape(1, bs, ts, N_HEADS, HEAD_DIM),
            ikf.reshape(1, bs, ts, IDX_DIM), va.reshape(1, bs, ts, A_WIDTH))
```

```python
import functools

import jax
import jax.numpy as jnp
from jax import lax
from jax.experimental import pallas as pl
from jax.experimental.pallas import tpu as pltpu

F32 = jnp.float32
BF16 = jnp.bfloat16
I32 = jnp.int32

D_MODEL = 1024
CHUNK_SHIFT = 6
A_GROUPS = 4
A_GROUP_DIM = 128
A_WIDTH = A_GROUPS * A_GROUP_DIM
A_CHUNK = 128
N_HEADS = 8
HEAD_DIM = 64
ATT_WIDTH = N_HEADS * HEAD_DIM
N_IDX_HEADS = 8
IDX_DIM = 64
TOPK_MAX = 256
ATTN_SCALE = HEAD_DIM ** -0.5
VT_HEAD_ROWS = HEAD_DIM + 16
VT_ROWS = N_HEADS * VT_HEAD_ROWS
IDX_W_SCALE = (N_IDX_HEADS ** -0.5) * (IDX_DIM ** -0.5)
N_EXPERTS = 32
TOP_K = 4
SWIGLU_LIMIT = 7.0
SWIGLU_ALPHA = 1.702
DEEPNORM_ALPHA = 2.0 ** 0.25
LN_EPS = 1e-5

LANES = 128
SUBLANES = 8
VMEM_LIMIT_BYTES = 56 * 1024 * 1024

ROW_TILE = 256
MOE_BLOCK = 256
ATT_TILE = 256
DMA_ISSUE_UNROLL = 8
STATIC_TILE_UNROLL = 4

OFF_AU, OFF_AV, OFF_Q, OFF_K, OFF_V, OFF_IQ = 0, 512, 1024, 1536, 2048, 2560
OFF_IK = 3072
OFF_IW = 3200
IN_COLS_PAD = 3328

MASKED_DIST = 3.0e32
F32_MAX = 3.4028234663852886e38
INT32_MIN = -(2 ** 31)
KEY_NEG_INF = INT32_MIN + 0x7FFFFF

_NT = (((1,), (1,)), ((), ()))


def _cparams(*sem):
    return pltpu.CompilerParams(dimension_semantics=sem, vmem_limit_bytes=VMEM_LIMIT_BYTES)


def _mods_kernel(c_ref, w_ref, b_ref, o_ref):
    c = c_ref[...]
    s = c * jax.nn.sigmoid(c)
    o_ref[...] = jnp.dot(s.astype(BF16), w_ref[...].astype(BF16), preferred_element_type=F32) + b_ref[...]


def _cond_mods(c, w_c, b_c):
    nb, d = c.shape
    n_out = w_c.shape[1]
    return pl.pallas_call(
        _mods_kernel,
        out_shape=jax.ShapeDtypeStruct((nb, n_out), F32),
        grid=(n_out // d,),
        in_specs=[pl.BlockSpec((nb, d), lambda j: (0, 0)),
                  pl.BlockSpec((d, d), lambda j: (0, j)),
                  pl.BlockSpec((1, d), lambda j: (0, j))],
        out_specs=pl.BlockSpec((nb, d), lambda j: (0, j)),
        compiler_params=_cparams("arbitrary"),
        name="mods",
    )(c, w_c, b_c.reshape(1, n_out))


def _gelu(x):
    return 0.5 * x * (1.0 + lax.erf(x * 0.7071067811865476))


def _project_kernel(x_ref, sc_ref, sh_ref, w_ref, lng_ref, lnb_ref, wm_ref, bm_ref,
                    ya_ref, q_ref, kf_ref, kb_ref, vf_ref, vt_ref, iq_ref, ikf_ref, ikb_ref, iwt_ref, va_ref):
    t = x_ref.shape[0]
    h = (x_ref[...] * (1.0 + sc_ref[...]) + sh_ref[...]).astype(BF16)

    def proj(c0, n):
        return jnp.dot(h, w_ref[:, c0:c0 + n], preferred_element_type=F32)

    u = _gelu(proj(OFF_AU, A_WIDTH))
    gv = _gelu(proj(OFF_AV, A_WIDTH))
    for g in range(A_GROUPS):
        lo, hi = g * A_GROUP_DIM, (g + 1) * A_GROUP_DIM
        xg = gv[:, lo:hi]
        mu = jnp.mean(xg, axis=-1, keepdims=True)
        xc = xg - mu
        var = jnp.mean(xc * xc, axis=-1, keepdims=True)
        vg = xc * lax.rsqrt(var + LN_EPS) * lng_ref[:, lo:hi] + lnb_ref[:, lo:hi]
        va_ref[:, lo:hi] = vg
        vgb = vg.astype(BF16)
        for c in range(t // A_CHUNK):
            r0, r1 = c * A_CHUNK, (c + 1) * A_CHUNK
            mixed = jnp.dot(wm_ref[g], vgb[r0:r1, :], preferred_element_type=F32) + bm_ref[g]
            ya_ref[r0:r1, lo:hi] = (u[r0:r1, lo:hi] * mixed).astype(BF16)

    q_ref[...] = proj(OFF_Q, ATT_WIDTH).astype(BF16)
    k = proj(OFF_K, ATT_WIDTH)
    kf_ref[...] = k
    kb_ref[...] = k.astype(BF16)
    v = proj(OFF_V, ATT_WIDTH)
    vf_ref[...] = v
    v_t = v.T.astype(BF16)
    ones = jnp.ones((VT_HEAD_ROWS - HEAD_DIM, t), BF16)
    vt_ref[0] = jnp.concatenate(
        [blk for h in range(N_HEADS) for blk in (v_t[h * HEAD_DIM:(h + 1) * HEAD_DIM, :], ones)], axis=0)
    iq_ref[...] = proj(OFF_IQ, N_IDX_HEADS * IDX_DIM).astype(BF16)
    ik2 = proj(OFF_IK, LANES)
    ikf_ref[...] = ik2[:, :IDX_DIM]
    ikb_ref[...] = ik2.astype(BF16)
    iw = proj(OFF_IW, LANES) * IDX_W_SCALE
    iwt_ref[...] = iw.T[:N_IDX_HEADS, :]


def _project(x, sc, sh, w_in_p, lng, lnb, wm, bm):
    n, d = x.shape
    t = ROW_TILE
    nt = n // t
    if sc.shape[0] == n:
        mod_spec = pl.BlockSpec((t, d), lambda i: (i, 0))
    else:
        tiles_per_seq = nt // sc.shape[0]
        sc = sc.reshape(sc.shape[0], 1, d)
        sh = sh.reshape(sh.shape[0], 1, d)
        mod_spec = pl.BlockSpec((None, 1, d), lambda i: (i // tiles_per_seq, 0, 0))
    row = lambda w: pl.BlockSpec((t, w), lambda i: (i, 0))
    const2 = lambda a: pl.BlockSpec(a.shape, lambda i: (0, 0))
    const3 = lambda a: pl.BlockSpec(a.shape, lambda i: (0, 0, 0))
    out_shape = (
        jax.ShapeDtypeStruct((n, A_WIDTH), BF16),
        jax.ShapeDtypeStruct((n, ATT_WIDTH), BF16),
        jax.ShapeDtypeStruct((n, ATT_WIDTH), F32),
        jax.ShapeDtypeStruct((n, ATT_WIDTH), BF16),
        jax.ShapeDtypeStruct((n, ATT_WIDTH), F32),
        jax.ShapeDtypeStruct((nt, VT_ROWS, t), BF16),
        jax.ShapeDtypeStruct((n, ATT_WIDTH), BF16),
        jax.ShapeDtypeStruct((n, IDX_DIM), F32),
        jax.ShapeDtypeStruct((n, LANES), BF16),
        jax.ShapeDtypeStruct((N_IDX_HEADS, n), F32),
        jax.ShapeDtypeStruct((n, A_WIDTH), F32),
    )
    out_specs = (row(A_WIDTH), row(ATT_WIDTH), row(ATT_WIDTH), row(ATT_WIDTH), row(ATT_WIDTH),
                 pl.BlockSpec((1, VT_ROWS, t), lambda i: (i, 0, 0)),
                 row(ATT_WIDTH), row(IDX_DIM), row(LANES),
                 pl.BlockSpec((N_IDX_HEADS, t), lambda i: (0, i)),
                 row(A_WIDTH))
    return pl.pallas_call(
        _project_kernel,
        out_shape=out_shape,
        grid=(nt,),
        in_specs=[row(d), mod_spec, mod_spec, const2(w_in_p), const2(lng), const2(lnb), const3(wm), const3(bm)],
        out_specs=out_specs,
        compiler_params=_cparams("arbitrary"),
        name="project",
    )(x, sc, sh, w_in_p, lng, lnb, wm, bm)


DIGIT_BITS = 8
N_DIGITS = 32 // DIGIT_BITS
DIGIT_ABOVE = 512.0
DIGIT_BELOW = -1.0
PACKED_ROWS = 16


def _static_unroll(trips):
    return STATIC_TILE_UNROLL if isinstance(trips, int) else 1


def _tile_loop(nkt, body, init):
    if isinstance(nkt, int):
        return lax.fori_loop(0, nkt, body, init, unroll=STATIC_TILE_UNROLL)
    pairs = lax.shift_right_logical(nkt, 1)
    carry = lax.fori_loop(0, pairs, lambda i, c: body(2 * i + 1, body(2 * i, c)), init)
    return lax.fori_loop(2 * pairs, nkt, body, carry)


def _mono_key(x):
    b = lax.bitcast_convert_type(x, I32)
    return jnp.where(b >= 0, b, b ^ jnp.int32(0x7FFFFFFF))


def _digit_plane(key, phase):
    shift = 32 - DIGIT_BITS * (phase + 1)
    d = lax.shift_right_arithmetic(key, jnp.int32(shift)) if shift else key
    d = d + (1 << (DIGIT_BITS - 1)) if phase == 0 else d & ((1 << DIGIT_BITS) - 1)
    return d.astype(F32).astype(BF16)


def _count_plane(plane_ref, nkt, cand, strict):
    _, tk, w = plane_ref.shape
    cb = cand.astype(BF16)
    one, zero = jnp.ones((), BF16), jnp.zeros((), BF16)

    def body(kt, cnt):
        e = plane_ref[kt]
        accs = [jnp.zeros((PACKED_ROWS, w), BF16) for _ in range(4)]
        for r in range(tk // PACKED_ROWS):
            blk = e[r * PACKED_ROWS:(r + 1) * PACKED_ROWS, :]
            accs[r % 4] = accs[r % 4] + jnp.where((blk > cb) if strict else (blk >= cb), one, zero)
        return cnt + ((accs[0] + accs[1]) + (accs[2] + accs[3])).astype(F32)

    cnt = _tile_loop(nkt, body, jnp.zeros((PACKED_ROWS, w), F32))
    return jnp.sum(cnt, axis=0, keepdims=True)


def _search_digit(plane_ref, nkt, topk):
    w = plane_ref.shape[2]

    def bit_body(i, d):
        cand = d + lax.shift_left(jnp.int32(1), jnp.int32(DIGIT_BITS - 1) - i).astype(F32)
        cnt = _count_plane(plane_ref, nkt, cand, strict=False)
        return jnp.where(cnt >= float(topk), cand, d)

    return lax.fori_loop(0, DIGIT_BITS, bit_body, jnp.zeros((1, w), F32))


def _topk_select(keys_ref, plane_ref, nkt, topk):
    _, tk, w = plane_ref.shape
    d = _search_digit(plane_ref, nkt, topk)
    for phase in range(1, N_DIGITS):
        db = d.astype(BF16)

        def refine(kt, _, phase=phase, db=db):
            e = plane_ref[kt]
            decided = jnp.where(e > db, jnp.asarray(DIGIT_ABOVE, BF16), jnp.asarray(DIGIT_BELOW, BF16))
            plane_ref[kt] = jnp.where(e == db, _digit_plane(keys_ref[kt], phase), decided)
            return 0

        _tile_loop(nkt, refine, 0)
        d = _search_digit(plane_ref, nkt, topk)

    cnt_ge = _count_plane(plane_ref, nkt, d, strict=False)

    @pl.when(jnp.max(cnt_ge) > float(topk))
    def _():
        need = float(topk) - _count_plane(plane_ref, nkt, d, strict=True)
        r = lax.broadcasted_iota(I32, (tk, tk), 0)
        c = lax.broadcasted_iota(I32, (tk, tk), 1)
        before = jnp.where(c < r, 1.0, 0.0).astype(BF16)

        def body(kt, seen):
            e = plane_ref[kt].astype(F32)
            eq = e == d
            eqf = jnp.where(eq, 1.0, 0.0)
            prior = jnp.dot(before, eqf.astype(BF16), preferred_element_type=F32) + seen
            plane_ref[kt] = jnp.where(eq & (prior >= need), DIGIT_BELOW, e).astype(BF16)
            return seen + jnp.sum(eqf.reshape(tk // SUBLANES, SUBLANES, w), axis=0).sum(axis=0, keepdims=True)

        lax.fori_loop(0, nkt, body, jnp.zeros((1, w), F32))

    return d


def _selected(keys_ref, plane_ref, kt, d):
    return (plane_ref[kt].astype(F32) >= d) & (keys_ref[kt] > jnp.int32(KEY_NEG_INF))


def _half_mask(x_pair, head):
    lane = lax.broadcasted_iota(I32, x_pair.shape, 1)
    keep = (lane >= HEAD_DIM) if head % 2 else (lane < HEAD_DIM)
    return jnp.where(keep, x_pair, jnp.zeros_like(x_pair))


def _attend_prompt_kernel(q_ref, iq_ref, iwt_ref, k_ref, vt_ref, ik_ref, o_ref, keys_ref, plane_ref, *head_refs, topk):
    qh_refs, acc_refs, lt_refs = (head_refs[i * N_HEADS:(i + 1) * N_HEADS] for i in range(3))
    tq = q_ref.shape[0]
    tk = keys_ref.shape[1]
    j = pl.program_id(1)
    nkt = j + 1
    q0 = j * tq
    row = lax.broadcasted_iota(I32, (tk, tq), 0)
    lane = lax.broadcasted_iota(I32, (tk, tq), 1)
    qpos = q0 + lane

    iq = iq_ref[...]
    iqm = [_half_mask(iq[:, (h // 2) * LANES:(h // 2 + 1) * LANES], h) for h in range(N_IDX_HEADS)]
    iw = iwt_ref[...]

    def score_body(kt, _):
        k0 = pl.multiple_of(kt * tk, tk)
        ikt = ik_ref[pl.ds(k0, tk), :]
        s = jnp.zeros((tk, tq), F32)
        for h in range(N_IDX_HEADS):
            r = lax.dot_general(ikt, iqm[h], _NT, preferred_element_type=F32)
            s = s + jnp.maximum(r, 0.0) * iw[h:h + 1, :]
        kpos = k0 + row
        adm = lax.shift_right_logical(kpos, CHUNK_SHIFT) <= lax.shift_right_logical(qpos, CHUNK_SHIFT)
        key = _mono_key(jnp.where(adm, s, -jnp.inf))
        keys_ref[kt] = key
        plane_ref[kt] = _digit_plane(key, 0)
        return 0

    lax.fori_loop(0, nkt, score_body, 0)

    d_last = _topk_select(keys_ref, plane_ref, nkt, topk)

    def dist_body(kt, _):
        kpos = kt * tk + row
        dist = jnp.abs(qpos - kpos).astype(F32)
        masked = jnp.where(_selected(keys_ref, plane_ref, kt, d_last), dist, MASKED_DIST)
        keys_ref[kt] = lax.bitcast_convert_type(masked, I32)
        return 0

    lax.fori_loop(0, nkt, dist_body, 0)

    qfull = q_ref[...]
    for h in range(N_HEADS):
        pair = h // 2
        qh_refs[h][...] = _half_mask(qfull[:, pair * LANES:(pair + 1) * LANES], h) * jnp.asarray(ATTN_SCALE, BF16)
        acc_refs[h][...] = jnp.zeros_like(acc_refs[h])

    def logits(kt, slot, m_all):
        k0 = pl.multiple_of(kt * tk, tk)
        dist = lax.bitcast_convert_type(keys_ref[kt], F32)
        ms = []
        for h in range(N_HEADS):
            pair = h // 2
            slope = 2.0 ** (-8.0 * (h + 1) / N_HEADS)
            kp = k_ref[pl.ds(k0, tk), pair * LANES:(pair + 1) * LANES]
            lt = lax.dot_general(kp, qh_refs[h][...], _NT, preferred_element_type=F32) - slope * dist
            lt_refs[h][slot] = lt.astype(BF16)
            tile_max = jnp.max(lt, axis=0, keepdims=True).astype(BF16).astype(F32)
            ms.append(jnp.maximum(m_all[h:h + 1, :], tile_max))
        return jnp.concatenate(ms, axis=0)

    def att_body(kt, carry):
        m_prev, m_cur, l_all = carry
        slot = kt % 2
        m_next = logits(jnp.minimum(kt + 1, nkt - 1), 1 - slot, m_cur)
        ls = []
        for h in range(N_HEADS):
            alpha = jnp.exp(m_prev[h:h + 1, :] - m_cur[h:h + 1, :])
            p = jnp.exp(lt_refs[h][slot] - m_cur[h:h + 1, :].astype(BF16))
            pv = jnp.dot(vt_ref[kt, h * VT_HEAD_ROWS:(h + 1) * VT_HEAD_ROWS, :], p, preferred_element_type=F32)
            ls.append(alpha * l_all[h:h + 1, :] + pv[HEAD_DIM:HEAD_DIM + 1, :])
            acc_refs[h][...] = alpha * acc_refs[h][...] + pv[:HEAD_DIM, :]
        return m_cur, m_next, jnp.concatenate(ls, axis=0)

    m_init = jnp.full((N_HEADS, tq), -jnp.inf, F32)
    _, _, l_all = lax.fori_loop(0, nkt, att_body, (m_init, logits(0, 0, m_init), jnp.zeros((N_HEADS, tq), F32)))
    out_t = jnp.concatenate([acc_refs[h][...] / l_all[h:h + 1, :] for h in range(N_HEADS)], axis=0)
    o_ref[...] = out_t.T.astype(BF16)


def _attend_prompt(q, iq, iwt, kb, vt, ikb, n_seq, seq_len, topk):
    n = q.shape[0]
    t = ATT_TILE
    nq = seq_len // t
    once = pl.Buffered(1)
    return pl.pallas_call(
        functools.partial(_attend_prompt_kernel, topk=topk),
        out_shape=jax.ShapeDtypeStruct((n, ATT_WIDTH), BF16),
        grid=(n_seq, nq),
        in_specs=[pl.BlockSpec((t, ATT_WIDTH), lambda b, j: (b * nq + j, 0)),
                  pl.BlockSpec((t, ATT_WIDTH), lambda b, j: (b * nq + j, 0)),
                  pl.BlockSpec((N_IDX_HEADS, t), lambda b, j: (0, b * nq + j)),
                  pl.BlockSpec((seq_len, ATT_WIDTH), lambda b, j: (b, 0), pipeline_mode=once),
                  pl.BlockSpec((nq, VT_ROWS, t), lambda b, j: (b, 0, 0), pipeline_mode=once),
                  pl.BlockSpec((seq_len, LANES), lambda b, j: (b, 0), pipeline_mode=once)],
        out_specs=pl.BlockSpec((t, ATT_WIDTH), lambda b, j: (b * nq + j, 0)),
        scratch_shapes=([pltpu.VMEM((nq, t, t), I32), pltpu.VMEM((nq, t, t), BF16)]
                        + [pltpu.VMEM((t, LANES), BF16)] * N_HEADS
                        + [pltpu.VMEM((HEAD_DIM, t), F32)] * N_HEADS + [pltpu.VMEM((2, t, t), BF16)] * N_HEADS),
        compiler_params=_cparams("arbitrary", "arbitrary"),
        name="attend_prompt",
    )(q, iq, iwt, kb, vt, ikb)


def _attend_sample_kernel(q_ref, iq_ref, iwl_ref, kn_ref, vn_ref, ikn_ref, ck_ref, cv_ref, ci_ref,
                          o_ref, keys_ref, plane_ref, acc_ref, *, topk, past):
    tq = q_ref.shape[0]
    nkt, tk, w = keys_ref.shape
    nct = nkt - 1
    row = lax.broadcasted_iota(I32, (tk, w), 0)
    lane = lax.broadcasted_iota(I32, (tk, w), 1)
    qpos = past + (lane & (tq - 1))
    lane_head = lax.shift_right_logical(lax.broadcasted_iota(I32, (1, w), 1), tq.bit_length() - 1)
    slope = lax.bitcast_convert_type(lax.shift_left(126 - lane_head, 23), F32)

    iq = iq_ref[...]
    q = q_ref[...]
    iq_rows = jnp.concatenate([iq[:, h * IDX_DIM:(h + 1) * IDX_DIM] for h in range(N_IDX_HEADS)], axis=0)
    q_rows = jnp.concatenate(
        [_half_mask_wide(q, h) for h in range(N_HEADS)], axis=0) * jnp.asarray(ATTN_SCALE, BF16)
    iwl = iwl_ref[...]
    pad = tk - tq
    kn = jnp.concatenate([kn_ref[...], jnp.zeros((pad, ATT_WIDTH), BF16)], axis=0)
    vn = jnp.concatenate([vn_ref[...].astype(BF16), jnp.zeros((pad, ATT_WIDTH), BF16)], axis=0)
    ikn = jnp.concatenate([ikn_ref[:, :IDX_DIM], jnp.zeros((pad, IDX_DIM), BF16)], axis=0)

    def score_tile(ik_tile, kpos, valid):
        r = lax.dot_general(ik_tile, iq_rows, _NT, preferred_element_type=F32)
        s = jnp.maximum(r, 0.0) * iwl
        s = s + pltpu.roll(s, w // 2, 1)
        s = s + pltpu.roll(s, w // 4, 1)
        s = s + pltpu.roll(s, w // 8, 1)
        adm = lax.shift_right_logical(kpos, CHUNK_SHIFT) <= lax.shift_right_logical(qpos, CHUNK_SHIFT)
        return jnp.where(adm & valid, s, -jnp.inf)

    def put_scores(kt, s):
        key = _mono_key(s)
        keys_ref[kt] = key
        plane_ref[kt] = _digit_plane(key, 0)

    def score_body(kt, _):
        k0 = pl.multiple_of(kt * tk, tk)
        put_scores(kt, score_tile(ci_ref[pl.ds(k0, tk), :].astype(BF16), k0 + row, True))
        return 0

    lax.fori_loop(0, nct, score_body, 0, unroll=_static_unroll(nct))
    put_scores(nct, score_tile(ikn, past + row, row < tq))

    d_last = _topk_select(keys_ref, plane_ref, nkt, topk)

    def logit_tile(kt, k_tile, kpos):
        dist = jnp.where(_selected(keys_ref, plane_ref, kt, d_last), jnp.abs(qpos - kpos).astype(F32), MASKED_DIST)
        lt = lax.dot_general(k_tile, q_rows, _NT, preferred_element_type=F32) - slope * dist
        keys_ref[kt] = lax.bitcast_convert_type(lt, I32)
        return jnp.max(lt, axis=0, keepdims=True)

    def logit_body(kt, m):
        k0 = pl.multiple_of(kt * tk, tk)
        return jnp.maximum(m, logit_tile(kt, ck_ref[pl.ds(k0, tk), :].astype(BF16), k0 + row))

    m = lax.fori_loop(0, nct, logit_body, jnp.full((1, w), -jnp.inf, F32), unroll=_static_unroll(nct))
    m = jnp.maximum(m, logit_tile(nct, kn, past + row))

    acc_ref[...] = jnp.zeros_like(acc_ref)

    def pv_tile(kt, v_tile):
        p = jnp.exp(lax.bitcast_convert_type(keys_ref[kt], F32) - m)
        acc_ref[...] += jnp.dot(p.T.astype(BF16), v_tile, preferred_element_type=F32)
        return jnp.sum(p, axis=0, keepdims=True)

    def pv_body(kt, l):
        k0 = pl.multiple_of(kt * tk, tk)
        return l + pv_tile(kt, cv_ref[pl.ds(k0, tk), :].astype(BF16))

    l = lax.fori_loop(0, nct, pv_body, jnp.zeros((1, w), F32), unroll=_static_unroll(nct))
    l = l + pv_tile(nct, vn)

    l_col = jnp.broadcast_to(l, (w, w)).T
    out_lane_head = lax.shift_right_logical(lax.broadcasted_iota(I32, (tq, ATT_WIDTH), 1), 6)
    y = jnp.zeros((tq, ATT_WIDTH), F32)
    for h in range(N_HEADS):
        blk = acc_ref[h * tq:(h + 1) * tq, :] / jnp.concatenate([l_col[h * tq:(h + 1) * tq, :]] * (ATT_WIDTH // w), axis=1)
        y = y + jnp.where(out_lane_head == h, blk, 0.0)
    o_ref[...] = y.astype(BF16)


def _half_mask_wide(x, head):
    lane = lax.broadcasted_iota(I32, x.shape, 1)
    keep = lax.shift_right_logical(lane, 6) == head
    return jnp.where(keep, x, jnp.zeros_like(x))


def _attend_sample(q, iq, iwl, kb, vf, ikb, cache_k, cache_v, cache_i, topk):
    n_seq, past, _ = cache_k.shape
    tq = q.shape[0] // n_seq
    tk = ATT_TILE
    w = N_HEADS * tq
    assert w == LANES and past % tk == 0
    new = lambda width: pl.BlockSpec((tq, width), lambda b: (b, 0))
    cache = lambda width: pl.BlockSpec((None, past, width), lambda b: (b, 0, 0))
    return pl.pallas_call(
        functools.partial(_attend_sample_kernel, topk=topk, past=past),
        out_shape=jax.ShapeDtypeStruct((n_seq * tq, ATT_WIDTH), BF16),
        grid=(n_seq,),
        in_specs=[new(ATT_WIDTH), new(ATT_WIDTH), pl.BlockSpec((None, 1, w), lambda b: (b, 0, 0)),
                  new(ATT_WIDTH), new(ATT_WIDTH), new(LANES),
                  cache(ATT_WIDTH), cache(ATT_WIDTH), cache(IDX_DIM)],
        out_specs=new(ATT_WIDTH),
        scratch_shapes=[pltpu.VMEM((past // tk + 1, tk, w), I32), pltpu.VMEM((past // tk + 1, tk, w), BF16),
                        pltpu.VMEM((w, ATT_WIDTH), F32)],
        compiler_params=_cparams("arbitrary"),
        name="attend_sample",
    )(q, iq, iwl, kb, vf, ikb, cache_k, cache_v, cache_i)


TOK_ROWS = D_MODEL // LANES


def _store_token_tiles(ref, x):
    t = x.shape[0]
    for c in range(TOK_ROWS):
        ref[pl.ds(c, t, stride=TOK_ROWS), :] = x[:, c * LANES:(c + 1) * LANES]


def _load_token_tiles(ref):
    t = ref.shape[0] // TOK_ROWS
    return jnp.concatenate([ref[pl.ds(c, t, stride=TOK_ROWS), :] for c in range(TOK_ROWS)], axis=1)


def _layer_norm(x, g, b):
    mu = jnp.mean(x, axis=-1, keepdims=True)
    xc = x - mu
    var = jnp.mean(xc * xc, axis=-1, keepdims=True)
    return xc * lax.rsqrt(var + LN_EPS) * g + b


def _post_kernel(x_ref, ya_ref, yb_ref, ga_ref, scf_ref, shf_ref, wo_ref, bo_ref, g1_ref, b1_ref,
                 wrh_ref, wrl_ref, br_ref,
                 x1_ref, h2_ref, tope_ref, gate_ref, rank_ref, cnt_ref):
    t = x_ref.shape[0]
    y = (jnp.dot(ya_ref[...], wo_ref[:A_WIDTH, :], preferred_element_type=F32)
         + jnp.dot(yb_ref[...], wo_ref[A_WIDTH:, :], preferred_element_type=F32) + bo_ref[...])
    x1 = _layer_norm(DEEPNORM_ALPHA * x_ref[...] + (1.0 + ga_ref[...]) * y, g1_ref[...], b1_ref[...])
    x1_ref[...] = x1
    h2 = x1 * (1.0 + scf_ref[...]) + shf_ref[...]
    _store_token_tiles(h2_ref, h2)

    hh = h2.astype(BF16)
    hl = (h2 - hh.astype(F32)).astype(BF16)
    logits = (lax.dot_general(wrh_ref[...], hh, _NT, preferred_element_type=F32)
              + lax.dot_general(wrh_ref[...], hl, _NT, preferred_element_type=F32)
              + lax.dot_general(wrl_ref[...], hh, _NT, preferred_element_type=F32) + br_ref[...])
    erow = lax.broadcasted_iota(I32, (N_EXPERTS, t), 0)
    vals, idxs = [], []
    for _ in range(TOP_K):
        v = jnp.max(logits, axis=0, keepdims=True)
        i = jnp.min(jnp.where(logits == v, erow, N_EXPERTS), axis=0, keepdims=True)
        vals.append(v)
        idxs.append(i)
        logits = jnp.where(erow == i, -jnp.inf, logits)
    ex = [jnp.exp(v - vals[0]) for v in vals]
    den = ex[0] + ex[1] + ex[2] + ex[3]
    gate_ref[...] = jnp.concatenate([e / den for e in ex], axis=0)
    tope_ref[...] = jnp.concatenate(idxs, axis=0)

    @pl.when(pl.program_id(0) == 0)
    def _():
        cnt_ref[...] = jnp.zeros_like(cnt_ref)

    hit = jnp.zeros((N_EXPERTS, t), F32)
    for i in idxs:
        hit = hit + jnp.where(erow == i, 1.0, 0.0)
    hitb = hit.astype(BF16)
    r = lax.broadcasted_iota(I32, (t, t), 0)
    c = lax.broadcasted_iota(I32, (t, t), 1)
    earlier = jnp.where(r < c, 1.0, 0.0).astype(BF16)
    before = jnp.dot(hitb, earlier, preferred_element_type=F32) + cnt_ref[...]
    total = jnp.dot(hitb, jnp.ones((t, t), BF16), preferred_element_type=F32)
    rank_ref[...] = jnp.concatenate(
        [jnp.sum(jnp.where(erow == i, before, 0.0), axis=0, keepdims=True) for i in idxs], axis=0).astype(I32)
    cnt_ref[...] += total


def _post(x, ya, yb, ga, scf, shf, w_o, b_o, g1, b1, wrh, wrl, br):
    n, d = x.shape
    t = ROW_TILE
    nt = n // t
    if ga.shape[0] == n:
        mod_spec = pl.BlockSpec((t, d), lambda i: (i, 0))
    else:
        tiles_per_seq = nt // ga.shape[0]
        ga, scf, shf = (a.reshape(a.shape[0], 1, d) for a in (ga, scf, shf))
        mod_spec = pl.BlockSpec((None, 1, d), lambda i: (i // tiles_per_seq, 0, 0))
    row = lambda w: pl.BlockSpec((t, w), lambda i: (i, 0))
    col = lambda r: pl.BlockSpec((r, t), lambda i: (0, i))
    const = lambda a: pl.BlockSpec(a.shape, lambda i: (0, 0))
    return pl.pallas_call(
        _post_kernel,
        out_shape=(jax.ShapeDtypeStruct((n, d), F32), jax.ShapeDtypeStruct((n * TOK_ROWS, LANES), F32),
                   jax.ShapeDtypeStruct((TOP_K, n), I32), jax.ShapeDtypeStruct((TOP_K, n), F32),
                   jax.ShapeDtypeStruct((TOP_K, n), I32), jax.ShapeDtypeStruct((N_EXPERTS, t), F32)),
        grid=(nt,),
        in_specs=[row(d), row(A_WIDTH), row(ATT_WIDTH), mod_spec, mod_spec, mod_spec,
                  const(w_o), const(b_o), const(g1), const(b1), const(wrh), const(wrl), const(br)],
        out_specs=(row(d), pl.BlockSpec((t * TOK_ROWS, LANES), lambda i: (i, 0)), col(TOP_K), col(TOP_K), col(TOP_K),
                   pl.BlockSpec((N_EXPERTS, t), lambda i: (0, 0))),
        compiler_params=_cparams("arbitrary"),
        name="post",
    )(x, ya, yb, ga, scf, shf, w_o, b_o, g1, b1, wrh, wrl, br)


def _token_rows(ref, r):
    return ref.at[pl.ds(pl.multiple_of(r * TOK_ROWS, TOK_ROWS), TOK_ROWS)]


def _token_copies_wait(hbm_ref, vmem_ref, sem, n_tokens):
    rows = n_tokens * TOK_ROWS
    pltpu.make_async_copy(hbm_ref.at[pl.ds(0, rows)], vmem_ref.at[pl.ds(0, rows)], sem).wait()


def _dispatch_kernel(zrow_ref, dest_ref, h_ref, xp_ref, zero_ref, sem):
    t = h_ref.shape[0] // TOK_ROWS

    block_rows = MOE_BLOCK * TOK_ROWS

    def zero_block(slot0):
        z0 = pl.multiple_of(slot0 * TOK_ROWS, block_rows)
        cp = pltpu.make_async_copy(zero_ref, xp_ref.at[pl.ds(z0, block_rows)], sem)
        cp.start()
        cp.wait()

    @pl.when(pl.program_id(0) == 0)
    def _():
        zero_ref[...] = jnp.zeros_like(zero_ref)
        for e in range(N_EXPERTS):
            @pl.when(zrow_ref[e] >= 0)
            def _():
                zero_block(zrow_ref[e])

        def unused(b, _):
            zero_block(b * MOE_BLOCK)
            return 0

        lax.fori_loop(zrow_ref[N_EXPERTS], xp_ref.shape[0] // block_rows, unused, 0)

    def body(i, _):
        for k in range(TOP_K):
            pltpu.make_async_copy(_token_rows(h_ref, i), _token_rows(xp_ref, dest_ref[k, i]), sem).start(priority=k % 2)
        return 0

    lax.fori_loop(0, t, body, 0, unroll=DMA_ISSUE_UNROLL)
    for k in range(TOP_K):
        _token_copies_wait(xp_ref, h_ref, sem, t)


def _dispatch(zrow, dest, h2, n_slots):
    t = ROW_TILE
    n = h2.shape[0] // TOK_ROWS
    grid_spec = pltpu.PrefetchScalarGridSpec(
        num_scalar_prefetch=1,
        grid=(n // t,),
        in_specs=[pl.BlockSpec((TOP_K, t), lambda i, z: (0, i), memory_space=pltpu.SMEM),
                  pl.BlockSpec((t * TOK_ROWS, LANES), lambda i, z: (i, 0))],
        out_specs=pl.BlockSpec(memory_space=pl.ANY),
        scratch_shapes=[pltpu.VMEM((MOE_BLOCK * TOK_ROWS, LANES), F32), pltpu.SemaphoreType.DMA],
    )
    return pl.pallas_call(
        _dispatch_kernel,
        out_shape=jax.ShapeDtypeStruct((n_slots * TOK_ROWS, LANES), F32),
        grid_spec=grid_spec,
        compiler_params=_cparams("arbitrary"),
        name="dispatch",
    )(zrow, dest, h2)


def _experts_kernel(be_ref, bi_ref, nu_ref, x_ref, wg_ref, bg_ref, wu_ref, bu_ref, wd_ref, bd_ref, y_ref,
                    wgb_ref, wub_ref, wdb_ref):
    i = pl.program_id(0)

    @pl.when((i == 0) | (be_ref[i] != be_ref[jnp.maximum(i - 1, 0)]))
    def _():
        wgb_ref[...] = wg_ref[...].astype(BF16)
        wub_ref[...] = wu_ref[...].astype(BF16)
        wdb_ref[...] = wd_ref[...].astype(BF16)

    @pl.when(i < nu_ref[0])
    def _():
        x = _load_token_tiles(x_ref).astype(BF16)
        g = jnp.minimum(jnp.dot(x, wgb_ref[...], preferred_element_type=F32) + bg_ref[...], SWIGLU_LIMIT)
        u = jnp.clip(jnp.dot(x, wub_ref[...], preferred_element_type=F32) + bu_ref[...], -SWIGLU_LIMIT, SWIGLU_LIMIT)
        a = g * jax.nn.sigmoid(SWIGLU_ALPHA * g)
        mid = ((u + 1.0) * a).astype(BF16)
        _store_token_tiles(y_ref, jnp.dot(mid, wdb_ref[...], preferred_element_type=F32) + bd_ref[...])

    @pl.when(pl.program_id(0) >= nu_ref[0])
    def _():
        y_ref[...] = jnp.zeros_like(y_ref)


def _experts(block_e, block_i, n_used, xp, wg, bg, wu, bu, wd, bd):
    d, f = wg.shape[1], wg.shape[2]
    nb = xp.shape[0] // (MOE_BLOCK * TOK_ROWS)
    wspec = lambda a, b: pl.BlockSpec((None, a, b), lambda i, be, bi, nu: (be[i], 0, 0))
    slots = pl.BlockSpec((MOE_BLOCK * TOK_ROWS, LANES), lambda i, be, bi, nu: (bi[i], 0))
    grid_spec = pltpu.PrefetchScalarGridSpec(
        num_scalar_prefetch=3,
        grid=(nb,),
        in_specs=[slots, wspec(d, f), wspec(1, f), wspec(d, f), wspec(1, f), wspec(f, d), wspec(1, d)],
        out_specs=pl.BlockSpec((MOE_BLOCK * TOK_ROWS, LANES), lambda i, be, bi, nu: (i, 0)),
        scratch_shapes=[pltpu.VMEM((d, f), BF16), pltpu.VMEM((d, f), BF16), pltpu.VMEM((f, d), BF16)],
    )
    return pl.pallas_call(
        _experts_kernel,
        out_shape=jax.ShapeDtypeStruct(xp.shape, F32),
        grid_spec=grid_spec,
        compiler_params=_cparams("arbitrary"),
        name="experts",
    )(block_e, block_i, n_used, xp, wg, bg, wu, bu, wd, bd)


def _combine_kernel(dest_ref, gate_ref, x1_ref, gf_ref, g2_ref, b2_ref, yp_ref, o_ref, buf_ref, sem):
    t = x1_ref.shape[0]

    def body(i, _):
        for k in range(TOP_K):
            pltpu.make_async_copy(_token_rows(yp_ref, dest_ref[k, i]), _token_rows(buf_ref.at[k], i), sem).start(
                priority=k % 2)
        return 0

    lax.fori_loop(0, t, body, 0, unroll=DMA_ISSUE_UNROLL)
    for k in range(TOP_K):
        _token_copies_wait(yp_ref, buf_ref.at[k], sem, t)

    gates = jnp.concatenate([gate_ref[...], jnp.zeros((LANES - TOP_K, t), F32)], axis=0).T
    f = gates[:, 0:1] * _load_token_tiles(buf_ref.at[0])
    for k in range(1, TOP_K):
        f = f + gates[:, k:k + 1] * _load_token_tiles(buf_ref.at[k])
    o_ref[...] = _layer_norm(DEEPNORM_ALPHA * x1_ref[...] + (1.0 + gf_ref[...]) * f, g2_ref[...], b2_ref[...])


def _combine(dest, gates, x1, gf, g2, b2, yp):
    n, d = x1.shape
    t = ROW_TILE
    nt = n // t
    if gf.shape[0] == n:
        mod_spec = pl.BlockSpec((t, d), lambda i: (i, 0))
    else:
        tiles_per_seq = nt // gf.shape[0]
        gf = gf.reshape(gf.shape[0], 1, d)
        mod_spec = pl.BlockSpec((None, 1, d), lambda i: (i // tiles_per_seq, 0, 0))
    return pl.pallas_call(
        _combine_kernel,
        out_shape=jax.ShapeDtypeStruct((n, d), F32),
        grid=(nt,),
        in_specs=[pl.BlockSpec((TOP_K, t), lambda i: (0, i), memory_space=pltpu.SMEM),
                  pl.BlockSpec((TOP_K, t), lambda i: (0, i)),
                  pl.BlockSpec((t, d), lambda i: (i, 0)), mod_spec,
                  pl.BlockSpec((1, d), lambda i: (0, 0)), pl.BlockSpec((1, d), lambda i: (0, 0)),
                  pl.BlockSpec(memory_space=pl.ANY)],
        out_specs=pl.BlockSpec((t, d), lambda i: (i, 0)),
        scratch_shapes=[pltpu.VMEM((TOP_K, t * TOK_ROWS, LANES), F32), pltpu.SemaphoreType.DMA],
        compiler_params=_cparams("arbitrary"),
        name="combine",
    )(dest, gates, x1, gf, g2, b2, yp)


def _moe(h2, tope, rank, counts, gates, x1, gf, g2, b2, experts_w):
    n = tope.shape[1]
    nb = (n * TOP_K + N_EXPERTS * (MOE_BLOCK - 1) + MOE_BLOCK - 1) // MOE_BLOCK
    pcounts = (counts + MOE_BLOCK - 1) // MOE_BLOCK * MOE_BLOCK
    pend = jnp.cumsum(pcounts)
    pstart = pend - pcounts
    eids = jnp.arange(N_EXPERTS, dtype=I32).reshape(N_EXPERTS, 1, 1)
    dest = rank + jnp.sum(jnp.where(tope[None] == eids, pstart.reshape(N_EXPERTS, 1, 1), 0), axis=0).astype(I32)
    n_used = (pend[-1] // MOE_BLOCK).astype(I32)
    zrow = jnp.concatenate([jnp.where(counts > 0, pend - MOE_BLOCK, -1), n_used.reshape(1)]).astype(I32)
    blk = jnp.arange(nb, dtype=I32)
    block_i = jnp.minimum(blk, n_used - 1)
    block_e = jnp.minimum(jnp.sum(pend[None, :] <= (block_i * MOE_BLOCK)[:, None], axis=1), N_EXPERTS - 1).astype(I32)
    xp = _dispatch(zrow, dest, h2, nb * MOE_BLOCK)
    yp = _experts(block_e, block_i, n_used.reshape(1), xp, *experts_w)
    return _combine(dest, gates, x1, gf, g2, b2, yp)


def kernel(x_prompt, x_sample, c_prompt, c_sample, cache_k, cache_v, cache_kidx, w_in, a_ln_g, a_ln_b, a_ws, a_bs,
           w_o, b_o, w_c, b_c, ln1_g, ln1_b, ln2_g, ln2_b, w_router, b_router, w_gate, b_gate, w_up, b_up,
           w_down, b_down):
    bp, s, d = x_prompt.shape
    bs, ts, _ = x_sample.shape
    past = cache_k.shape[2]
    np_, ns = bp * s, bs * ts

    wi = w_in[0]
    w_in_p = jnp.zeros((d, IN_COLS_PAD), F32)
    w_in_p = w_in_p.at[:, :OFF_IK].set(wi[:, :OFF_IK])
    w_in_p = w_in_p.at[:, OFF_IK:OFF_IK + IDX_DIM].set(wi[:, OFF_IK:OFF_IK + IDX_DIM])
    w_in_p = w_in_p.at[:, OFF_IK + IDX_DIM:OFF_IK + 2 * IDX_DIM].set(wi[:, OFF_IK:OFF_IK + IDX_DIM])
    w_in_p = w_in_p.at[:, OFF_IW:OFF_IW + N_IDX_HEADS].set(wi[:, OFF_IK + IDX_DIM:OFF_IK + IDX_DIM + N_IDX_HEADS])
    w_in_p = w_in_p.astype(BF16)
    lng, lnb = a_ln_g[0].reshape(1, A_WIDTH), a_ln_b[0].reshape(1, A_WIDTH)
    wtril = jnp.tril(a_ws[0])
    wm_p = wtril.astype(BF16)
    bm_p = jnp.broadcast_to(a_bs[0][:, :, None], (A_GROUPS, A_CHUNK, A_GROUP_DIM)).astype(F32)
    rep = A_CHUNK // ts
    wm_s = jnp.einsum("ab,gij->gaibj", jnp.eye(rep, dtype=F32), wtril[:, :ts, :ts]).reshape(
        A_GROUPS, A_CHUNK, A_CHUNK).astype(BF16)
    bm_s = jnp.broadcast_to(jnp.tile(a_bs[0][:, :ts], (1, rep))[:, :, None], (A_GROUPS, A_CHUNK, A_GROUP_DIM)).astype(F32)
    w_o_b = w_o[0].astype(BF16)
    b_o_r = b_o[0].reshape(1, d)
    wr_t = w_router[0].T
    wrh = wr_t.astype(BF16)
    wrl = (wr_t - wrh.astype(F32)).astype(BF16)
    br = jnp.broadcast_to(b_router[0][:, None], (N_EXPERTS, ROW_TILE)).astype(F32)
    experts_w = (w_gate[0], b_gate[0][:, None, :], w_up[0], b_up[0][:, None, :], w_down[0], b_down[0][:, None, :])
    g1, b1 = ln1_g[0].reshape(1, d), ln1_b[0].reshape(1, d)
    g2, b2 = ln2_g[0].reshape(1, d), ln2_b[0].reshape(1, d)

    mods = _cond_mods(jnp.concatenate([c_prompt, c_sample], axis=0), w_c[0], b_c[0]).reshape(bp + bs, 6, d)
    mods_p = [mods[:bp, i] for i in range(6)]
    mods_s = [jnp.repeat(mods[bp:, i], ts, axis=0) for i in range(6)]

    xp2 = x_prompt.reshape(np_, d)
    ya, q, kf, kb, vf, vt, iq, ikf, ikb, iwt, _ = _project(xp2, mods_p[1], mods_p[0], w_in_p, lng, lnb, wm_p, bm_p)
    yb = _attend_prompt(q, iq, iwt, kb, vt, ikb, bp, s, min(TOPK_MAX, s // 4))
    x1, h2, tope, gates, rank, cnt = _post(xp2, ya, yb, mods_p[2], mods_p[4], mods_p[3], w_o_b, b_o_r, g1, b1, wrh, wrl, br)
    y_p = _moe(h2, tope, rank, cnt[:, 0].astype(I32), gates, x1, mods_p[5], g2, b2, experts_w)
    out_p = (y_p.reshape(bp, s, d), kf.reshape(1, bp, s, N_HEADS, HEAD_DIM), vf.reshape(1, bp, s, N_HEADS, HEAD_DIM),
             ikf.reshape(1, bp, s, IDX_DIM))

    xs2 = x_sample.reshape(ns, d)
    ya, q, kf, kb, vf, vt, iq, ikf, ikb, iwt, va = _project(xs2, mods_s[1], mods_s[0], w_in_p, lng, lnb, wm_s, bm_s)
    iwl = jnp.transpose(iwt.reshape(N_IDX_HEADS, bs, ts), (1, 0, 2)).reshape(bs, 1, N_IDX_HEADS * ts)
    yb = _attend_sample(q, iq, iwl, kb, vf, ikb, cache_k[0].reshape(bs, past, ATT_WIDTH),
                        cache_v[0].reshape(bs, past, ATT_WIDTH), cache_kidx[0], min(TOPK_MAX, (past + ts) // 4))
    x1, h2, tope, gates, rank, cnt = _post(xs2, ya, yb, mods_s[2], mods_s[4], mods_s[3], w_o_b, b_o_r, g1, b1, wrh, wrl, br)
    y_s = _moe(h2, tope, rank, cnt[:, 0].astype(I32), gates, x1, mods_s[5], g2, b2, experts_w)

    return (out_p[0], y_s.reshape(bs, ts, d), out_p[1], out_p[2], out_p[3],
            kf.reshape(1, bs, ts, N_HEADS, HEAD_DIM), vf.reshape(1, bs, ts, N_HEADS, HEAD_DIM),
            ikf.reshape(1, bs, ts, IDX_DIM), va.reshape(1, bs, ts, A_WIDTH))
```

```python
import functools

import jax
import jax.numpy as jnp
from jax import lax
from jax.experimental import pallas as pl
from jax.experimental.pallas import tpu as pltpu

F32 = jnp.float32
BF16 = jnp.bfloat16
I32 = jnp.int32

D_MODEL = 1024
CHUNK_SHIFT = 6
A_GROUPS = 4
A_GROUP_DIM = 128
A_WIDTH = A_GROUPS * A_GROUP_DIM
A_CHUNK = 128
N_HEADS = 8
HEAD_DIM = 64
ATT_WIDTH = N_HEADS * HEAD_DIM
N_IDX_HEADS = 8
IDX_DIM = 64
TOPK_MAX = 256
ATTN_SCALE = HEAD_DIM ** -0.5
VT_HEAD_ROWS = HEAD_DIM + 16
VT_ROWS = N_HEADS * VT_HEAD_ROWS
IDX_W_SCALE = (N_IDX_HEADS ** -0.5) * (IDX_DIM ** -0.5)
N_EXPERTS = 32
TOP_K = 4
SWIGLU_LIMIT = 7.0
SWIGLU_ALPHA = 1.702
DEEPNORM_ALPHA = 2.0 ** 0.25
LN_EPS = 1e-5

LANES = 128
SUBLANES = 8
VMEM_LIMIT_BYTES = 56 * 1024 * 1024

ROW_TILE = 256
POST_TILE = 512
MOE_BLOCK = 256
ATT_TILE = 256
DMA_ISSUE_UNROLL = 8
STATIC_TILE_UNROLL = 4

OFF_AU, OFF_AV, OFF_Q, OFF_K, OFF_V, OFF_IQ = 0, 512, 1024, 1536, 2048, 2560
OFF_IK = 3072
OFF_IW = 3200
IN_COLS_PAD = 3328

MASKED_DIST = 3.0e32
F32_MAX = 3.4028234663852886e38
INT32_MIN = -(2 ** 31)
KEY_NEG_INF = INT32_MIN + 0x7FFFFF

_NT = (((1,), (1,)), ((), ()))


def _cparams(*sem):
    return pltpu.CompilerParams(dimension_semantics=sem, vmem_limit_bytes=VMEM_LIMIT_BYTES)


def _mods_kernel(c_ref, w_ref, b_ref, o_ref):
    c = c_ref[...]
    s = c * jax.nn.sigmoid(c)
    o_ref[...] = jnp.dot(s.astype(BF16), w_ref[...].astype(BF16), preferred_element_type=F32) + b_ref[...]


def _cond_mods(c, w_c, b_c):
    nb, d = c.shape
    n_out = w_c.shape[1]
    return pl.pallas_call(
        _mods_kernel,
        out_shape=jax.ShapeDtypeStruct((nb, n_out), F32),
        grid=(n_out // d,),
        in_specs=[pl.BlockSpec((nb, d), lambda j: (0, 0)),
                  pl.BlockSpec((d, d), lambda j: (0, j)),
                  pl.BlockSpec((1, d), lambda j: (0, j))],
        out_specs=pl.BlockSpec((nb, d), lambda j: (0, j)),
        compiler_params=_cparams("arbitrary"),
        name="mods",
    )(c, w_c, b_c.reshape(1, n_out))


def _gelu(x):
    return 0.5 * x * (1.0 + lax.erf(x * 0.7071067811865476))


def _project_kernel(x_ref, sc_ref, sh_ref, w_ref, lng_ref, lnb_ref, wm_ref, bm_ref,
                    ya_ref, q_ref, kf_ref, kb_ref, vf_ref, vt_ref, iq_ref, ikf_ref, ikb_ref, iwt_ref, va_ref):
    t = x_ref.shape[0]
    h = (x_ref[...] * (1.0 + sc_ref[...]) + sh_ref[...]).astype(BF16)

    def proj(c0, n):
        return jnp.dot(h, w_ref[:, c0:c0 + n], preferred_element_type=F32)

    u = _gelu(proj(OFF_AU, A_WIDTH))
    gv = _gelu(proj(OFF_AV, A_WIDTH))
    for g in range(A_GROUPS):
        lo, hi = g * A_GROUP_DIM, (g + 1) * A_GROUP_DIM
        xg = gv[:, lo:hi]
        mu = jnp.mean(xg, axis=-1, keepdims=True)
        xc = xg - mu
        var = jnp.mean(xc * xc, axis=-1, keepdims=True)
        vg = xc * lax.rsqrt(var + LN_EPS) * lng_ref[:, lo:hi] + lnb_ref[:, lo:hi]
        va_ref[:, lo:hi] = vg
        vgb = vg.astype(BF16)
        for c in range(t // A_CHUNK):
            r0, r1 = c * A_CHUNK, (c + 1) * A_CHUNK
            mixed = jnp.dot(wm_ref[g], vgb[r0:r1, :], preferred_element_type=F32) + bm_ref[g]
            ya_ref[r0:r1, lo:hi] = (u[r0:r1, lo:hi] * mixed).astype(BF16)

    q_ref[...] = proj(OFF_Q, ATT_WIDTH).astype(BF16)
    k = proj(OFF_K, ATT_WIDTH)
    kf_ref[...] = k
    kb_ref[...] = k.astype(BF16)
    v = proj(OFF_V, ATT_WIDTH)
    vf_ref[...] = v
    v_t = v.T.astype(BF16)
    ones = jnp.ones((VT_HEAD_ROWS - HEAD_DIM, t), BF16)
    vt_ref[0] = jnp.concatenate(
        [blk for h in range(N_HEADS) for blk in (v_t[h * HEAD_DIM:(h + 1) * HEAD_DIM, :], ones)], axis=0)
    iq_ref[...] = proj(OFF_IQ, N_IDX_HEADS * IDX_DIM).astype(BF16)
    ik2 = proj(OFF_IK, LANES)
    ikf_ref[...] = ik2[:, :IDX_DIM]
    ikb_ref[...] = ik2.astype(BF16)
    iw = proj(OFF_IW, LANES) * IDX_W_SCALE
    iwt_ref[...] = iw.T[:N_IDX_HEADS, :]


def _project(x, sc, sh, w_in_p, lng, lnb, wm, bm):
    n, d = x.shape
    t = ROW_TILE
    nt = n // t
    if sc.shape[0] == n:
        mod_spec = pl.BlockSpec((t, d), lambda i: (i, 0))
    else:
        tiles_per_seq = nt // sc.shape[0]
        sc = sc.reshape(sc.shape[0], 1, d)
        sh = sh.reshape(sh.shape[0], 1, d)
        mod_spec = pl.BlockSpec((None, 1, d), lambda i: (i // tiles_per_seq, 0, 0))
    row = lambda w: pl.BlockSpec((t, w), lambda i: (i, 0))
    const2 = lambda a: pl.BlockSpec(a.shape, lambda i: (0, 0))
    const3 = lambda a: pl.BlockSpec(a.shape, lambda i: (0, 0, 0))
    out_shape = (
        jax.ShapeDtypeStruct((n, A_WIDTH), BF16),
        jax.ShapeDtypeStruct((n, ATT_WIDTH), BF16),
        jax.ShapeDtypeStruct((n, ATT_WIDTH), F32),
        jax.ShapeDtypeStruct((n, ATT_WIDTH), BF16),
        jax.ShapeDtypeStruct((n, ATT_WIDTH), F32),
        jax.ShapeDtypeStruct((nt, VT_ROWS, t), BF16),
        jax.ShapeDtypeStruct((n, ATT_WIDTH), BF16),
        jax.ShapeDtypeStruct((n, IDX_DIM), F32),
        jax.ShapeDtypeStruct((n, LANES), BF16),
        jax.ShapeDtypeStruct((N_IDX_HEADS, n), F32),
        jax.ShapeDtypeStruct((n, A_WIDTH), F32),
    )
    out_specs = (row(A_WIDTH), row(ATT_WIDTH), row(ATT_WIDTH), row(ATT_WIDTH), row(ATT_WIDTH),
                 pl.BlockSpec((1, VT_ROWS, t), lambda i: (i, 0, 0)),
                 row(ATT_WIDTH), row(IDX_DIM), row(LANES),
                 pl.BlockSpec((N_IDX_HEADS, t), lambda i: (0, i)),
                 row(A_WIDTH))
    return pl.pallas_call(
        _project_kernel,
        out_shape=out_shape,
        grid=(nt,),
        in_specs=[row(d), mod_spec, mod_spec, const2(w_in_p), const2(lng), const2(lnb), const3(wm), const3(bm)],
        out_specs=out_specs,
        compiler_params=_cparams("arbitrary"),
        name="project",
    )(x, sc, sh, w_in_p, lng, lnb, wm, bm)


DIGIT_BITS = 8
N_DIGITS = 32 // DIGIT_BITS
DIGIT_ABOVE = 512.0
DIGIT_BELOW = -1.0
PACKED_ROWS = 16


def _static_unroll(trips):
    return STATIC_TILE_UNROLL if isinstance(trips, int) else 1


def _tile_loop(nkt, body, init):
    if isinstance(nkt, int):
        return lax.fori_loop(0, nkt, body, init, unroll=STATIC_TILE_UNROLL)
    pairs = lax.shift_right_logical(nkt, 1)
    carry = lax.fori_loop(0, pairs, lambda i, c: body(2 * i + 1, body(2 * i, c)), init)
    return lax.fori_loop(2 * pairs, nkt, body, carry)


def _mono_key(x):
    b = lax.bitcast_convert_type(x, I32)
    return jnp.where(b >= 0, b, b ^ jnp.int32(0x7FFFFFFF))


def _digit_plane(key, phase):
    shift = 32 - DIGIT_BITS * (phase + 1)
    d = lax.shift_right_arithmetic(key, jnp.int32(shift)) if shift else key
    d = d + (1 << (DIGIT_BITS - 1)) if phase == 0 else d & ((1 << DIGIT_BITS) - 1)
    return d.astype(F32).astype(BF16)


def _count_plane(plane_ref, nkt, cand, strict):
    _, tk, w = plane_ref.shape
    cb = cand.astype(BF16)
    one, zero = jnp.ones((), BF16), jnp.zeros((), BF16)

    def body(kt, cnt):
        e = plane_ref[kt]
        accs = [jnp.zeros((PACKED_ROWS, w), BF16) for _ in range(4)]
        for r in range(tk // PACKED_ROWS):
            blk = e[r * PACKED_ROWS:(r + 1) * PACKED_ROWS, :]
            accs[r % 4] = accs[r % 4] + jnp.where((blk > cb) if strict else (blk >= cb), one, zero)
        return cnt + ((accs[0] + accs[1]) + (accs[2] + accs[3])).astype(F32)

    cnt = _tile_loop(nkt, body, jnp.zeros((PACKED_ROWS, w), F32))
    return jnp.sum(cnt, axis=0, keepdims=True)


def _search_digit(plane_ref, nkt, topk):
    w = plane_ref.shape[2]

    def bit_body(i, d):
        cand = d + lax.shift_left(jnp.int32(1), jnp.int32(DIGIT_BITS - 1) - i).astype(F32)
        cnt = _count_plane(plane_ref, nkt, cand, strict=False)
        return jnp.where(cnt >= float(topk), cand, d)

    return lax.fori_loop(0, DIGIT_BITS, bit_body, jnp.zeros((1, w), F32))


def _search_top_digit(plane_ref, nkt, topk):
    _, tk, w = plane_ref.shape

    def max_body(kt, m):
        e = plane_ref[kt]
        for r in range(tk // PACKED_ROWS):
            m = jnp.maximum(m, e[r * PACKED_ROWS:(r + 1) * PACKED_ROWS, :])
        return m

    m = _tile_loop(nkt, max_body, jnp.zeros((PACKED_ROWS, w), BF16))
    top = jnp.max(m.astype(F32), axis=0, keepdims=True)

    def unsettled(state):
        return jnp.min(state[1]) < 0.5

    def step(state):
        cand, _ = state
        enough = _count_plane(plane_ref, nkt, cand, strict=False) >= float(topk)
        return jnp.where(enough, cand, cand - 1.0), jnp.where(enough, 1.0, 0.0)

    return lax.while_loop(unsettled, step, (top, jnp.zeros((1, w), F32)))[0]


def _topk_select(keys_ref, plane_ref, nkt, topk):
    _, tk, w = plane_ref.shape
    d = _search_top_digit(plane_ref, nkt, topk)
    for phase in range(1, N_DIGITS):
        db = d.astype(BF16)

        def refine(kt, _, phase=phase, db=db):
            e = plane_ref[kt]
            decided = jnp.where(e > db, jnp.asarray(DIGIT_ABOVE, BF16), jnp.asarray(DIGIT_BELOW, BF16))
            plane_ref[kt] = jnp.where(e == db, _digit_plane(keys_ref[kt], phase), decided)
            return 0

        _tile_loop(nkt, refine, 0)
        d = _search_digit(plane_ref, nkt, topk)

    cnt_ge = _count_plane(plane_ref, nkt, d, strict=False)

    @pl.when(jnp.max(cnt_ge) > float(topk))
    def _():
        need = float(topk) - _count_plane(plane_ref, nkt, d, strict=True)
        r = lax.broadcasted_iota(I32, (tk, tk), 0)
        c = lax.broadcasted_iota(I32, (tk, tk), 1)
        before = jnp.where(c < r, 1.0, 0.0).astype(BF16)

        def body(kt, seen):
            e = plane_ref[kt].astype(F32)
            eq = e == d
            eqf = jnp.where(eq, 1.0, 0.0)
            prior = jnp.dot(before, eqf.astype(BF16), preferred_element_type=F32) + seen
            plane_ref[kt] = jnp.where(eq & (prior >= need), DIGIT_BELOW, e).astype(BF16)
            return seen + jnp.sum(eqf.reshape(tk // SUBLANES, SUBLANES, w), axis=0).sum(axis=0, keepdims=True)

        lax.fori_loop(0, nkt, body, jnp.zeros((1, w), F32))

    return d


def _selected(keys_ref, plane_ref, kt, d):
    return (plane_ref[kt].astype(F32) >= d) & (keys_ref[kt] > jnp.int32(KEY_NEG_INF))


def _half_mask(x_pair, head):
    lane = lax.broadcasted_iota(I32, x_pair.shape, 1)
    keep = (lane >= HEAD_DIM) if head % 2 else (lane < HEAD_DIM)
    return jnp.where(keep, x_pair, jnp.zeros_like(x_pair))


def _attend_prompt_kernel(q_ref, iq_ref, iwt_ref, k_ref, vt_ref, ik_ref, o_ref, keys_ref, plane_ref, *head_refs, topk):
    qh_refs, acc_refs, lt_refs = (head_refs[i * N_HEADS:(i + 1) * N_HEADS] for i in range(3))
    tq = q_ref.shape[0]
    tk = keys_ref.shape[1]
    j = pl.program_id(1)
    nkt = j + 1
    q0 = j * tq
    row = lax.broadcasted_iota(I32, (tk, tq), 0)
    lane = lax.broadcasted_iota(I32, (tk, tq), 1)
    qpos = q0 + lane

    iq = iq_ref[...]
    iqm = [_half_mask(iq[:, (h // 2) * LANES:(h // 2 + 1) * LANES], h) for h in range(N_IDX_HEADS)]
    iw = iwt_ref[...]

    def score_tile(kt, diagonal):
        k0 = pl.multiple_of(kt * tk, tk)
        ikt = ik_ref[pl.ds(k0, tk), :]
        s = jnp.zeros((tk, tq), F32)
        for h in range(N_IDX_HEADS):
            r = lax.dot_general(ikt, iqm[h], _NT, preferred_element_type=F32)
            s = s + jnp.maximum(r, 0.0) * iw[h:h + 1, :]
        if diagonal:
            adm = lax.shift_right_logical(k0 + row, CHUNK_SHIFT) <= lax.shift_right_logical(qpos, CHUNK_SHIFT)
            s = jnp.where(adm, s, -jnp.inf)
        key = _mono_key(s)
        keys_ref[kt] = key
        plane_ref[kt] = _digit_plane(key, 0)
        return 0

    assert tk == tq
    lax.fori_loop(0, j, lambda kt, _: score_tile(kt, False), 0)
    score_tile(j, True)

    d_last = _topk_select(keys_ref, plane_ref, nkt, topk)

    def dist_body(kt, _):
        kpos = kt * tk + row
        dist = jnp.abs(qpos - kpos).astype(F32)
        masked = jnp.where(_selected(keys_ref, plane_ref, kt, d_last), dist, MASKED_DIST)
        keys_ref[kt] = lax.bitcast_convert_type(masked, I32)
        return 0

    lax.fori_loop(0, nkt, dist_body, 0)

    qfull = q_ref[...]
    for h in range(N_HEADS):
        pair = h // 2
        qh_refs[h][...] = _half_mask(qfull[:, pair * LANES:(pair + 1) * LANES], h) * jnp.asarray(ATTN_SCALE, BF16)
        acc_refs[h][...] = jnp.zeros_like(acc_refs[h])

    def logits(kt, slot, m_all):
        k0 = pl.multiple_of(kt * tk, tk)
        dist = lax.bitcast_convert_type(keys_ref[kt], F32)
        ms = []
        for h in range(N_HEADS):
            pair = h // 2
            slope = 2.0 ** (-8.0 * (h + 1) / N_HEADS)
            kp = k_ref[pl.ds(k0, tk), pair * LANES:(pair + 1) * LANES]
            lt = lax.dot_general(kp, qh_refs[h][...], _NT, preferred_element_type=F32) - slope * dist
            lt_refs[h][slot] = lt.astype(BF16)
            tile_max = jnp.max(lt, axis=0, keepdims=True).astype(BF16).astype(F32)
            ms.append(jnp.maximum(m_all[h:h + 1, :], tile_max))
        return jnp.concatenate(ms, axis=0)

    def att_body(kt, carry):
        m_prev, m_cur, l_all = carry
        slot = kt % 2
        m_next = logits(jnp.minimum(kt + 1, nkt - 1), 1 - slot, m_cur)
        ls = []
        for h in range(N_HEADS):
            alpha = jnp.exp(m_prev[h:h + 1, :] - m_cur[h:h + 1, :])
            p = jnp.exp(lt_refs[h][slot] - m_cur[h:h + 1, :].astype(BF16))
            pv = jnp.dot(vt_ref[kt, h * VT_HEAD_ROWS:(h + 1) * VT_HEAD_ROWS, :], p, preferred_element_type=F32)
            ls.append(alpha * l_all[h:h + 1, :] + pv[HEAD_DIM:HEAD_DIM + 1, :])
            acc_refs[h][...] = alpha * acc_refs[h][...] + pv[:HEAD_DIM, :]
        return m_cur, m_next, jnp.concatenate(ls, axis=0)

    m_init = jnp.full((N_HEADS, tq), -jnp.inf, F32)
    _, _, l_all = lax.fori_loop(0, nkt, att_body, (m_init, logits(0, 0, m_init), jnp.zeros((N_HEADS, tq), F32)))
    out_t = jnp.concatenate([acc_refs[h][...] / l_all[h:h + 1, :] for h in range(N_HEADS)], axis=0)
    o_ref[...] = out_t.T.astype(BF16)


def _attend_prompt(q, iq, iwt, kb, vt, ikb, n_seq, seq_len, topk):
    n = q.shape[0]
    t = ATT_TILE
    nq = seq_len // t
    once = pl.Buffered(1)
    return pl.pallas_call(
        functools.partial(_attend_prompt_kernel, topk=topk),
        out_shape=jax.ShapeDtypeStruct((n, ATT_WIDTH), BF16),
        grid=(n_seq, nq),
        in_specs=[pl.BlockSpec((t, ATT_WIDTH), lambda b, j: (b * nq + j, 0)),
                  pl.BlockSpec((t, ATT_WIDTH), lambda b, j: (b * nq + j, 0)),
                  pl.BlockSpec((N_IDX_HEADS, t), lambda b, j: (0, b * nq + j)),
                  pl.BlockSpec((seq_len, ATT_WIDTH), lambda b, j: (b, 0), pipeline_mode=once),
                  pl.BlockSpec((nq, VT_ROWS, t), lambda b, j: (b, 0, 0), pipeline_mode=once),
                  pl.BlockSpec((seq_len, LANES), lambda b, j: (b, 0), pipeline_mode=once)],
        out_specs=pl.BlockSpec((t, ATT_WIDTH), lambda b, j: (b * nq + j, 0)),
        scratch_shapes=([pltpu.VMEM((nq, t, t), I32), pltpu.VMEM((nq, t, t), BF16)]
                        + [pltpu.VMEM((t, LANES), BF16)] * N_HEADS
                        + [pltpu.VMEM((HEAD_DIM, t), F32)] * N_HEADS + [pltpu.VMEM((2, t, t), BF16)] * N_HEADS),
        compiler_params=_cparams("arbitrary", "arbitrary"),
        name="attend_prompt",
    )(q, iq, iwt, kb, vt, ikb)


def _attend_sample_kernel(q_ref, iq_ref, iwl_ref, kn_ref, vn_ref, ikn_ref, ck_ref, cv_ref, ci_ref,
                          o_ref, keys_ref, plane_ref, acc_ref, *, topk, past):
    tq = q_ref.shape[0]
    nkt, tk, w = keys_ref.shape
    nct = nkt - 1
    row = lax.broadcasted_iota(I32, (tk, w), 0)
    lane = lax.broadcasted_iota(I32, (tk, w), 1)
    qpos = past + (lane & (tq - 1))
    lane_head = lax.shift_right_logical(lax.broadcasted_iota(I32, (1, w), 1), tq.bit_length() - 1)
    slope = lax.bitcast_convert_type(lax.shift_left(126 - lane_head, 23), F32)

    iq = iq_ref[...]
    q = q_ref[...]
    iq_rows = jnp.concatenate([iq[:, h * IDX_DIM:(h + 1) * IDX_DIM] for h in range(N_IDX_HEADS)], axis=0)
    q_rows = jnp.concatenate(
        [_half_mask_wide(q, h) for h in range(N_HEADS)], axis=0) * jnp.asarray(ATTN_SCALE, BF16)
    iwl = iwl_ref[...]
    pad = tk - tq
    kn = jnp.concatenate([kn_ref[...], jnp.zeros((pad, ATT_WIDTH), BF16)], axis=0)
    vn = jnp.concatenate([vn_ref[...].astype(BF16), jnp.zeros((pad, ATT_WIDTH), BF16)], axis=0)
    ikn = jnp.concatenate([ikn_ref[:, :IDX_DIM], jnp.zeros((pad, IDX_DIM), BF16)], axis=0)

    def score_tile(ik_tile, kpos, valid):
        r = lax.dot_general(ik_tile, iq_rows, _NT, preferred_element_type=F32)
        s = jnp.maximum(r, 0.0) * iwl
        s = s + pltpu.roll(s, w // 2, 1)
        s = s + pltpu.roll(s, w // 4, 1)
        s = s + pltpu.roll(s, w // 8, 1)
        adm = lax.shift_right_logical(kpos, CHUNK_SHIFT) <= lax.shift_right_logical(qpos, CHUNK_SHIFT)
        return jnp.where(adm & valid, s, -jnp.inf)

    def put_scores(kt, s):
        key = _mono_key(s)
        keys_ref[kt] = key
        plane_ref[kt] = _digit_plane(key, 0)

    def score_body(kt, _):
        k0 = pl.multiple_of(kt * tk, tk)
        put_scores(kt, score_tile(ci_ref[pl.ds(k0, tk), :].astype(BF16), k0 + row, True))
        return 0

    lax.fori_loop(0, nct, score_body, 0, unroll=_static_unroll(nct))
    put_scores(nct, score_tile(ikn, past + row, row < tq))

    d_last = _topk_select(keys_ref, plane_ref, nkt, topk)

    def logit_tile(kt, k_tile, kpos):
        dist = jnp.where(_selected(keys_ref, plane_ref, kt, d_last), jnp.abs(qpos - kpos).astype(F32), MASKED_DIST)
        lt = lax.dot_general(k_tile, q_rows, _NT, preferred_element_type=F32) - slope * dist
        keys_ref[kt] = lax.bitcast_convert_type(lt, I32)
        return jnp.max(lt, axis=0, keepdims=True)

    def logit_body(kt, m):
        k0 = pl.multiple_of(kt * tk, tk)
        return jnp.maximum(m, logit_tile(kt, ck_ref[pl.ds(k0, tk), :].astype(BF16), k0 + row))

    m = lax.fori_loop(0, nct, logit_body, jnp.full((1, w), -jnp.inf, F32), unroll=_static_unroll(nct))
    m = jnp.maximum(m, logit_tile(nct, kn, past + row))

    acc_ref[...] = jnp.zeros_like(acc_ref)

    def pv_tile(kt, v_tile):
        p = jnp.exp(lax.bitcast_convert_type(keys_ref[kt], F32) - m)
        acc_ref[...] += jnp.dot(p.T.astype(BF16), v_tile, preferred_element_type=F32)
        return jnp.sum(p, axis=0, keepdims=True)

    def pv_body(kt, l):
        k0 = pl.multiple_of(kt * tk, tk)
        return l + pv_tile(kt, cv_ref[pl.ds(k0, tk), :].astype(BF16))

    l = lax.fori_loop(0, nct, pv_body, jnp.zeros((1, w), F32), unroll=_static_unroll(nct))
    l = l + pv_tile(nct, vn)

    l_col = jnp.broadcast_to(l, (w, w)).T
    out_lane_head = lax.shift_right_logical(lax.broadcasted_iota(I32, (tq, ATT_WIDTH), 1), 6)
    y = jnp.zeros((tq, ATT_WIDTH), F32)
    for h in range(N_HEADS):
        blk = acc_ref[h * tq:(h + 1) * tq, :] / jnp.concatenate([l_col[h * tq:(h + 1) * tq, :]] * (ATT_WIDTH // w), axis=1)
        y = y + jnp.where(out_lane_head == h, blk, 0.0)
    o_ref[...] = y.astype(BF16)


def _half_mask_wide(x, head):
    lane = lax.broadcasted_iota(I32, x.shape, 1)
    keep = lax.shift_right_logical(lane, 6) == head
    return jnp.where(keep, x, jnp.zeros_like(x))


def _attend_sample(q, iq, iwl, kb, vf, ikb, cache_k, cache_v, cache_i, topk):
    n_seq, past, _ = cache_k.shape
    tq = q.shape[0] // n_seq
    tk = ATT_TILE
    w = N_HEADS * tq
    assert w == LANES and past % tk == 0
    new = lambda width: pl.BlockSpec((tq, width), lambda b: (b, 0))
    cache = lambda width: pl.BlockSpec((None, past, width), lambda b: (b, 0, 0))
    return pl.pallas_call(
        functools.partial(_attend_sample_kernel, topk=topk, past=past),
        out_shape=jax.ShapeDtypeStruct((n_seq * tq, ATT_WIDTH), BF16),
        grid=(n_seq,),
        in_specs=[new(ATT_WIDTH), new(ATT_WIDTH), pl.BlockSpec((None, 1, w), lambda b: (b, 0, 0)),
                  new(ATT_WIDTH), new(ATT_WIDTH), new(LANES),
                  cache(ATT_WIDTH), cache(ATT_WIDTH), cache(IDX_DIM)],
        out_specs=new(ATT_WIDTH),
        scratch_shapes=[pltpu.VMEM((past // tk + 1, tk, w), I32), pltpu.VMEM((past // tk + 1, tk, w), BF16),
                        pltpu.VMEM((w, ATT_WIDTH), F32)],
        compiler_params=_cparams("arbitrary"),
        name="attend_sample",
    )(q, iq, iwl, kb, vf, ikb, cache_k, cache_v, cache_i)


TOK_ROWS = D_MODEL // LANES


def _store_token_tiles(ref, x):
    t = x.shape[0]
    for c in range(TOK_ROWS):
        ref[pl.ds(c, t, stride=TOK_ROWS), :] = x[:, c * LANES:(c + 1) * LANES]


def _load_token_tiles(ref):
    t = ref.shape[0] // TOK_ROWS
    return jnp.concatenate([ref[pl.ds(c, t, stride=TOK_ROWS), :] for c in range(TOK_ROWS)], axis=1)


def _layer_norm(x, g, b):
    mu = jnp.mean(x, axis=-1, keepdims=True)
    xc = x - mu
    var = jnp.mean(xc * xc, axis=-1, keepdims=True)
    return xc * lax.rsqrt(var + LN_EPS) * g + b


def _post_kernel(x_ref, ya_ref, yb_ref, ga_ref, scf_ref, shf_ref, wo_ref, bo_ref, g1_ref, b1_ref,
                 wrh_ref, wrl_ref, br_ref,
                 x1_ref, h2_ref, tope_ref, gate_ref, rank_ref, cnt_ref):
    t = x_ref.shape[0]
    y = (jnp.dot(ya_ref[...], wo_ref[:A_WIDTH, :], preferred_element_type=F32)
         + jnp.dot(yb_ref[...], wo_ref[A_WIDTH:, :], preferred_element_type=F32) + bo_ref[...])
    x1 = _layer_norm(DEEPNORM_ALPHA * x_ref[...] + (1.0 + ga_ref[...]) * y, g1_ref[...], b1_ref[...])
    x1_ref[...] = x1
    h2 = x1 * (1.0 + scf_ref[...]) + shf_ref[...]
    _store_token_tiles(h2_ref, h2)

    hh = h2.astype(BF16)
    hl = (h2 - hh.astype(F32)).astype(BF16)
    logits = (lax.dot_general(wrh_ref[...], hh, _NT, preferred_element_type=F32)
              + lax.dot_general(wrh_ref[...], hl, _NT, preferred_element_type=F32)
              + lax.dot_general(wrl_ref[...], hh, _NT, preferred_element_type=F32) + br_ref[...])
    erow = lax.broadcasted_iota(I32, (N_EXPERTS, t), 0)
    vals, idxs = [], []
    for _ in range(TOP_K):
        v = jnp.max(logits, axis=0, keepdims=True)
        i = jnp.min(jnp.where(logits == v, erow, N_EXPERTS), axis=0, keepdims=True)
        vals.append(v)
        idxs.append(i)
        logits = jnp.where(erow == i, -jnp.inf, logits)
    ex = [jnp.exp(v - vals[0]) for v in vals]
    den = ex[0] + ex[1] + ex[2] + ex[3]
    gate_ref[...] = jnp.concatenate([e / den for e in ex], axis=0)
    tope_ref[...] = jnp.concatenate(idxs, axis=0)

    @pl.when(pl.program_id(0) == 0)
    def _():
        cnt_ref[...] = jnp.zeros_like(cnt_ref)

    hit = jnp.zeros((N_EXPERTS, t), F32)
    for i in idxs:
        hit = hit + jnp.where(erow == i, 1.0, 0.0)
    hitb = hit.astype(BF16)
    r = lax.broadcasted_iota(I32, (t, t), 0)
    c = lax.broadcasted_iota(I32, (t, t), 1)
    earlier = jnp.where(r < c, 1.0, 0.0).astype(BF16)
    before = jnp.dot(hitb, earlier, preferred_element_type=F32) + cnt_ref[...]
    total = jnp.dot(hitb, jnp.ones((t, t), BF16), preferred_element_type=F32)
    rank_ref[...] = jnp.concatenate(
        [jnp.sum(jnp.where(erow == i, before, 0.0), axis=0, keepdims=True) for i in idxs], axis=0).astype(I32)
    cnt_ref[...] += total


def _post(x, ya, yb, ga, scf, shf, w_o, b_o, g1, b1, wrh, wrl, br):
    n, d = x.shape
    t = POST_TILE
    nt = n // t
    if ga.shape[0] == n:
        mod_spec = pl.BlockSpec((t, d), lambda i: (i, 0))
    else:
        tiles_per_seq = nt // ga.shape[0]
        ga, scf, shf = (a.reshape(a.shape[0], 1, d) for a in (ga, scf, shf))
        mod_spec = pl.BlockSpec((None, 1, d), lambda i: (i // tiles_per_seq, 0, 0))
    row = lambda w: pl.BlockSpec((t, w), lambda i: (i, 0))
    col = lambda r: pl.BlockSpec((r, t), lambda i: (0, i))
    const = lambda a: pl.BlockSpec(a.shape, lambda i: (0, 0))
    return pl.pallas_call(
        _post_kernel,
        out_shape=(jax.ShapeDtypeStruct((n, d), F32), jax.ShapeDtypeStruct((n * TOK_ROWS, LANES), F32),
                   jax.ShapeDtypeStruct((TOP_K, n), I32), jax.ShapeDtypeStruct((TOP_K, n), F32),
                   jax.ShapeDtypeStruct((TOP_K, n), I32), jax.ShapeDtypeStruct((N_EXPERTS, t), F32)),
        grid=(nt,),
        in_specs=[row(d), row(A_WIDTH), row(ATT_WIDTH), mod_spec, mod_spec, mod_spec,
                  const(w_o), const(b_o), const(g1), const(b1), const(wrh), const(wrl), const(br)],
        out_specs=(row(d), pl.BlockSpec((t * TOK_ROWS, LANES), lambda i: (i, 0)), col(TOP_K), col(TOP_K), col(TOP_K),
                   pl.BlockSpec((N_EXPERTS, t), lambda i: (0, 0))),
        compiler_params=_cparams("arbitrary"),
        name="post",
    )(x, ya, yb, ga, scf, shf, w_o, b_o, g1, b1, wrh, wrl, br)


def _token_rows(ref, r):
    return ref.at[pl.ds(pl.multiple_of(r * TOK_ROWS, TOK_ROWS), TOK_ROWS)]


def _token_copies_wait(hbm_ref, vmem_ref, sem, n_tokens):
    rows = n_tokens * TOK_ROWS
    pltpu.make_async_copy(hbm_ref.at[pl.ds(0, rows)], vmem_ref.at[pl.ds(0, rows)], sem).wait()


def _dispatch_kernel(zrow_ref, dest_ref, h_ref, xp_ref, zero_ref, sem):
    t = h_ref.shape[0] // TOK_ROWS

    block_rows = MOE_BLOCK * TOK_ROWS

    def zero_block(slot0):
        z0 = pl.multiple_of(slot0 * TOK_ROWS, block_rows)
        cp = pltpu.make_async_copy(zero_ref, xp_ref.at[pl.ds(z0, block_rows)], sem)
        cp.start()
        cp.wait()

    @pl.when(pl.program_id(0) == 0)
    def _():
        zero_ref[...] = jnp.zeros_like(zero_ref)
        for e in range(N_EXPERTS):
            @pl.when(zrow_ref[e] >= 0)
            def _():
                zero_block(zrow_ref[e])

        def unused(b, _):
            zero_block(b * MOE_BLOCK)
            return 0

        lax.fori_loop(zrow_ref[N_EXPERTS], xp_ref.shape[0] // block_rows, unused, 0)

    def body(i, _):
        for k in range(TOP_K):
            pltpu.make_async_copy(_token_rows(h_ref, i), _token_rows(xp_ref, dest_ref[k, i]), sem).start(priority=k % 2)
        return 0

    lax.fori_loop(0, t, body, 0, unroll=DMA_ISSUE_UNROLL)
    for k in range(TOP_K):
        _token_copies_wait(xp_ref, h_ref, sem, t)


def _dispatch(zrow, dest, h2, n_slots):
    t = ROW_TILE
    n = h2.shape[0] // TOK_ROWS
    grid_spec = pltpu.PrefetchScalarGridSpec(
        num_scalar_prefetch=1,
        grid=(n // t,),
        in_specs=[pl.BlockSpec((TOP_K, t), lambda i, z: (0, i), memory_space=pltpu.SMEM),
                  pl.BlockSpec((t * TOK_ROWS, LANES), lambda i, z: (i, 0))],
        out_specs=pl.BlockSpec(memory_space=pl.ANY),
        scratch_shapes=[pltpu.VMEM((MOE_BLOCK * TOK_ROWS, LANES), F32), pltpu.SemaphoreType.DMA],
    )
    return pl.pallas_call(
        _dispatch_kernel,
        out_shape=jax.ShapeDtypeStruct((n_slots * TOK_ROWS, LANES), F32),
        grid_spec=grid_spec,
        compiler_params=_cparams("arbitrary"),
        name="dispatch",
    )(zrow, dest, h2)


def _experts_kernel(be_ref, bi_ref, nu_ref, x_ref, wg_ref, bg_ref, wu_ref, bu_ref, wd_ref, bd_ref, y_ref,
                    wgb_ref, wub_ref, wdb_ref):
    i = pl.program_id(0)

    @pl.when((i == 0) | (be_ref[i] != be_ref[jnp.maximum(i - 1, 0)]))
    def _():
        wgb_ref[...] = wg_ref[...].astype(BF16)
        wub_ref[...] = wu_ref[...].astype(BF16)
        wdb_ref[...] = wd_ref[...].astype(BF16)

    @pl.when(i < nu_ref[0])
    def _():
        x = _load_token_tiles(x_ref).astype(BF16)
        g = jnp.minimum(jnp.dot(x, wgb_ref[...], preferred_element_type=F32) + bg_ref[...], SWIGLU_LIMIT)
        u = jnp.clip(jnp.dot(x, wub_ref[...], preferred_element_type=F32) + bu_ref[...], -SWIGLU_LIMIT, SWIGLU_LIMIT)
        a = g * jax.nn.sigmoid(SWIGLU_ALPHA * g)
        mid = ((u + 1.0) * a).astype(BF16)
        _store_token_tiles(y_ref, jnp.dot(mid, wdb_ref[...], preferred_element_type=F32) + bd_ref[...])

    @pl.when(pl.program_id(0) >= nu_ref[0])
    def _():
        y_ref[...] = jnp.zeros_like(y_ref)


def _experts(block_e, block_i, n_used, xp, wg, bg, wu, bu, wd, bd):
    d, f = wg.shape[1], wg.shape[2]
    nb = xp.shape[0] // (MOE_BLOCK * TOK_ROWS)
    wspec = lambda a, b: pl.BlockSpec((None, a, b), lambda i, be, bi, nu: (be[i], 0, 0))
    slots = pl.BlockSpec((MOE_BLOCK * TOK_ROWS, LANES), lambda i, be, bi, nu: (bi[i], 0))
    grid_spec = pltpu.PrefetchScalarGridSpec(
        num_scalar_prefetch=3,
        grid=(nb,),
        in_specs=[slots, wspec(d, f), wspec(1, f), wspec(d, f), wspec(1, f), wspec(f, d), wspec(1, d)],
        out_specs=pl.BlockSpec((MOE_BLOCK * TOK_ROWS, LANES), lambda i, be, bi, nu: (i, 0)),
        scratch_shapes=[pltpu.VMEM((d, f), BF16), pltpu.VMEM((d, f), BF16), pltpu.VMEM((f, d), BF16)],
    )
    return pl.pallas_call(
        _experts_kernel,
        out_shape=jax.ShapeDtypeStruct(xp.shape, F32),
        grid_spec=grid_spec,
        compiler_params=_cparams("arbitrary"),
        name="experts",
    )(block_e, block_i, n_used, xp, wg, bg, wu, bu, wd, bd)


def _combine_kernel(dest_ref, dest_next_ref, gate_ref, x1_ref, gf_ref, g2_ref, b2_ref, yp_ref, o_ref, buf_ref, sems):
    t = x1_ref.shape[0]
    step = pl.program_id(0)
    slot = step % 2

    def start_gather(dst_ref, into):
        def body(i, _):
            for k in range(TOP_K):
                pltpu.make_async_copy(_token_rows(yp_ref, dst_ref[k, i]), _token_rows(buf_ref.at[into, k], i),
                                      sems.at[into]).start(priority=k % 2)
            return 0

        lax.fori_loop(0, t, body, 0, unroll=DMA_ISSUE_UNROLL)

    @pl.when(step == 0)
    def _():
        start_gather(dest_ref, 0)

    @pl.when(step + 1 < pl.num_programs(0))
    def _():
        start_gather(dest_next_ref, 1 - slot)

    for k in range(TOP_K):
        _token_copies_wait(yp_ref, buf_ref.at[slot, k], sems.at[slot], t)

    gates = jnp.concatenate([gate_ref[...], jnp.zeros((LANES - TOP_K, t), F32)], axis=0).T
    f = gates[:, 0:1] * _load_token_tiles(buf_ref.at[slot, 0])
    for k in range(1, TOP_K):
        f = f + gates[:, k:k + 1] * _load_token_tiles(buf_ref.at[slot, k])
    o_ref[...] = _layer_norm(DEEPNORM_ALPHA * x1_ref[...] + (1.0 + gf_ref[...]) * f, g2_ref[...], b2_ref[...])


def _combine(dest, gates, x1, gf, g2, b2, yp):
    n, d = x1.shape
    t = ROW_TILE
    nt = n // t
    if gf.shape[0] == n:
        mod_spec = pl.BlockSpec((t, d), lambda i: (i, 0))
    else:
        tiles_per_seq = nt // gf.shape[0]
        gf = gf.reshape(gf.shape[0], 1, d)
        mod_spec = pl.BlockSpec((None, 1, d), lambda i: (i // tiles_per_seq, 0, 0))
    return pl.pallas_call(
        _combine_kernel,
        out_shape=jax.ShapeDtypeStruct((n, d), F32),
        grid=(nt,),
        in_specs=[pl.BlockSpec((TOP_K, t), lambda i: (0, i), memory_space=pltpu.SMEM),
                  pl.BlockSpec((TOP_K, t), lambda i: (0, jnp.minimum(i + 1, nt - 1)), memory_space=pltpu.SMEM),
                  pl.BlockSpec((TOP_K, t), lambda i: (0, i)),
                  pl.BlockSpec((t, d), lambda i: (i, 0)), mod_spec,
                  pl.BlockSpec((1, d), lambda i: (0, 0)), pl.BlockSpec((1, d), lambda i: (0, 0)),
                  pl.BlockSpec(memory_space=pl.ANY)],
        out_specs=pl.BlockSpec((t, d), lambda i: (i, 0)),
        scratch_shapes=[pltpu.VMEM((2, TOP_K, t * TOK_ROWS, LANES), F32), pltpu.SemaphoreType.DMA((2,))],
        compiler_params=_cparams("arbitrary"),
        name="combine",
    )(dest, dest, gates, x1, gf, g2, b2, yp)


def _moe(h2, tope, rank, counts, gates, x1, gf, g2, b2, experts_w):
    n = tope.shape[1]
    nb = (n * TOP_K + N_EXPERTS * (MOE_BLOCK - 1) + MOE_BLOCK - 1) // MOE_BLOCK
    pcounts = (counts + MOE_BLOCK - 1) // MOE_BLOCK * MOE_BLOCK
    pend = jnp.cumsum(pcounts)
    pstart = pend - pcounts
    eids = jnp.arange(N_EXPERTS, dtype=I32).reshape(N_EXPERTS, 1, 1)
    dest = rank + jnp.sum(jnp.where(tope[None] == eids, pstart.reshape(N_EXPERTS, 1, 1), 0), axis=0).astype(I32)
    n_used = (pend[-1] // MOE_BLOCK).astype(I32)
    zrow = jnp.concatenate([jnp.where(counts > 0, pend - MOE_BLOCK, -1), n_used.reshape(1)]).astype(I32)
    blk = jnp.arange(nb, dtype=I32)
    block_i = jnp.minimum(blk, n_used - 1)
    block_e = jnp.minimum(jnp.sum(pend[None, :] <= (block_i * MOE_BLOCK)[:, None], axis=1), N_EXPERTS - 1).astype(I32)
    xp = _dispatch(zrow, dest, h2, nb * MOE_BLOCK)
    yp = _experts(block_e, block_i, n_used.reshape(1), xp, *experts_w)
    return _combine(dest, gates, x1, gf, g2, b2, yp)


def kernel(x_prompt, x_sample, c_prompt, c_sample, cache_k, cache_v, cache_kidx, w_in, a_ln_g, a_ln_b, a_ws, a_bs,
           w_o, b_o, w_c, b_c, ln1_g, ln1_b, ln2_g, ln2_b, w_router, b_router, w_gate, b_gate, w_up, b_up,
           w_down, b_down):
    bp, s, d = x_prompt.shape
    bs, ts, _ = x_sample.shape
    past = cache_k.shape[2]
    np_, ns = bp * s, bs * ts

    wi = w_in[0]
    w_in_p = jnp.zeros((d, IN_COLS_PAD), F32)
    w_in_p = w_in_p.at[:, :OFF_IK].set(wi[:, :OFF_IK])
    w_in_p = w_in_p.at[:, OFF_IK:OFF_IK + IDX_DIM].set(wi[:, OFF_IK:OFF_IK + IDX_DIM])
    w_in_p = w_in_p.at[:, OFF_IK + IDX_DIM:OFF_IK + 2 * IDX_DIM].set(wi[:, OFF_IK:OFF_IK + IDX_DIM])
    w_in_p = w_in_p.at[:, OFF_IW:OFF_IW + N_IDX_HEADS].set(wi[:, OFF_IK + IDX_DIM:OFF_IK + IDX_DIM + N_IDX_HEADS])
    w_in_p = w_in_p.astype(BF16)
    lng, lnb = a_ln_g[0].reshape(1, A_WIDTH), a_ln_b[0].reshape(1, A_WIDTH)
    wtril = jnp.tril(a_ws[0])
    wm_p = wtril.astype(BF16)
    bm_p = jnp.broadcast_to(a_bs[0][:, :, None], (A_GROUPS, A_CHUNK, A_GROUP_DIM)).astype(F32)
    rep = A_CHUNK // ts
    wm_s = jnp.einsum("ab,gij->gaibj", jnp.eye(rep, dtype=F32), wtril[:, :ts, :ts]).reshape(
        A_GROUPS, A_CHUNK, A_CHUNK).astype(BF16)
    bm_s = jnp.broadcast_to(jnp.tile(a_bs[0][:, :ts], (1, rep))[:, :, None], (A_GROUPS, A_CHUNK, A_GROUP_DIM)).astype(F32)
    w_o_b = w_o[0].astype(BF16)
    b_o_r = b_o[0].reshape(1, d)
    wr_t = w_router[0].T
    wrh = wr_t.astype(BF16)
    wrl = (wr_t - wrh.astype(F32)).astype(BF16)
    br = jnp.broadcast_to(b_router[0][:, None], (N_EXPERTS, POST_TILE)).astype(F32)
    experts_w = (w_gate[0], b_gate[0][:, None, :], w_up[0], b_up[0][:, None, :], w_down[0], b_down[0][:, None, :])
    g1, b1 = ln1_g[0].reshape(1, d), ln1_b[0].reshape(1, d)
    g2, b2 = ln2_g[0].reshape(1, d), ln2_b[0].reshape(1, d)

    mods = _cond_mods(jnp.concatenate([c_prompt, c_sample], axis=0), w_c[0], b_c[0]).reshape(bp + bs, 6, d)
    mods_p = [mods[:bp, i] for i in range(6)]
    mods_s = [jnp.repeat(mods[bp:, i], ts, axis=0) for i in range(6)]

    xp2 = x_prompt.reshape(np_, d)
    ya, q, kf, kb, vf, vt, iq, ikf, ikb, iwt, _ = _project(xp2, mods_p[1], mods_p[0], w_in_p, lng, lnb, wm_p, bm_p)
    yb = _attend_prompt(q, iq, iwt, kb, vt, ikb, bp, s, min(TOPK_MAX, s // 4))
    x1, h2, tope, gates, rank, cnt = _post(xp2, ya, yb, mods_p[2], mods_p[4], mods_p[3], w_o_b, b_o_r, g1, b1, wrh, wrl, br)
    y_p = _moe(h2, tope, rank, cnt[:, 0].astype(I32), gates, x1, mods_p[5], g2, b2, experts_w)
    out_p = (y_p.reshape(bp, s, d), kf.reshape(1, bp, s, N_HEADS, HEAD_DIM), vf.reshape(1, bp, s, N_HEADS, HEAD_DIM),
             ikf.reshape(1, bp, s, IDX_DIM))

    xs2 = x_sample.reshape(ns, d)
    ya, q, kf, kb, vf, vt, iq, ikf, ikb, iwt, va = _project(xs2, mods_s[1], mods_s[0], w_in_p, lng, lnb, wm_s, bm_s)
    iwl = jnp.transpose(iwt.reshape(N_IDX_HEADS, bs, ts), (1, 0, 2)).reshape(bs, 1, N_IDX_HEADS * ts)
    yb = _attend_sample(q, iq, iwl, kb, vf, ikb, cache_k[0].reshape(bs, past, ATT_WIDTH),
                        cache_v[0].reshape(bs, past, ATT_WIDTH), cache_kidx[0], min(TOPK_MAX, (past + ts) // 4))
    x1, h2, tope, gates, rank, cnt = _post(xs2, ya, yb, mods_s[2], mods_s[4], mods_s[3], w_o_b, b_o_r, g1, b1, wrh, wrl, br)
    y_s = _moe(h2, tope, rank, cnt[:, 0].astype(I32), gates, x1, mods_s[5], g2, b2, experts_w)

    return (out_p[0], y_s.reshape(bs, ts, d), out_p[1], out_p[2], out_p[3],
            kf.reshape(1, bs, ts, N_HEADS, HEAD_DIM), vf.reshape(1, bs, ts, N_HEADS, HEAD_DIM),
            ikf.reshape(1, bs, ts, IDX_DIM), va.reshape(1, bs, ts, A_WIDTH))
```

```python
import functools

import jax
import jax.numpy as jnp
from jax import lax
from jax.experimental import pallas as pl
from jax.experimental.pallas import tpu as pltpu

F32 = jnp.float32
BF16 = jnp.bfloat16
I32 = jnp.int32

D_MODEL = 1024
CHUNK_SHIFT = 6
A_GROUPS = 4
A_GROUP_DIM = 128
A_WIDTH = A_GROUPS * A_GROUP_DIM
A_CHUNK = 128
N_HEADS = 8
HEAD_DIM = 64
ATT_WIDTH = N_HEADS * HEAD_DIM
N_IDX_HEADS = 8
IDX_DIM = 64
TOPK_MAX = 256
ATTN_SCALE = HEAD_DIM ** -0.5
VT_HEAD_ROWS = HEAD_DIM + 16
VT_ROWS = N_HEADS * VT_HEAD_ROWS
IDX_W_SCALE = (N_IDX_HEADS ** -0.5) * (IDX_DIM ** -0.5)
N_EXPERTS = 32
TOP_K = 4
SWIGLU_LIMIT = 7.0
SWIGLU_ALPHA = 1.702
DEEPNORM_ALPHA = 2.0 ** 0.25
LN_EPS = 1e-5

LANES = 128
SUBLANES = 8
VMEM_LIMIT_BYTES = 56 * 1024 * 1024

ROW_TILE = 256
POST_TILE = 512
MOE_BLOCK = 256
ATT_TILE = 256
DMA_ISSUE_UNROLL = 8
STATIC_TILE_UNROLL = 4

OFF_AU, OFF_AV, OFF_Q, OFF_K, OFF_V, OFF_IQ = 0, 512, 1024, 1536, 2048, 2560
OFF_IK = 3072
OFF_IW = 3200
IN_COLS_PAD = 3328

MASKED_DIST = 3.0e32
F32_MAX = 3.4028234663852886e38
INT32_MIN = -(2 ** 31)
KEY_NEG_INF = INT32_MIN + 0x7FFFFF

_NT = (((1,), (1,)), ((), ()))


def _cparams(*sem):
    return pltpu.CompilerParams(dimension_semantics=sem, vmem_limit_bytes=VMEM_LIMIT_BYTES)


def _mods_kernel(c_ref, w_ref, b_ref, o_ref):
    c = c_ref[...]
    s = c * jax.nn.sigmoid(c)
    o_ref[...] = jnp.dot(s.astype(BF16), w_ref[...].astype(BF16), preferred_element_type=F32) + b_ref[...]


def _cond_mods(c, w_c, b_c):
    nb, d = c.shape
    n_out = w_c.shape[1]
    return pl.pallas_call(
        _mods_kernel,
        out_shape=jax.ShapeDtypeStruct((nb, n_out), F32),
        grid=(n_out // d,),
        in_specs=[pl.BlockSpec((nb, d), lambda j: (0, 0)),
                  pl.BlockSpec((d, d), lambda j: (0, j)),
                  pl.BlockSpec((1, d), lambda j: (0, j))],
        out_specs=pl.BlockSpec((nb, d), lambda j: (0, j)),
        compiler_params=_cparams("arbitrary"),
        name="mods",
    )(c, w_c, b_c.reshape(1, n_out))


def _gelu(x):
    return 0.5 * x * (1.0 + lax.erf(x * 0.7071067811865476))


def _project_kernel(x_ref, sc_ref, sh_ref, w_ref, lng_ref, lnb_ref, wm_ref, bm_ref,
                    ya_ref, q_ref, kf_ref, kb_ref, vf_ref, vt_ref, iq_ref, ikf_ref, ikb_ref, iwt_ref, va_ref):
    t = x_ref.shape[0]
    h = (x_ref[...] * (1.0 + sc_ref[...]) + sh_ref[...]).astype(BF16)

    def proj(c0, n):
        return jnp.dot(h, w_ref[:, c0:c0 + n], preferred_element_type=F32)

    u = _gelu(proj(OFF_AU, A_WIDTH))
    gv = _gelu(proj(OFF_AV, A_WIDTH))
    for g in range(A_GROUPS):
        lo, hi = g * A_GROUP_DIM, (g + 1) * A_GROUP_DIM
        xg = gv[:, lo:hi]
        mu = jnp.mean(xg, axis=-1, keepdims=True)
        xc = xg - mu
        var = jnp.mean(xc * xc, axis=-1, keepdims=True)
        vg = xc * lax.rsqrt(var + LN_EPS) * lng_ref[:, lo:hi] + lnb_ref[:, lo:hi]
        va_ref[:, lo:hi] = vg
        vgb = vg.astype(BF16)
        for c in range(t // A_CHUNK):
            r0, r1 = c * A_CHUNK, (c + 1) * A_CHUNK
            mixed = jnp.dot(wm_ref[g], vgb[r0:r1, :], preferred_element_type=F32) + bm_ref[g]
            ya_ref[r0:r1, lo:hi] = (u[r0:r1, lo:hi] * mixed).astype(BF16)

    q_ref[...] = proj(OFF_Q, ATT_WIDTH).astype(BF16)
    k = proj(OFF_K, ATT_WIDTH)
    kf_ref[...] = k
    kb_ref[...] = k.astype(BF16)
    v = proj(OFF_V, ATT_WIDTH)
    vf_ref[...] = v
    v_t = v.T.astype(BF16)
    ones = jnp.ones((VT_HEAD_ROWS - HEAD_DIM, t), BF16)
    vt_ref[0] = jnp.concatenate(
        [blk for h in range(N_HEADS) for blk in (v_t[h * HEAD_DIM:(h + 1) * HEAD_DIM, :], ones)], axis=0)
    iq_ref[...] = proj(OFF_IQ, N_IDX_HEADS * IDX_DIM).astype(BF16)
    ik2 = proj(OFF_IK, LANES)
    ikf_ref[...] = ik2[:, :IDX_DIM]
    ikb_ref[...] = ik2.astype(BF16)
    iw = proj(OFF_IW, LANES) * IDX_W_SCALE
    iwt_ref[...] = iw.T[:N_IDX_HEADS, :]


def _project(x, sc, sh, w_in_p, lng, lnb, wm, bm):
    n, d = x.shape
    t = ROW_TILE
    nt = n // t
    if sc.shape[0] == n:
        mod_spec = pl.BlockSpec((t, d), lambda i: (i, 0))
    else:
        tiles_per_seq = nt // sc.shape[0]
        sc = sc.reshape(sc.shape[0], 1, d)
        sh = sh.reshape(sh.shape[0], 1, d)
        mod_spec = pl.BlockSpec((None, 1, d), lambda i: (i // tiles_per_seq, 0, 0))
    row = lambda w: pl.BlockSpec((t, w), lambda i: (i, 0))
    const2 = lambda a: pl.BlockSpec(a.shape, lambda i: (0, 0))
    const3 = lambda a: pl.BlockSpec(a.shape, lambda i: (0, 0, 0))
    out_shape = (
        jax.ShapeDtypeStruct((n, A_WIDTH), BF16),
        jax.ShapeDtypeStruct((n, ATT_WIDTH), BF16),
        jax.ShapeDtypeStruct((n, ATT_WIDTH), F32),
        jax.ShapeDtypeStruct((n, ATT_WIDTH), BF16),
        jax.ShapeDtypeStruct((n, ATT_WIDTH), F32),
        jax.ShapeDtypeStruct((nt, VT_ROWS, t), BF16),
        jax.ShapeDtypeStruct((n, ATT_WIDTH), BF16),
        jax.ShapeDtypeStruct((n, IDX_DIM), F32),
        jax.ShapeDtypeStruct((n, LANES), BF16),
        jax.ShapeDtypeStruct((N_IDX_HEADS, n), F32),
        jax.ShapeDtypeStruct((n, A_WIDTH), F32),
    )
    out_specs = (row(A_WIDTH), row(ATT_WIDTH), row(ATT_WIDTH), row(ATT_WIDTH), row(ATT_WIDTH),
                 pl.BlockSpec((1, VT_ROWS, t), lambda i: (i, 0, 0)),
                 row(ATT_WIDTH), row(IDX_DIM), row(LANES),
                 pl.BlockSpec((N_IDX_HEADS, t), lambda i: (0, i)),
                 row(A_WIDTH))
    return pl.pallas_call(
        _project_kernel,
        out_shape=out_shape,
        grid=(nt,),
        in_specs=[row(d), mod_spec, mod_spec, const2(w_in_p), const2(lng), const2(lnb), const3(wm), const3(bm)],
        out_specs=out_specs,
        compiler_params=_cparams("arbitrary"),
        name="project",
    )(x, sc, sh, w_in_p, lng, lnb, wm, bm)


DIGIT_BITS = 8
N_DIGITS = 32 // DIGIT_BITS
DIGIT_ABOVE = 512.0
DIGIT_BELOW = -1.0
PACKED_ROWS = 16


def _static_unroll(trips):
    return STATIC_TILE_UNROLL if isinstance(trips, int) else 1


def _tile_loop(nkt, body, init):
    if isinstance(nkt, int):
        return lax.fori_loop(0, nkt, body, init, unroll=STATIC_TILE_UNROLL)
    pairs = lax.shift_right_logical(nkt, 1)
    carry = lax.fori_loop(0, pairs, lambda i, c: body(2 * i + 1, body(2 * i, c)), init)
    return lax.fori_loop(2 * pairs, nkt, body, carry)


def _mono_key(x):
    b = lax.bitcast_convert_type(x, I32)
    return jnp.where(b >= 0, b, b ^ jnp.int32(0x7FFFFFFF))


def _digit_plane(key, phase):
    shift = 32 - DIGIT_BITS * (phase + 1)
    d = lax.shift_right_arithmetic(key, jnp.int32(shift)) if shift else key
    d = d + (1 << (DIGIT_BITS - 1)) if phase == 0 else d & ((1 << DIGIT_BITS) - 1)
    return d.astype(F32).astype(BF16)


def _count_plane(plane_ref, nkt, cand, strict):
    _, tk, w = plane_ref.shape
    cb = cand.astype(BF16)
    one, zero = jnp.ones((), BF16), jnp.zeros((), BF16)

    def body(kt, cnt):
        e = plane_ref[kt]
        accs = [jnp.zeros((PACKED_ROWS, w), BF16) for _ in range(4)]
        for r in range(tk // PACKED_ROWS):
            blk = e[r * PACKED_ROWS:(r + 1) * PACKED_ROWS, :]
            accs[r % 4] = accs[r % 4] + jnp.where((blk > cb) if strict else (blk >= cb), one, zero)
        return cnt + ((accs[0] + accs[1]) + (accs[2] + accs[3])).astype(F32)

    cnt = _tile_loop(nkt, body, jnp.zeros((PACKED_ROWS, w), F32))
    return jnp.sum(cnt, axis=0, keepdims=True)


def _search_digit(plane_ref, nkt, topk):
    w = plane_ref.shape[2]

    def bit_body(i, d):
        cand = d + lax.shift_left(jnp.int32(1), jnp.int32(DIGIT_BITS - 1) - i).astype(F32)
        cnt = _count_plane(plane_ref, nkt, cand, strict=False)
        return jnp.where(cnt >= float(topk), cand, d)

    return lax.fori_loop(0, DIGIT_BITS, bit_body, jnp.zeros((1, w), F32))


def _topk_select(keys_ref, plane_ref, nkt, topk):
    _, tk, w = plane_ref.shape
    d = _search_digit(plane_ref, nkt, topk)
    for phase in range(1, N_DIGITS):
        db = d.astype(BF16)

        def refine(kt, _, phase=phase, db=db):
            e = plane_ref[kt]
            decided = jnp.where(e > db, jnp.asarray(DIGIT_ABOVE, BF16), jnp.asarray(DIGIT_BELOW, BF16))
            plane_ref[kt] = jnp.where(e == db, _digit_plane(keys_ref[kt], phase), decided)
            return 0

        _tile_loop(nkt, refine, 0)
        d = _search_digit(plane_ref, nkt, topk)

    cnt_ge = _count_plane(plane_ref, nkt, d, strict=False)

    @pl.when(jnp.max(cnt_ge) > float(topk))
    def _():
        need = float(topk) - _count_plane(plane_ref, nkt, d, strict=True)
        r = lax.broadcasted_iota(I32, (tk, tk), 0)
        c = lax.broadcasted_iota(I32, (tk, tk), 1)
        before = jnp.where(c < r, 1.0, 0.0).astype(BF16)

        def body(kt, seen):
            e = plane_ref[kt].astype(F32)
            eq = e == d
            eqf = jnp.where(eq, 1.0, 0.0)
            prior = jnp.dot(before, eqf.astype(BF16), preferred_element_type=F32) + seen
            plane_ref[kt] = jnp.where(eq & (prior >= need), DIGIT_BELOW, e).astype(BF16)
            return seen + jnp.sum(eqf.reshape(tk // SUBLANES, SUBLANES, w), axis=0).sum(axis=0, keepdims=True)

        lax.fori_loop(0, nkt, body, jnp.zeros((1, w), F32))

    return d


def _selected(keys_ref, plane_ref, kt, d):
    return (plane_ref[kt].astype(F32) >= d) & (keys_ref[kt] > jnp.int32(KEY_NEG_INF))


def _half_mask(x_pair, head):
    lane = lax.broadcasted_iota(I32, x_pair.shape, 1)
    keep = (lane >= HEAD_DIM) if head % 2 else (lane < HEAD_DIM)
    return jnp.where(keep, x_pair, jnp.zeros_like(x_pair))


def _attend_prompt_kernel(q_ref, iq_ref, iwt_ref, k_ref, vt_ref, ik_ref, o_ref, keys_ref, plane_ref, *head_refs, topk):
    qh_refs, acc_refs, lt_refs = (head_refs[i * N_HEADS:(i + 1) * N_HEADS] for i in range(3))
    tq = q_ref.shape[0]
    tk = keys_ref.shape[1]
    j = pl.program_id(1)
    nkt = j + 1
    q0 = j * tq
    row = lax.broadcasted_iota(I32, (tk, tq), 0)
    lane = lax.broadcasted_iota(I32, (tk, tq), 1)
    qpos = q0 + lane

    iq = iq_ref[...]
    iqm = [_half_mask(iq[:, (h // 2) * LANES:(h // 2 + 1) * LANES], h) for h in range(N_IDX_HEADS)]
    iw = iwt_ref[...]

    def score_tile(kt, diagonal):
        k0 = pl.multiple_of(kt * tk, tk)
        ikt = ik_ref[pl.ds(k0, tk), :]
        s = jnp.zeros((tk, tq), F32)
        for h in range(N_IDX_HEADS):
            r = lax.dot_general(ikt, iqm[h], _NT, preferred_element_type=F32)
            s = s + jnp.maximum(r, 0.0) * iw[h:h + 1, :]
        if diagonal:
            adm = lax.shift_right_logical(k0 + row, CHUNK_SHIFT) <= lax.shift_right_logical(qpos, CHUNK_SHIFT)
            s = jnp.where(adm, s, -jnp.inf)
        key = _mono_key(s)
        keys_ref[kt] = key
        plane_ref[kt] = _digit_plane(key, 0)
        return 0

    assert tk == tq
    lax.fori_loop(0, j, lambda kt, _: score_tile(kt, False), 0)
    score_tile(j, True)

    d_last = _topk_select(keys_ref, plane_ref, nkt, topk)

    def dist_body(kt, _):
        kpos = kt * tk + row
        dist = jnp.abs(qpos - kpos).astype(F32)
        masked = jnp.where(_selected(keys_ref, plane_ref, kt, d_last), dist, MASKED_DIST)
        keys_ref[kt] = lax.bitcast_convert_type(masked, I32)
        return 0

    lax.fori_loop(0, nkt, dist_body, 0)

    qfull = q_ref[...]
    for h in range(N_HEADS):
        pair = h // 2
        qh_refs[h][...] = _half_mask(qfull[:, pair * LANES:(pair + 1) * LANES], h) * jnp.asarray(ATTN_SCALE, BF16)
        acc_refs[h][...] = jnp.zeros_like(acc_refs[h])

    def logits(kt, slot, m_all):
        k0 = pl.multiple_of(kt * tk, tk)
        dist = lax.bitcast_convert_type(keys_ref[kt], F32)
        ms = []
        for h in range(N_HEADS):
            pair = h // 2
            slope = 2.0 ** (-8.0 * (h + 1) / N_HEADS)
            kp = k_ref[pl.ds(k0, tk), pair * LANES:(pair + 1) * LANES]
            lt = lax.dot_general(kp, qh_refs[h][...], _NT, preferred_element_type=F32) - slope * dist
            lt_refs[h][slot] = lt.astype(BF16)
            tile_max = jnp.max(lt, axis=0, keepdims=True).astype(BF16).astype(F32)
            ms.append(jnp.maximum(m_all[h:h + 1, :], tile_max))
        return jnp.concatenate(ms, axis=0)

    def att_body(kt, carry):
        m_prev, m_cur, l_all = carry
        slot = kt % 2
        m_next = logits(jnp.minimum(kt + 1, nkt - 1), 1 - slot, m_cur)
        ls = []
        for h in range(N_HEADS):
            alpha = jnp.exp(m_prev[h:h + 1, :] - m_cur[h:h + 1, :])
            p = jnp.exp(lt_refs[h][slot] - m_cur[h:h + 1, :].astype(BF16))
            pv = jnp.dot(vt_ref[kt, h * VT_HEAD_ROWS:(h + 1) * VT_HEAD_ROWS, :], p, preferred_element_type=F32)
            ls.append(alpha * l_all[h:h + 1, :] + pv[HEAD_DIM:HEAD_DIM + 1, :])
            acc_refs[h][...] = alpha * acc_refs[h][...] + pv[:HEAD_DIM, :]
        return m_cur, m_next, jnp.concatenate(ls, axis=0)

    m_init = jnp.full((N_HEADS, tq), -jnp.inf, F32)
    _, _, l_all = lax.fori_loop(0, nkt, att_body, (m_init, logits(0, 0, m_init), jnp.zeros((N_HEADS, tq), F32)))
    out_t = jnp.concatenate([acc_refs[h][...] / l_all[h:h + 1, :] for h in range(N_HEADS)], axis=0)
    o_ref[...] = out_t.T.astype(BF16)


def _attend_prompt(q, iq, iwt, kb, vt, ikb, n_seq, seq_len, topk):
    n = q.shape[0]
    t = ATT_TILE
    nq = seq_len // t
    once = pl.Buffered(1)
    return pl.pallas_call(
        functools.partial(_attend_prompt_kernel, topk=topk),
        out_shape=jax.ShapeDtypeStruct((n, ATT_WIDTH), BF16),
        grid=(n_seq, nq),
        in_specs=[pl.BlockSpec((t, ATT_WIDTH), lambda b, j: (b * nq + j, 0)),
                  pl.BlockSpec((t, ATT_WIDTH), lambda b, j: (b * nq + j, 0)),
                  pl.BlockSpec((N_IDX_HEADS, t), lambda b, j: (0, b * nq + j)),
                  pl.BlockSpec((seq_len, ATT_WIDTH), lambda b, j: (b, 0), pipeline_mode=once),
                  pl.BlockSpec((nq, VT_ROWS, t), lambda b, j: (b, 0, 0), pipeline_mode=once),
                  pl.BlockSpec((seq_len, LANES), lambda b, j: (b, 0), pipeline_mode=once)],
        out_specs=pl.BlockSpec((t, ATT_WIDTH), lambda b, j: (b * nq + j, 0)),
        scratch_shapes=([pltpu.VMEM((nq, t, t), I32), pltpu.VMEM((nq, t, t), BF16)]
                        + [pltpu.VMEM((t, LANES), BF16)] * N_HEADS
                        + [pltpu.VMEM((HEAD_DIM, t), F32)] * N_HEADS + [pltpu.VMEM((2, t, t), BF16)] * N_HEADS),
        compiler_params=_cparams("arbitrary", "arbitrary"),
        name="attend_prompt",
    )(q, iq, iwt, kb, vt, ikb)


def _attend_sample_kernel(q_ref, iq_ref, iwl_ref, kn_ref, vn_ref, ikn_ref, ck_ref, cv_ref, ci_ref,
                          o_ref, keys_ref, plane_ref, acc_ref, *, topk, past):
    tq = q_ref.shape[0]
    nkt, tk, w = keys_ref.shape
    nct = nkt - 1
    row = lax.broadcasted_iota(I32, (tk, w), 0)
    lane = lax.broadcasted_iota(I32, (tk, w), 1)
    qpos = past + (lane & (tq - 1))
    lane_head = lax.shift_right_logical(lax.broadcasted_iota(I32, (1, w), 1), tq.bit_length() - 1)
    slope = lax.bitcast_convert_type(lax.shift_left(126 - lane_head, 23), F32)

    iq = iq_ref[...]
    q = q_ref[...]
    iq_rows = jnp.concatenate([iq[:, h * IDX_DIM:(h + 1) * IDX_DIM] for h in range(N_IDX_HEADS)], axis=0)
    q_rows = jnp.concatenate(
        [_half_mask_wide(q, h) for h in range(N_HEADS)], axis=0) * jnp.asarray(ATTN_SCALE, BF16)
    iwl = iwl_ref[...]
    pad = tk - tq
    kn = jnp.concatenate([kn_ref[...], jnp.zeros((pad, ATT_WIDTH), BF16)], axis=0)
    vn = jnp.concatenate([vn_ref[...].astype(BF16), jnp.zeros((pad, ATT_WIDTH), BF16)], axis=0)
    ikn = jnp.concatenate([ikn_ref[:, :IDX_DIM], jnp.zeros((pad, IDX_DIM), BF16)], axis=0)

    def score_tile(ik_tile, kpos, valid):
        r = lax.dot_general(ik_tile, iq_rows, _NT, preferred_element_type=F32)
        s = jnp.maximum(r, 0.0) * iwl
        s = s + pltpu.roll(s, w // 2, 1)
        s = s + pltpu.roll(s, w // 4, 1)
        s = s + pltpu.roll(s, w // 8, 1)
        adm = lax.shift_right_logical(kpos, CHUNK_SHIFT) <= lax.shift_right_logical(qpos, CHUNK_SHIFT)
        return jnp.where(adm & valid, s, -jnp.inf)

    def put_scores(kt, s):
        key = _mono_key(s)
        keys_ref[kt] = key
        plane_ref[kt] = _digit_plane(key, 0)

    def score_body(kt, _):
        k0 = pl.multiple_of(kt * tk, tk)
        put_scores(kt, score_tile(ci_ref[pl.ds(k0, tk), :].astype(BF16), k0 + row, True))
        return 0

    lax.fori_loop(0, nct, score_body, 0, unroll=_static_unroll(nct))
    put_scores(nct, score_tile(ikn, past + row, row < tq))

    d_last = _topk_select(keys_ref, plane_ref, nkt, topk)

    def logit_tile(kt, k_tile, kpos):
        dist = jnp.where(_selected(keys_ref, plane_ref, kt, d_last), jnp.abs(qpos - kpos).astype(F32), MASKED_DIST)
        lt = lax.dot_general(k_tile, q_rows, _NT, preferred_element_type=F32) - slope * dist
        keys_ref[kt] = lax.bitcast_convert_type(lt, I32)
        return jnp.max(lt, axis=0, keepdims=True)

    def logit_body(kt, m):
        k0 = pl.multiple_of(kt * tk, tk)
        return jnp.maximum(m, logit_tile(kt, ck_ref[pl.ds(k0, tk), :].astype(BF16), k0 + row))

    m = lax.fori_loop(0, nct, logit_body, jnp.full((1, w), -jnp.inf, F32), unroll=_static_unroll(nct))
    m = jnp.maximum(m, logit_tile(nct, kn, past + row))

    acc_ref[...] = jnp.zeros_like(acc_ref)

    def pv_tile(kt, v_tile):
        p = jnp.exp(lax.bitcast_convert_type(keys_ref[kt], F32) - m)
        acc_ref[...] += jnp.dot(p.T.astype(BF16), v_tile, preferred_element_type=F32)
        return jnp.sum(p, axis=0, keepdims=True)

    def pv_body(kt, l):
        k0 = pl.multiple_of(kt * tk, tk)
        return l + pv_tile(kt, cv_ref[pl.ds(k0, tk), :].astype(BF16))

    l = lax.fori_loop(0, nct, pv_body, jnp.zeros((1, w), F32), unroll=_static_unroll(nct))
    l = l + pv_tile(nct, vn)

    l_col = jnp.broadcast_to(l, (w, w)).T
    out_lane_head = lax.shift_right_logical(lax.broadcasted_iota(I32, (tq, ATT_WIDTH), 1), 6)
    y = jnp.zeros((tq, ATT_WIDTH), F32)
    for h in range(N_HEADS):
        blk = acc_ref[h * tq:(h + 1) * tq, :] / jnp.concatenate([l_col[h * tq:(h + 1) * tq, :]] * (ATT_WIDTH // w), axis=1)
        y = y + jnp.where(out_lane_head == h, blk, 0.0)
    o_ref[...] = y.astype(BF16)


def _half_mask_wide(x, head):
    lane = lax.broadcasted_iota(I32, x.shape, 1)
    keep = lax.shift_right_logical(lane, 6) == head
    return jnp.where(keep, x, jnp.zeros_like(x))


def _attend_sample(q, iq, iwl, kb, vf, ikb, cache_k, cache_v, cache_i, topk):
    n_seq, past, _ = cache_k.shape
    tq = q.shape[0] // n_seq
    tk = ATT_TILE
    w = N_HEADS * tq
    assert w == LANES and past % tk == 0
    new = lambda width: pl.BlockSpec((tq, width), lambda b: (b, 0))
    cache = lambda width: pl.BlockSpec((None, past, width), lambda b: (b, 0, 0))
    return pl.pallas_call(
        functools.partial(_attend_sample_kernel, topk=topk, past=past),
        out_shape=jax.ShapeDtypeStruct((n_seq * tq, ATT_WIDTH), BF16),
        grid=(n_seq,),
        in_specs=[new(ATT_WIDTH), new(ATT_WIDTH), pl.BlockSpec((None, 1, w), lambda b: (b, 0, 0)),
                  new(ATT_WIDTH), new(ATT_WIDTH), new(LANES),
                  cache(ATT_WIDTH), cache(ATT_WIDTH), cache(IDX_DIM)],
        out_specs=new(ATT_WIDTH),
        scratch_shapes=[pltpu.VMEM((past // tk + 1, tk, w), I32), pltpu.VMEM((past // tk + 1, tk, w), BF16),
                        pltpu.VMEM((w, ATT_WIDTH), F32)],
        compiler_params=_cparams("arbitrary"),
        name="attend_sample",
    )(q, iq, iwl, kb, vf, ikb, cache_k, cache_v, cache_i)


TOK_ROWS = D_MODEL // LANES


def _store_token_tiles(ref, x):
    t = x.shape[0]
    for c in range(TOK_ROWS):
        ref[pl.ds(c, t, stride=TOK_ROWS), :] = x[:, c * LANES:(c + 1) * LANES]


def _load_token_tiles(ref):
    t = ref.shape[0] // TOK_ROWS
    return jnp.concatenate([ref[pl.ds(c, t, stride=TOK_ROWS), :] for c in range(TOK_ROWS)], axis=1)


def _layer_norm(x, g, b):
    mu = jnp.mean(x, axis=-1, keepdims=True)
    xc = x - mu
    var = jnp.mean(xc * xc, axis=-1, keepdims=True)
    return xc * lax.rsqrt(var + LN_EPS) * g + b


def _post_kernel(x_ref, ya_ref, yb_ref, ga_ref, scf_ref, shf_ref, wo_ref, bo_ref, g1_ref, b1_ref,
                 wrh_ref, wrl_ref, br_ref,
                 x1_ref, h2_ref, tope_ref, gate_ref, rank_ref, cnt_ref):
    t = x_ref.shape[0]
    y = (jnp.dot(ya_ref[...], wo_ref[:A_WIDTH, :], preferred_element_type=F32)
         + jnp.dot(yb_ref[...], wo_ref[A_WIDTH:, :], preferred_element_type=F32) + bo_ref[...])
    x1 = _layer_norm(DEEPNORM_ALPHA * x_ref[...] + (1.0 + ga_ref[...]) * y, g1_ref[...], b1_ref[...])
    x1_ref[...] = x1
    h2 = x1 * (1.0 + scf_ref[...]) + shf_ref[...]
    _store_token_tiles(h2_ref, h2)

    hh = h2.astype(BF16)
    hl = (h2 - hh.astype(F32)).astype(BF16)
    logits = (lax.dot_general(wrh_ref[...], hh, _NT, preferred_element_type=F32)
              + lax.dot_general(wrh_ref[...], hl, _NT, preferred_element_type=F32)
              + lax.dot_general(wrl_ref[...], hh, _NT, preferred_element_type=F32) + br_ref[...])
    erow = lax.broadcasted_iota(I32, (N_EXPERTS, t), 0)
    vals, idxs = [], []
    for _ in range(TOP_K):
        v = jnp.max(logits, axis=0, keepdims=True)
        i = jnp.min(jnp.where(logits == v, erow, N_EXPERTS), axis=0, keepdims=True)
        vals.append(v)
        idxs.append(i)
        logits = jnp.where(erow == i, -jnp.inf, logits)
    ex = [jnp.exp(v - vals[0]) for v in vals]
    den = ex[0] + ex[1] + ex[2] + ex[3]
    gate_ref[...] = jnp.concatenate([e / den for e in ex], axis=0)
    tope_ref[...] = jnp.concatenate(idxs, axis=0)

    @pl.when(pl.program_id(0) == 0)
    def _():
        cnt_ref[...] = jnp.zeros_like(cnt_ref)

    hit = jnp.zeros((N_EXPERTS, t), F32)
    for i in idxs:
        hit = hit + jnp.where(erow == i, 1.0, 0.0)
    hitb = hit.astype(BF16)
    r = lax.broadcasted_iota(I32, (t, t), 0)
    c = lax.broadcasted_iota(I32, (t, t), 1)
    earlier = jnp.where(r < c, 1.0, 0.0).astype(BF16)
    before = jnp.dot(hitb, earlier, preferred_element_type=F32) + cnt_ref[...]
    total = jnp.dot(hitb, jnp.ones((t, t), BF16), preferred_element_type=F32)
    rank_ref[...] = jnp.concatenate(
        [jnp.sum(jnp.where(erow == i, before, 0.0), axis=0, keepdims=True) for i in idxs], axis=0).astype(I32)
    cnt_ref[...] += total


def _post(x, ya, yb, ga, scf, shf, w_o, b_o, g1, b1, wrh, wrl, br):
    n, d = x.shape
    t = POST_TILE
    nt = n // t
    if ga.shape[0] == n:
        mod_spec = pl.BlockSpec((t, d), lambda i: (i, 0))
    else:
        tiles_per_seq = nt // ga.shape[0]
        ga, scf, shf = (a.reshape(a.shape[0], 1, d) for a in (ga, scf, shf))
        mod_spec = pl.BlockSpec((None, 1, d), lambda i: (i // tiles_per_seq, 0, 0))
    row = lambda w: pl.BlockSpec((t, w), lambda i: (i, 0))
    col = lambda r: pl.BlockSpec((r, t), lambda i: (0, i))
    const = lambda a: pl.BlockSpec(a.shape, lambda i: (0, 0))
    return pl.pallas_call(
        _post_kernel,
        out_shape=(jax.ShapeDtypeStruct((n, d), F32), jax.ShapeDtypeStruct((n * TOK_ROWS, LANES), F32),
                   jax.ShapeDtypeStruct((TOP_K, n), I32), jax.ShapeDtypeStruct((TOP_K, n), F32),
                   jax.ShapeDtypeStruct((TOP_K, n), I32), jax.ShapeDtypeStruct((N_EXPERTS, t), F32)),
        grid=(nt,),
        in_specs=[row(d), row(A_WIDTH), row(ATT_WIDTH), mod_spec, mod_spec, mod_spec,
                  const(w_o), const(b_o), const(g1), const(b1), const(wrh), const(wrl), const(br)],
        out_specs=(row(d), pl.BlockSpec((t * TOK_ROWS, LANES), lambda i: (i, 0)), col(TOP_K), col(TOP_K), col(TOP_K),
                   pl.BlockSpec((N_EXPERTS, t), lambda i: (0, 0))),
        compiler_params=_cparams("arbitrary"),
        name="post",
    )(x, ya, yb, ga, scf, shf, w_o, b_o, g1, b1, wrh, wrl, br)


def _token_rows(ref, r):
    return ref.at[pl.ds(pl.multiple_of(r * TOK_ROWS, TOK_ROWS), TOK_ROWS)]


def _token_copies_wait(hbm_ref, vmem_ref, sem, n_tokens):
    rows = n_tokens * TOK_ROWS
    pltpu.make_async_copy(hbm_ref.at[pl.ds(0, rows)], vmem_ref.at[pl.ds(0, rows)], sem).wait()


def _dispatch_kernel(zrow_ref, dest_ref, h_ref, xp_ref, zero_ref, sem):
    t = h_ref.shape[0] // TOK_ROWS

    block_rows = MOE_BLOCK * TOK_ROWS

    def zero_block(slot0):
        z0 = pl.multiple_of(slot0 * TOK_ROWS, block_rows)
        cp = pltpu.make_async_copy(zero_ref, xp_ref.at[pl.ds(z0, block_rows)], sem)
        cp.start()
        cp.wait()

    @pl.when(pl.program_id(0) == 0)
    def _():
        zero_ref[...] = jnp.zeros_like(zero_ref)
        for e in range(N_EXPERTS):
            @pl.when(zrow_ref[e] >= 0)
            def _():
                zero_block(zrow_ref[e])

        def unused(b, _):
            zero_block(b * MOE_BLOCK)
            return 0

        lax.fori_loop(zrow_ref[N_EXPERTS], xp_ref.shape[0] // block_rows, unused, 0)

    def body(i, _):
        for k in range(TOP_K):
            pltpu.make_async_copy(_token_rows(h_ref, i), _token_rows(xp_ref, dest_ref[k, i]), sem).start(priority=k % 2)
        return 0

    lax.fori_loop(0, t, body, 0, unroll=DMA_ISSUE_UNROLL)
    for k in range(TOP_K):
        _token_copies_wait(xp_ref, h_ref, sem, t)


def _dispatch(zrow, dest, h2, n_slots):
    t = ROW_TILE
    n = h2.shape[0] // TOK_ROWS
    grid_spec = pltpu.PrefetchScalarGridSpec(
        num_scalar_prefetch=1,
        grid=(n // t,),
        in_specs=[pl.BlockSpec((TOP_K, t), lambda i, z: (0, i), memory_space=pltpu.SMEM),
                  pl.BlockSpec((t * TOK_ROWS, LANES), lambda i, z: (i, 0))],
        out_specs=pl.BlockSpec(memory_space=pl.ANY),
        scratch_shapes=[pltpu.VMEM((MOE_BLOCK * TOK_ROWS, LANES), F32), pltpu.SemaphoreType.DMA],
    )
    return pl.pallas_call(
        _dispatch_kernel,
        out_shape=jax.ShapeDtypeStruct((n_slots * TOK_ROWS, LANES), F32),
        grid_spec=grid_spec,
        compiler_params=_cparams("arbitrary"),
        name="dispatch",
    )(zrow, dest, h2)


def _experts_kernel(be_ref, bi_ref, nu_ref, x_ref, wg_ref, bg_ref, wu_ref, bu_ref, wd_ref, bd_ref, y_ref,
                    wgb_ref, wub_ref, wdb_ref):
    i = pl.program_id(0)

    @pl.when((i == 0) | (be_ref[i] != be_ref[jnp.maximum(i - 1, 0)]))
    def _():
        wgb_ref[...] = wg_ref[...].astype(BF16)
        wub_ref[...] = wu_ref[...].astype(BF16)
        wdb_ref[...] = wd_ref[...].astype(BF16)

    @pl.when(i < nu_ref[0])
    def _():
        x = _load_token_tiles(x_ref).astype(BF16)
        g = jnp.minimum(jnp.dot(x, wgb_ref[...], preferred_element_type=F32) + bg_ref[...], SWIGLU_LIMIT)
        u = jnp.clip(jnp.dot(x, wub_ref[...], preferred_element_type=F32) + bu_ref[...], -SWIGLU_LIMIT, SWIGLU_LIMIT)
        a = g * jax.nn.sigmoid(SWIGLU_ALPHA * g)
        mid = ((u + 1.0) * a).astype(BF16)
        _store_token_tiles(y_ref, jnp.dot(mid, wdb_ref[...], preferred_element_type=F32) + bd_ref[...])

    @pl.when(pl.program_id(0) >= nu_ref[0])
    def _():
        y_ref[...] = jnp.zeros_like(y_ref)


def _experts(block_e, block_i, n_used, xp, wg, bg, wu, bu, wd, bd):
    d, f = wg.shape[1], wg.shape[2]
    nb = xp.shape[0] // (MOE_BLOCK * TOK_ROWS)
    wspec = lambda a, b: pl.BlockSpec((None, a, b), lambda i, be, bi, nu: (be[i], 0, 0))
    slots = pl.BlockSpec((MOE_BLOCK * TOK_ROWS, LANES), lambda i, be, bi, nu: (bi[i], 0))
    grid_spec = pltpu.PrefetchScalarGridSpec(
        num_scalar_prefetch=3,
        grid=(nb,),
        in_specs=[slots, wspec(d, f), wspec(1, f), wspec(d, f), wspec(1, f), wspec(f, d), wspec(1, d)],
        out_specs=pl.BlockSpec((MOE_BLOCK * TOK_ROWS, LANES), lambda i, be, bi, nu: (i, 0)),
        scratch_shapes=[pltpu.VMEM((d, f), BF16), pltpu.VMEM((d, f), BF16), pltpu.VMEM((f, d), BF16)],
    )
    return pl.pallas_call(
        _experts_kernel,
        out_shape=jax.ShapeDtypeStruct(xp.shape, F32),
        grid_spec=grid_spec,
        compiler_params=_cparams("arbitrary"),
        name="experts",
    )(block_e, block_i, n_used, xp, wg, bg, wu, bu, wd, bd)


def _combine_kernel(dest_ref, dest_next_ref, gate_ref, x1_ref, gf_ref, g2_ref, b2_ref, yp_ref, o_ref, buf_ref, sems):
    t = x1_ref.shape[0]
    step = pl.program_id(0)
    slot = step % 2

    def start_gather(dst_ref, into):
        def body(i, _):
            for k in range(TOP_K):
                pltpu.make_async_copy(_token_rows(yp_ref, dst_ref[k, i]), _token_rows(buf_ref.at[into, k], i),
                                      sems.at[into]).start(priority=k % 2)
            return 0

        lax.fori_loop(0, t, body, 0, unroll=DMA_ISSUE_UNROLL)

    @pl.when(step == 0)
    def _():
        start_gather(dest_ref, 0)

    @pl.when(step + 1 < pl.num_programs(0))
    def _():
        start_gather(dest_next_ref, 1 - slot)

    for k in range(TOP_K):
        _token_copies_wait(yp_ref, buf_ref.at[slot, k], sems.at[slot], t)

    gates = jnp.concatenate([gate_ref[...], jnp.zeros((LANES - TOP_K, t), F32)], axis=0).T
    f = gates[:, 0:1] * _load_token_tiles(buf_ref.at[slot, 0])
    for k in range(1, TOP_K):
        f = f + gates[:, k:k + 1] * _load_token_tiles(buf_ref.at[slot, k])
    o_ref[...] = _layer_norm(DEEPNORM_ALPHA * x1_ref[...] + (1.0 + gf_ref[...]) * f, g2_ref[...], b2_ref[...])


def _combine(dest, gates, x1, gf, g2, b2, yp):
    n, d = x1.shape
    t = ROW_TILE
    nt = n // t
    if gf.shape[0] == n:
        mod_spec = pl.BlockSpec((t, d), lambda i: (i, 0))
    else:
        tiles_per_seq = nt // gf.shape[0]
        gf = gf.reshape(gf.shape[0], 1, d)
        mod_spec = pl.BlockSpec((None, 1, d), lambda i: (i // tiles_per_seq, 0, 0))
    return pl.pallas_call(
        _combine_kernel,
        out_shape=jax.ShapeDtypeStruct((n, d), F32),
        grid=(nt,),
        in_specs=[pl.BlockSpec((TOP_K, t), lambda i: (0, i), memory_space=pltpu.SMEM),
                  pl.BlockSpec((TOP_K, t), lambda i: (0, jnp.minimum(i + 1, nt - 1)), memory_space=pltpu.SMEM),
                  pl.BlockSpec((TOP_K, t), lambda i: (0, i)),
                  pl.BlockSpec((t, d), lambda i: (i, 0)), mod_spec,
                  pl.BlockSpec((1, d), lambda i: (0, 0)), pl.BlockSpec((1, d), lambda i: (0, 0)),
                  pl.BlockSpec(memory_space=pl.ANY)],
        out_specs=pl.BlockSpec((t, d), lambda i: (i, 0)),
        scratch_shapes=[pltpu.VMEM((2, TOP_K, t * TOK_ROWS, LANES), F32), pltpu.SemaphoreType.DMA((2,))],
        compiler_params=_cparams("arbitrary"),
        name="combine",
    )(dest, dest, gates, x1, gf, g2, b2, yp)


def _moe(h2, tope, rank, counts, gates, x1, gf, g2, b2, experts_w):
    n = tope.shape[1]
    nb = (n * TOP_K + N_EXPERTS * (MOE_BLOCK - 1) + MOE_BLOCK - 1) // MOE_BLOCK
    pcounts = (counts + MOE_BLOCK - 1) // MOE_BLOCK * MOE_BLOCK
    pend = jnp.cumsum(pcounts)
    pstart = pend - pcounts
    eids = jnp.arange(N_EXPERTS, dtype=I32).reshape(N_EXPERTS, 1, 1)
    dest = rank + jnp.sum(jnp.where(tope[None] == eids, pstart.reshape(N_EXPERTS, 1, 1), 0), axis=0).astype(I32)
    n_used = (pend[-1] // MOE_BLOCK).astype(I32)
    zrow = jnp.concatenate([jnp.where(counts > 0, pend - MOE_BLOCK, -1), n_used.reshape(1)]).astype(I32)
    blk = jnp.arange(nb, dtype=I32)
    block_i = jnp.minimum(blk, n_used - 1)
    block_e = jnp.minimum(jnp.sum(pend[None, :] <= (block_i * MOE_BLOCK)[:, None], axis=1), N_EXPERTS - 1).astype(I32)
    xp = _dispatch(zrow, dest, h2, nb * MOE_BLOCK)
    yp = _experts(block_e, block_i, n_used.reshape(1), xp, *experts_w)
    return _combine(dest, gates, x1, gf, g2, b2, yp)


def kernel(x_prompt, x_sample, c_prompt, c_sample, cache_k, cache_v, cache_kidx, w_in, a_ln_g, a_ln_b, a_ws, a_bs,
           w_o, b_o, w_c, b_c, ln1_g, ln1_b, ln2_g, ln2_b, w_router, b_router, w_gate, b_gate, w_up, b_up,
           w_down, b_down):
    bp, s, d = x_prompt.shape
    bs, ts, _ = x_sample.shape
    past = cache_k.shape[2]
    np_, ns = bp * s, bs * ts

    wi = w_in[0]
    w_in_p = jnp.zeros((d, IN_COLS_PAD), F32)
    w_in_p = w_in_p.at[:, :OFF_IK].set(wi[:, :OFF_IK])
    w_in_p = w_in_p.at[:, OFF_IK:OFF_IK + IDX_DIM].set(wi[:, OFF_IK:OFF_IK + IDX_DIM])
    w_in_p = w_in_p.at[:, OFF_IK + IDX_DIM:OFF_IK + 2 * IDX_DIM].set(wi[:, OFF_IK:OFF_IK + IDX_DIM])
    w_in_p = w_in_p.at[:, OFF_IW:OFF_IW + N_IDX_HEADS].set(wi[:, OFF_IK + IDX_DIM:OFF_IK + IDX_DIM + N_IDX_HEADS])
    w_in_p = w_in_p.astype(BF16)
    lng, lnb = a_ln_g[0].reshape(1, A_WIDTH), a_ln_b[0].reshape(1, A_WIDTH)
    wtril = jnp.tril(a_ws[0])
    wm_p = wtril.astype(BF16)
    bm_p = jnp.broadcast_to(a_bs[0][:, :, None], (A_GROUPS, A_CHUNK, A_GROUP_DIM)).astype(F32)
    rep = A_CHUNK // ts
    wm_s = jnp.einsum("ab,gij->gaibj", jnp.eye(rep, dtype=F32), wtril[:, :ts, :ts]).reshape(
        A_GROUPS, A_CHUNK, A_CHUNK).astype(BF16)
    bm_s = jnp.broadcast_to(jnp.tile(a_bs[0][:, :ts], (1, rep))[:, :, None], (A_GROUPS, A_CHUNK, A_GROUP_DIM)).astype(F32)
    w_o_b = w_o[0].astype(BF16)
    b_o_r = b_o[0].reshape(1, d)
    wr_t = w_router[0].T
    wrh = wr_t.astype(BF16)
    wrl = (wr_t - wrh.astype(F32)).astype(BF16)
    br = jnp.broadcast_to(b_router[0][:, None], (N_EXPERTS, POST_TILE)).astype(F32)
    experts_w = (w_gate[0], b_gate[0][:, None, :], w_up[0], b_up[0][:, None, :], w_down[0], b_down[0][:, None, :])
    g1, b1 = ln1_g[0].reshape(1, d), ln1_b[0].reshape(1, d)
    g2, b2 = ln2_g[0].reshape(1, d), ln2_b[0].reshape(1, d)

    mods = _cond_mods(jnp.concatenate([c_prompt, c_sample], axis=0), w_c[0], b_c[0]).reshape(bp + bs, 6, d)
    mods_p = [mods[:bp, i] for i in range(6)]
    mods_s = [jnp.repeat(mods[bp:, i], ts, axis=0) for i in range(6)]

    xp2 = x_prompt.reshape(np_, d)
    ya, q, kf, kb, vf, vt, iq, ikf, ikb, iwt, _ = _project(xp2, mods_p[1], mods_p[0], w_in_p, lng, lnb, wm_p, bm_p)
    yb = _attend_prompt(q, iq, iwt, kb, vt, ikb, bp, s, min(TOPK_MAX, s // 4))
    x1, h2, tope, gates, rank, cnt = _post(xp2, ya, yb, mods_p[2], mods_p[4], mods_p[3], w_o_b, b_o_r, g1, b1, wrh, wrl, br)
    y_p = _moe(h2, tope, rank, cnt[:, 0].astype(I32), gates, x1, mods_p[5], g2, b2, experts_w)
    out_p = (y_p.reshape(bp, s, d), kf.reshape(1, bp, s, N_HEADS, HEAD_DIM), vf.reshape(1, bp, s, N_HEADS, HEAD_DIM),
             ikf.reshape(1, bp, s, IDX_DIM))

    xs2 = x_sample.reshape(ns, d)
    ya, q, kf, kb, vf, vt, iq, ikf, ikb, iwt, va = _project(xs2, mods_s[1], mods_s[0], w_in_p, lng, lnb, wm_s, bm_s)
    iwl = jnp.transpose(iwt.reshape(N_IDX_HEADS, bs, ts), (1, 0, 2)).reshape(bs, 1, N_IDX_HEADS * ts)
    yb = _attend_sample(q, iq, iwl, kb, vf, ikb, cache_k[0].reshape(bs, past, ATT_WIDTH),
                        cache_v[0].reshape(bs, past, ATT_WIDTH), cache_kidx[0], min(TOPK_MAX, (past + ts) // 4))
    x1, h2, tope, gates, rank, cnt = _post(xs2, ya, yb, mods_s[2], mods_s[4], mods_s[3], w_o_b, b_o_r, g1, b1, wrh, wrl, br)
    y_s = _moe(h2, tope, rank, cnt[:, 0].astype(I32), gates, x1, mods_s[5], g2, b2, experts_w)

    return (out_p[0], y_s.reshape(bs, ts, d), out_p[1], out_p[2], out_p[3],
            kf.reshape(1, bs, ts, N_HEADS, HEAD_DIM), vf.reshape(1, bs, ts, N_HEADS, HEAD_DIM),
            ikf.reshape(1, bs, ts, IDX_DIM), va.reshape(1, bs, ts, A_WIDTH))
```

```python
import functools

import jax
import jax.numpy as jnp
from jax import lax
from jax.experimental import pallas as pl
from jax.experimental.pallas import tpu as pltpu

F32 = jnp.float32
BF16 = jnp.bfloat16
I32 = jnp.int32

D_MODEL = 1024
CHUNK_SHIFT = 6
A_GROUPS = 4
A_GROUP_DIM = 128
A_WIDTH = A_GROUPS * A_GROUP_DIM
A_CHUNK = 128
N_HEADS = 8
HEAD_DIM = 64
ATT_WIDTH = N_HEADS * HEAD_DIM
N_IDX_HEADS = 8
IDX_DIM = 64
TOPK_MAX = 256
ATTN_SCALE = HEAD_DIM ** -0.5
VT_HEAD_ROWS = HEAD_DIM + 16
VT_ROWS = N_HEADS * VT_HEAD_ROWS
IDX_W_SCALE = (N_IDX_HEADS ** -0.5) * (IDX_DIM ** -0.5)
N_EXPERTS = 32
TOP_K = 4
SWIGLU_LIMIT = 7.0
SWIGLU_ALPHA = 1.702
DEEPNORM_ALPHA = 2.0 ** 0.25
LN_EPS = 1e-5

LANES = 128
SUBLANES = 8
VMEM_LIMIT_BYTES = 56 * 1024 * 1024

ROW_TILE = 256
POST_TILE = 512
MOE_BLOCK = 256
ATT_TILE = 256
DMA_ISSUE_UNROLL = 8
STATIC_TILE_UNROLL = 4

OFF_AU, OFF_AV, OFF_Q, OFF_K, OFF_V, OFF_IQ = 0, 512, 1024, 1536, 2048, 2560
OFF_IK = 3072
OFF_IW = 3200
IN_COLS_PAD = 3328

MASKED_DIST = 3.0e32
F32_MAX = 3.4028234663852886e38
INT32_MIN = -(2 ** 31)
KEY_NEG_INF = INT32_MIN + 0x7FFFFF

_NT = (((1,), (1,)), ((), ()))


def _cparams(*sem):
    return pltpu.CompilerParams(dimension_semantics=sem, vmem_limit_bytes=VMEM_LIMIT_BYTES)


def _mods_kernel(c_ref, w_ref, b_ref, o_ref):
    c = c_ref[...]
    s = c * jax.nn.sigmoid(c)
    o_ref[...] = jnp.dot(s.astype(BF16), w_ref[...].astype(BF16), preferred_element_type=F32) + b_ref[...]


def _cond_mods(c, w_c, b_c):
    nb, d = c.shape
    n_out = w_c.shape[1]
    return pl.pallas_call(
        _mods_kernel,
        out_shape=jax.ShapeDtypeStruct((nb, n_out), F32),
        grid=(n_out // d,),
        in_specs=[pl.BlockSpec((nb, d), lambda j: (0, 0)),
                  pl.BlockSpec((d, d), lambda j: (0, j)),
                  pl.BlockSpec((1, d), lambda j: (0, j))],
        out_specs=pl.BlockSpec((nb, d), lambda j: (0, j)),
        compiler_params=_cparams("arbitrary"),
        name="mods",
    )(c, w_c, b_c.reshape(1, n_out))


def _gelu(x):
    return 0.5 * x * (1.0 + lax.erf(x * 0.7071067811865476))


def _project_kernel(x_ref, sc_ref, sh_ref, w_ref, lng_ref, lnb_ref, wm_ref, bm_ref,
                    ya_ref, q_ref, kft_ref, kb_ref, vf_ref, vft_ref, vt_ref, iq_ref, ikft_ref, ikb_ref, iwt_ref,
                    va_ref):
    t = x_ref.shape[0]
    h = (x_ref[...] * (1.0 + sc_ref[...]) + sh_ref[...]).astype(BF16)

    def proj(c0, n):
        return jnp.dot(h, w_ref[:, c0:c0 + n], preferred_element_type=F32)

    u = _gelu(proj(OFF_AU, A_WIDTH))
    gv = _gelu(proj(OFF_AV, A_WIDTH))
    for g in range(A_GROUPS):
        lo, hi = g * A_GROUP_DIM, (g + 1) * A_GROUP_DIM
        xg = gv[:, lo:hi]
        mu = jnp.mean(xg, axis=-1, keepdims=True)
        xc = xg - mu
        var = jnp.mean(xc * xc, axis=-1, keepdims=True)
        vg = xc * lax.rsqrt(var + LN_EPS) * lng_ref[:, lo:hi] + lnb_ref[:, lo:hi]
        va_ref[:, lo:hi] = vg
        vgb = vg.astype(BF16)
        for c in range(t // A_CHUNK):
            r0, r1 = c * A_CHUNK, (c + 1) * A_CHUNK
            mixed = jnp.dot(wm_ref[g], vgb[r0:r1, :], preferred_element_type=F32) + bm_ref[g]
            ya_ref[r0:r1, lo:hi] = (u[r0:r1, lo:hi] * mixed).astype(BF16)

    q_ref[...] = proj(OFF_Q, ATT_WIDTH).astype(BF16)
    k = proj(OFF_K, ATT_WIDTH)
    kft_ref[...] = k.T
    kb_ref[...] = k.astype(BF16)
    v = proj(OFF_V, ATT_WIDTH)
    vf_ref[...] = v
    v_t32 = v.T
    vft_ref[...] = v_t32
    v_t = v_t32.astype(BF16)
    ones = jnp.ones((VT_HEAD_ROWS - HEAD_DIM, t), BF16)
    vt_ref[0] = jnp.concatenate(
        [blk for h in range(N_HEADS) for blk in (v_t[h * HEAD_DIM:(h + 1) * HEAD_DIM, :], ones)], axis=0)
    iq_ref[...] = proj(OFF_IQ, N_IDX_HEADS * IDX_DIM).astype(BF16)
    ik2 = proj(OFF_IK, LANES)
    ikft_ref[...] = ik2.T[:IDX_DIM, :]
    ikb_ref[...] = ik2.astype(BF16)
    iw = proj(OFF_IW, LANES) * IDX_W_SCALE
    iwt_ref[...] = iw.T[:N_IDX_HEADS, :]


def _project(x, sc, sh, w_in_p, lng, lnb, wm, bm, n_out_seq):
    n, d = x.shape
    t = ROW_TILE
    nt = n // t
    tiles_per_out = nt // n_out_seq
    trans = lambda rows: pl.BlockSpec((None, rows, t), lambda i: (i // tiles_per_out, 0, i % tiles_per_out))
    if sc.shape[0] == n:
        mod_spec = pl.BlockSpec((t, d), lambda i: (i, 0))
    else:
        tiles_per_seq = nt // sc.shape[0]
        sc = sc.reshape(sc.shape[0], 1, d)
        sh = sh.reshape(sh.shape[0], 1, d)
        mod_spec = pl.BlockSpec((None, 1, d), lambda i: (i // tiles_per_seq, 0, 0))
    row = lambda w: pl.BlockSpec((t, w), lambda i: (i, 0))
    const2 = lambda a: pl.BlockSpec(a.shape, lambda i: (0, 0))
    const3 = lambda a: pl.BlockSpec(a.shape, lambda i: (0, 0, 0))
    out_shape = (
        jax.ShapeDtypeStruct((n, A_WIDTH), BF16),
        jax.ShapeDtypeStruct((n, ATT_WIDTH), BF16),
        jax.ShapeDtypeStruct((n_out_seq, ATT_WIDTH, n // n_out_seq), F32),
        jax.ShapeDtypeStruct((n, ATT_WIDTH), BF16),
        jax.ShapeDtypeStruct((n, ATT_WIDTH), F32),
        jax.ShapeDtypeStruct((n_out_seq, ATT_WIDTH, n // n_out_seq), F32),
        jax.ShapeDtypeStruct((nt, VT_ROWS, t), BF16),
        jax.ShapeDtypeStruct((n, ATT_WIDTH), BF16),
        jax.ShapeDtypeStruct((n_out_seq, IDX_DIM, n // n_out_seq), F32),
        jax.ShapeDtypeStruct((n, LANES), BF16),
        jax.ShapeDtypeStruct((N_IDX_HEADS, n), F32),
        jax.ShapeDtypeStruct((n, A_WIDTH), F32),
    )
    out_specs = (row(A_WIDTH), row(ATT_WIDTH), trans(ATT_WIDTH), row(ATT_WIDTH), row(ATT_WIDTH), trans(ATT_WIDTH),
                 pl.BlockSpec((1, VT_ROWS, t), lambda i: (i, 0, 0)),
                 row(ATT_WIDTH), trans(IDX_DIM), row(LANES),
                 pl.BlockSpec((N_IDX_HEADS, t), lambda i: (0, i)),
                 row(A_WIDTH))
    return pl.pallas_call(
        _project_kernel,
        out_shape=out_shape,
        grid=(nt,),
        in_specs=[row(d), mod_spec, mod_spec, const2(w_in_p), const2(lng), const2(lnb), const3(wm), const3(bm)],
        out_specs=out_specs,
        compiler_params=_cparams("arbitrary"),
        name="project",
    )(x, sc, sh, w_in_p, lng, lnb, wm, bm)


DIGIT_BITS = 8
N_DIGITS = 32 // DIGIT_BITS
DIGIT_ABOVE = 512.0
DIGIT_BELOW = -1.0
PACKED_ROWS = 16


def _static_unroll(trips):
    return STATIC_TILE_UNROLL if isinstance(trips, int) else 1


def _tile_loop(nkt, body, init):
    if isinstance(nkt, int):
        return lax.fori_loop(0, nkt, body, init, unroll=STATIC_TILE_UNROLL)
    pairs = lax.shift_right_logical(nkt, 1)
    carry = lax.fori_loop(0, pairs, lambda i, c: body(2 * i + 1, body(2 * i, c)), init)
    return lax.fori_loop(2 * pairs, nkt, body, carry)


def _mono_key(x):
    b = lax.bitcast_convert_type(x, I32)
    return jnp.where(b >= 0, b, b ^ jnp.int32(0x7FFFFFFF))


def _digit_plane(key, phase):
    shift = 32 - DIGIT_BITS * (phase + 1)
    d = lax.shift_right_arithmetic(key, jnp.int32(shift)) if shift else key
    d = d + (1 << (DIGIT_BITS - 1)) if phase == 0 else d & ((1 << DIGIT_BITS) - 1)
    return d.astype(F32).astype(BF16)


def _count_plane(plane_ref, nkt, cand, strict):
    _, tk, w = plane_ref.shape
    cb = cand.astype(BF16)
    one, zero = jnp.ones((), BF16), jnp.zeros((), BF16)

    def body(kt, cnt):
        e = plane_ref[kt]
        accs = [jnp.zeros((PACKED_ROWS, w), BF16) for _ in range(4)]
        for r in range(tk // PACKED_ROWS):
            blk = e[r * PACKED_ROWS:(r + 1) * PACKED_ROWS, :]
            accs[r % 4] = accs[r % 4] + jnp.where((blk > cb) if strict else (blk >= cb), one, zero)
        return cnt + ((accs[0] + accs[1]) + (accs[2] + accs[3])).astype(F32)

    cnt = _tile_loop(nkt, body, jnp.zeros((PACKED_ROWS, w), F32))
    return jnp.sum(cnt, axis=0, keepdims=True)


def _search_digit(plane_ref, nkt, topk):
    w = plane_ref.shape[2]

    def bit_body(i, d):
        cand = d + lax.shift_left(jnp.int32(1), jnp.int32(DIGIT_BITS - 1) - i).astype(F32)
        cnt = _count_plane(plane_ref, nkt, cand, strict=False)
        return jnp.where(cnt >= float(topk), cand, d)

    return lax.fori_loop(0, DIGIT_BITS, bit_body, jnp.zeros((1, w), F32))


def _topk_select(keys_ref, plane_ref, nkt, topk):
    _, tk, w = plane_ref.shape
    d = _search_digit(plane_ref, nkt, topk)
    for phase in range(1, N_DIGITS):
        db = d.astype(BF16)

        def refine(kt, _, phase=phase, db=db):
            e = plane_ref[kt]
            decided = jnp.where(e > db, jnp.asarray(DIGIT_ABOVE, BF16), jnp.asarray(DIGIT_BELOW, BF16))
            plane_ref[kt] = jnp.where(e == db, _digit_plane(keys_ref[kt], phase), decided)
            return 0

        _tile_loop(nkt, refine, 0)
        d = _search_digit(plane_ref, nkt, topk)

    cnt_ge = _count_plane(plane_ref, nkt, d, strict=False)

    @pl.when(jnp.max(cnt_ge) > float(topk))
    def _():
        need = float(topk) - _count_plane(plane_ref, nkt, d, strict=True)
        r = lax.broadcasted_iota(I32, (tk, tk), 0)
        c = lax.broadcasted_iota(I32, (tk, tk), 1)
        before = jnp.where(c < r, 1.0, 0.0).astype(BF16)

        def body(kt, seen):
            e = plane_ref[kt].astype(F32)
            eq = e == d
            eqf = jnp.where(eq, 1.0, 0.0)
            prior = jnp.dot(before, eqf.astype(BF16), preferred_element_type=F32) + seen
            plane_ref[kt] = jnp.where(eq & (prior >= need), DIGIT_BELOW, e).astype(BF16)
            return seen + jnp.sum(eqf.reshape(tk // SUBLANES, SUBLANES, w), axis=0).sum(axis=0, keepdims=True)

        lax.fori_loop(0, nkt, body, jnp.zeros((1, w), F32))

    return d


def _selected(keys_ref, plane_ref, kt, d):
    return (plane_ref[kt].astype(F32) >= d) & (keys_ref[kt] > jnp.int32(KEY_NEG_INF))


def _half_mask(x_pair, head):
    lane = lax.broadcasted_iota(I32, x_pair.shape, 1)
    keep = (lane >= HEAD_DIM) if head % 2 else (lane < HEAD_DIM)
    return jnp.where(keep, x_pair, jnp.zeros_like(x_pair))


def _attend_prompt_kernel(q_ref, iq_ref, iwt_ref, k_ref, vt_ref, ik_ref, o_ref, keys_ref, plane_ref, *head_refs, topk):
    qh_refs, acc_refs, lt_refs = (head_refs[i * N_HEADS:(i + 1) * N_HEADS] for i in range(3))
    tq = q_ref.shape[0]
    tk = keys_ref.shape[1]
    j = pl.program_id(1)
    nkt = j + 1
    q0 = j * tq
    row = lax.broadcasted_iota(I32, (tk, tq), 0)
    lane = lax.broadcasted_iota(I32, (tk, tq), 1)
    qpos = q0 + lane

    iq = iq_ref[...]
    iqm = [_half_mask(iq[:, (h // 2) * LANES:(h // 2 + 1) * LANES], h) for h in range(N_IDX_HEADS)]
    iw = iwt_ref[...]

    def score_tile(kt, diagonal):
        k0 = pl.multiple_of(kt * tk, tk)
        ikt = ik_ref[pl.ds(k0, tk), :]
        s = jnp.zeros((tk, tq), F32)
        for h in range(N_IDX_HEADS):
            r = lax.dot_general(ikt, iqm[h], _NT, preferred_element_type=F32)
            s = s + jnp.maximum(r, 0.0) * iw[h:h + 1, :]
        if diagonal:
            adm = lax.shift_right_logical(k0 + row, CHUNK_SHIFT) <= lax.shift_right_logical(qpos, CHUNK_SHIFT)
            s = jnp.where(adm, s, -jnp.inf)
        key = _mono_key(s)
        keys_ref[kt] = key
        plane_ref[kt] = _digit_plane(key, 0)
        return 0

    assert tk == tq
    lax.fori_loop(0, j, lambda kt, _: score_tile(kt, False), 0)
    score_tile(j, True)

    d_last = _topk_select(keys_ref, plane_ref, nkt, topk)

    def dist_body(kt, _):
        kpos = kt * tk + row
        dist = jnp.abs(qpos - kpos).astype(F32)
        masked = jnp.where(_selected(keys_ref, plane_ref, kt, d_last), dist, MASKED_DIST)
        keys_ref[kt] = lax.bitcast_convert_type(masked, I32)
        return 0

    lax.fori_loop(0, nkt, dist_body, 0)

    qfull = q_ref[...]
    for h in range(N_HEADS):
        pair = h // 2
        qh_refs[h][...] = _half_mask(qfull[:, pair * LANES:(pair + 1) * LANES], h) * jnp.asarray(ATTN_SCALE, BF16)
        acc_refs[h][...] = jnp.zeros_like(acc_refs[h])

    def logits(kt, slot, m_all):
        k0 = pl.multiple_of(kt * tk, tk)
        dist = lax.bitcast_convert_type(keys_ref[kt], F32)
        ms = []
        for h in range(N_HEADS):
            pair = h // 2
            slope = 2.0 ** (-8.0 * (h + 1) / N_HEADS)
            kp = k_ref[pl.ds(k0, tk), pair * LANES:(pair + 1) * LANES]
            lt = lax.dot_general(kp, qh_refs[h][...], _NT, preferred_element_type=F32) - slope * dist
            lt_refs[h][slot] = lt.astype(BF16)
            tile_max = jnp.max(lt, axis=0, keepdims=True).astype(BF16).astype(F32)
            ms.append(jnp.maximum(m_all[h:h + 1, :], tile_max))
        return jnp.concatenate(ms, axis=0)

    def att_body(kt, carry):
        m_prev, m_cur, l_all = carry
        slot = kt % 2
        m_next = logits(jnp.minimum(kt + 1, nkt - 1), 1 - slot, m_cur)
        ls = []
        for h in range(N_HEADS):
            alpha = jnp.exp(m_prev[h:h + 1, :] - m_cur[h:h + 1, :])
            p = jnp.exp(lt_refs[h][slot] - m_cur[h:h + 1, :].astype(BF16))
            pv = jnp.dot(vt_ref[kt, h * VT_HEAD_ROWS:(h + 1) * VT_HEAD_ROWS, :], p, preferred_element_type=F32)
            ls.append(alpha * l_all[h:h + 1, :] + pv[HEAD_DIM:HEAD_DIM + 1, :])
            acc_refs[h][...] = alpha * acc_refs[h][...] + pv[:HEAD_DIM, :]
        return m_cur, m_next, jnp.concatenate(ls, axis=0)

    m_init = jnp.full((N_HEADS, tq), -jnp.inf, F32)
    _, _, l_all = lax.fori_loop(0, nkt, att_body, (m_init, logits(0, 0, m_init), jnp.zeros((N_HEADS, tq), F32)))
    out_t = jnp.concatenate([acc_refs[h][...] / l_all[h:h + 1, :] for h in range(N_HEADS)], axis=0)
    o_ref[...] = out_t.T.astype(BF16)


def _attend_prompt(q, iq, iwt, kb, vt, ikb, n_seq, seq_len, topk):
    n = q.shape[0]
    t = ATT_TILE
    nq = seq_len // t
    once = pl.Buffered(1)
    return pl.pallas_call(
        functools.partial(_attend_prompt_kernel, topk=topk),
        out_shape=jax.ShapeDtypeStruct((n, ATT_WIDTH), BF16),
        grid=(n_seq, nq),
        in_specs=[pl.BlockSpec((t, ATT_WIDTH), lambda b, j: (b * nq + j, 0)),
                  pl.BlockSpec((t, ATT_WIDTH), lambda b, j: (b * nq + j, 0)),
                  pl.BlockSpec((N_IDX_HEADS, t), lambda b, j: (0, b * nq + j)),
                  pl.BlockSpec((seq_len, ATT_WIDTH), lambda b, j: (b, 0), pipeline_mode=once),
                  pl.BlockSpec((nq, VT_ROWS, t), lambda b, j: (b, 0, 0), pipeline_mode=once),
                  pl.BlockSpec((seq_len, LANES), lambda b, j: (b, 0), pipeline_mode=once)],
        out_specs=pl.BlockSpec((t, ATT_WIDTH), lambda b, j: (b * nq + j, 0)),
        scratch_shapes=([pltpu.VMEM((nq, t, t), I32), pltpu.VMEM((nq, t, t), BF16)]
                        + [pltpu.VMEM((t, LANES), BF16)] * N_HEADS
                        + [pltpu.VMEM((HEAD_DIM, t), F32)] * N_HEADS + [pltpu.VMEM((2, t, t), BF16)] * N_HEADS),
        compiler_params=_cparams("arbitrary", "arbitrary"),
        name="attend_prompt",
    )(q, iq, iwt, kb, vt, ikb)


def _attend_sample_kernel(q_ref, iq_ref, iwl_ref, kn_ref, vn_ref, ikn_ref, ck_ref, cv_ref, ci_ref,
                          o_ref, keys_ref, plane_ref, acc_ref, *, topk, past):
    tq = q_ref.shape[0]
    nkt, tk, w = keys_ref.shape
    nct = nkt - 1
    row = lax.broadcasted_iota(I32, (tk, w), 0)
    lane = lax.broadcasted_iota(I32, (tk, w), 1)
    qpos = past + (lane & (tq - 1))
    lane_head = lax.shift_right_logical(lax.broadcasted_iota(I32, (1, w), 1), tq.bit_length() - 1)
    slope = lax.bitcast_convert_type(lax.shift_left(126 - lane_head, 23), F32)

    iq = iq_ref[...]
    q = q_ref[...]
    iq_rows = jnp.concatenate([iq[:, h * IDX_DIM:(h + 1) * IDX_DIM] for h in range(N_IDX_HEADS)], axis=0)
    q_rows = jnp.concatenate(
        [_half_mask_wide(q, h) for h in range(N_HEADS)], axis=0) * jnp.asarray(ATTN_SCALE, BF16)
    iwl = iwl_ref[...]
    pad = tk - tq
    kn = jnp.concatenate([kn_ref[...], jnp.zeros((pad, ATT_WIDTH), BF16)], axis=0)
    vn = jnp.concatenate([vn_ref[...].astype(BF16), jnp.zeros((pad, ATT_WIDTH), BF16)], axis=0)
    ikn = jnp.concatenate([ikn_ref[:, :IDX_DIM], jnp.zeros((pad, IDX_DIM), BF16)], axis=0)

    def score_tile(ik_tile, kpos, valid):
        r = lax.dot_general(ik_tile, iq_rows, _NT, preferred_element_type=F32)
        s = jnp.maximum(r, 0.0) * iwl
        s = s + pltpu.roll(s, w // 2, 1)
        s = s + pltpu.roll(s, w // 4, 1)
        s = s + pltpu.roll(s, w // 8, 1)
        adm = lax.shift_right_logical(kpos, CHUNK_SHIFT) <= lax.shift_right_logical(qpos, CHUNK_SHIFT)
        return jnp.where(adm & valid, s, -jnp.inf)

    def put_scores(kt, s):
        key = _mono_key(s)
        keys_ref[kt] = key
        plane_ref[kt] = _digit_plane(key, 0)

    def score_body(kt, _):
        k0 = pl.multiple_of(kt * tk, tk)
        put_scores(kt, score_tile(ci_ref[pl.ds(k0, tk), :].astype(BF16), k0 + row, True))
        return 0

    lax.fori_loop(0, nct, score_body, 0, unroll=_static_unroll(nct))
    put_scores(nct, score_tile(ikn, past + row, row < tq))

    d_last = _topk_select(keys_ref, plane_ref, nkt, topk)

    def logit_tile(kt, k_tile, kpos):
        dist = jnp.where(_selected(keys_ref, plane_ref, kt, d_last), jnp.abs(qpos - kpos).astype(F32), MASKED_DIST)
        lt = lax.dot_general(k_tile, q_rows, _NT, preferred_element_type=F32) - slope * dist
        keys_ref[kt] = lax.bitcast_convert_type(lt, I32)
        return jnp.max(lt, axis=0, keepdims=True)

    def logit_body(kt, m):
        k0 = pl.multiple_of(kt * tk, tk)
        return jnp.maximum(m, logit_tile(kt, ck_ref[pl.ds(k0, tk), :].astype(BF16), k0 + row))

    m = lax.fori_loop(0, nct, logit_body, jnp.full((1, w), -jnp.inf, F32), unroll=_static_unroll(nct))
    m = jnp.maximum(m, logit_tile(nct, kn, past + row))

    acc_ref[...] = jnp.zeros_like(acc_ref)

    def pv_tile(kt, v_tile):
        p = jnp.exp(lax.bitcast_convert_type(keys_ref[kt], F32) - m)
        acc_ref[...] += jnp.dot(p.T.astype(BF16), v_tile, preferred_element_type=F32)
        return jnp.sum(p, axis=0, keepdims=True)

    def pv_body(kt, l):
        k0 = pl.multiple_of(kt * tk, tk)
        return l + pv_tile(kt, cv_ref[pl.ds(k0, tk), :].astype(BF16))

    l = lax.fori_loop(0, nct, pv_body, jnp.zeros((1, w), F32), unroll=_static_unroll(nct))
    l = l + pv_tile(nct, vn)

    l_col = jnp.broadcast_to(l, (w, w)).T
    out_lane_head = lax.shift_right_logical(lax.broadcasted_iota(I32, (tq, ATT_WIDTH), 1), 6)
    y = jnp.zeros((tq, ATT_WIDTH), F32)
    for h in range(N_HEADS):
        blk = acc_ref[h * tq:(h + 1) * tq, :] / jnp.concatenate([l_col[h * tq:(h + 1) * tq, :]] * (ATT_WIDTH // w), axis=1)
        y = y + jnp.where(out_lane_head == h, blk, 0.0)
    o_ref[...] = y.astype(BF16)


def _half_mask_wide(x, head):
    lane = lax.broadcasted_iota(I32, x.shape, 1)
    keep = lax.shift_right_logical(lane, 6) == head
    return jnp.where(keep, x, jnp.zeros_like(x))


def _attend_sample(q, iq, iwl, kb, vf, ikb, cache_k, cache_v, cache_i, topk):
    n_seq, past, _ = cache_k.shape
    tq = q.shape[0] // n_seq
    tk = ATT_TILE
    w = N_HEADS * tq
    assert w == LANES and past % tk == 0
    new = lambda width: pl.BlockSpec((tq, width), lambda b: (b, 0))
    cache = lambda width: pl.BlockSpec((None, past, width), lambda b: (b, 0, 0))
    return pl.pallas_call(
        functools.partial(_attend_sample_kernel, topk=topk, past=past),
        out_shape=jax.ShapeDtypeStruct((n_seq * tq, ATT_WIDTH), BF16),
        grid=(n_seq,),
        in_specs=[new(ATT_WIDTH), new(ATT_WIDTH), pl.BlockSpec((None, 1, w), lambda b: (b, 0, 0)),
                  new(ATT_WIDTH), new(ATT_WIDTH), new(LANES),
                  cache(ATT_WIDTH), cache(ATT_WIDTH), cache(IDX_DIM)],
        out_specs=new(ATT_WIDTH),
        scratch_shapes=[pltpu.VMEM((past // tk + 1, tk, w), I32), pltpu.VMEM((past // tk + 1, tk, w), BF16),
                        pltpu.VMEM((w, ATT_WIDTH), F32)],
        compiler_params=_cparams("arbitrary"),
        name="attend_sample",
    )(q, iq, iwl, kb, vf, ikb, cache_k, cache_v, cache_i)


TOK_ROWS = D_MODEL // LANES


def _store_token_tiles(ref, x):
    t = x.shape[0]
    for c in range(TOK_ROWS):
        ref[pl.ds(c, t, stride=TOK_ROWS), :] = x[:, c * LANES:(c + 1) * LANES]


def _load_token_tiles(ref):
    t = ref.shape[0] // TOK_ROWS
    return jnp.concatenate([ref[pl.ds(c, t, stride=TOK_ROWS), :] for c in range(TOK_ROWS)], axis=1)


def _layer_norm(x, g, b):
    mu = jnp.mean(x, axis=-1, keepdims=True)
    xc = x - mu
    var = jnp.mean(xc * xc, axis=-1, keepdims=True)
    return xc * lax.rsqrt(var + LN_EPS) * g + b


def _post_kernel(x_ref, ya_ref, yb_ref, ga_ref, scf_ref, shf_ref, wo_ref, bo_ref, g1_ref, b1_ref,
                 wrh_ref, wrl_ref, br_ref,
                 x1_ref, h2_ref, tope_ref, gate_ref, rank_ref, cnt_ref):
    t = x_ref.shape[0]
    y = (jnp.dot(ya_ref[...], wo_ref[:A_WIDTH, :], preferred_element_type=F32)
         + jnp.dot(yb_ref[...], wo_ref[A_WIDTH:, :], preferred_element_type=F32) + bo_ref[...])
    x1 = _layer_norm(DEEPNORM_ALPHA * x_ref[...] + (1.0 + ga_ref[...]) * y, g1_ref[...], b1_ref[...])
    x1_ref[...] = x1
    h2 = x1 * (1.0 + scf_ref[...]) + shf_ref[...]
    _store_token_tiles(h2_ref, h2)

    hh = h2.astype(BF16)
    hl = (h2 - hh.astype(F32)).astype(BF16)
    logits = (lax.dot_general(wrh_ref[...], hh, _NT, preferred_element_type=F32)
              + lax.dot_general(wrh_ref[...], hl, _NT, preferred_element_type=F32)
              + lax.dot_general(wrl_ref[...], hh, _NT, preferred_element_type=F32) + br_ref[...])
    erow = lax.broadcasted_iota(I32, (N_EXPERTS, t), 0)
    vals, idxs = [], []
    for _ in range(TOP_K):
        v = jnp.max(logits, axis=0, keepdims=True)
        i = jnp.min(jnp.where(logits == v, erow, N_EXPERTS), axis=0, keepdims=True)
        vals.append(v)
        idxs.append(i)
        logits = jnp.where(erow == i, -jnp.inf, logits)
    ex = [jnp.exp(v - vals[0]) for v in vals]
    den = ex[0] + ex[1] + ex[2] + ex[3]
    gate_ref[...] = jnp.concatenate([e / den for e in ex], axis=0)
    tope_ref[...] = jnp.concatenate(idxs, axis=0)

    @pl.when(pl.program_id(0) == 0)
    def _():
        cnt_ref[...] = jnp.zeros_like(cnt_ref)

    hit = jnp.zeros((N_EXPERTS, t), F32)
    for i in idxs:
        hit = hit + jnp.where(erow == i, 1.0, 0.0)
    hitb = hit.astype(BF16)
    r = lax.broadcasted_iota(I32, (t, t), 0)
    c = lax.broadcasted_iota(I32, (t, t), 1)
    earlier = jnp.where(r < c, 1.0, 0.0).astype(BF16)
    before = jnp.dot(hitb, earlier, preferred_element_type=F32) + cnt_ref[...]
    total = jnp.dot(hitb, jnp.ones((t, t), BF16), preferred_element_type=F32)
    rank_ref[...] = jnp.concatenate(
        [jnp.sum(jnp.where(erow == i, before, 0.0), axis=0, keepdims=True) for i in idxs], axis=0).astype(I32)
    cnt_ref[...] += total


def _post(x, ya, yb, ga, scf, shf, w_o, b_o, g1, b1, wrh, wrl, br):
    n, d = x.shape
    t = POST_TILE
    nt = n // t
    if ga.shape[0] == n:
        mod_spec = pl.BlockSpec((t, d), lambda i: (i, 0))
    else:
        tiles_per_seq = nt // ga.shape[0]
        ga, scf, shf = (a.reshape(a.shape[0], 1, d) for a in (ga, scf, shf))
        mod_spec = pl.BlockSpec((None, 1, d), lambda i: (i // tiles_per_seq, 0, 0))
    row = lambda w: pl.BlockSpec((t, w), lambda i: (i, 0))
    col = lambda r: pl.BlockSpec((r, t), lambda i: (0, i))
    const = lambda a: pl.BlockSpec(a.shape, lambda i: (0, 0))
    return pl.pallas_call(
        _post_kernel,
        out_shape=(jax.ShapeDtypeStruct((n, d), F32), jax.ShapeDtypeStruct((n * TOK_ROWS, LANES), F32),
                   jax.ShapeDtypeStruct((TOP_K, n), I32), jax.ShapeDtypeStruct((TOP_K, n), F32),
                   jax.ShapeDtypeStruct((TOP_K, n), I32), jax.ShapeDtypeStruct((N_EXPERTS, t), F32)),
        grid=(nt,),
        in_specs=[row(d), row(A_WIDTH), row(ATT_WIDTH), mod_spec, mod_spec, mod_spec,
                  const(w_o), const(b_o), const(g1), const(b1), const(wrh), const(wrl), const(br)],
        out_specs=(row(d), pl.BlockSpec((t * TOK_ROWS, LANES), lambda i: (i, 0)), col(TOP_K), col(TOP_K), col(TOP_K),
                   pl.BlockSpec((N_EXPERTS, t), lambda i: (0, 0))),
        compiler_params=_cparams("arbitrary"),
        name="post",
    )(x, ya, yb, ga, scf, shf, w_o, b_o, g1, b1, wrh, wrl, br)


def _token_rows(ref, r):
    return ref.at[pl.ds(pl.multiple_of(r * TOK_ROWS, TOK_ROWS), TOK_ROWS)]


def _token_copies_wait(hbm_ref, vmem_ref, sem, n_tokens):
    rows = n_tokens * TOK_ROWS
    pltpu.make_async_copy(hbm_ref.at[pl.ds(0, rows)], vmem_ref.at[pl.ds(0, rows)], sem).wait()


def _dispatch_kernel(zrow_ref, dest_ref, h_ref, xp_ref, zero_ref, sem):
    t = h_ref.shape[0] // TOK_ROWS

    block_rows = MOE_BLOCK * TOK_ROWS

    def zero_block(slot0):
        z0 = pl.multiple_of(slot0 * TOK_ROWS, block_rows)
        cp = pltpu.make_async_copy(zero_ref, xp_ref.at[pl.ds(z0, block_rows)], sem)
        cp.start()
        cp.wait()

    @pl.when(pl.program_id(0) == 0)
    def _():
        zero_ref[...] = jnp.zeros_like(zero_ref)
        for e in range(N_EXPERTS):
            @pl.when(zrow_ref[e] >= 0)
            def _():
                zero_block(zrow_ref[e])

        def unused(b, _):
            zero_block(b * MOE_BLOCK)
            return 0

        lax.fori_loop(zrow_ref[N_EXPERTS], xp_ref.shape[0] // block_rows, unused, 0)

    def body(i, _):
        for k in range(TOP_K):
            pltpu.make_async_copy(_token_rows(h_ref, i), _token_rows(xp_ref, dest_ref[k, i]), sem).start(priority=k % 2)
        return 0

    lax.fori_loop(0, t, body, 0, unroll=DMA_ISSUE_UNROLL)
    for k in range(TOP_K):
        _token_copies_wait(xp_ref, h_ref, sem, t)


def _dispatch(zrow, dest, h2, n_slots):
    t = ROW_TILE
    n = h2.shape[0] // TOK_ROWS
    grid_spec = pltpu.PrefetchScalarGridSpec(
        num_scalar_prefetch=1,
        grid=(n // t,),
        in_specs=[pl.BlockSpec((TOP_K, t), lambda i, z: (0, i), memory_space=pltpu.SMEM),
                  pl.BlockSpec((t * TOK_ROWS, LANES), lambda i, z: (i, 0))],
        out_specs=pl.BlockSpec(memory_space=pl.ANY),
        scratch_shapes=[pltpu.VMEM((MOE_BLOCK * TOK_ROWS, LANES), F32), pltpu.SemaphoreType.DMA],
    )
    return pl.pallas_call(
        _dispatch_kernel,
        out_shape=jax.ShapeDtypeStruct((n_slots * TOK_ROWS, LANES), F32),
        grid_spec=grid_spec,
        compiler_params=_cparams("arbitrary"),
        name="dispatch",
    )(zrow, dest, h2)


def _experts_kernel(be_ref, bi_ref, nu_ref, x_ref, wg_ref, bg_ref, wu_ref, bu_ref, wd_ref, bd_ref, y_ref,
                    wgb_ref, wub_ref, wdb_ref):
    i = pl.program_id(0)

    @pl.when((i == 0) | (be_ref[i] != be_ref[jnp.maximum(i - 1, 0)]))
    def _():
        wgb_ref[...] = wg_ref[...].astype(BF16)
        wub_ref[...] = wu_ref[...].astype(BF16)
        wdb_ref[...] = wd_ref[...].astype(BF16)

    @pl.when(i < nu_ref[0])
    def _():
        x = _load_token_tiles(x_ref).astype(BF16)
        g = jnp.minimum(jnp.dot(x, wgb_ref[...], preferred_element_type=F32) + bg_ref[...], SWIGLU_LIMIT)
        u = jnp.clip(jnp.dot(x, wub_ref[...], preferred_element_type=F32) + bu_ref[...], -SWIGLU_LIMIT, SWIGLU_LIMIT)
        a = g * jax.nn.sigmoid(SWIGLU_ALPHA * g)
        mid = ((u + 1.0) * a).astype(BF16)
        _store_token_tiles(y_ref, jnp.dot(mid, wdb_ref[...], preferred_element_type=F32) + bd_ref[...])

    @pl.when(pl.program_id(0) >= nu_ref[0])
    def _():
        y_ref[...] = jnp.zeros_like(y_ref)


def _experts(block_e, block_i, n_used, xp, wg, bg, wu, bu, wd, bd):
    d, f = wg.shape[1], wg.shape[2]
    nb = xp.shape[0] // (MOE_BLOCK * TOK_ROWS)
    wspec = lambda a, b: pl.BlockSpec((None, a, b), lambda i, be, bi, nu: (be[i], 0, 0))
    slots = pl.BlockSpec((MOE_BLOCK * TOK_ROWS, LANES), lambda i, be, bi, nu: (bi[i], 0))
    grid_spec = pltpu.PrefetchScalarGridSpec(
        num_scalar_prefetch=3,
        grid=(nb,),
        in_specs=[slots, wspec(d, f), wspec(1, f), wspec(d, f), wspec(1, f), wspec(f, d), wspec(1, d)],
        out_specs=pl.BlockSpec((MOE_BLOCK * TOK_ROWS, LANES), lambda i, be, bi, nu: (i, 0)),
        scratch_shapes=[pltpu.VMEM((d, f), BF16), pltpu.VMEM((d, f), BF16), pltpu.VMEM((f, d), BF16)],
    )
    return pl.pallas_call(
        _experts_kernel,
        out_shape=jax.ShapeDtypeStruct(xp.shape, F32),
        grid_spec=grid_spec,
        compiler_params=_cparams("arbitrary"),
        name="experts",
    )(block_e, block_i, n_used, xp, wg, bg, wu, bu, wd, bd)


def _combine_kernel(dest_ref, dest_next_ref, gate_ref, x1_ref, gf_ref, g2_ref, b2_ref, yp_ref, o_ref, buf_ref, sems):
    t = x1_ref.shape[0]
    step = pl.program_id(0)
    slot = step % 2

    def start_gather(dst_ref, into):
        def body(i, _):
            for k in range(TOP_K):
                pltpu.make_async_copy(_token_rows(yp_ref, dst_ref[k, i]), _token_rows(buf_ref.at[into, k], i),
                                      sems.at[into]).start(priority=k % 2)
            return 0

        lax.fori_loop(0, t, body, 0, unroll=DMA_ISSUE_UNROLL)

    @pl.when(step == 0)
    def _():
        start_gather(dest_ref, 0)

    @pl.when(step + 1 < pl.num_programs(0))
    def _():
        start_gather(dest_next_ref, 1 - slot)

    for k in range(TOP_K):
        _token_copies_wait(yp_ref, buf_ref.at[slot, k], sems.at[slot], t)

    gates = jnp.concatenate([gate_ref[...], jnp.zeros((LANES - TOP_K, t), F32)], axis=0).T
    f = gates[:, 0:1] * _load_token_tiles(buf_ref.at[slot, 0])
    for k in range(1, TOP_K):
        f = f + gates[:, k:k + 1] * _load_token_tiles(buf_ref.at[slot, k])
    o_ref[...] = _layer_norm(DEEPNORM_ALPHA * x1_ref[...] + (1.0 + gf_ref[...]) * f, g2_ref[...], b2_ref[...])


def _combine(dest, gates, x1, gf, g2, b2, yp):
    n, d = x1.shape
    t = ROW_TILE
    nt = n // t
    if gf.shape[0] == n:
        mod_spec = pl.BlockSpec((t, d), lambda i: (i, 0))
    else:
        tiles_per_seq = nt // gf.shape[0]
        gf = gf.reshape(gf.shape[0], 1, d)
        mod_spec = pl.BlockSpec((None, 1, d), lambda i: (i // tiles_per_seq, 0, 0))
    return pl.pallas_call(
        _combine_kernel,
        out_shape=jax.ShapeDtypeStruct((n, d), F32),
        grid=(nt,),
        in_specs=[pl.BlockSpec((TOP_K, t), lambda i: (0, i), memory_space=pltpu.SMEM),
                  pl.BlockSpec((TOP_K, t), lambda i: (0, jnp.minimum(i + 1, nt - 1)), memory_space=pltpu.SMEM),
                  pl.BlockSpec((TOP_K, t), lambda i: (0, i)),
                  pl.BlockSpec((t, d), lambda i: (i, 0)), mod_spec,
                  pl.BlockSpec((1, d), lambda i: (0, 0)), pl.BlockSpec((1, d), lambda i: (0, 0)),
                  pl.BlockSpec(memory_space=pl.ANY)],
        out_specs=pl.BlockSpec((t, d), lambda i: (i, 0)),
        scratch_shapes=[pltpu.VMEM((2, TOP_K, t * TOK_ROWS, LANES), F32), pltpu.SemaphoreType.DMA((2,))],
        compiler_params=_cparams("arbitrary"),
        name="combine",
    )(dest, dest, gates, x1, gf, g2, b2, yp)


def _moe(h2, tope, rank, counts, gates, x1, gf, g2, b2, experts_w):
    n = tope.shape[1]
    nb = (n * TOP_K + N_EXPERTS * (MOE_BLOCK - 1) + MOE_BLOCK - 1) // MOE_BLOCK
    pcounts = (counts + MOE_BLOCK - 1) // MOE_BLOCK * MOE_BLOCK
    pend = jnp.cumsum(pcounts)
    pstart = pend - pcounts
    eids = jnp.arange(N_EXPERTS, dtype=I32).reshape(N_EXPERTS, 1, 1)
    dest = rank + jnp.sum(jnp.where(tope[None] == eids, pstart.reshape(N_EXPERTS, 1, 1), 0), axis=0).astype(I32)
    n_used = (pend[-1] // MOE_BLOCK).astype(I32)
    zrow = jnp.concatenate([jnp.where(counts > 0, pend - MOE_BLOCK, -1), n_used.reshape(1)]).astype(I32)
    blk = jnp.arange(nb, dtype=I32)
    block_i = jnp.minimum(blk, n_used - 1)
    block_e = jnp.minimum(jnp.sum(pend[None, :] <= (block_i * MOE_BLOCK)[:, None], axis=1), N_EXPERTS - 1).astype(I32)
    xp = _dispatch(zrow, dest, h2, nb * MOE_BLOCK)
    yp = _experts(block_e, block_i, n_used.reshape(1), xp, *experts_w)
    return _combine(dest, gates, x1, gf, g2, b2, yp)


def kernel(x_prompt, x_sample, c_prompt, c_sample, cache_k, cache_v, cache_kidx, w_in, a_ln_g, a_ln_b, a_ws, a_bs,
           w_o, b_o, w_c, b_c, ln1_g, ln1_b, ln2_g, ln2_b, w_router, b_router, w_gate, b_gate, w_up, b_up,
           w_down, b_down):
    bp, s, d = x_prompt.shape
    bs, ts, _ = x_sample.shape
    past = cache_k.shape[2]
    np_, ns = bp * s, bs * ts

    wi = w_in[0]
    w_in_p = jnp.zeros((d, IN_COLS_PAD), F32)
    w_in_p = w_in_p.at[:, :OFF_IK].set(wi[:, :OFF_IK])
    w_in_p = w_in_p.at[:, OFF_IK:OFF_IK + IDX_DIM].set(wi[:, OFF_IK:OFF_IK + IDX_DIM])
    w_in_p = w_in_p.at[:, OFF_IK + IDX_DIM:OFF_IK + 2 * IDX_DIM].set(wi[:, OFF_IK:OFF_IK + IDX_DIM])
    w_in_p = w_in_p.at[:, OFF_IW:OFF_IW + N_IDX_HEADS].set(wi[:, OFF_IK + IDX_DIM:OFF_IK + IDX_DIM + N_IDX_HEADS])
    w_in_p = w_in_p.astype(BF16)
    lng, lnb = a_ln_g[0].reshape(1, A_WIDTH), a_ln_b[0].reshape(1, A_WIDTH)
    wtril = jnp.tril(a_ws[0])
    wm_p = wtril.astype(BF16)
    bm_p = jnp.broadcast_to(a_bs[0][:, :, None], (A_GROUPS, A_CHUNK, A_GROUP_DIM)).astype(F32)
    rep = A_CHUNK // ts
    wm_s = jnp.einsum("ab,gij->gaibj", jnp.eye(rep, dtype=F32), wtril[:, :ts, :ts]).reshape(
        A_GROUPS, A_CHUNK, A_CHUNK).astype(BF16)
    bm_s = jnp.broadcast_to(jnp.tile(a_bs[0][:, :ts], (1, rep))[:, :, None], (A_GROUPS, A_CHUNK, A_GROUP_DIM)).astype(F32)
    w_o_b = w_o[0].astype(BF16)
    b_o_r = b_o[0].reshape(1, d)
    wr_t = w_router[0].T
    wrh = wr_t.astype(BF16)
    wrl = (wr_t - wrh.astype(F32)).astype(BF16)
    br = jnp.broadcast_to(b_router[0][:, None], (N_EXPERTS, POST_TILE)).astype(F32)
    experts_w = (w_gate[0], b_gate[0][:, None, :], w_up[0], b_up[0][:, None, :], w_down[0], b_down[0][:, None, :])
    g1, b1 = ln1_g[0].reshape(1, d), ln1_b[0].reshape(1, d)
    g2, b2 = ln2_g[0].reshape(1, d), ln2_b[0].reshape(1, d)

    mods = _cond_mods(jnp.concatenate([c_prompt, c_sample], axis=0), w_c[0], b_c[0]).reshape(bp + bs, 6, d)
    mods_p = [mods[:bp, i] for i in range(6)]
    mods_s = [jnp.repeat(mods[bp:, i], ts, axis=0) for i in range(6)]

    xp2 = x_prompt.reshape(np_, d)
    ya, q, kft, kb, _, vft, vt, iq, ikft, ikb, iwt, _ = _project(
        xp2, mods_p[1], mods_p[0], w_in_p, lng, lnb, wm_p, bm_p, bp)
    yb = _attend_prompt(q, iq, iwt, kb, vt, ikb, bp, s, min(TOPK_MAX, s // 4))
    x1, h2, tope, gates, rank, cnt = _post(xp2, ya, yb, mods_p[2], mods_p[4], mods_p[3], w_o_b, b_o_r, g1, b1, wrh, wrl, br)
    y_p = _moe(h2, tope, rank, cnt[:, 0].astype(I32), gates, x1, mods_p[5], g2, b2, experts_w)
    heads_last = lambda a: jnp.transpose(a.reshape(1, bp, N_HEADS, HEAD_DIM, s), (0, 1, 4, 2, 3))
    out_p = (y_p.reshape(bp, s, d), heads_last(kft), heads_last(vft),
             jnp.transpose(ikft.reshape(1, bp, IDX_DIM, s), (0, 1, 3, 2)))

    xs2 = x_sample.reshape(ns, d)
    ya, q, kft, kb, vf, _, vt, iq, ikft, ikb, iwt, va = _project(
        xs2, mods_s[1], mods_s[0], w_in_p, lng, lnb, wm_s, bm_s, 1)
    kf, ikf = kft[0].T, ikft[0].T
    iwl = jnp.transpose(iwt.reshape(N_IDX_HEADS, bs, ts), (1, 0, 2)).reshape(bs, 1, N_IDX_HEADS * ts)
    yb = _attend_sample(q, iq, iwl, kb, vf, ikb, cache_k[0].reshape(bs, past, ATT_WIDTH),
                        cache_v[0].reshape(bs, past, ATT_WIDTH), cache_kidx[0], min(TOPK_MAX, (past + ts) // 4))
    x1, h2, tope, gates, rank, cnt = _post(xs2, ya, yb, mods_s[2], mods_s[4], mods_s[3], w_o_b, b_o_r, g1, b1, wrh, wrl, br)
    y_s = _moe(h2, tope, rank, cnt[:, 0].astype(I32), gates, x1, mods_s[5], g2, b2, experts_w)

    return (out_p[0], y_s.reshape(bs, ts, d), out_p[1], out_p[2], out_p[3],
            kf.reshape(1, bs, ts, N_HEADS, HEAD_DIM), vf.reshape(1, bs, ts, N_HEADS, HEAD_DIM),
            ikf.reshape(1, bs, ts, IDX_DIM), va.reshape(1, bs, ts, A_WIDTH))
```

```python
import functools

import jax
import jax.numpy as jnp
from jax import lax
from jax.experimental import pallas as pl
from jax.experimental.pallas import tpu as pltpu

F32 = jnp.float32
BF16 = jnp.bfloat16
I32 = jnp.int32

D_MODEL = 1024
CHUNK_SHIFT = 6
A_GROUPS = 4
A_GROUP_DIM = 128
A_WIDTH = A_GROUPS * A_GROUP_DIM
A_CHUNK = 128
N_HEADS = 8
HEAD_DIM = 64
ATT_WIDTH = N_HEADS * HEAD_DIM
N_IDX_HEADS = 8
IDX_DIM = 64
TOPK_MAX = 256
ATTN_SCALE = HEAD_DIM ** -0.5
VT_HEAD_ROWS = HEAD_DIM + 16
VT_ROWS = N_HEADS * VT_HEAD_ROWS
IDX_W_SCALE = (N_IDX_HEADS ** -0.5) * (IDX_DIM ** -0.5)
N_EXPERTS = 32
TOP_K = 4
SWIGLU_LIMIT = 7.0
SWIGLU_ALPHA = 1.702
DEEPNORM_ALPHA = 2.0 ** 0.25
LN_EPS = 1e-5

LANES = 128
SUBLANES = 8
VMEM_LIMIT_BYTES = 56 * 1024 * 1024

ROW_TILE = 256
POST_TILE = 512
MOE_BLOCK = 256
ATT_TILE = 256
DMA_ISSUE_UNROLL = 8
STATIC_TILE_UNROLL = 4

OFF_AU, OFF_AV, OFF_Q, OFF_K, OFF_V, OFF_IQ = 0, 512, 1024, 1536, 2048, 2560
OFF_IK = 3072
OFF_IW = 3200
IN_COLS_PAD = 3328

MASKED_DIST = 3.0e32
F32_MAX = 3.4028234663852886e38
INT32_MIN = -(2 ** 31)
KEY_NEG_INF = INT32_MIN + 0x7FFFFF

_NT = (((1,), (1,)), ((), ()))


def _cparams(*sem):
    return pltpu.CompilerParams(dimension_semantics=sem, vmem_limit_bytes=VMEM_LIMIT_BYTES)


def _mods_kernel(c_ref, w_ref, b_ref, o_ref):
    c = c_ref[...]
    s = c * jax.nn.sigmoid(c)
    o_ref[...] = jnp.dot(s.astype(BF16), w_ref[...].astype(BF16), preferred_element_type=F32) + b_ref[...]


def _cond_mods(c, w_c, b_c):
    nb, d = c.shape
    n_out = w_c.shape[1]
    return pl.pallas_call(
        _mods_kernel,
        out_shape=jax.ShapeDtypeStruct((nb, n_out), F32),
        grid=(n_out // d,),
        in_specs=[pl.BlockSpec((nb, d), lambda j: (0, 0)),
                  pl.BlockSpec((d, d), lambda j: (0, j)),
                  pl.BlockSpec((1, d), lambda j: (0, j))],
        out_specs=pl.BlockSpec((nb, d), lambda j: (0, j)),
        compiler_params=_cparams("arbitrary"),
        name="mods",
    )(c, w_c, b_c.reshape(1, n_out))


def _gelu(x):
    return 0.5 * x * (1.0 + lax.erf(x * 0.7071067811865476))


def _project_kernel(x_ref, sc_ref, sh_ref, w_ref, lng_ref, lnb_ref, wm_ref, bm_ref,
                    ya_ref, q_ref, kft_ref, kb_ref, vf_ref, vft_ref, vt_ref, iq_ref, ikft_ref, ikb_ref, iwt_ref,
                    va_ref):
    t = x_ref.shape[0]
    h = (x_ref[...] * (1.0 + sc_ref[...]) + sh_ref[...]).astype(BF16)

    def proj(c0, n):
        return jnp.dot(h, w_ref[:, c0:c0 + n], preferred_element_type=F32)

    u = _gelu(proj(OFF_AU, A_WIDTH))
    gv = _gelu(proj(OFF_AV, A_WIDTH))
    for g in range(A_GROUPS):
        lo, hi = g * A_GROUP_DIM, (g + 1) * A_GROUP_DIM
        xg = gv[:, lo:hi]
        mu = jnp.mean(xg, axis=-1, keepdims=True)
        xc = xg - mu
        var = jnp.mean(xc * xc, axis=-1, keepdims=True)
        vg = xc * lax.rsqrt(var + LN_EPS) * lng_ref[:, lo:hi] + lnb_ref[:, lo:hi]
        va_ref[:, lo:hi] = vg
        vgb = vg.astype(BF16)
        for c in range(t // A_CHUNK):
            r0, r1 = c * A_CHUNK, (c + 1) * A_CHUNK
            mixed = jnp.dot(wm_ref[g], vgb[r0:r1, :], preferred_element_type=F32) + bm_ref[g]
            ya_ref[r0:r1, lo:hi] = (u[r0:r1, lo:hi] * mixed).astype(BF16)

    q_ref[...] = proj(OFF_Q, ATT_WIDTH).astype(BF16)
    k = proj(OFF_K, ATT_WIDTH)
    kft_ref[...] = k.T
    kb_ref[...] = k.astype(BF16)
    v = proj(OFF_V, ATT_WIDTH)
    vf_ref[...] = v
    v_t32 = v.T
    vft_ref[...] = v_t32
    v_t = v_t32.astype(BF16)
    ones = jnp.ones((VT_HEAD_ROWS - HEAD_DIM, t), BF16)
    vt_ref[0] = jnp.concatenate(
        [blk for h in range(N_HEADS) for blk in (v_t[h * HEAD_DIM:(h + 1) * HEAD_DIM, :], ones)], axis=0)
    iq_ref[...] = proj(OFF_IQ, N_IDX_HEADS * IDX_DIM).astype(BF16)
    ik2 = proj(OFF_IK, LANES)
    ikft_ref[...] = ik2.T[:IDX_DIM, :]
    ikb_ref[...] = ik2.astype(BF16)
    iw = proj(OFF_IW, LANES) * IDX_W_SCALE
    iwt_ref[...] = iw.T[:N_IDX_HEADS, :]


def _project(x, sc, sh, w_in_p, lng, lnb, wm, bm, n_out_seq):
    n, d = x.shape
    t = ROW_TILE
    nt = n // t
    tiles_per_out = nt // n_out_seq
    trans = lambda rows: pl.BlockSpec((None, rows, t), lambda i: (i // tiles_per_out, 0, i % tiles_per_out))
    if sc.shape[0] == n:
        mod_spec = pl.BlockSpec((t, d), lambda i: (i, 0))
    else:
        tiles_per_seq = nt // sc.shape[0]
        sc = sc.reshape(sc.shape[0], 1, d)
        sh = sh.reshape(sh.shape[0], 1, d)
        mod_spec = pl.BlockSpec((None, 1, d), lambda i: (i // tiles_per_seq, 0, 0))
    row = lambda w: pl.BlockSpec((t, w), lambda i: (i, 0))
    const2 = lambda a: pl.BlockSpec(a.shape, lambda i: (0, 0))
    const3 = lambda a: pl.BlockSpec(a.shape, lambda i: (0, 0, 0))
    out_shape = (
        jax.ShapeDtypeStruct((n, A_WIDTH), BF16),
        jax.ShapeDtypeStruct((n, ATT_WIDTH), BF16),
        jax.ShapeDtypeStruct((n_out_seq, ATT_WIDTH, n // n_out_seq), F32),
        jax.ShapeDtypeStruct((n, ATT_WIDTH), BF16),
        jax.ShapeDtypeStruct((n, ATT_WIDTH), F32),
        jax.ShapeDtypeStruct((n_out_seq, ATT_WIDTH, n // n_out_seq), F32),
        jax.ShapeDtypeStruct((nt, VT_ROWS, t), BF16),
        jax.ShapeDtypeStruct((n, ATT_WIDTH), BF16),
        jax.ShapeDtypeStruct((n_out_seq, IDX_DIM, n // n_out_seq), F32),
        jax.ShapeDtypeStruct((n, LANES), BF16),
        jax.ShapeDtypeStruct((N_IDX_HEADS, n), F32),
        jax.ShapeDtypeStruct((n, A_WIDTH), F32),
    )
    out_specs = (row(A_WIDTH), row(ATT_WIDTH), trans(ATT_WIDTH), row(ATT_WIDTH), row(ATT_WIDTH), trans(ATT_WIDTH),
                 pl.BlockSpec((1, VT_ROWS, t), lambda i: (i, 0, 0)),
                 row(ATT_WIDTH), trans(IDX_DIM), row(LANES),
                 pl.BlockSpec((N_IDX_HEADS, t), lambda i: (0, i)),
                 row(A_WIDTH))
    return pl.pallas_call(
        _project_kernel,
        out_shape=out_shape,
        grid=(nt,),
        in_specs=[row(d), mod_spec, mod_spec, const2(w_in_p), const2(lng), const2(lnb), const3(wm), const3(bm)],
        out_specs=out_specs,
        compiler_params=_cparams("arbitrary"),
        name="project",
    )(x, sc, sh, w_in_p, lng, lnb, wm, bm)


DIGIT_BITS = 8
N_DIGITS = 32 // DIGIT_BITS
DIGIT_ABOVE = 512.0
DIGIT_BELOW = -1.0
PACKED_ROWS = 16


def _static_unroll(trips):
    return STATIC_TILE_UNROLL if isinstance(trips, int) else 1


def _tile_loop(nkt, body, init):
    if isinstance(nkt, int):
        return lax.fori_loop(0, nkt, body, init, unroll=STATIC_TILE_UNROLL)
    pairs = lax.shift_right_logical(nkt, 1)
    carry = lax.fori_loop(0, pairs, lambda i, c: body(2 * i + 1, body(2 * i, c)), init)
    return lax.fori_loop(2 * pairs, nkt, body, carry)


def _mono_key(x):
    b = lax.bitcast_convert_type(x, I32)
    return jnp.where(b >= 0, b, b ^ jnp.int32(0x7FFFFFFF))


def _digit_plane(key, phase):
    shift = 32 - DIGIT_BITS * (phase + 1)
    d = lax.shift_right_arithmetic(key, jnp.int32(shift)) if shift else key
    d = d + (1 << (DIGIT_BITS - 1)) if phase == 0 else d & ((1 << DIGIT_BITS) - 1)
    return d.astype(F32).astype(BF16)


def _count_plane(plane_ref, nkt, cand, strict):
    _, tk, w = plane_ref.shape
    cb = cand.astype(BF16)
    one, zero = jnp.ones((), BF16), jnp.zeros((), BF16)

    def body(kt, cnt):
        e = plane_ref[kt]
        accs = [jnp.zeros((PACKED_ROWS, w), BF16) for _ in range(4)]
        for r in range(tk // PACKED_ROWS):
            blk = e[r * PACKED_ROWS:(r + 1) * PACKED_ROWS, :]
            accs[r % 4] = accs[r % 4] + jnp.where((blk > cb) if strict else (blk >= cb), one, zero)
        return cnt + ((accs[0] + accs[1]) + (accs[2] + accs[3])).astype(F32)

    cnt = _tile_loop(nkt, body, jnp.zeros((PACKED_ROWS, w), F32))
    return jnp.sum(cnt, axis=0, keepdims=True)


def _search_digit(plane_ref, nkt, topk):
    w = plane_ref.shape[2]

    def bit_body(i, d):
        cand = d + lax.shift_left(jnp.int32(1), jnp.int32(DIGIT_BITS - 1) - i).astype(F32)
        cnt = _count_plane(plane_ref, nkt, cand, strict=False)
        return jnp.where(cnt >= float(topk), cand, d)

    return lax.fori_loop(0, DIGIT_BITS, bit_body, jnp.zeros((1, w), F32))


def _topk_select(keys_ref, plane_ref, nkt, topk):
    _, tk, w = plane_ref.shape
    d = _search_digit(plane_ref, nkt, topk)
    for phase in range(1, N_DIGITS):
        db = d.astype(BF16)

        def refine(kt, _, phase=phase, db=db):
            e = plane_ref[kt]
            decided = jnp.where(e > db, jnp.asarray(DIGIT_ABOVE, BF16), jnp.asarray(DIGIT_BELOW, BF16))
            plane_ref[kt] = jnp.where(e == db, _digit_plane(keys_ref[kt], phase), decided)
            return 0

        _tile_loop(nkt, refine, 0)
        d = _search_digit(plane_ref, nkt, topk)

    cnt_ge = _count_plane(plane_ref, nkt, d, strict=False)

    @pl.when(jnp.max(cnt_ge) > float(topk))
    def _():
        need = float(topk) - _count_plane(plane_ref, nkt, d, strict=True)
        r = lax.broadcasted_iota(I32, (tk, tk), 0)
        c = lax.broadcasted_iota(I32, (tk, tk), 1)
        before = jnp.where(c < r, 1.0, 0.0).astype(BF16)

        def body(kt, seen):
            e = plane_ref[kt].astype(F32)
            eq = e == d
            eqf = jnp.where(eq, 1.0, 0.0)
            prior = jnp.dot(before, eqf.astype(BF16), preferred_element_type=F32) + seen
            plane_ref[kt] = jnp.where(eq & (prior >= need), DIGIT_BELOW, e).astype(BF16)
            return seen + jnp.sum(eqf.reshape(tk // SUBLANES, SUBLANES, w), axis=0).sum(axis=0, keepdims=True)

        lax.fori_loop(0, nkt, body, jnp.zeros((1, w), F32))

    return d


def _selected(keys_ref, plane_ref, kt, d):
    return (plane_ref[kt].astype(F32) >= d) & (keys_ref[kt] > jnp.int32(KEY_NEG_INF))


def _half_mask(x_pair, head):
    lane = lax.broadcasted_iota(I32, x_pair.shape, 1)
    keep = (lane >= HEAD_DIM) if head % 2 else (lane < HEAD_DIM)
    return jnp.where(keep, x_pair, jnp.zeros_like(x_pair))


def _attend_prompt_kernel(q_ref, iq_ref, iwt_ref, k_ref, vt_ref, ik_ref, o_ref, keys_ref, plane_ref, *head_refs, topk):
    qh_refs, acc_refs, lt_refs = (head_refs[i * N_HEADS:(i + 1) * N_HEADS] for i in range(3))
    tq = q_ref.shape[0]
    tk = keys_ref.shape[1]
    j = pl.program_id(1)
    nkt = j + 1
    q0 = j * tq
    row = lax.broadcasted_iota(I32, (tk, tq), 0)
    lane = lax.broadcasted_iota(I32, (tk, tq), 1)
    qpos = q0 + lane

    iq = iq_ref[...]
    iqm = [_half_mask(iq[:, (h // 2) * LANES:(h // 2 + 1) * LANES], h) for h in range(N_IDX_HEADS)]
    iw = iwt_ref[...]

    def score_tile(kt, diagonal):
        k0 = pl.multiple_of(kt * tk, tk)
        ikt = ik_ref[pl.ds(k0, tk), :]
        s = jnp.zeros((tk, tq), F32)
        for h in range(N_IDX_HEADS):
            r = lax.dot_general(ikt, iqm[h], _NT, preferred_element_type=F32)
            s = s + jnp.maximum(r, 0.0) * iw[h:h + 1, :]
        if diagonal:
            adm = lax.shift_right_logical(k0 + row, CHUNK_SHIFT) <= lax.shift_right_logical(qpos, CHUNK_SHIFT)
            s = jnp.where(adm, s, -jnp.inf)
        key = _mono_key(s)
        keys_ref[kt] = key
        plane_ref[kt] = _digit_plane(key, 0)
        return 0

    assert tk == tq
    lax.fori_loop(0, j, lambda kt, _: score_tile(kt, False), 0)
    score_tile(j, True)

    d_last = _topk_select(keys_ref, plane_ref, nkt, topk)

    def dist_body(kt, _):
        kpos = kt * tk + row
        dist = jnp.abs(qpos - kpos).astype(F32)
        masked = jnp.where(_selected(keys_ref, plane_ref, kt, d_last), dist, MASKED_DIST)
        keys_ref[kt] = lax.bitcast_convert_type(masked, I32)
        return 0

    lax.fori_loop(0, nkt, dist_body, 0)

    qfull = q_ref[...]
    for h in range(N_HEADS):
        pair = h // 2
        qh_refs[h][...] = _half_mask(qfull[:, pair * LANES:(pair + 1) * LANES], h) * jnp.asarray(ATTN_SCALE, BF16)
        acc_refs[h][...] = jnp.zeros_like(acc_refs[h])

    def logits(kt, slot, m_all):
        k0 = pl.multiple_of(kt * tk, tk)
        dist = lax.bitcast_convert_type(keys_ref[kt], F32)
        ms = []
        for h in range(N_HEADS):
            pair = h // 2
            slope = 2.0 ** (-8.0 * (h + 1) / N_HEADS)
            kp = k_ref[pl.ds(k0, tk), pair * LANES:(pair + 1) * LANES]
            lt = lax.dot_general(kp, qh_refs[h][...], _NT, preferred_element_type=F32) - slope * dist
            lt_refs[h][slot] = lt.astype(BF16)
            tile_max = jnp.max(lt, axis=0, keepdims=True).astype(BF16).astype(F32)
            ms.append(jnp.maximum(m_all[h:h + 1, :], tile_max))
        return jnp.concatenate(ms, axis=0)

    def att_body(kt, carry):
        m_prev, m_cur, l_all = carry
        slot = kt % 2
        m_next = logits(jnp.minimum(kt + 1, nkt - 1), 1 - slot, m_cur)
        ls = []
        for h in range(N_HEADS):
            alpha = jnp.exp(m_prev[h:h + 1, :] - m_cur[h:h + 1, :])
            p = jnp.exp(lt_refs[h][slot] - m_cur[h:h + 1, :].astype(BF16))
            pv = jnp.dot(vt_ref[kt, h * VT_HEAD_ROWS:(h + 1) * VT_HEAD_ROWS, :], p, preferred_element_type=F32)
            ls.append(alpha * l_all[h:h + 1, :] + pv[HEAD_DIM:HEAD_DIM + 1, :])
            acc_refs[h][...] = alpha * acc_refs[h][...] + pv[:HEAD_DIM, :]
        return m_cur, m_next, jnp.concatenate(ls, axis=0)

    m_init = jnp.full((N_HEADS, tq), -jnp.inf, F32)
    _, _, l_all = lax.fori_loop(0, nkt, att_body, (m_init, logits(0, 0, m_init), jnp.zeros((N_HEADS, tq), F32)))
    out_t = jnp.concatenate([acc_refs[h][...] / l_all[h:h + 1, :] for h in range(N_HEADS)], axis=0)
    o_ref[...] = out_t.T.astype(BF16)


def _attend_prompt(q, iq, iwt, kb, vt, ikb, n_seq, seq_len, topk):
    n = q.shape[0]
    t = ATT_TILE
    nq = seq_len // t
    once = pl.Buffered(1)
    return pl.pallas_call(
        functools.partial(_attend_prompt_kernel, topk=topk),
        out_shape=jax.ShapeDtypeStruct((n, ATT_WIDTH), BF16),
        grid=(n_seq, nq),
        in_specs=[pl.BlockSpec((t, ATT_WIDTH), lambda b, j: (b * nq + j, 0)),
                  pl.BlockSpec((t, ATT_WIDTH), lambda b, j: (b * nq + j, 0)),
                  pl.BlockSpec((N_IDX_HEADS, t), lambda b, j: (0, b * nq + j)),
                  pl.BlockSpec((seq_len, ATT_WIDTH), lambda b, j: (b, 0), pipeline_mode=once),
                  pl.BlockSpec((nq, VT_ROWS, t), lambda b, j: (b, 0, 0), pipeline_mode=once),
                  pl.BlockSpec((seq_len, LANES), lambda b, j: (b, 0), pipeline_mode=once)],
        out_specs=pl.BlockSpec((t, ATT_WIDTH), lambda b, j: (b * nq + j, 0)),
        scratch_shapes=([pltpu.VMEM((nq, t, t), I32), pltpu.VMEM((nq, t, t), BF16)]
                        + [pltpu.VMEM((t, LANES), BF16)] * N_HEADS
                        + [pltpu.VMEM((HEAD_DIM, t), F32)] * N_HEADS + [pltpu.VMEM((2, t, t), BF16)] * N_HEADS),
        compiler_params=_cparams("arbitrary", "arbitrary"),
        name="attend_prompt",
    )(q, iq, iwt, kb, vt, ikb)


SAMPLE_KEY_TILE = 512


def _wide(a, width):
    return a if width == LANES else jnp.concatenate([a] * (width // LANES), axis=1)


def _row_sums(x_bf16):
    return jnp.dot(x_bf16, jnp.ones((x_bf16.shape[1], LANES), BF16), preferred_element_type=F32)


def _count_rows(planes, cand, strict):
    cb = cand.astype(BF16)
    one, zero = jnp.ones((), BF16), jnp.zeros((), BF16)
    accs = [jnp.zeros(cand.shape, BF16) for _ in range(4)]
    i = 0
    for ref in planes:
        e = ref[...]
        for c in range(e.shape[1] // LANES):
            blk = e[:, c * LANES:(c + 1) * LANES]
            accs[i % 4] = accs[i % 4] + jnp.where((blk > cb) if strict else (blk >= cb), one, zero)
            i += 1
    assert i <= 256
    return _row_sums((accs[0] + accs[1]) + (accs[2] + accs[3]))


def _topk_select_rows(keys, planes, topk):
    rows = planes[0].shape[0]

    def search():
        def bit_body(i, d):
            cand = d + lax.shift_left(jnp.int32(1), jnp.int32(DIGIT_BITS - 1) - i).astype(F32)
            return jnp.where(_count_rows(planes, cand, strict=False) >= float(topk), cand, d)

        return lax.fori_loop(0, DIGIT_BITS, bit_body, jnp.zeros((rows, LANES), F32))

    d = search()
    for phase in range(1, N_DIGITS):
        db = d.astype(BF16)
        for kref, pref in zip(keys, planes):
            e = pref[...]
            dw = _wide(db, e.shape[1])
            decided = jnp.where(e > dw, jnp.asarray(DIGIT_ABOVE, BF16), jnp.asarray(DIGIT_BELOW, BF16))
            pref[...] = jnp.where(e == dw, _digit_plane(kref[...], phase), decided)
        d = search()

    cnt_ge = _count_rows(planes, d, strict=False)

    @pl.when(jnp.max(cnt_ge) > float(topk))
    def _():
        need = float(topk) - _count_rows(planes, d, strict=True)
        seen = jnp.zeros((rows, LANES), F32)
        for pref in planes:
            width = pref.shape[1]
            e = pref[...].astype(F32)
            eq = e == _wide(d, width)
            eqb = jnp.where(eq, 1.0, 0.0).astype(BF16)
            r = lax.broadcasted_iota(I32, (width, width), 0)
            c = lax.broadcasted_iota(I32, (width, width), 1)
            before = jnp.where(r < c, 1.0, 0.0).astype(BF16)
            prior = jnp.dot(eqb, before, preferred_element_type=F32) + _wide(seen, width)
            pref[...] = jnp.where(eq & (prior >= _wide(need, width)), DIGIT_BELOW, e).astype(BF16)
            seen = seen + _row_sums(eqb)

    return d


def _attend_sample_kernel(q_ref, iq_ref, iwr_ref, knt_ref, vnt_ref, iknt_ref, ckt_ref, cvt_ref, cit_ref, o_ref,
                          keys_ref, plane_ref, keys_new_ref, plane_new_ref, lt_ref, lt_new_ref, *, topk, past):
    tq = q_ref.shape[0]
    nct, _, tkc = keys_ref.shape
    rows = N_HEADS * tq
    row = lax.broadcasted_iota(I32, (rows, LANES), 0)
    qpos = past + (row & (tq - 1))
    slope = lax.bitcast_convert_type(
        lax.shift_left(126 - lax.shift_right_logical(row, tq.bit_length() - 1), 23), F32)
    qpos16 = past + lax.broadcasted_iota(I32, (tq, LANES), 0)

    iq = iq_ref[...]
    q = q_ref[...]
    iq_rows = jnp.concatenate([iq[:, h * IDX_DIM:(h + 1) * IDX_DIM] for h in range(N_IDX_HEADS)], axis=0)
    q_rows = jnp.concatenate(
        [_half_mask_wide(q, h) for h in range(N_HEADS)], axis=0) * jnp.asarray(ATTN_SCALE, BF16)
    iw_rows = iwr_ref[...]

    tiles = [(kt * tkc, tkc, cit_ref.at[:, kt * tkc:(kt + 1) * tkc], ckt_ref.at[:, kt * tkc:(kt + 1) * tkc],
              cvt_ref.at[:, kt * tkc:(kt + 1) * tkc], keys_ref.at[kt], plane_ref.at[kt], lt_ref.at[kt])
             for kt in range(nct)]
    tiles.append((past, LANES, iknt_ref, knt_ref, vnt_ref, keys_new_ref, plane_new_ref, lt_new_ref))

    def key_positions(k0, width, n_rows):
        return k0 + lax.broadcasted_iota(I32, (n_rows, width), 1)

    for k0, width, ikt, _, _, kref, pref, _ in tiles:
        s = jnp.dot(iq_rows, ikt[...].astype(BF16), preferred_element_type=F32)
        s = jnp.maximum(s, 0.0) * _wide(iw_rows, width)
        score = s[0:tq, :]
        for h in range(1, N_IDX_HEADS):
            score = score + s[h * tq:(h + 1) * tq, :]
        kpos = key_positions(k0, width, tq)
        adm = lax.shift_right_logical(kpos, CHUNK_SHIFT) <= lax.shift_right_logical(_wide(qpos16, width), CHUNK_SHIFT)
        if k0 == past:
            adm = adm & (kpos < past + tq)
        key = _mono_key(jnp.where(adm, score, -jnp.inf))
        kref[...] = key
        pref[...] = _digit_plane(key, 0)

    d_last = _topk_select_rows([t[5] for t in tiles], [t[6] for t in tiles], topk)

    m_part = jnp.full((rows, LANES), -jnp.inf, F32)
    for k0, width, _, kt_ref, _, kref, pref, ltref in tiles:
        sel = (pref[...].astype(F32) >= _wide(d_last, width)) & (kref[...] > jnp.int32(KEY_NEG_INF))
        sel = jnp.concatenate([jnp.where(sel, 1.0, 0.0)] * N_HEADS, axis=0) > 0.5
        dist = jnp.abs(_wide(qpos, width) - key_positions(k0, width, rows)).astype(F32)
        lt = (jnp.dot(q_rows, kt_ref[...].astype(BF16), preferred_element_type=F32)
              - _wide(slope, width) * jnp.where(sel, dist, MASKED_DIST))
        ltref[...] = lt
        for c in range(width // LANES):
            m_part = jnp.maximum(m_part, lt[:, c * LANES:(c + 1) * LANES])
    m = jnp.broadcast_to(jnp.max(m_part, axis=1, keepdims=True), (rows, LANES))

    acc = jnp.zeros((rows, ATT_WIDTH), F32)
    l_part = jnp.zeros((rows, LANES), F32)
    for _, width, _, _, vt_ref, _, _, ltref in tiles:
        p = jnp.exp(ltref[...] - _wide(m, width))
        for c in range(width // LANES):
            l_part = l_part + p[:, c * LANES:(c + 1) * LANES]
        acc = acc + lax.dot_general(p.astype(BF16), vt_ref[...].astype(BF16), _NT, preferred_element_type=F32)
    out = acc / jnp.sum(l_part, axis=1, keepdims=True)

    out_lane_head = lax.shift_right_logical(lax.broadcasted_iota(I32, (tq, ATT_WIDTH), 1), 6)
    y = jnp.zeros((tq, ATT_WIDTH), F32)
    for h in range(N_HEADS):
        y = y + jnp.where(out_lane_head == h, out[h * tq:(h + 1) * tq, :], 0.0)
    o_ref[...] = y.astype(BF16)


def _half_mask_wide(x, head):
    lane = lax.broadcasted_iota(I32, x.shape, 1)
    keep = lax.shift_right_logical(lane, 6) == head
    return jnp.where(keep, x, jnp.zeros_like(x))


def _attend_sample(q, iq, iw_rows, knt, vnt, iknt, cache_kt, cache_vt, cache_it, topk):
    n_seq, _, past = cache_kt.shape
    tq = q.shape[0] // n_seq
    tkc = SAMPLE_KEY_TILE
    rows = N_HEADS * tq
    assert rows == LANES and past % tkc == 0
    new = lambda width: pl.BlockSpec((tq, width), lambda b: (b, 0))
    per_seq = lambda a: pl.BlockSpec((None,) + a.shape[1:], lambda b: (b, 0, 0))
    return pl.pallas_call(
        functools.partial(_attend_sample_kernel, topk=topk, past=past),
        out_shape=jax.ShapeDtypeStruct((n_seq * tq, ATT_WIDTH), BF16),
        grid=(n_seq,),
        in_specs=[new(ATT_WIDTH), new(ATT_WIDTH), per_seq(iw_rows), per_seq(knt), per_seq(vnt), per_seq(iknt),
                  per_seq(cache_kt), per_seq(cache_vt), per_seq(cache_it)],
        out_specs=new(ATT_WIDTH),
        scratch_shapes=[pltpu.VMEM((past // tkc, tq, tkc), I32), pltpu.VMEM((past // tkc, tq, tkc), BF16),
                        pltpu.VMEM((tq, LANES), I32), pltpu.VMEM((tq, LANES), BF16),
                        pltpu.VMEM((past // tkc, rows, tkc), F32), pltpu.VMEM((rows, LANES), F32)],
        compiler_params=_cparams("arbitrary"),
        name="attend_sample",
    )(q, iq, iw_rows, knt, vnt, iknt, cache_kt, cache_vt, cache_it)


TOK_ROWS = D_MODEL // LANES


def _store_token_tiles(ref, x):
    t = x.shape[0]
    for c in range(TOK_ROWS):
        ref[pl.ds(c, t, stride=TOK_ROWS), :] = x[:, c * LANES:(c + 1) * LANES]


def _load_token_tiles(ref):
    t = ref.shape[0] // TOK_ROWS
    return jnp.concatenate([ref[pl.ds(c, t, stride=TOK_ROWS), :] for c in range(TOK_ROWS)], axis=1)


def _layer_norm(x, g, b):
    mu = jnp.mean(x, axis=-1, keepdims=True)
    xc = x - mu
    var = jnp.mean(xc * xc, axis=-1, keepdims=True)
    return xc * lax.rsqrt(var + LN_EPS) * g + b


def _post_kernel(x_ref, ya_ref, yb_ref, ga_ref, scf_ref, shf_ref, wo_ref, bo_ref, g1_ref, b1_ref,
                 wrh_ref, wrl_ref, br_ref,
                 x1_ref, h2_ref, tope_ref, gate_ref, rank_ref, cnt_ref):
    t = x_ref.shape[0]
    y = (jnp.dot(ya_ref[...], wo_ref[:A_WIDTH, :], preferred_element_type=F32)
         + jnp.dot(yb_ref[...], wo_ref[A_WIDTH:, :], preferred_element_type=F32) + bo_ref[...])
    x1 = _layer_norm(DEEPNORM_ALPHA * x_ref[...] + (1.0 + ga_ref[...]) * y, g1_ref[...], b1_ref[...])
    x1_ref[...] = x1
    h2 = x1 * (1.0 + scf_ref[...]) + shf_ref[...]
    _store_token_tiles(h2_ref, h2)

    hh = h2.astype(BF16)
    hl = (h2 - hh.astype(F32)).astype(BF16)
    logits = (lax.dot_general(wrh_ref[...], hh, _NT, preferred_element_type=F32)
              + lax.dot_general(wrh_ref[...], hl, _NT, preferred_element_type=F32)
              + lax.dot_general(wrl_ref[...], hh, _NT, preferred_element_type=F32) + br_ref[...])
    erow = lax.broadcasted_iota(I32, (N_EXPERTS, t), 0)
    vals, idxs = [], []
    for _ in range(TOP_K):
        v = jnp.max(logits, axis=0, keepdims=True)
        i = jnp.min(jnp.where(logits == v, erow, N_EXPERTS), axis=0, keepdims=True)
        vals.append(v)
        idxs.append(i)
        logits = jnp.where(erow == i, -jnp.inf, logits)
    ex = [jnp.exp(v - vals[0]) for v in vals]
    den = ex[0] + ex[1] + ex[2] + ex[3]
    gate_ref[...] = jnp.concatenate([e / den for e in ex], axis=0)
    tope_ref[...] = jnp.concatenate(idxs, axis=0)

    @pl.when(pl.program_id(0) == 0)
    def _():
        cnt_ref[...] = jnp.zeros_like(cnt_ref)

    hit = jnp.zeros((N_EXPERTS, t), F32)
    for i in idxs:
        hit = hit + jnp.where(erow == i, 1.0, 0.0)
    hitb = hit.astype(BF16)
    r = lax.broadcasted_iota(I32, (t, t), 0)
    c = lax.broadcasted_iota(I32, (t, t), 1)
    earlier = jnp.where(r < c, 1.0, 0.0).astype(BF16)
    before = jnp.dot(hitb, earlier, preferred_element_type=F32) + cnt_ref[...]
    total = jnp.dot(hitb, jnp.ones((t, t), BF16), preferred_element_type=F32)
    rank_ref[...] = jnp.concatenate(
        [jnp.sum(jnp.where(erow == i, before, 0.0), axis=0, keepdims=True) for i in idxs], axis=0).astype(I32)
    cnt_ref[...] += total


def _post(x, ya, yb, ga, scf, shf, w_o, b_o, g1, b1, wrh, wrl, br):
    n, d = x.shape
    t = POST_TILE
    nt = n // t
    if ga.shape[0] == n:
        mod_spec = pl.BlockSpec((t, d), lambda i: (i, 0))
    else:
        tiles_per_seq = nt // ga.shape[0]
        ga, scf, shf = (a.reshape(a.shape[0], 1, d) for a in (ga, scf, shf))
        mod_spec = pl.BlockSpec((None, 1, d), lambda i: (i // tiles_per_seq, 0, 0))
    row = lambda w: pl.BlockSpec((t, w), lambda i: (i, 0))
    col = lambda r: pl.BlockSpec((r, t), lambda i: (0, i))
    const = lambda a: pl.BlockSpec(a.shape, lambda i: (0, 0))
    return pl.pallas_call(
        _post_kernel,
        out_shape=(jax.ShapeDtypeStruct((n, d), F32), jax.ShapeDtypeStruct((n * TOK_ROWS, LANES), F32),
                   jax.ShapeDtypeStruct((TOP_K, n), I32), jax.ShapeDtypeStruct((TOP_K, n), F32),
                   jax.ShapeDtypeStruct((TOP_K, n), I32), jax.ShapeDtypeStruct((N_EXPERTS, t), F32)),
        grid=(nt,),
        in_specs=[row(d), row(A_WIDTH), row(ATT_WIDTH), mod_spec, mod_spec, mod_spec,
                  const(w_o), const(b_o), const(g1), const(b1), const(wrh), const(wrl), const(br)],
        out_specs=(row(d), pl.BlockSpec((t * TOK_ROWS, LANES), lambda i: (i, 0)), col(TOP_K), col(TOP_K), col(TOP_K),
                   pl.BlockSpec((N_EXPERTS, t), lambda i: (0, 0))),
        compiler_params=_cparams("arbitrary"),
        name="post",
    )(x, ya, yb, ga, scf, shf, w_o, b_o, g1, b1, wrh, wrl, br)


def _token_rows(ref, r):
    return ref.at[pl.ds(pl.multiple_of(r * TOK_ROWS, TOK_ROWS), TOK_ROWS)]


def _token_copies_wait(hbm_ref, vmem_ref, sem, n_tokens):
    rows = n_tokens * TOK_ROWS
    pltpu.make_async_copy(hbm_ref.at[pl.ds(0, rows)], vmem_ref.at[pl.ds(0, rows)], sem).wait()


def _dispatch_kernel(zrow_ref, dest_ref, h_ref, xp_ref, zero_ref, sem):
    t = h_ref.shape[0] // TOK_ROWS

    block_rows = MOE_BLOCK * TOK_ROWS

    def zero_block(slot0):
        z0 = pl.multiple_of(slot0 * TOK_ROWS, block_rows)
        cp = pltpu.make_async_copy(zero_ref, xp_ref.at[pl.ds(z0, block_rows)], sem)
        cp.start()
        cp.wait()

    @pl.when(pl.program_id(0) == 0)
    def _():
        zero_ref[...] = jnp.zeros_like(zero_ref)
        for e in range(N_EXPERTS):
            @pl.when(zrow_ref[e] >= 0)
            def _():
                zero_block(zrow_ref[e])

        def unused(b, _):
            zero_block(b * MOE_BLOCK)
            return 0

        lax.fori_loop(zrow_ref[N_EXPERTS], xp_ref.shape[0] // block_rows, unused, 0)

    def body(i, _):
        for k in range(TOP_K):
            pltpu.make_async_copy(_token_rows(h_ref, i), _token_rows(xp_ref, dest_ref[k, i]), sem).start(priority=k % 2)
        return 0

    lax.fori_loop(0, t, body, 0, unroll=DMA_ISSUE_UNROLL)
    for k in range(TOP_K):
        _token_copies_wait(xp_ref, h_ref, sem, t)


def _dispatch(zrow, dest, h2, n_slots):
    t = ROW_TILE
    n = h2.shape[0] // TOK_ROWS
    grid_spec = pltpu.PrefetchScalarGridSpec(
        num_scalar_prefetch=1,
        grid=(n // t,),
        in_specs=[pl.BlockSpec((TOP_K, t), lambda i, z: (0, i), memory_space=pltpu.SMEM),
                  pl.BlockSpec((t * TOK_ROWS, LANES), lambda i, z: (i, 0))],
        out_specs=pl.BlockSpec(memory_space=pl.ANY),
        scratch_shapes=[pltpu.VMEM((MOE_BLOCK * TOK_ROWS, LANES), F32), pltpu.SemaphoreType.DMA],
    )
    return pl.pallas_call(
        _dispatch_kernel,
        out_shape=jax.ShapeDtypeStruct((n_slots * TOK_ROWS, LANES), F32),
        grid_spec=grid_spec,
        compiler_params=_cparams("arbitrary"),
        name="dispatch",
    )(zrow, dest, h2)


def _experts_kernel(be_ref, bi_ref, nu_ref, x_ref, wg_ref, bg_ref, wu_ref, bu_ref, wd_ref, bd_ref, y_ref,
                    wgb_ref, wub_ref, wdb_ref):
    i = pl.program_id(0)

    @pl.when((i == 0) | (be_ref[i] != be_ref[jnp.maximum(i - 1, 0)]))
    def _():
        wgb_ref[...] = wg_ref[...].astype(BF16)
        wub_ref[...] = wu_ref[...].astype(BF16)
        wdb_ref[...] = wd_ref[...].astype(BF16)

    @pl.when(i < nu_ref[0])
    def _():
        x = _load_token_tiles(x_ref).astype(BF16)
        g = jnp.minimum(jnp.dot(x, wgb_ref[...], preferred_element_type=F32) + bg_ref[...], SWIGLU_LIMIT)
        u = jnp.clip(jnp.dot(x, wub_ref[...], preferred_element_type=F32) + bu_ref[...], -SWIGLU_LIMIT, SWIGLU_LIMIT)
        a = g * jax.nn.sigmoid(SWIGLU_ALPHA * g)
        mid = ((u + 1.0) * a).astype(BF16)
        _store_token_tiles(y_ref, jnp.dot(mid, wdb_ref[...], preferred_element_type=F32) + bd_ref[...])

    @pl.when(pl.program_id(0) >= nu_ref[0])
    def _():
        y_ref[...] = jnp.zeros_like(y_ref)


def _experts(block_e, block_i, n_used, xp, wg, bg, wu, bu, wd, bd):
    d, f = wg.shape[1], wg.shape[2]
    nb = xp.shape[0] // (MOE_BLOCK * TOK_ROWS)
    wspec = lambda a, b: pl.BlockSpec((None, a, b), lambda i, be, bi, nu: (be[i], 0, 0))
    slots = pl.BlockSpec((MOE_BLOCK * TOK_ROWS, LANES), lambda i, be, bi, nu: (bi[i], 0))
    grid_spec = pltpu.PrefetchScalarGridSpec(
        num_scalar_prefetch=3,
        grid=(nb,),
        in_specs=[slots, wspec(d, f), wspec(1, f), wspec(d, f), wspec(1, f), wspec(f, d), wspec(1, d)],
        out_specs=pl.BlockSpec((MOE_BLOCK * TOK_ROWS, LANES), lambda i, be, bi, nu: (i, 0)),
        scratch_shapes=[pltpu.VMEM((d, f), BF16), pltpu.VMEM((d, f), BF16), pltpu.VMEM((f, d), BF16)],
    )
    return pl.pallas_call(
        _experts_kernel,
        out_shape=jax.ShapeDtypeStruct(xp.shape, F32),
        grid_spec=grid_spec,
        compiler_params=_cparams("arbitrary"),
        name="experts",
    )(block_e, block_i, n_used, xp, wg, bg, wu, bu, wd, bd)


def _combine_kernel(dest_ref, dest_next_ref, gate_ref, x1_ref, gf_ref, g2_ref, b2_ref, yp_ref, o_ref, buf_ref, sems):
    t = x1_ref.shape[0]
    step = pl.program_id(0)
    slot = step % 2

    def start_gather(dst_ref, into):
        def body(i, _):
            for k in range(TOP_K):
                pltpu.make_async_copy(_token_rows(yp_ref, dst_ref[k, i]), _token_rows(buf_ref.at[into, k], i),
                                      sems.at[into]).start(priority=k % 2)
            return 0

        lax.fori_loop(0, t, body, 0, unroll=DMA_ISSUE_UNROLL)

    @pl.when(step == 0)
    def _():
        start_gather(dest_ref, 0)

    @pl.when(step + 1 < pl.num_programs(0))
    def _():
        start_gather(dest_next_ref, 1 - slot)

    for k in range(TOP_K):
        _token_copies_wait(yp_ref, buf_ref.at[slot, k], sems.at[slot], t)

    gates = jnp.concatenate([gate_ref[...], jnp.zeros((LANES - TOP_K, t), F32)], axis=0).T
    f = gates[:, 0:1] * _load_token_tiles(buf_ref.at[slot, 0])
    for k in range(1, TOP_K):
        f = f + gates[:, k:k + 1] * _load_token_tiles(buf_ref.at[slot, k])
    o_ref[...] = _layer_norm(DEEPNORM_ALPHA * x1_ref[...] + (1.0 + gf_ref[...]) * f, g2_ref[...], b2_ref[...])


def _combine(dest, gates, x1, gf, g2, b2, yp):
    n, d = x1.shape
    t = ROW_TILE
    nt = n // t
    if gf.shape[0] == n:
        mod_spec = pl.BlockSpec((t, d), lambda i: (i, 0))
    else:
        tiles_per_seq = nt // gf.shape[0]
        gf = gf.reshape(gf.shape[0], 1, d)
        mod_spec = pl.BlockSpec((None, 1, d), lambda i: (i // tiles_per_seq, 0, 0))
    return pl.pallas_call(
        _combine_kernel,
        out_shape=jax.ShapeDtypeStruct((n, d), F32),
        grid=(nt,),
        in_specs=[pl.BlockSpec((TOP_K, t), lambda i: (0, i), memory_space=pltpu.SMEM),
                  pl.BlockSpec((TOP_K, t), lambda i: (0, jnp.minimum(i + 1, nt - 1)), memory_space=pltpu.SMEM),
                  pl.BlockSpec((TOP_K, t), lambda i: (0, i)),
                  pl.BlockSpec((t, d), lambda i: (i, 0)), mod_spec,
                  pl.BlockSpec((1, d), lambda i: (0, 0)), pl.BlockSpec((1, d), lambda i: (0, 0)),
                  pl.BlockSpec(memory_space=pl.ANY)],
        out_specs=pl.BlockSpec((t, d), lambda i: (i, 0)),
        scratch_shapes=[pltpu.VMEM((2, TOP_K, t * TOK_ROWS, LANES), F32), pltpu.SemaphoreType.DMA((2,))],
        compiler_params=_cparams("arbitrary"),
        name="combine",
    )(dest, dest, gates, x1, gf, g2, b2, yp)


def _moe(h2, tope, rank, counts, gates, x1, gf, g2, b2, experts_w):
    n = tope.shape[1]
    nb = (n * TOP_K + N_EXPERTS * (MOE_BLOCK - 1) + MOE_BLOCK - 1) // MOE_BLOCK
    pcounts = (counts + MOE_BLOCK - 1) // MOE_BLOCK * MOE_BLOCK
    pend = jnp.cumsum(pcounts)
    pstart = pend - pcounts
    eids = jnp.arange(N_EXPERTS, dtype=I32).reshape(N_EXPERTS, 1, 1)
    dest = rank + jnp.sum(jnp.where(tope[None] == eids, pstart.reshape(N_EXPERTS, 1, 1), 0), axis=0).astype(I32)
    n_used = (pend[-1] // MOE_BLOCK).astype(I32)
    zrow = jnp.concatenate([jnp.where(counts > 0, pend - MOE_BLOCK, -1), n_used.reshape(1)]).astype(I32)
    blk = jnp.arange(nb, dtype=I32)
    block_i = jnp.minimum(blk, n_used - 1)
    block_e = jnp.minimum(jnp.sum(pend[None, :] <= (block_i * MOE_BLOCK)[:, None], axis=1), N_EXPERTS - 1).astype(I32)
    xp = _dispatch(zrow, dest, h2, nb * MOE_BLOCK)
    yp = _experts(block_e, block_i, n_used.reshape(1), xp, *experts_w)
    return _combine(dest, gates, x1, gf, g2, b2, yp)


def kernel(x_prompt, x_sample, c_prompt, c_sample, cache_k, cache_v, cache_kidx, w_in, a_ln_g, a_ln_b, a_ws, a_bs,
           w_o, b_o, w_c, b_c, ln1_g, ln1_b, ln2_g, ln2_b, w_router, b_router, w_gate, b_gate, w_up, b_up,
           w_down, b_down):
    bp, s, d = x_prompt.shape
    bs, ts, _ = x_sample.shape
    past = cache_k.shape[2]
    np_, ns = bp * s, bs * ts

    wi = w_in[0]
    w_in_p = jnp.zeros((d, IN_COLS_PAD), F32)
    w_in_p = w_in_p.at[:, :OFF_IK].set(wi[:, :OFF_IK])
    w_in_p = w_in_p.at[:, OFF_IK:OFF_IK + IDX_DIM].set(wi[:, OFF_IK:OFF_IK + IDX_DIM])
    w_in_p = w_in_p.at[:, OFF_IK + IDX_DIM:OFF_IK + 2 * IDX_DIM].set(wi[:, OFF_IK:OFF_IK + IDX_DIM])
    w_in_p = w_in_p.at[:, OFF_IW:OFF_IW + N_IDX_HEADS].set(wi[:, OFF_IK + IDX_DIM:OFF_IK + IDX_DIM + N_IDX_HEADS])
    w_in_p = w_in_p.astype(BF16)
    lng, lnb = a_ln_g[0].reshape(1, A_WIDTH), a_ln_b[0].reshape(1, A_WIDTH)
    wtril = jnp.tril(a_ws[0])
    wm_p = wtril.astype(BF16)
    bm_p = jnp.broadcast_to(a_bs[0][:, :, None], (A_GROUPS, A_CHUNK, A_GROUP_DIM)).astype(F32)
    rep = A_CHUNK // ts
    wm_s = jnp.einsum("ab,gij->gaibj", jnp.eye(rep, dtype=F32), wtril[:, :ts, :ts]).reshape(
        A_GROUPS, A_CHUNK, A_CHUNK).astype(BF16)
    bm_s = jnp.broadcast_to(jnp.tile(a_bs[0][:, :ts], (1, rep))[:, :, None], (A_GROUPS, A_CHUNK, A_GROUP_DIM)).astype(F32)
    w_o_b = w_o[0].astype(BF16)
    b_o_r = b_o[0].reshape(1, d)
    wr_t = w_router[0].T
    wrh = wr_t.astype(BF16)
    wrl = (wr_t - wrh.astype(F32)).astype(BF16)
    br = jnp.broadcast_to(b_router[0][:, None], (N_EXPERTS, POST_TILE)).astype(F32)
    experts_w = (w_gate[0], b_gate[0][:, None, :], w_up[0], b_up[0][:, None, :], w_down[0], b_down[0][:, None, :])
    g1, b1 = ln1_g[0].reshape(1, d), ln1_b[0].reshape(1, d)
    g2, b2 = ln2_g[0].reshape(1, d), ln2_b[0].reshape(1, d)

    mods = _cond_mods(jnp.concatenate([c_prompt, c_sample], axis=0), w_c[0], b_c[0]).reshape(bp + bs, 6, d)
    mods_p = [mods[:bp, i] for i in range(6)]
    mods_s = [jnp.repeat(mods[bp:, i], ts, axis=0) for i in range(6)]

    xp2 = x_prompt.reshape(np_, d)
    ya, q, kft, kb, _, vft, vt, iq, ikft, ikb, iwt, _ = _project(
        xp2, mods_p[1], mods_p[0], w_in_p, lng, lnb, wm_p, bm_p, bp)
    yb = _attend_prompt(q, iq, iwt, kb, vt, ikb, bp, s, min(TOPK_MAX, s // 4))
    x1, h2, tope, gates, rank, cnt = _post(xp2, ya, yb, mods_p[2], mods_p[4], mods_p[3], w_o_b, b_o_r, g1, b1, wrh, wrl, br)
    y_p = _moe(h2, tope, rank, cnt[:, 0].astype(I32), gates, x1, mods_p[5], g2, b2, experts_w)
    heads_last = lambda a: jnp.transpose(a.reshape(1, bp, N_HEADS, HEAD_DIM, s), (0, 1, 4, 2, 3))
    out_p = (y_p.reshape(bp, s, d), heads_last(kft), heads_last(vft),
             jnp.transpose(ikft.reshape(1, bp, IDX_DIM, s), (0, 1, 3, 2)))

    xs2 = x_sample.reshape(ns, d)
    ya, q, kft, _, vf, vft, _, iq, ikft, _, iwt, va = _project(
        xs2, mods_s[1], mods_s[0], w_in_p, lng, lnb, wm_s, bm_s, 1)
    kf, ikf = kft[0].T, ikft[0].T
    per_seq_t = lambda a: jnp.pad(jnp.transpose(a[0].reshape(a.shape[1], bs, ts), (1, 0, 2)),
                                  ((0, 0), (0, 0), (0, LANES - ts)))
    iw_rows = jnp.broadcast_to(
        jnp.transpose(iwt.reshape(N_IDX_HEADS, bs, ts), (1, 0, 2)).reshape(bs, N_IDX_HEADS * ts, 1),
        (bs, N_IDX_HEADS * ts, LANES))
    cache_kt = jnp.transpose(cache_k[0], (0, 2, 3, 1)).reshape(bs, ATT_WIDTH, past)
    cache_vt = jnp.transpose(cache_v[0], (0, 2, 3, 1)).reshape(bs, ATT_WIDTH, past)
    cache_it = jnp.transpose(cache_kidx[0], (0, 2, 1))
    yb = _attend_sample(q, iq, iw_rows, per_seq_t(kft), per_seq_t(vft), per_seq_t(ikft), cache_kt, cache_vt, cache_it,
                        min(TOPK_MAX, (past + ts) // 4))
    x1, h2, tope, gates, rank, cnt = _post(xs2, ya, yb, mods_s[2], mods_s[4], mods_s[3], w_o_b, b_o_r, g1, b1, wrh, wrl, br)
    y_s = _moe(h2, tope, rank, cnt[:, 0].astype(I32), gates, x1, mods_s[5], g2, b2, experts_w)

    return (out_p[0], y_s.reshape(bs, ts, d), out_p[1], out_p[2], out_p[3],
            kf.reshape(1, bs, ts, N_HEADS, HEAD_DIM), vf.reshape(1, bs, ts, N_HEADS, HEAD_DIM),
            ikf.reshape(1, bs, ts, IDX_DIM), va.reshape(1, bs, ts, A_WIDTH))
```

```python
import functools

import jax
import jax.numpy as jnp
from jax import lax
from jax.experimental import pallas as pl
from jax.experimental.pallas import tpu as pltpu

F32 = jnp.float32
BF16 = jnp.bfloat16
I32 = jnp.int32

D_MODEL = 1024
CHUNK_SHIFT = 6
A_GROUPS = 4
A_GROUP_DIM = 128
A_WIDTH = A_GROUPS * A_GROUP_DIM
A_CHUNK = 128
N_HEADS = 8
HEAD_DIM = 64
ATT_WIDTH = N_HEADS * HEAD_DIM
N_IDX_HEADS = 8
IDX_DIM = 64
TOPK_MAX = 256
ATTN_SCALE = HEAD_DIM ** -0.5
VT_HEAD_ROWS = HEAD_DIM + 16
VT_ROWS = N_HEADS * VT_HEAD_ROWS
IDX_W_SCALE = (N_IDX_HEADS ** -0.5) * (IDX_DIM ** -0.5)
N_EXPERTS = 32
TOP_K = 4
SWIGLU_LIMIT = 7.0
SWIGLU_ALPHA = 1.702
DEEPNORM_ALPHA = 2.0 ** 0.25
LN_EPS = 1e-5

LANES = 128
SUBLANES = 8
VMEM_LIMIT_BYTES = 56 * 1024 * 1024

ROW_TILE = 256
PROJECT_TILE = 512
POST_TILE = 512
MOE_BLOCK = 256
ATT_TILE = 256
SCORE_GROUP = 4
DMA_ISSUE_UNROLL = 8
STATIC_TILE_UNROLL = 4

OFF_AU, OFF_AV, OFF_Q, OFF_K, OFF_V, OFF_IQ = 0, 512, 1024, 1536, 2048, 2560
OFF_IK = 3072
OFF_IW = 3200
IN_COLS_PAD = 3328

MASKED_DIST = 3.0e32
F32_MAX = 3.4028234663852886e38
INT32_MIN = -(2 ** 31)
KEY_NEG_INF = INT32_MIN + 0x7FFFFF

_NT = (((1,), (1,)), ((), ()))


def _cparams(*sem):
    return pltpu.CompilerParams(dimension_semantics=sem, vmem_limit_bytes=VMEM_LIMIT_BYTES)


def _mods_kernel(c_ref, w_ref, b_ref, o_ref):
    c = c_ref[...]
    s = c * jax.nn.sigmoid(c)
    o_ref[...] = jnp.dot(s.astype(BF16), w_ref[...].astype(BF16), preferred_element_type=F32) + b_ref[...]


def _cond_mods(c, w_c, b_c):
    nb, d = c.shape
    n_out = w_c.shape[1]
    return pl.pallas_call(
        _mods_kernel,
        out_shape=jax.ShapeDtypeStruct((nb, n_out), F32),
        grid=(n_out // d,),
        in_specs=[pl.BlockSpec((nb, d), lambda j: (0, 0)),
                  pl.BlockSpec((d, d), lambda j: (0, j)),
                  pl.BlockSpec((1, d), lambda j: (0, j))],
        out_specs=pl.BlockSpec((nb, d), lambda j: (0, j)),
        compiler_params=_cparams("arbitrary"),
        name="mods",
    )(c, w_c, b_c.reshape(1, n_out))


def _gelu(x):
    return 0.5 * x * (1.0 + lax.erf(x * 0.7071067811865476))


def _project_kernel(x_ref, sc_ref, sh_ref, w_ref, lng_ref, lnb_ref, wm_ref, bm_ref,
                    ya_ref, q_ref, kft_ref, kb_ref, vf_ref, vft_ref, vt_ref, iq_ref, ikft_ref, ikb_ref, iwt_ref,
                    va_ref):
    t = x_ref.shape[0]
    h = (x_ref[...] * (1.0 + sc_ref[...]) + sh_ref[...]).astype(BF16)

    def proj(c0, n):
        return jnp.dot(h, w_ref[:, c0:c0 + n], preferred_element_type=F32)

    u = _gelu(proj(OFF_AU, A_WIDTH))
    gv = _gelu(proj(OFF_AV, A_WIDTH))
    for g in range(A_GROUPS):
        lo, hi = g * A_GROUP_DIM, (g + 1) * A_GROUP_DIM
        xg = gv[:, lo:hi]
        mu = jnp.mean(xg, axis=-1, keepdims=True)
        xc = xg - mu
        var = jnp.mean(xc * xc, axis=-1, keepdims=True)
        vg = xc * lax.rsqrt(var + LN_EPS) * lng_ref[:, lo:hi] + lnb_ref[:, lo:hi]
        va_ref[:, lo:hi] = vg
        vgb = vg.astype(BF16)
        for c in range(t // A_CHUNK):
            r0, r1 = c * A_CHUNK, (c + 1) * A_CHUNK
            mixed = jnp.dot(wm_ref[g], vgb[r0:r1, :], preferred_element_type=F32) + bm_ref[g]
            ya_ref[r0:r1, lo:hi] = (u[r0:r1, lo:hi] * mixed).astype(BF16)

    q_ref[...] = proj(OFF_Q, ATT_WIDTH).astype(BF16)
    k = proj(OFF_K, ATT_WIDTH)
    kft_ref[...] = k.T
    kb_ref[...] = k.astype(BF16)
    v = proj(OFF_V, ATT_WIDTH)
    vf_ref[...] = v
    v_t32 = v.T
    vft_ref[...] = v_t32
    v_t = v_t32.astype(BF16)
    ones = jnp.ones((VT_HEAD_ROWS - HEAD_DIM, ATT_TILE), BF16)
    for c in range(t // ATT_TILE):
        cols = slice(c * ATT_TILE, (c + 1) * ATT_TILE)
        vt_ref[c] = jnp.concatenate(
            [blk for h in range(N_HEADS) for blk in (v_t[h * HEAD_DIM:(h + 1) * HEAD_DIM, cols], ones)], axis=0)
    iq_ref[...] = proj(OFF_IQ, N_IDX_HEADS * IDX_DIM).astype(BF16)
    ik2 = proj(OFF_IK, LANES)
    ikft_ref[...] = ik2.T[:IDX_DIM, :]
    ikb_ref[...] = ik2.astype(BF16)
    iw = proj(OFF_IW, LANES) * IDX_W_SCALE
    iwt_ref[...] = iw.T[:N_IDX_HEADS, :]


def _project(x, sc, sh, w_in_p, lng, lnb, wm, bm, n_out_seq):
    n, d = x.shape
    t = PROJECT_TILE
    nt = n // t
    tiles_per_out = nt // n_out_seq
    trans = lambda rows: pl.BlockSpec((None, rows, t), lambda i: (i // tiles_per_out, 0, i % tiles_per_out))
    if sc.shape[0] == n:
        mod_spec = pl.BlockSpec((t, d), lambda i: (i, 0))
    else:
        tiles_per_seq = nt // sc.shape[0]
        sc = sc.reshape(sc.shape[0], 1, d)
        sh = sh.reshape(sh.shape[0], 1, d)
        mod_spec = pl.BlockSpec((None, 1, d), lambda i: (i // tiles_per_seq, 0, 0))
    row = lambda w: pl.BlockSpec((t, w), lambda i: (i, 0))
    const2 = lambda a: pl.BlockSpec(a.shape, lambda i: (0, 0))
    const3 = lambda a: pl.BlockSpec(a.shape, lambda i: (0, 0, 0))
    out_shape = (
        jax.ShapeDtypeStruct((n, A_WIDTH), BF16),
        jax.ShapeDtypeStruct((n, ATT_WIDTH), BF16),
        jax.ShapeDtypeStruct((n_out_seq, ATT_WIDTH, n // n_out_seq), F32),
        jax.ShapeDtypeStruct((n, ATT_WIDTH), BF16),
        jax.ShapeDtypeStruct((n, ATT_WIDTH), F32),
        jax.ShapeDtypeStruct((n_out_seq, ATT_WIDTH, n // n_out_seq), F32),
        jax.ShapeDtypeStruct((n // ATT_TILE, VT_ROWS, ATT_TILE), BF16),
        jax.ShapeDtypeStruct((n, ATT_WIDTH), BF16),
        jax.ShapeDtypeStruct((n_out_seq, IDX_DIM, n // n_out_seq), F32),
        jax.ShapeDtypeStruct((n, LANES), BF16),
        jax.ShapeDtypeStruct((N_IDX_HEADS, n), F32),
        jax.ShapeDtypeStruct((n, A_WIDTH), F32),
    )
    out_specs = (row(A_WIDTH), row(ATT_WIDTH), trans(ATT_WIDTH), row(ATT_WIDTH), row(ATT_WIDTH), trans(ATT_WIDTH),
                 pl.BlockSpec((t // ATT_TILE, VT_ROWS, ATT_TILE), lambda i: (i, 0, 0)),
                 row(ATT_WIDTH), trans(IDX_DIM), row(LANES),
                 pl.BlockSpec((N_IDX_HEADS, t), lambda i: (0, i)),
                 row(A_WIDTH))
    return pl.pallas_call(
        _project_kernel,
        out_shape=out_shape,
        grid=(nt,),
        in_specs=[row(d), mod_spec, mod_spec, const2(w_in_p), const2(lng), const2(lnb), const3(wm), const3(bm)],
        out_specs=out_specs,
        compiler_params=_cparams("arbitrary"),
        name="project",
    )(x, sc, sh, w_in_p, lng, lnb, wm, bm)


DIGIT_BITS = 8
N_DIGITS = 32 // DIGIT_BITS
DIGIT_ABOVE = 512.0
DIGIT_BELOW = -1.0
PACKED_ROWS = 16


def _static_unroll(trips):
    return STATIC_TILE_UNROLL if isinstance(trips, int) else 1


def _tile_loop(nkt, body, init):
    if isinstance(nkt, int):
        return lax.fori_loop(0, nkt, body, init, unroll=STATIC_TILE_UNROLL)
    pairs = lax.shift_right_logical(nkt, 1)
    carry = lax.fori_loop(0, pairs, lambda i, c: body(2 * i + 1, body(2 * i, c)), init)
    return lax.fori_loop(2 * pairs, nkt, body, carry)


def _mono_key(x):
    b = lax.bitcast_convert_type(x, I32)
    return jnp.where(b >= 0, b, b ^ jnp.int32(0x7FFFFFFF))


def _digit_plane(key, phase):
    shift = 32 - DIGIT_BITS * (phase + 1)
    d = lax.shift_right_arithmetic(key, jnp.int32(shift)) if shift else key
    d = d + (1 << (DIGIT_BITS - 1)) if phase == 0 else d & ((1 << DIGIT_BITS) - 1)
    return d.astype(F32).astype(BF16)


def _count_plane(plane_ref, nkt, cand, strict):
    _, tk, w = plane_ref.shape
    cb = cand.astype(BF16)
    one, zero = jnp.ones((), BF16), jnp.zeros((), BF16)

    def body(kt, cnt):
        e = plane_ref[kt]
        accs = [jnp.zeros((PACKED_ROWS, w), BF16) for _ in range(4)]
        for r in range(tk // PACKED_ROWS):
            blk = e[r * PACKED_ROWS:(r + 1) * PACKED_ROWS, :]
            accs[r % 4] = accs[r % 4] + jnp.where((blk > cb) if strict else (blk >= cb), one, zero)
        return cnt + ((accs[0] + accs[1]) + (accs[2] + accs[3])).astype(F32)

    cnt = _tile_loop(nkt, body, jnp.zeros((PACKED_ROWS, w), F32))
    return jnp.sum(cnt, axis=0, keepdims=True)


def _search_digit(plane_ref, nkt, topk):
    w = plane_ref.shape[2]

    def bit_body(i, d):
        cand = d + lax.shift_left(jnp.int32(1), jnp.int32(DIGIT_BITS - 1) - i).astype(F32)
        cnt = _count_plane(plane_ref, nkt, cand, strict=False)
        return jnp.where(cnt >= float(topk), cand, d)

    return lax.fori_loop(0, DIGIT_BITS, bit_body, jnp.zeros((1, w), F32))


def _topk_select(keys_ref, plane_ref, nkt, topk):
    _, tk, w = plane_ref.shape
    d = _search_digit(plane_ref, nkt, topk)
    for phase in range(1, N_DIGITS):
        db = d.astype(BF16)

        def refine(kt, _, phase=phase, db=db):
            e = plane_ref[kt]
            decided = jnp.where(e > db, jnp.asarray(DIGIT_ABOVE, BF16), jnp.asarray(DIGIT_BELOW, BF16))
            plane_ref[kt] = jnp.where(e == db, _digit_plane(keys_ref[kt], phase), decided)
            return 0

        _tile_loop(nkt, refine, 0)
        d = _search_digit(plane_ref, nkt, topk)

    cnt_ge = _count_plane(plane_ref, nkt, d, strict=False)

    @pl.when(jnp.max(cnt_ge) > float(topk))
    def _():
        need = float(topk) - _count_plane(plane_ref, nkt, d, strict=True)
        r = lax.broadcasted_iota(I32, (tk, tk), 0)
        c = lax.broadcasted_iota(I32, (tk, tk), 1)
        before = jnp.where(c < r, 1.0, 0.0).astype(BF16)

        def body(kt, seen):
            e = plane_ref[kt].astype(F32)
            eq = e == d
            eqf = jnp.where(eq, 1.0, 0.0)
            prior = jnp.dot(before, eqf.astype(BF16), preferred_element_type=F32) + seen
            plane_ref[kt] = jnp.where(eq & (prior >= need), DIGIT_BELOW, e).astype(BF16)
            return seen + jnp.sum(eqf.reshape(tk // SUBLANES, SUBLANES, w), axis=0).sum(axis=0, keepdims=True)

        lax.fori_loop(0, nkt, body, jnp.zeros((1, w), F32))

    return d


def _selected(keys_ref, plane_ref, kt, d):
    return (plane_ref[kt].astype(F32) >= d) & (keys_ref[kt] > jnp.int32(KEY_NEG_INF))


def _half_mask(x_pair, head):
    lane = lax.broadcasted_iota(I32, x_pair.shape, 1)
    keep = (lane >= HEAD_DIM) if head % 2 else (lane < HEAD_DIM)
    return jnp.where(keep, x_pair, jnp.zeros_like(x_pair))


def _attend_prompt_kernel(q_ref, iq_ref, iwt_ref, k_ref, vt_ref, ik_ref, o_ref, keys_ref, plane_ref, *head_refs, topk):
    qh_refs, acc_refs, lt_refs = (head_refs[i * N_HEADS:(i + 1) * N_HEADS] for i in range(3))
    tq = q_ref.shape[0]
    tk = keys_ref.shape[1]
    j = pl.program_id(1)
    nkt = j + 1
    q0 = j * tq
    row = lax.broadcasted_iota(I32, (tk, tq), 0)
    lane = lax.broadcasted_iota(I32, (tk, tq), 1)
    qpos = q0 + lane

    iq = iq_ref[...]
    iqm = [_half_mask(iq[:, (h // 2) * LANES:(h // 2 + 1) * LANES], h) for h in range(N_IDX_HEADS)]
    iw = iwt_ref[...]

    def score_tiles(kt, n, diagonal):
        k0 = pl.multiple_of(kt * tk, tk)
        ikt = ik_ref[pl.ds(k0, n * tk), :]
        s = jnp.zeros((n * tk, tq), F32)
        for h in range(N_IDX_HEADS):
            r = lax.dot_general(ikt, iqm[h], _NT, preferred_element_type=F32)
            s = s + jnp.maximum(r, 0.0) * iw[h:h + 1, :]
        if diagonal:
            adm = lax.shift_right_logical(k0 + row, CHUNK_SHIFT) <= lax.shift_right_logical(qpos, CHUNK_SHIFT)
            s = jnp.where(adm, s, -jnp.inf)
        for c in range(n):
            key = _mono_key(s[c * tk:(c + 1) * tk, :])
            keys_ref[kt + c] = key
            plane_ref[kt + c] = _digit_plane(key, 0)
        return 0

    assert tk == tq
    groups = j // SCORE_GROUP
    lax.fori_loop(0, groups, lambda i, _: score_tiles(SCORE_GROUP * i, SCORE_GROUP, False), 0)
    lax.fori_loop(SCORE_GROUP * groups, j, lambda kt, _: score_tiles(kt, 1, False), 0)
    score_tiles(j, 1, True)

    d_last = _topk_select(keys_ref, plane_ref, nkt, topk)

    def dist_body(kt, _):
        kpos = kt * tk + row
        dist = jnp.abs(qpos - kpos).astype(F32)
        masked = jnp.where(_selected(keys_ref, plane_ref, kt, d_last), dist, MASKED_DIST)
        keys_ref[kt] = lax.bitcast_convert_type(masked, I32)
        return 0

    lax.fori_loop(0, nkt, dist_body, 0)

    qfull = q_ref[...]
    for h in range(N_HEADS):
        pair = h // 2
        qh_refs[h][...] = _half_mask(qfull[:, pair * LANES:(pair + 1) * LANES], h) * jnp.asarray(ATTN_SCALE, BF16)
        acc_refs[h][...] = jnp.zeros_like(acc_refs[h])

    def logits(kt, slot, m_all):
        k0 = pl.multiple_of(kt * tk, tk)
        dist = lax.bitcast_convert_type(keys_ref[kt], F32)
        ms = []
        for h in range(N_HEADS):
            pair = h // 2
            slope = 2.0 ** (-8.0 * (h + 1) / N_HEADS)
            kp = k_ref[pl.ds(k0, tk), pair * LANES:(pair + 1) * LANES]
            lt = lax.dot_general(kp, qh_refs[h][...], _NT, preferred_element_type=F32) - slope * dist
            lt_refs[h][slot] = lt.astype(BF16)
            tile_max = jnp.max(lt, axis=0, keepdims=True).astype(BF16).astype(F32)
            ms.append(jnp.maximum(m_all[h:h + 1, :], tile_max))
        return jnp.concatenate(ms, axis=0)

    def att_body(kt, carry):
        m_prev, m_cur, l_all = carry
        slot = kt % 2
        m_next = logits(jnp.minimum(kt + 1, nkt - 1), 1 - slot, m_cur)
        ls = []
        for h in range(N_HEADS):
            alpha = jnp.exp(m_prev[h:h + 1, :] - m_cur[h:h + 1, :])
            p = jnp.exp(lt_refs[h][slot] - m_cur[h:h + 1, :].astype(BF16))
            pv = jnp.dot(vt_ref[kt, h * VT_HEAD_ROWS:(h + 1) * VT_HEAD_ROWS, :], p, preferred_element_type=F32)
            ls.append(alpha * l_all[h:h + 1, :] + pv[HEAD_DIM:HEAD_DIM + 1, :])
            acc_refs[h][...] = alpha * acc_refs[h][...] + pv[:HEAD_DIM, :]
        return m_cur, m_next, jnp.concatenate(ls, axis=0)

    m_init = jnp.full((N_HEADS, tq), -jnp.inf, F32)
    _, _, l_all = lax.fori_loop(0, nkt, att_body, (m_init, logits(0, 0, m_init), jnp.zeros((N_HEADS, tq), F32)))
    out_t = jnp.concatenate([acc_refs[h][...] / l_all[h:h + 1, :] for h in range(N_HEADS)], axis=0)
    o_ref[...] = out_t.T.astype(BF16)


def _attend_prompt(q, iq, iwt, kb, vt, ikb, n_seq, seq_len, topk):
    n = q.shape[0]
    t = ATT_TILE
    nq = seq_len // t
    once = pl.Buffered(1)
    return pl.pallas_call(
        functools.partial(_attend_prompt_kernel, topk=topk),
        out_shape=jax.ShapeDtypeStruct((n, ATT_WIDTH), BF16),
        grid=(n_seq, nq),
        in_specs=[pl.BlockSpec((t, ATT_WIDTH), lambda b, j: (b * nq + j, 0)),
                  pl.BlockSpec((t, ATT_WIDTH), lambda b, j: (b * nq + j, 0)),
                  pl.BlockSpec((N_IDX_HEADS, t), lambda b, j: (0, b * nq + j)),
                  pl.BlockSpec((seq_len, ATT_WIDTH), lambda b, j: (b, 0), pipeline_mode=once),
                  pl.BlockSpec((nq, VT_ROWS, t), lambda b, j: (b, 0, 0), pipeline_mode=once),
                  pl.BlockSpec((seq_len, LANES), lambda b, j: (b, 0), pipeline_mode=once)],
        out_specs=pl.BlockSpec((t, ATT_WIDTH), lambda b, j: (b * nq + j, 0)),
        scratch_shapes=([pltpu.VMEM((nq, t, t), I32), pltpu.VMEM((nq, t, t), BF16)]
                        + [pltpu.VMEM((t, LANES), BF16)] * N_HEADS
                        + [pltpu.VMEM((HEAD_DIM, t), F32)] * N_HEADS + [pltpu.VMEM((2, t, t), BF16)] * N_HEADS),
        compiler_params=_cparams("arbitrary", "arbitrary"),
        name="attend_prompt",
    )(q, iq, iwt, kb, vt, ikb)


SAMPLE_KEY_TILE = 512


def _wide(a, width):
    return a if width == LANES else jnp.concatenate([a] * (width // LANES), axis=1)


def _row_sums(x_bf16):
    return jnp.dot(x_bf16, jnp.ones((x_bf16.shape[1], LANES), BF16), preferred_element_type=F32)


def _count_rows(planes, cand, strict):
    cb = cand.astype(BF16)
    one, zero = jnp.ones((), BF16), jnp.zeros((), BF16)
    accs = [jnp.zeros(cand.shape, BF16) for _ in range(4)]
    i = 0
    for ref in planes:
        e = ref[...]
        for c in range(e.shape[1] // LANES):
            blk = e[:, c * LANES:(c + 1) * LANES]
            accs[i % 4] = accs[i % 4] + jnp.where((blk > cb) if strict else (blk >= cb), one, zero)
            i += 1
    assert i <= 256
    return _row_sums((accs[0] + accs[1]) + (accs[2] + accs[3]))


def _topk_select_rows(keys, planes, topk):
    rows = planes[0].shape[0]

    def search():
        def bit_body(i, d):
            cand = d + lax.shift_left(jnp.int32(1), jnp.int32(DIGIT_BITS - 1) - i).astype(F32)
            return jnp.where(_count_rows(planes, cand, strict=False) >= float(topk), cand, d)

        return lax.fori_loop(0, DIGIT_BITS, bit_body, jnp.zeros((rows, LANES), F32))

    d = search()
    for phase in range(1, N_DIGITS):
        db = d.astype(BF16)
        for kref, pref in zip(keys, planes):
            e = pref[...]
            dw = _wide(db, e.shape[1])
            decided = jnp.where(e > dw, jnp.asarray(DIGIT_ABOVE, BF16), jnp.asarray(DIGIT_BELOW, BF16))
            pref[...] = jnp.where(e == dw, _digit_plane(kref[...], phase), decided)
        d = search()

    cnt_ge = _count_rows(planes, d, strict=False)

    @pl.when(jnp.max(cnt_ge) > float(topk))
    def _():
        need = float(topk) - _count_rows(planes, d, strict=True)
        seen = jnp.zeros((rows, LANES), F32)
        for pref in planes:
            width = pref.shape[1]
            e = pref[...].astype(F32)
            eq = e == _wide(d, width)
            eqb = jnp.where(eq, 1.0, 0.0).astype(BF16)
            r = lax.broadcasted_iota(I32, (width, width), 0)
            c = lax.broadcasted_iota(I32, (width, width), 1)
            before = jnp.where(r < c, 1.0, 0.0).astype(BF16)
            prior = jnp.dot(eqb, before, preferred_element_type=F32) + _wide(seen, width)
            pref[...] = jnp.where(eq & (prior >= _wide(need, width)), DIGIT_BELOW, e).astype(BF16)
            seen = seen + _row_sums(eqb)

    return d


def _attend_sample_kernel(q_ref, iq_ref, iwr_ref, knt_ref, vnt_ref, iknt_ref, ckt_ref, cvt_ref, cit_ref, o_ref,
                          keys_ref, plane_ref, keys_new_ref, plane_new_ref, lt_ref, lt_new_ref, *, topk, past):
    tq = q_ref.shape[0]
    nct, _, tkc = keys_ref.shape
    rows = N_HEADS * tq
    row = lax.broadcasted_iota(I32, (rows, LANES), 0)
    qpos = past + (row & (tq - 1))
    slope = lax.bitcast_convert_type(
        lax.shift_left(126 - lax.shift_right_logical(row, tq.bit_length() - 1), 23), F32)
    qpos16 = past + lax.broadcasted_iota(I32, (tq, LANES), 0)

    iq = iq_ref[...]
    q = q_ref[...]
    iq_rows = jnp.concatenate([iq[:, h * IDX_DIM:(h + 1) * IDX_DIM] for h in range(N_IDX_HEADS)], axis=0)
    q_rows = jnp.concatenate(
        [_half_mask_wide(q, h) for h in range(N_HEADS)], axis=0) * jnp.asarray(ATTN_SCALE, BF16)
    iw_rows = iwr_ref[...]

    tiles = [(kt * tkc, tkc, cit_ref.at[:, kt * tkc:(kt + 1) * tkc], ckt_ref.at[:, kt * tkc:(kt + 1) * tkc],
              cvt_ref.at[:, kt * tkc:(kt + 1) * tkc], keys_ref.at[kt], plane_ref.at[kt], lt_ref.at[kt])
             for kt in range(nct)]
    tiles.append((past, LANES, iknt_ref, knt_ref, vnt_ref, keys_new_ref, plane_new_ref, lt_new_ref))

    def key_positions(k0, width, n_rows):
        return k0 + lax.broadcasted_iota(I32, (n_rows, width), 1)

    for k0, width, ikt, _, _, kref, pref, _ in tiles:
        s = jnp.dot(iq_rows, ikt[...].astype(BF16), preferred_element_type=F32)
        s = jnp.maximum(s, 0.0) * _wide(iw_rows, width)
        score = s[0:tq, :]
        for h in range(1, N_IDX_HEADS):
            score = score + s[h * tq:(h + 1) * tq, :]
        kpos = key_positions(k0, width, tq)
        adm = lax.shift_right_logical(kpos, CHUNK_SHIFT) <= lax.shift_right_logical(_wide(qpos16, width), CHUNK_SHIFT)
        if k0 == past:
            adm = adm & (kpos < past + tq)
        key = _mono_key(jnp.where(adm, score, -jnp.inf))
        kref[...] = key
        pref[...] = _digit_plane(key, 0)

    d_last = _topk_select_rows([t[5] for t in tiles], [t[6] for t in tiles], topk)

    m_part = jnp.full((rows, LANES), -jnp.inf, F32)
    for k0, width, _, kt_ref, _, kref, pref, ltref in tiles:
        sel = (pref[...].astype(F32) >= _wide(d_last, width)) & (kref[...] > jnp.int32(KEY_NEG_INF))
        sel = jnp.concatenate([jnp.where(sel, 1.0, 0.0)] * N_HEADS, axis=0) > 0.5
        dist = jnp.abs(_wide(qpos, width) - key_positions(k0, width, rows)).astype(F32)
        lt = (jnp.dot(q_rows, kt_ref[...].astype(BF16), preferred_element_type=F32)
              - _wide(slope, width) * jnp.where(sel, dist, MASKED_DIST))
        ltref[...] = lt
        for c in range(width // LANES):
            m_part = jnp.maximum(m_part, lt[:, c * LANES:(c + 1) * LANES])
    m = jnp.broadcast_to(jnp.max(m_part, axis=1, keepdims=True), (rows, LANES))

    acc = jnp.zeros((rows, ATT_WIDTH), F32)
    l_part = jnp.zeros((rows, LANES), F32)
    for _, width, _, _, vt_ref, _, _, ltref in tiles:
        p = jnp.exp(ltref[...] - _wide(m, width))
        for c in range(width // LANES):
            l_part = l_part + p[:, c * LANES:(c + 1) * LANES]
        acc = acc + lax.dot_general(p.astype(BF16), vt_ref[...].astype(BF16), _NT, preferred_element_type=F32)
    out = acc / jnp.sum(l_part, axis=1, keepdims=True)

    out_lane_head = lax.shift_right_logical(lax.broadcasted_iota(I32, (tq, ATT_WIDTH), 1), 6)
    y = jnp.zeros((tq, ATT_WIDTH), F32)
    for h in range(N_HEADS):
        y = y + jnp.where(out_lane_head == h, out[h * tq:(h + 1) * tq, :], 0.0)
    o_ref[...] = y.astype(BF16)


def _half_mask_wide(x, head):
    lane = lax.broadcasted_iota(I32, x.shape, 1)
    keep = lax.shift_right_logical(lane, 6) == head
    return jnp.where(keep, x, jnp.zeros_like(x))


def _attend_sample(q, iq, iw_rows, knt, vnt, iknt, cache_kt, cache_vt, cache_it, topk):
    n_seq, _, past = cache_kt.shape
    tq = q.shape[0] // n_seq
    tkc = SAMPLE_KEY_TILE
    rows = N_HEADS * tq
    assert rows == LANES and past % tkc == 0
    new = lambda width: pl.BlockSpec((tq, width), lambda b: (b, 0))
    per_seq = lambda a: pl.BlockSpec((None,) + a.shape[1:], lambda b: (b, 0, 0))
    return pl.pallas_call(
        functools.partial(_attend_sample_kernel, topk=topk, past=past),
        out_shape=jax.ShapeDtypeStruct((n_seq * tq, ATT_WIDTH), BF16),
        grid=(n_seq,),
        in_specs=[new(ATT_WIDTH), new(ATT_WIDTH), per_seq(iw_rows), per_seq(knt), per_seq(vnt), per_seq(iknt),
                  per_seq(cache_kt), per_seq(cache_vt), per_seq(cache_it)],
        out_specs=new(ATT_WIDTH),
        scratch_shapes=[pltpu.VMEM((past // tkc, tq, tkc), I32), pltpu.VMEM((past // tkc, tq, tkc), BF16),
                        pltpu.VMEM((tq, LANES), I32), pltpu.VMEM((tq, LANES), BF16),
                        pltpu.VMEM((past // tkc, rows, tkc), F32), pltpu.VMEM((rows, LANES), F32)],
        compiler_params=_cparams("arbitrary"),
        name="attend_sample",
    )(q, iq, iw_rows, knt, vnt, iknt, cache_kt, cache_vt, cache_it)


TOK_ROWS = D_MODEL // LANES


def _store_token_tiles(ref, x):
    t = x.shape[0]
    for c in range(TOK_ROWS):
        ref[pl.ds(c, t, stride=TOK_ROWS), :] = x[:, c * LANES:(c + 1) * LANES]


def _load_token_tiles(ref):
    t = ref.shape[0] // TOK_ROWS
    return jnp.concatenate([ref[pl.ds(c, t, stride=TOK_ROWS), :] for c in range(TOK_ROWS)], axis=1)


def _layer_norm(x, g, b):
    mu = jnp.mean(x, axis=-1, keepdims=True)
    xc = x - mu
    var = jnp.mean(xc * xc, axis=-1, keepdims=True)
    return xc * lax.rsqrt(var + LN_EPS) * g + b


def _post_kernel(x_ref, ya_ref, yb_ref, ga_ref, scf_ref, shf_ref, wo_ref, bo_ref, g1_ref, b1_ref,
                 wrh_ref, wrl_ref, br_ref,
                 x1_ref, h2_ref, tope_ref, gate_ref, rank_ref, cnt_ref):
    t = x_ref.shape[0]
    y = (jnp.dot(ya_ref[...], wo_ref[:A_WIDTH, :], preferred_element_type=F32)
         + jnp.dot(yb_ref[...], wo_ref[A_WIDTH:, :], preferred_element_type=F32) + bo_ref[...])
    x1 = _layer_norm(DEEPNORM_ALPHA * x_ref[...] + (1.0 + ga_ref[...]) * y, g1_ref[...], b1_ref[...])
    x1_ref[...] = x1
    h2 = x1 * (1.0 + scf_ref[...]) + shf_ref[...]
    _store_token_tiles(h2_ref, h2)

    hh = h2.astype(BF16)
    hl = (h2 - hh.astype(F32)).astype(BF16)
    logits = (lax.dot_general(wrh_ref[...], hh, _NT, preferred_element_type=F32)
              + lax.dot_general(wrh_ref[...], hl, _NT, preferred_element_type=F32)
              + lax.dot_general(wrl_ref[...], hh, _NT, preferred_element_type=F32) + br_ref[...])
    erow = lax.broadcasted_iota(I32, (N_EXPERTS, t), 0)
    vals, idxs = [], []
    for _ in range(TOP_K):
        v = jnp.max(logits, axis=0, keepdims=True)
        i = jnp.min(jnp.where(logits == v, erow, N_EXPERTS), axis=0, keepdims=True)
        vals.append(v)
        idxs.append(i)
        logits = jnp.where(erow == i, -jnp.inf, logits)
    ex = [jnp.exp(v - vals[0]) for v in vals]
    den = ex[0] + ex[1] + ex[2] + ex[3]
    gate_ref[...] = jnp.concatenate([e / den for e in ex], axis=0)
    tope_ref[...] = jnp.concatenate(idxs, axis=0)

    @pl.when(pl.program_id(0) == 0)
    def _():
        cnt_ref[...] = jnp.zeros_like(cnt_ref)

    hit = jnp.zeros((N_EXPERTS, t), F32)
    for i in idxs:
        hit = hit + jnp.where(erow == i, 1.0, 0.0)
    hitb = hit.astype(BF16)
    r = lax.broadcasted_iota(I32, (t, t), 0)
    c = lax.broadcasted_iota(I32, (t, t), 1)
    earlier = jnp.where(r < c, 1.0, 0.0).astype(BF16)
    before = jnp.dot(hitb, earlier, preferred_element_type=F32) + cnt_ref[...]
    total = jnp.dot(hitb, jnp.ones((t, t), BF16), preferred_element_type=F32)
    rank_ref[...] = jnp.concatenate(
        [jnp.sum(jnp.where(erow == i, before, 0.0), axis=0, keepdims=True) for i in idxs], axis=0).astype(I32)
    cnt_ref[...] += total


def _post(x, ya, yb, ga, scf, shf, w_o, b_o, g1, b1, wrh, wrl, br):
    n, d = x.shape
    t = POST_TILE
    nt = n // t
    if ga.shape[0] == n:
        mod_spec = pl.BlockSpec((t, d), lambda i: (i, 0))
    else:
        tiles_per_seq = nt // ga.shape[0]
        ga, scf, shf = (a.reshape(a.shape[0], 1, d) for a in (ga, scf, shf))
        mod_spec = pl.BlockSpec((None, 1, d), lambda i: (i // tiles_per_seq, 0, 0))
    row = lambda w: pl.BlockSpec((t, w), lambda i: (i, 0))
    col = lambda r: pl.BlockSpec((r, t), lambda i: (0, i))
    const = lambda a: pl.BlockSpec(a.shape, lambda i: (0, 0))
    return pl.pallas_call(
        _post_kernel,
        out_shape=(jax.ShapeDtypeStruct((n, d), F32), jax.ShapeDtypeStruct((n * TOK_ROWS, LANES), F32),
                   jax.ShapeDtypeStruct((TOP_K, n), I32), jax.ShapeDtypeStruct((TOP_K, n), F32),
                   jax.ShapeDtypeStruct((TOP_K, n), I32), jax.ShapeDtypeStruct((N_EXPERTS, t), F32)),
        grid=(nt,),
        in_specs=[row(d), row(A_WIDTH), row(ATT_WIDTH), mod_spec, mod_spec, mod_spec,
                  const(w_o), const(b_o), const(g1), const(b1), const(wrh), const(wrl), const(br)],
        out_specs=(row(d), pl.BlockSpec((t * TOK_ROWS, LANES), lambda i: (i, 0)), col(TOP_K), col(TOP_K), col(TOP_K),
                   pl.BlockSpec((N_EXPERTS, t), lambda i: (0, 0))),
        compiler_params=_cparams("arbitrary"),
        name="post",
    )(x, ya, yb, ga, scf, shf, w_o, b_o, g1, b1, wrh, wrl, br)


def _token_rows(ref, r):
    return ref.at[pl.ds(pl.multiple_of(r * TOK_ROWS, TOK_ROWS), TOK_ROWS)]


def _token_copies_wait(hbm_ref, vmem_ref, sem, n_tokens):
    rows = n_tokens * TOK_ROWS
    pltpu.make_async_copy(hbm_ref.at[pl.ds(0, rows)], vmem_ref.at[pl.ds(0, rows)], sem).wait()


def _dispatch_kernel(zrow_ref, dest_ref, h_ref, xp_ref, zero_ref, sem):
    t = h_ref.shape[0] // TOK_ROWS

    block_rows = MOE_BLOCK * TOK_ROWS

    def zero_block(slot0):
        z0 = pl.multiple_of(slot0 * TOK_ROWS, block_rows)
        cp = pltpu.make_async_copy(zero_ref, xp_ref.at[pl.ds(z0, block_rows)], sem)
        cp.start()
        cp.wait()

    @pl.when(pl.program_id(0) == 0)
    def _():
        zero_ref[...] = jnp.zeros_like(zero_ref)
        for e in range(N_EXPERTS):
            @pl.when(zrow_ref[e] >= 0)
            def _():
                zero_block(zrow_ref[e])

        def unused(b, _):
            zero_block(b * MOE_BLOCK)
            return 0

        lax.fori_loop(zrow_ref[N_EXPERTS], xp_ref.shape[0] // block_rows, unused, 0)

    def body(i, _):
        for k in range(TOP_K):
            pltpu.make_async_copy(_token_rows(h_ref, i), _token_rows(xp_ref, dest_ref[k, i]), sem).start(priority=k % 2)
        return 0

    lax.fori_loop(0, t, body, 0, unroll=DMA_ISSUE_UNROLL)
    for k in range(TOP_K):
        _token_copies_wait(xp_ref, h_ref, sem, t)


def _dispatch(zrow, dest, h2, n_slots):
    t = ROW_TILE
    n = h2.shape[0] // TOK_ROWS
    grid_spec = pltpu.PrefetchScalarGridSpec(
        num_scalar_prefetch=1,
        grid=(n // t,),
        in_specs=[pl.BlockSpec((TOP_K, t), lambda i, z: (0, i), memory_space=pltpu.SMEM),
                  pl.BlockSpec((t * TOK_ROWS, LANES), lambda i, z: (i, 0))],
        out_specs=pl.BlockSpec(memory_space=pl.ANY),
        scratch_shapes=[pltpu.VMEM((MOE_BLOCK * TOK_ROWS, LANES), F32), pltpu.SemaphoreType.DMA],
    )
    return pl.pallas_call(
        _dispatch_kernel,
        out_shape=jax.ShapeDtypeStruct((n_slots * TOK_ROWS, LANES), F32),
        grid_spec=grid_spec,
        compiler_params=_cparams("arbitrary"),
        name="dispatch",
    )(zrow, dest, h2)


def _experts_kernel(be_ref, bi_ref, nu_ref, x_ref, wg_ref, bg_ref, wu_ref, bu_ref, wd_ref, bd_ref, y_ref,
                    wgb_ref, wub_ref, wdb_ref):
    i = pl.program_id(0)

    @pl.when((i == 0) | (be_ref[i] != be_ref[jnp.maximum(i - 1, 0)]))
    def _():
        wgb_ref[...] = wg_ref[...].astype(BF16)
        wub_ref[...] = wu_ref[...].astype(BF16)
        wdb_ref[...] = wd_ref[...].astype(BF16)

    @pl.when(i < nu_ref[0])
    def _():
        x = _load_token_tiles(x_ref).astype(BF16)
        g = jnp.minimum(jnp.dot(x, wgb_ref[...], preferred_element_type=F32) + bg_ref[...], SWIGLU_LIMIT)
        u = jnp.clip(jnp.dot(x, wub_ref[...], preferred_element_type=F32) + bu_ref[...], -SWIGLU_LIMIT, SWIGLU_LIMIT)
        a = g * jax.nn.sigmoid(SWIGLU_ALPHA * g)
        mid = ((u + 1.0) * a).astype(BF16)
        _store_token_tiles(y_ref, jnp.dot(mid, wdb_ref[...], preferred_element_type=F32) + bd_ref[...])

    @pl.when(pl.program_id(0) >= nu_ref[0])
    def _():
        y_ref[...] = jnp.zeros_like(y_ref)


def _experts(block_e, block_i, n_used, xp, wg, bg, wu, bu, wd, bd):
    d, f = wg.shape[1], wg.shape[2]
    nb = xp.shape[0] // (MOE_BLOCK * TOK_ROWS)
    wspec = lambda a, b: pl.BlockSpec((None, a, b), lambda i, be, bi, nu: (be[i], 0, 0))
    slots = pl.BlockSpec((MOE_BLOCK * TOK_ROWS, LANES), lambda i, be, bi, nu: (bi[i], 0))
    grid_spec = pltpu.PrefetchScalarGridSpec(
        num_scalar_prefetch=3,
        grid=(nb,),
        in_specs=[slots, wspec(d, f), wspec(1, f), wspec(d, f), wspec(1, f), wspec(f, d), wspec(1, d)],
        out_specs=pl.BlockSpec((MOE_BLOCK * TOK_ROWS, LANES), lambda i, be, bi, nu: (i, 0)),
        scratch_shapes=[pltpu.VMEM((d, f), BF16), pltpu.VMEM((d, f), BF16), pltpu.VMEM((f, d), BF16)],
    )
    return pl.pallas_call(
        _experts_kernel,
        out_shape=jax.ShapeDtypeStruct(xp.shape, F32),
        grid_spec=grid_spec,
        compiler_params=_cparams("arbitrary"),
        name="experts",
    )(block_e, block_i, n_used, xp, wg, bg, wu, bu, wd, bd)


def _combine_kernel(dest_ref, dest_next_ref, gate_ref, x1_ref, gf_ref, g2_ref, b2_ref, yp_ref, o_ref, buf_ref, sems):
    t = x1_ref.shape[0]
    step = pl.program_id(0)
    slot = step % 2

    def start_gather(dst_ref, into):
        def body(i, _):
            for k in range(TOP_K):
                pltpu.make_async_copy(_token_rows(yp_ref, dst_ref[k, i]), _token_rows(buf_ref.at[into, k], i),
                                      sems.at[into]).start(priority=k % 2)
            return 0

        lax.fori_loop(0, t, body, 0, unroll=DMA_ISSUE_UNROLL)

    @pl.when(step == 0)
    def _():
        start_gather(dest_ref, 0)

    @pl.when(step + 1 < pl.num_programs(0))
    def _():
        start_gather(dest_next_ref, 1 - slot)

    for k in range(TOP_K):
        _token_copies_wait(yp_ref, buf_ref.at[slot, k], sems.at[slot], t)

    gates = jnp.concatenate([gate_ref[...], jnp.zeros((LANES - TOP_K, t), F32)], axis=0).T
    f = gates[:, 0:1] * _load_token_tiles(buf_ref.at[slot, 0])
    for k in range(1, TOP_K):
        f = f + gates[:, k:k + 1] * _load_token_tiles(buf_ref.at[slot, k])
    o_ref[...] = _layer_norm(DEEPNORM_ALPHA * x1_ref[...] + (1.0 + gf_ref[...]) * f, g2_ref[...], b2_ref[...])


def _combine(dest, gates, x1, gf, g2, b2, yp):
    n, d = x1.shape
    t = ROW_TILE
    nt = n // t
    if gf.shape[0] == n:
        mod_spec = pl.BlockSpec((t, d), lambda i: (i, 0))
    else:
        tiles_per_seq = nt // gf.shape[0]
        gf = gf.reshape(gf.shape[0], 1, d)
        mod_spec = pl.BlockSpec((None, 1, d), lambda i: (i // tiles_per_seq, 0, 0))
    return pl.pallas_call(
        _combine_kernel,
        out_shape=jax.ShapeDtypeStruct((n, d), F32),
        grid=(nt,),
        in_specs=[pl.BlockSpec((TOP_K, t), lambda i: (0, i), memory_space=pltpu.SMEM),
                  pl.BlockSpec((TOP_K, t), lambda i: (0, jnp.minimum(i + 1, nt - 1)), memory_space=pltpu.SMEM),
                  pl.BlockSpec((TOP_K, t), lambda i: (0, i)),
                  pl.BlockSpec((t, d), lambda i: (i, 0)), mod_spec,
                  pl.BlockSpec((1, d), lambda i: (0, 0)), pl.BlockSpec((1, d), lambda i: (0, 0)),
                  pl.BlockSpec(memory_space=pl.ANY)],
        out_specs=pl.BlockSpec((t, d), lambda i: (i, 0)),
        scratch_shapes=[pltpu.VMEM((2, TOP_K, t * TOK_ROWS, LANES), F32), pltpu.SemaphoreType.DMA((2,))],
        compiler_params=_cparams("arbitrary"),
        name="combine",
    )(dest, dest, gates, x1, gf, g2, b2, yp)


def _moe(h2, tope, rank, counts, gates, x1, gf, g2, b2, experts_w):
    n = tope.shape[1]
    nb = (n * TOP_K + N_EXPERTS * (MOE_BLOCK - 1) + MOE_BLOCK - 1) // MOE_BLOCK
    pcounts = (counts + MOE_BLOCK - 1) // MOE_BLOCK * MOE_BLOCK
    pend = jnp.cumsum(pcounts)
    pstart = pend - pcounts
    eids = jnp.arange(N_EXPERTS, dtype=I32).reshape(N_EXPERTS, 1, 1)
    dest = rank + jnp.sum(jnp.where(tope[None] == eids, pstart.reshape(N_EXPERTS, 1, 1), 0), axis=0).astype(I32)
    n_used = (pend[-1] // MOE_BLOCK).astype(I32)
    zrow = jnp.concatenate([jnp.where(counts > 0, pend - MOE_BLOCK, -1), n_used.reshape(1)]).astype(I32)
    blk = jnp.arange(nb, dtype=I32)
    block_i = jnp.minimum(blk, n_used - 1)
    block_e = jnp.minimum(jnp.sum(pend[None, :] <= (block_i * MOE_BLOCK)[:, None], axis=1), N_EXPERTS - 1).astype(I32)
    xp = _dispatch(zrow, dest, h2, nb * MOE_BLOCK)
    yp = _experts(block_e, block_i, n_used.reshape(1), xp, *experts_w)
    return _combine(dest, gates, x1, gf, g2, b2, yp)


def kernel(x_prompt, x_sample, c_prompt, c_sample, cache_k, cache_v, cache_kidx, w_in, a_ln_g, a_ln_b, a_ws, a_bs,
           w_o, b_o, w_c, b_c, ln1_g, ln1_b, ln2_g, ln2_b, w_router, b_router, w_gate, b_gate, w_up, b_up,
           w_down, b_down):
    bp, s, d = x_prompt.shape
    bs, ts, _ = x_sample.shape
    past = cache_k.shape[2]
    np_, ns = bp * s, bs * ts

    wi = w_in[0]
    w_in_p = jnp.zeros((d, IN_COLS_PAD), F32)
    w_in_p = w_in_p.at[:, :OFF_IK].set(wi[:, :OFF_IK])
    w_in_p = w_in_p.at[:, OFF_IK:OFF_IK + IDX_DIM].set(wi[:, OFF_IK:OFF_IK + IDX_DIM])
    w_in_p = w_in_p.at[:, OFF_IK + IDX_DIM:OFF_IK + 2 * IDX_DIM].set(wi[:, OFF_IK:OFF_IK + IDX_DIM])
    w_in_p = w_in_p.at[:, OFF_IW:OFF_IW + N_IDX_HEADS].set(wi[:, OFF_IK + IDX_DIM:OFF_IK + IDX_DIM + N_IDX_HEADS])
    w_in_p = w_in_p.astype(BF16)
    lng, lnb = a_ln_g[0].reshape(1, A_WIDTH), a_ln_b[0].reshape(1, A_WIDTH)
    wtril = jnp.tril(a_ws[0])
    wm_p = wtril.astype(BF16)
    bm_p = jnp.broadcast_to(a_bs[0][:, :, None], (A_GROUPS, A_CHUNK, A_GROUP_DIM)).astype(F32)
    rep = A_CHUNK // ts
    wm_s = jnp.einsum("ab,gij->gaibj", jnp.eye(rep, dtype=F32), wtril[:, :ts, :ts]).reshape(
        A_GROUPS, A_CHUNK, A_CHUNK).astype(BF16)
    bm_s = jnp.broadcast_to(jnp.tile(a_bs[0][:, :ts], (1, rep))[:, :, None], (A_GROUPS, A_CHUNK, A_GROUP_DIM)).astype(F32)
    w_o_b = w_o[0].astype(BF16)
    b_o_r = b_o[0].reshape(1, d)
    wr_t = w_router[0].T
    wrh = wr_t.astype(BF16)
    wrl = (wr_t - wrh.astype(F32)).astype(BF16)
    br = jnp.broadcast_to(b_router[0][:, None], (N_EXPERTS, POST_TILE)).astype(F32)
    experts_w = (w_gate[0], b_gate[0][:, None, :], w_up[0], b_up[0][:, None, :], w_down[0], b_down[0][:, None, :])
    g1, b1 = ln1_g[0].reshape(1, d), ln1_b[0].reshape(1, d)
    g2, b2 = ln2_g[0].reshape(1, d), ln2_b[0].reshape(1, d)

    mods = _cond_mods(jnp.concatenate([c_prompt, c_sample], axis=0), w_c[0], b_c[0]).reshape(bp + bs, 6, d)
    mods_p = [mods[:bp, i] for i in range(6)]
    mods_s = [jnp.repeat(mods[bp:, i], ts, axis=0) for i in range(6)]

    xp2 = x_prompt.reshape(np_, d)
    ya, q, kft, kb, _, vft, vt, iq, ikft, ikb, iwt, _ = _project(
        xp2, mods_p[1], mods_p[0], w_in_p, lng, lnb, wm_p, bm_p, bp)
    yb = _attend_prompt(q, iq, iwt, kb, vt, ikb, bp, s, min(TOPK_MAX, s // 4))
    x1, h2, tope, gates, rank, cnt = _post(xp2, ya, yb, mods_p[2], mods_p[4], mods_p[3], w_o_b, b_o_r, g1, b1, wrh, wrl, br)
    y_p = _moe(h2, tope, rank, cnt[:, 0].astype(I32), gates, x1, mods_p[5], g2, b2, experts_w)
    heads_last = lambda a: jnp.transpose(a.reshape(1, bp, N_HEADS, HEAD_DIM, s), (0, 1, 4, 2, 3))
    out_p = (y_p.reshape(bp, s, d), heads_last(kft), heads_last(vft),
             jnp.transpose(ikft.reshape(1, bp, IDX_DIM, s), (0, 1, 3, 2)))

    xs2 = x_sample.reshape(ns, d)
    ya, q, kft, _, vf, vft, _, iq, ikft, _, iwt, va = _project(
        xs2, mods_s[1], mods_s[0], w_in_p, lng, lnb, wm_s, bm_s, 1)
    kf, ikf = kft[0].T, ikft[0].T
    per_seq_t = lambda a: jnp.pad(jnp.transpose(a[0].reshape(a.shape[1], bs, ts), (1, 0, 2)),
                                  ((0, 0), (0, 0), (0, LANES - ts)))
    iw_rows = jnp.broadcast_to(
        jnp.transpose(iwt.reshape(N_IDX_HEADS, bs, ts), (1, 0, 2)).reshape(bs, N_IDX_HEADS * ts, 1),
        (bs, N_IDX_HEADS * ts, LANES))
    cache_kt = jnp.transpose(cache_k[0], (0, 2, 3, 1)).reshape(bs, ATT_WIDTH, past)
    cache_vt = jnp.transpose(cache_v[0], (0, 2, 3, 1)).reshape(bs, ATT_WIDTH, past)
    cache_it = jnp.transpose(cache_kidx[0], (0, 2, 1))
    yb = _attend_sample(q, iq, iw_rows, per_seq_t(kft), per_seq_t(vft), per_seq_t(ikft), cache_kt, cache_vt, cache_it,
                        min(TOPK_MAX, (past + ts) // 4))
    x1, h2, tope, gates, rank, cnt = _post(xs2, ya, yb, mods_s[2], mods_s[4], mods_s[3], w_o_b, b_o_r, g1, b1, wrh, wrl, br)
    y_s = _moe(h2, tope, rank, cnt[:, 0].astype(I32), gates, x1, mods_s[5], g2, b2, experts_w)

    return (out_p[0], y_s.reshape(bs, ts, d), out_p[1], out_p[2], out_p[3],
            kf.reshape(1, bs, ts, N_HEADS, HEAD_DIM), vf.reshape(1, bs, ts, N_HEADS, HEAD_DIM),
            ikf.reshape(1, bs, ts, IDX_DIM), va.reshape(1, bs, ts, A_WIDTH))
```

```python
import functools

import jax
import jax.numpy as jnp
from jax import lax
from jax.experimental import pallas as pl
from jax.experimental.pallas import tpu as pltpu

F32 = jnp.float32
BF16 = jnp.bfloat16
I32 = jnp.int32

D_MODEL = 1024
CHUNK_SHIFT = 6
A_GROUPS = 4
A_GROUP_DIM = 128
A_WIDTH = A_GROUPS * A_GROUP_DIM
A_CHUNK = 128
N_HEADS = 8
HEAD_DIM = 64
ATT_WIDTH = N_HEADS * HEAD_DIM
N_IDX_HEADS = 8
IDX_DIM = 64
TOPK_MAX = 256
ATTN_SCALE = HEAD_DIM ** -0.5
VT_HEAD_ROWS = HEAD_DIM + 16
VT_ROWS = N_HEADS * VT_HEAD_ROWS
IDX_W_SCALE = (N_IDX_HEADS ** -0.5) * (IDX_DIM ** -0.5)
N_EXPERTS = 32
TOP_K = 4
SWIGLU_LIMIT = 7.0
SWIGLU_ALPHA = 1.702
DEEPNORM_ALPHA = 2.0 ** 0.25
LN_EPS = 1e-5

LANES = 128
SUBLANES = 8
VMEM_LIMIT_BYTES = 56 * 1024 * 1024

ROW_TILE = 256
PROJECT_TILE = 512
POST_TILE = 512
MOE_BLOCK = 512
ATT_TILE = 256
SCORE_GROUP = 4
ATT_GROUP = 4
DMA_ISSUE_UNROLL = 8
STATIC_TILE_UNROLL = 4

OFF_AU, OFF_AV, OFF_Q, OFF_K, OFF_V, OFF_IQ = 0, 512, 1024, 1536, 2048, 2560
OFF_IK = 3072
OFF_IW = 3200
IN_COLS_PAD = 3328

MASKED_DIST = 3.0e32
F32_MAX = 3.4028234663852886e38
INT32_MIN = -(2 ** 31)
KEY_NEG_INF = INT32_MIN + 0x7FFFFF

_NT = (((1,), (1,)), ((), ()))


def _cparams(*sem):
    return pltpu.CompilerParams(dimension_semantics=sem, vmem_limit_bytes=VMEM_LIMIT_BYTES)


def _mods_kernel(c_ref, w_ref, b_ref, o_ref):
    c = c_ref[...]
    s = c * jax.nn.sigmoid(c)
    o_ref[...] = jnp.dot(s.astype(BF16), w_ref[...].astype(BF16), preferred_element_type=F32) + b_ref[...]


def _cond_mods(c, w_c, b_c):
    nb, d = c.shape
    n_out = w_c.shape[1]
    return pl.pallas_call(
        _mods_kernel,
        out_shape=jax.ShapeDtypeStruct((nb, n_out), F32),
        grid=(n_out // d,),
        in_specs=[pl.BlockSpec((nb, d), lambda j: (0, 0)),
                  pl.BlockSpec((d, d), lambda j: (0, j)),
                  pl.BlockSpec((1, d), lambda j: (0, j))],
        out_specs=pl.BlockSpec((nb, d), lambda j: (0, j)),
        compiler_params=_cparams("arbitrary"),
        name="mods",
    )(c, w_c, b_c.reshape(1, n_out))


def _gelu(x):
    return 0.5 * x * (1.0 + lax.erf(x * 0.7071067811865476))


def _project_kernel(x_ref, sc_ref, sh_ref, w_ref, lng_ref, lnb_ref, wm_ref, bm_ref,
                    ya_ref, q_ref, kft_ref, kb_ref, vf_ref, vft_ref, vt_ref, iq_ref, ikft_ref, ikb_ref, iwt_ref,
                    va_ref):
    t = x_ref.shape[0]
    h = (x_ref[...] * (1.0 + sc_ref[...]) + sh_ref[...]).astype(BF16)

    def proj(c0, n):
        return jnp.dot(h, w_ref[:, c0:c0 + n], preferred_element_type=F32)

    u = _gelu(proj(OFF_AU, A_WIDTH))
    gv = _gelu(proj(OFF_AV, A_WIDTH))
    for g in range(A_GROUPS):
        lo, hi = g * A_GROUP_DIM, (g + 1) * A_GROUP_DIM
        xg = gv[:, lo:hi]
        mu = jnp.mean(xg, axis=-1, keepdims=True)
        xc = xg - mu
        var = jnp.mean(xc * xc, axis=-1, keepdims=True)
        vg = xc * lax.rsqrt(var + LN_EPS) * lng_ref[:, lo:hi] + lnb_ref[:, lo:hi]
        va_ref[:, lo:hi] = vg
        vgb = vg.astype(BF16)
        for c in range(t // A_CHUNK):
            r0, r1 = c * A_CHUNK, (c + 1) * A_CHUNK
            mixed = jnp.dot(wm_ref[g], vgb[r0:r1, :], preferred_element_type=F32) + bm_ref[g]
            ya_ref[r0:r1, lo:hi] = (u[r0:r1, lo:hi] * mixed).astype(BF16)

    q_ref[...] = proj(OFF_Q, ATT_WIDTH).astype(BF16)
    k = proj(OFF_K, ATT_WIDTH)
    kft_ref[...] = k.T
    kb_ref[...] = k.astype(BF16)
    v = proj(OFF_V, ATT_WIDTH)
    vf_ref[...] = v
    v_t32 = v.T
    vft_ref[...] = v_t32
    v_t = v_t32.astype(BF16)
    ones = jnp.ones((VT_HEAD_ROWS - HEAD_DIM, ATT_TILE), BF16)
    for c in range(t // ATT_TILE):
        cols = slice(c * ATT_TILE, (c + 1) * ATT_TILE)
        vt_ref[c] = jnp.concatenate(
            [blk for h in range(N_HEADS) for blk in (v_t[h * HEAD_DIM:(h + 1) * HEAD_DIM, cols], ones)], axis=0)
    iq_ref[...] = proj(OFF_IQ, N_IDX_HEADS * IDX_DIM).astype(BF16)
    ik2 = proj(OFF_IK, LANES)
    ikft_ref[...] = ik2.T[:IDX_DIM, :]
    ikb_ref[...] = ik2.astype(BF16)
    iw = proj(OFF_IW, LANES) * IDX_W_SCALE
    iwt_ref[...] = iw.T[:N_IDX_HEADS, :]


def _project(x, sc, sh, w_in_p, lng, lnb, wm, bm, n_out_seq):
    n, d = x.shape
    t = PROJECT_TILE
    nt = n // t
    tiles_per_out = nt // n_out_seq
    trans = lambda rows: pl.BlockSpec((None, rows, t), lambda i: (i // tiles_per_out, 0, i % tiles_per_out))
    if sc.shape[0] == n:
        mod_spec = pl.BlockSpec((t, d), lambda i: (i, 0))
    else:
        tiles_per_seq = nt // sc.shape[0]
        sc = sc.reshape(sc.shape[0], 1, d)
        sh = sh.reshape(sh.shape[0], 1, d)
        mod_spec = pl.BlockSpec((None, 1, d), lambda i: (i // tiles_per_seq, 0, 0))
    row = lambda w: pl.BlockSpec((t, w), lambda i: (i, 0))
    const2 = lambda a: pl.BlockSpec(a.shape, lambda i: (0, 0))
    const3 = lambda a: pl.BlockSpec(a.shape, lambda i: (0, 0, 0))
    out_shape = (
        jax.ShapeDtypeStruct((n, A_WIDTH), BF16),
        jax.ShapeDtypeStruct((n, ATT_WIDTH), BF16),
        jax.ShapeDtypeStruct((n_out_seq, ATT_WIDTH, n // n_out_seq), F32),
        jax.ShapeDtypeStruct((n, ATT_WIDTH), BF16),
        jax.ShapeDtypeStruct((n, ATT_WIDTH), F32),
        jax.ShapeDtypeStruct((n_out_seq, ATT_WIDTH, n // n_out_seq), F32),
        jax.ShapeDtypeStruct((n // ATT_TILE, VT_ROWS, ATT_TILE), BF16),
        jax.ShapeDtypeStruct((n, ATT_WIDTH), BF16),
        jax.ShapeDtypeStruct((n_out_seq, IDX_DIM, n // n_out_seq), F32),
        jax.ShapeDtypeStruct((n, LANES), BF16),
        jax.ShapeDtypeStruct((N_IDX_HEADS, n), F32),
        jax.ShapeDtypeStruct((n, A_WIDTH), F32),
    )
    out_specs = (row(A_WIDTH), row(ATT_WIDTH), trans(ATT_WIDTH), row(ATT_WIDTH), row(ATT_WIDTH), trans(ATT_WIDTH),
                 pl.BlockSpec((t // ATT_TILE, VT_ROWS, ATT_TILE), lambda i: (i, 0, 0)),
                 row(ATT_WIDTH), trans(IDX_DIM), row(LANES),
                 pl.BlockSpec((N_IDX_HEADS, t), lambda i: (0, i)),
                 row(A_WIDTH))
    return pl.pallas_call(
        _project_kernel,
        out_shape=out_shape,
        grid=(nt,),
        in_specs=[row(d), mod_spec, mod_spec, const2(w_in_p), const2(lng), const2(lnb), const3(wm), const3(bm)],
        out_specs=out_specs,
        compiler_params=_cparams("arbitrary"),
        name="project",
    )(x, sc, sh, w_in_p, lng, lnb, wm, bm)


DIGIT_BITS = 8
N_DIGITS = 32 // DIGIT_BITS
DIGIT_ABOVE = 512.0
DIGIT_BELOW = -1.0
PACKED_ROWS = 16


def _static_unroll(trips):
    return STATIC_TILE_UNROLL if isinstance(trips, int) else 1


def _tile_loop(nkt, body, init):
    if isinstance(nkt, int):
        return lax.fori_loop(0, nkt, body, init, unroll=STATIC_TILE_UNROLL)
    pairs = lax.shift_right_logical(nkt, 1)
    carry = lax.fori_loop(0, pairs, lambda i, c: body(2 * i + 1, body(2 * i, c)), init)
    return lax.fori_loop(2 * pairs, nkt, body, carry)


def _mono_key(x):
    b = lax.bitcast_convert_type(x, I32)
    return jnp.where(b >= 0, b, b ^ jnp.int32(0x7FFFFFFF))


def _digit_plane(key, phase):
    shift = 32 - DIGIT_BITS * (phase + 1)
    d = lax.shift_right_arithmetic(key, jnp.int32(shift)) if shift else key
    d = d + (1 << (DIGIT_BITS - 1)) if phase == 0 else d & ((1 << DIGIT_BITS) - 1)
    return d.astype(F32).astype(BF16)


def _count_plane(plane_ref, nkt, cand, strict):
    _, tk, w = plane_ref.shape
    cb = cand.astype(BF16)
    one, zero = jnp.ones((), BF16), jnp.zeros((), BF16)

    def body(kt, cnt):
        e = plane_ref[kt]
        accs = [jnp.zeros((PACKED_ROWS, w), BF16) for _ in range(4)]
        for r in range(tk // PACKED_ROWS):
            blk = e[r * PACKED_ROWS:(r + 1) * PACKED_ROWS, :]
            accs[r % 4] = accs[r % 4] + jnp.where((blk > cb) if strict else (blk >= cb), one, zero)
        return cnt + ((accs[0] + accs[1]) + (accs[2] + accs[3])).astype(F32)

    cnt = _tile_loop(nkt, body, jnp.zeros((PACKED_ROWS, w), F32))
    return jnp.sum(cnt, axis=0, keepdims=True)


def _search_digit(plane_ref, nkt, topk):
    w = plane_ref.shape[2]

    def bit_body(i, d):
        cand = d + lax.shift_left(jnp.int32(1), jnp.int32(DIGIT_BITS - 1) - i).astype(F32)
        cnt = _count_plane(plane_ref, nkt, cand, strict=False)
        return jnp.where(cnt >= float(topk), cand, d)

    return lax.fori_loop(0, DIGIT_BITS, bit_body, jnp.zeros((1, w), F32))


def _topk_select(keys_ref, plane_ref, nkt, topk):
    _, tk, w = plane_ref.shape
    d = _search_digit(plane_ref, nkt, topk)
    for phase in range(1, N_DIGITS):
        db = d.astype(BF16)

        def refine(kt, _, phase=phase, db=db):
            e = plane_ref[kt]
            decided = jnp.where(e > db, jnp.asarray(DIGIT_ABOVE, BF16), jnp.asarray(DIGIT_BELOW, BF16))
            plane_ref[kt] = jnp.where(e == db, _digit_plane(keys_ref[kt], phase), decided)
            return 0

        _tile_loop(nkt, refine, 0)
        d = _search_digit(plane_ref, nkt, topk)

    cnt_ge = _count_plane(plane_ref, nkt, d, strict=False)

    @pl.when(jnp.max(cnt_ge) > float(topk))
    def _():
        need = float(topk) - _count_plane(plane_ref, nkt, d, strict=True)
        r = lax.broadcasted_iota(I32, (tk, tk), 0)
        c = lax.broadcasted_iota(I32, (tk, tk), 1)
        before = jnp.where(c < r, 1.0, 0.0).astype(BF16)

        def body(kt, seen):
            e = plane_ref[kt].astype(F32)
            eq = e == d
            eqf = jnp.where(eq, 1.0, 0.0)
            prior = jnp.dot(before, eqf.astype(BF16), preferred_element_type=F32) + seen
            plane_ref[kt] = jnp.where(eq & (prior >= need), DIGIT_BELOW, e).astype(BF16)
            return seen + jnp.sum(eqf.reshape(tk // SUBLANES, SUBLANES, w), axis=0).sum(axis=0, keepdims=True)

        lax.fori_loop(0, nkt, body, jnp.zeros((1, w), F32))

    return d


def _selected(keys_ref, plane_ref, kt, d):
    return (plane_ref[kt].astype(F32) >= d) & (keys_ref[kt] > jnp.int32(KEY_NEG_INF))


def _half_mask(x_pair, head):
    lane = lax.broadcasted_iota(I32, x_pair.shape, 1)
    keep = (lane >= HEAD_DIM) if head % 2 else (lane < HEAD_DIM)
    return jnp.where(keep, x_pair, jnp.zeros_like(x_pair))


def _attend_prompt_kernel(q_ref, iq_ref, iwt_ref, k_ref, vt_ref, ik_ref, o_ref, keys_ref, plane_ref, *head_refs, topk):
    qh_refs, acc_refs, lt_refs = (head_refs[i * N_HEADS:(i + 1) * N_HEADS] for i in range(3))
    tq = q_ref.shape[0]
    tk = keys_ref.shape[1]
    j = pl.program_id(1)
    nkt = j + 1
    q0 = j * tq
    row = lax.broadcasted_iota(I32, (tk, tq), 0)
    lane = lax.broadcasted_iota(I32, (tk, tq), 1)
    qpos = q0 + lane

    iq = iq_ref[...]
    iqm = [_half_mask(iq[:, (h // 2) * LANES:(h // 2 + 1) * LANES], h) for h in range(N_IDX_HEADS)]
    iw = iwt_ref[...]

    def score_tiles(kt, n, diagonal):
        k0 = pl.multiple_of(kt * tk, tk)
        ikt = ik_ref[pl.ds(k0, n * tk), :]
        s = jnp.zeros((n * tk, tq), F32)
        for h in range(N_IDX_HEADS):
            r = lax.dot_general(ikt, iqm[h], _NT, preferred_element_type=F32)
            s = s + jnp.maximum(r, 0.0) * iw[h:h + 1, :]
        if diagonal:
            adm = lax.shift_right_logical(k0 + row, CHUNK_SHIFT) <= lax.shift_right_logical(qpos, CHUNK_SHIFT)
            s = jnp.where(adm, s, -jnp.inf)
        for c in range(n):
            key = _mono_key(s[c * tk:(c + 1) * tk, :])
            keys_ref[kt + c] = key
            plane_ref[kt + c] = _digit_plane(key, 0)
        return 0

    assert tk == tq
    groups = j // SCORE_GROUP
    lax.fori_loop(0, groups, lambda i, _: score_tiles(SCORE_GROUP * i, SCORE_GROUP, False), 0)
    lax.fori_loop(SCORE_GROUP * groups, j, lambda kt, _: score_tiles(kt, 1, False), 0)
    score_tiles(j, 1, True)

    d_last = _topk_select(keys_ref, plane_ref, nkt, topk)

    def dist_body(kt, _):
        kpos = kt * tk + row
        dist = jnp.abs(qpos - kpos).astype(F32)
        masked = jnp.where(_selected(keys_ref, plane_ref, kt, d_last), dist, MASKED_DIST)
        keys_ref[kt] = lax.bitcast_convert_type(masked, I32)
        return 0

    lax.fori_loop(0, nkt, dist_body, 0)

    qfull = q_ref[...]
    for h in range(N_HEADS):
        pair = h // 2
        qh_refs[h][...] = _half_mask(qfull[:, pair * LANES:(pair + 1) * LANES], h) * jnp.asarray(ATTN_SCALE, BF16)
        acc_refs[h][...] = jnp.zeros_like(acc_refs[h])

    def att_tiles(kt, n, carry):
        m_all, l_all = carry
        k0 = pl.multiple_of(kt * tk, tk)
        dist = jnp.concatenate([lax.bitcast_convert_type(keys_ref[kt + c], F32) for c in range(n)], axis=0)
        ms = []
        for h in range(N_HEADS):
            pair = h // 2
            slope = 2.0 ** (-8.0 * (h + 1) / N_HEADS)
            kp = k_ref[pl.ds(k0, n * tk), pair * LANES:(pair + 1) * LANES]
            lt = lax.dot_general(kp, qh_refs[h][...], _NT, preferred_element_type=F32) - slope * dist
            lt_refs[h][0:n * tk, :] = lt.astype(BF16)
            tile_max = jnp.max(lt, axis=0, keepdims=True).astype(BF16).astype(F32)
            ms.append(jnp.maximum(m_all[h:h + 1, :], tile_max))
        ls = []
        for h in range(N_HEADS):
            alpha = jnp.exp(m_all[h:h + 1, :] - ms[h])
            mb = ms[h].astype(BF16)
            acc = alpha * acc_refs[h][...]
            l = alpha * l_all[h:h + 1, :]
            for c in range(n):
                p = jnp.exp(lt_refs[h][c * tk:(c + 1) * tk, :] - mb)
                pv = jnp.dot(vt_ref[kt + c, h * VT_HEAD_ROWS:(h + 1) * VT_HEAD_ROWS, :], p,
                             preferred_element_type=F32)
                acc = acc + pv[:HEAD_DIM, :]
                l = l + pv[HEAD_DIM:HEAD_DIM + 1, :]
            acc_refs[h][...] = acc
            ls.append(l)
        return jnp.concatenate(ms, axis=0), jnp.concatenate(ls, axis=0)

    groups = nkt // ATT_GROUP
    carry = (jnp.full((N_HEADS, tq), -jnp.inf, F32), jnp.zeros((N_HEADS, tq), F32))
    carry = lax.fori_loop(0, groups, lambda i, c: att_tiles(ATT_GROUP * i, ATT_GROUP, c), carry)
    _, l_all = lax.fori_loop(ATT_GROUP * groups, nkt, lambda kt, c: att_tiles(kt, 1, c), carry)
    out_t = jnp.concatenate([acc_refs[h][...] / l_all[h:h + 1, :] for h in range(N_HEADS)], axis=0)
    o_ref[...] = out_t.T.astype(BF16)


def _attend_prompt(q, iq, iwt, kb, vt, ikb, n_seq, seq_len, topk):
    n = q.shape[0]
    t = ATT_TILE
    nq = seq_len // t
    once = pl.Buffered(1)
    return pl.pallas_call(
        functools.partial(_attend_prompt_kernel, topk=topk),
        out_shape=jax.ShapeDtypeStruct((n, ATT_WIDTH), BF16),
        grid=(n_seq, nq),
        in_specs=[pl.BlockSpec((t, ATT_WIDTH), lambda b, j: (b * nq + j, 0)),
                  pl.BlockSpec((t, ATT_WIDTH), lambda b, j: (b * nq + j, 0)),
                  pl.BlockSpec((N_IDX_HEADS, t), lambda b, j: (0, b * nq + j)),
                  pl.BlockSpec((seq_len, ATT_WIDTH), lambda b, j: (b, 0), pipeline_mode=once),
                  pl.BlockSpec((nq, VT_ROWS, t), lambda b, j: (b, 0, 0), pipeline_mode=once),
                  pl.BlockSpec((seq_len, LANES), lambda b, j: (b, 0), pipeline_mode=once)],
        out_specs=pl.BlockSpec((t, ATT_WIDTH), lambda b, j: (b * nq + j, 0)),
        scratch_shapes=([pltpu.VMEM((nq, t, t), I32), pltpu.VMEM((nq, t, t), BF16)]
                        + [pltpu.VMEM((t, LANES), BF16)] * N_HEADS
                        + [pltpu.VMEM((HEAD_DIM, t), F32)] * N_HEADS + [pltpu.VMEM((ATT_GROUP * t, t), BF16)] * N_HEADS),
        compiler_params=_cparams("arbitrary", "arbitrary"),
        name="attend_prompt",
    )(q, iq, iwt, kb, vt, ikb)


SAMPLE_KEY_TILE = 512


def _wide(a, width):
    return a if width == LANES else jnp.concatenate([a] * (width // LANES), axis=1)


def _row_sums(x_bf16):
    return jnp.dot(x_bf16, jnp.ones((x_bf16.shape[1], LANES), BF16), preferred_element_type=F32)


def _count_rows(planes, cand, strict):
    cb = cand.astype(BF16)
    one, zero = jnp.ones((), BF16), jnp.zeros((), BF16)
    accs = [jnp.zeros(cand.shape, BF16) for _ in range(4)]
    i = 0
    for ref in planes:
        e = ref[...]
        for c in range(e.shape[1] // LANES):
            blk = e[:, c * LANES:(c + 1) * LANES]
            accs[i % 4] = accs[i % 4] + jnp.where((blk > cb) if strict else (blk >= cb), one, zero)
            i += 1
    assert i <= 256
    return _row_sums((accs[0] + accs[1]) + (accs[2] + accs[3]))


def _topk_select_rows(keys, planes, topk):
    rows = planes[0].shape[0]

    def search():
        def bit_body(i, d):
            cand = d + lax.shift_left(jnp.int32(1), jnp.int32(DIGIT_BITS - 1) - i).astype(F32)
            return jnp.where(_count_rows(planes, cand, strict=False) >= float(topk), cand, d)

        return lax.fori_loop(0, DIGIT_BITS, bit_body, jnp.zeros((rows, LANES), F32))

    d = search()
    for phase in range(1, N_DIGITS):
        db = d.astype(BF16)
        for kref, pref in zip(keys, planes):
            e = pref[...]
            dw = _wide(db, e.shape[1])
            decided = jnp.where(e > dw, jnp.asarray(DIGIT_ABOVE, BF16), jnp.asarray(DIGIT_BELOW, BF16))
            pref[...] = jnp.where(e == dw, _digit_plane(kref[...], phase), decided)
        d = search()

    cnt_ge = _count_rows(planes, d, strict=False)

    @pl.when(jnp.max(cnt_ge) > float(topk))
    def _():
        need = float(topk) - _count_rows(planes, d, strict=True)
        seen = jnp.zeros((rows, LANES), F32)
        for pref in planes:
            width = pref.shape[1]
            e = pref[...].astype(F32)
            eq = e == _wide(d, width)
            eqb = jnp.where(eq, 1.0, 0.0).astype(BF16)
            r = lax.broadcasted_iota(I32, (width, width), 0)
            c = lax.broadcasted_iota(I32, (width, width), 1)
            before = jnp.where(r < c, 1.0, 0.0).astype(BF16)
            prior = jnp.dot(eqb, before, preferred_element_type=F32) + _wide(seen, width)
            pref[...] = jnp.where(eq & (prior >= _wide(need, width)), DIGIT_BELOW, e).astype(BF16)
            seen = seen + _row_sums(eqb)

    return d


def _attend_sample_kernel(q_ref, iq_ref, iwr_ref, knt_ref, vnt_ref, iknt_ref, ckt_ref, cvt_ref, cit_ref, o_ref,
                          keys_ref, plane_ref, keys_new_ref, plane_new_ref, lt_ref, lt_new_ref, *, topk, past):
    tq = q_ref.shape[0]
    nct, _, tkc = keys_ref.shape
    rows = N_HEADS * tq
    row = lax.broadcasted_iota(I32, (rows, LANES), 0)
    qpos = past + (row & (tq - 1))
    slope = lax.bitcast_convert_type(
        lax.shift_left(126 - lax.shift_right_logical(row, tq.bit_length() - 1), 23), F32)
    qpos16 = past + lax.broadcasted_iota(I32, (tq, LANES), 0)

    iq = iq_ref[...]
    q = q_ref[...]
    iq_rows = jnp.concatenate([iq[:, h * IDX_DIM:(h + 1) * IDX_DIM] for h in range(N_IDX_HEADS)], axis=0)
    q_rows = jnp.concatenate(
        [_half_mask_wide(q, h) for h in range(N_HEADS)], axis=0) * jnp.asarray(ATTN_SCALE, BF16)
    iw_rows = iwr_ref[...]

    tiles = [(kt * tkc, tkc, cit_ref.at[:, kt * tkc:(kt + 1) * tkc], ckt_ref.at[:, kt * tkc:(kt + 1) * tkc],
              cvt_ref.at[:, kt * tkc:(kt + 1) * tkc], keys_ref.at[kt], plane_ref.at[kt], lt_ref.at[kt])
             for kt in range(nct)]
    tiles.append((past, LANES, iknt_ref, knt_ref, vnt_ref, keys_new_ref, plane_new_ref, lt_new_ref))

    def key_positions(k0, width, n_rows):
        return k0 + lax.broadcasted_iota(I32, (n_rows, width), 1)

    for k0, width, ikt, _, _, kref, pref, _ in tiles:
        s = jnp.dot(iq_rows, ikt[...].astype(BF16), preferred_element_type=F32)
        s = jnp.maximum(s, 0.0) * _wide(iw_rows, width)
        score = s[0:tq, :]
        for h in range(1, N_IDX_HEADS):
            score = score + s[h * tq:(h + 1) * tq, :]
        kpos = key_positions(k0, width, tq)
        adm = lax.shift_right_logical(kpos, CHUNK_SHIFT) <= lax.shift_right_logical(_wide(qpos16, width), CHUNK_SHIFT)
        if k0 == past:
            adm = adm & (kpos < past + tq)
        key = _mono_key(jnp.where(adm, score, -jnp.inf))
        kref[...] = key
        pref[...] = _digit_plane(key, 0)

    d_last = _topk_select_rows([t[5] for t in tiles], [t[6] for t in tiles], topk)

    m_part = jnp.full((rows, LANES), -jnp.inf, F32)
    for k0, width, _, kt_ref, _, kref, pref, ltref in tiles:
        sel = (pref[...].astype(F32) >= _wide(d_last, width)) & (kref[...] > jnp.int32(KEY_NEG_INF))
        sel = jnp.concatenate([jnp.where(sel, 1.0, 0.0)] * N_HEADS, axis=0) > 0.5
        dist = jnp.abs(_wide(qpos, width) - key_positions(k0, width, rows)).astype(F32)
        lt = (jnp.dot(q_rows, kt_ref[...].astype(BF16), preferred_element_type=F32)
              - _wide(slope, width) * jnp.where(sel, dist, MASKED_DIST))
        ltref[...] = lt
        for c in range(width // LANES):
            m_part = jnp.maximum(m_part, lt[:, c * LANES:(c + 1) * LANES])
    m = jnp.broadcast_to(jnp.max(m_part, axis=1, keepdims=True), (rows, LANES))

    acc = jnp.zeros((rows, ATT_WIDTH), F32)
    l_part = jnp.zeros((rows, LANES), F32)
    for _, width, _, _, vt_ref, _, _, ltref in tiles:
        p = jnp.exp(ltref[...] - _wide(m, width))
        for c in range(width // LANES):
            l_part = l_part + p[:, c * LANES:(c + 1) * LANES]
        acc = acc + lax.dot_general(p.astype(BF16), vt_ref[...].astype(BF16), _NT, preferred_element_type=F32)
    out = acc / jnp.sum(l_part, axis=1, keepdims=True)

    out_lane_head = lax.shift_right_logical(lax.broadcasted_iota(I32, (tq, ATT_WIDTH), 1), 6)
    y = jnp.zeros((tq, ATT_WIDTH), F32)
    for h in range(N_HEADS):
        y = y + jnp.where(out_lane_head == h, out[h * tq:(h + 1) * tq, :], 0.0)
    o_ref[...] = y.astype(BF16)


def _half_mask_wide(x, head):
    lane = lax.broadcasted_iota(I32, x.shape, 1)
    keep = lax.shift_right_logical(lane, 6) == head
    return jnp.where(keep, x, jnp.zeros_like(x))


def _attend_sample(q, iq, iw_rows, knt, vnt, iknt, cache_kt, cache_vt, cache_it, topk):
    n_seq, _, past = cache_kt.shape
    tq = q.shape[0] // n_seq
    tkc = SAMPLE_KEY_TILE
    rows = N_HEADS * tq
    assert rows == LANES and past % tkc == 0
    new = lambda width: pl.BlockSpec((tq, width), lambda b: (b, 0))
    per_seq = lambda a: pl.BlockSpec((None,) + a.shape[1:], lambda b: (b, 0, 0))
    return pl.pallas_call(
        functools.partial(_attend_sample_kernel, topk=topk, past=past),
        out_shape=jax.ShapeDtypeStruct((n_seq * tq, ATT_WIDTH), BF16),
        grid=(n_seq,),
        in_specs=[new(ATT_WIDTH), new(ATT_WIDTH), per_seq(iw_rows), per_seq(knt), per_seq(vnt), per_seq(iknt),
                  per_seq(cache_kt), per_seq(cache_vt), per_seq(cache_it)],
        out_specs=new(ATT_WIDTH),
        scratch_shapes=[pltpu.VMEM((past // tkc, tq, tkc), I32), pltpu.VMEM((past // tkc, tq, tkc), BF16),
                        pltpu.VMEM((tq, LANES), I32), pltpu.VMEM((tq, LANES), BF16),
                        pltpu.VMEM((past // tkc, rows, tkc), F32), pltpu.VMEM((rows, LANES), F32)],
        compiler_params=_cparams("arbitrary"),
        name="attend_sample",
    )(q, iq, iw_rows, knt, vnt, iknt, cache_kt, cache_vt, cache_it)


TOK_ROWS = D_MODEL // LANES


def _store_token_tiles(ref, x):
    t = x.shape[0]
    for c in range(TOK_ROWS):
        ref[pl.ds(c, t, stride=TOK_ROWS), :] = x[:, c * LANES:(c + 1) * LANES]


def _load_token_tiles(ref):
    t = ref.shape[0] // TOK_ROWS
    return jnp.concatenate([ref[pl.ds(c, t, stride=TOK_ROWS), :] for c in range(TOK_ROWS)], axis=1)


def _layer_norm(x, g, b):
    mu = jnp.mean(x, axis=-1, keepdims=True)
    xc = x - mu
    var = jnp.mean(xc * xc, axis=-1, keepdims=True)
    return xc * lax.rsqrt(var + LN_EPS) * g + b


def _post_kernel(x_ref, ya_ref, yb_ref, ga_ref, scf_ref, shf_ref, wo_ref, bo_ref, g1_ref, b1_ref,
                 wrh_ref, wrl_ref, br_ref,
                 x1_ref, h2_ref, tope_ref, gate_ref, rank_ref, cnt_ref):
    t = x_ref.shape[0]
    y = (jnp.dot(ya_ref[...], wo_ref[:A_WIDTH, :], preferred_element_type=F32)
         + jnp.dot(yb_ref[...], wo_ref[A_WIDTH:, :], preferred_element_type=F32) + bo_ref[...])
    x1 = _layer_norm(DEEPNORM_ALPHA * x_ref[...] + (1.0 + ga_ref[...]) * y, g1_ref[...], b1_ref[...])
    x1_ref[...] = x1
    h2 = x1 * (1.0 + scf_ref[...]) + shf_ref[...]
    _store_token_tiles(h2_ref, h2)

    hh = h2.astype(BF16)
    hl = (h2 - hh.astype(F32)).astype(BF16)
    logits = (lax.dot_general(wrh_ref[...], hh, _NT, preferred_element_type=F32)
              + lax.dot_general(wrh_ref[...], hl, _NT, preferred_element_type=F32)
              + lax.dot_general(wrl_ref[...], hh, _NT, preferred_element_type=F32) + br_ref[...])
    erow = lax.broadcasted_iota(I32, (N_EXPERTS, t), 0)
    vals, idxs = [], []
    for _ in range(TOP_K):
        v = jnp.max(logits, axis=0, keepdims=True)
        i = jnp.min(jnp.where(logits == v, erow, N_EXPERTS), axis=0, keepdims=True)
        vals.append(v)
        idxs.append(i)
        logits = jnp.where(erow == i, -jnp.inf, logits)
    ex = [jnp.exp(v - vals[0]) for v in vals]
    den = ex[0] + ex[1] + ex[2] + ex[3]
    gate_ref[...] = jnp.concatenate([e / den for e in ex], axis=0)
    tope_ref[...] = jnp.concatenate(idxs, axis=0)

    @pl.when(pl.program_id(0) == 0)
    def _():
        cnt_ref[...] = jnp.zeros_like(cnt_ref)

    hit = jnp.zeros((N_EXPERTS, t), F32)
    for i in idxs:
        hit = hit + jnp.where(erow == i, 1.0, 0.0)
    hitb = hit.astype(BF16)
    r = lax.broadcasted_iota(I32, (t, t), 0)
    c = lax.broadcasted_iota(I32, (t, t), 1)
    earlier = jnp.where(r < c, 1.0, 0.0).astype(BF16)
    before = jnp.dot(hitb, earlier, preferred_element_type=F32) + cnt_ref[...]
    total = jnp.dot(hitb, jnp.ones((t, t), BF16), preferred_element_type=F32)
    rank_ref[...] = jnp.concatenate(
        [jnp.sum(jnp.where(erow == i, before, 0.0), axis=0, keepdims=True) for i in idxs], axis=0).astype(I32)
    cnt_ref[...] += total


def _post(x, ya, yb, ga, scf, shf, w_o, b_o, g1, b1, wrh, wrl, br):
    n, d = x.shape
    t = POST_TILE
    nt = n // t
    if ga.shape[0] == n:
        mod_spec = pl.BlockSpec((t, d), lambda i: (i, 0))
    else:
        tiles_per_seq = nt // ga.shape[0]
        ga, scf, shf = (a.reshape(a.shape[0], 1, d) for a in (ga, scf, shf))
        mod_spec = pl.BlockSpec((None, 1, d), lambda i: (i // tiles_per_seq, 0, 0))
    row = lambda w: pl.BlockSpec((t, w), lambda i: (i, 0))
    col = lambda r: pl.BlockSpec((r, t), lambda i: (0, i))
    const = lambda a: pl.BlockSpec(a.shape, lambda i: (0, 0))
    return pl.pallas_call(
        _post_kernel,
        out_shape=(jax.ShapeDtypeStruct((n, d), F32), jax.ShapeDtypeStruct((n * TOK_ROWS, LANES), F32),
                   jax.ShapeDtypeStruct((TOP_K, n), I32), jax.ShapeDtypeStruct((TOP_K, n), F32),
                   jax.ShapeDtypeStruct((TOP_K, n), I32), jax.ShapeDtypeStruct((N_EXPERTS, t), F32)),
        grid=(nt,),
        in_specs=[row(d), row(A_WIDTH), row(ATT_WIDTH), mod_spec, mod_spec, mod_spec,
                  const(w_o), const(b_o), const(g1), const(b1), const(wrh), const(wrl), const(br)],
        out_specs=(row(d), pl.BlockSpec((t * TOK_ROWS, LANES), lambda i: (i, 0)), col(TOP_K), col(TOP_K), col(TOP_K),
                   pl.BlockSpec((N_EXPERTS, t), lambda i: (0, 0))),
        compiler_params=_cparams("arbitrary"),
        name="post",
    )(x, ya, yb, ga, scf, shf, w_o, b_o, g1, b1, wrh, wrl, br)


def _token_rows(ref, r):
    return ref.at[pl.ds(pl.multiple_of(r * TOK_ROWS, TOK_ROWS), TOK_ROWS)]


def _token_copies_wait(hbm_ref, vmem_ref, sem, n_tokens):
    rows = n_tokens * TOK_ROWS
    pltpu.make_async_copy(hbm_ref.at[pl.ds(0, rows)], vmem_ref.at[pl.ds(0, rows)], sem).wait()


def _dispatch_kernel(zrow_ref, dest_ref, h_ref, xp_ref, zero_ref, sem):
    t = h_ref.shape[0] // TOK_ROWS

    block_rows = MOE_BLOCK * TOK_ROWS

    def zero_block(slot0):
        z0 = pl.multiple_of(slot0 * TOK_ROWS, block_rows)
        cp = pltpu.make_async_copy(zero_ref, xp_ref.at[pl.ds(z0, block_rows)], sem)
        cp.start()
        cp.wait()

    @pl.when(pl.program_id(0) == 0)
    def _():
        zero_ref[...] = jnp.zeros_like(zero_ref)
        for e in range(N_EXPERTS):
            @pl.when(zrow_ref[e] >= 0)
            def _():
                zero_block(zrow_ref[e])

        def unused(b, _):
            zero_block(b * MOE_BLOCK)
            return 0

        lax.fori_loop(zrow_ref[N_EXPERTS], xp_ref.shape[0] // block_rows, unused, 0)

    def body(i, _):
        for k in range(TOP_K):
            pltpu.make_async_copy(_token_rows(h_ref, i), _token_rows(xp_ref, dest_ref[k, i]), sem).start(priority=k % 2)
        return 0

    lax.fori_loop(0, t, body, 0, unroll=DMA_ISSUE_UNROLL)
    for k in range(TOP_K):
        _token_copies_wait(xp_ref, h_ref, sem, t)


def _dispatch(zrow, dest, h2, n_slots):
    t = ROW_TILE
    n = h2.shape[0] // TOK_ROWS
    grid_spec = pltpu.PrefetchScalarGridSpec(
        num_scalar_prefetch=1,
        grid=(n // t,),
        in_specs=[pl.BlockSpec((TOP_K, t), lambda i, z: (0, i), memory_space=pltpu.SMEM),
                  pl.BlockSpec((t * TOK_ROWS, LANES), lambda i, z: (i, 0))],
        out_specs=pl.BlockSpec(memory_space=pl.ANY),
        scratch_shapes=[pltpu.VMEM((MOE_BLOCK * TOK_ROWS, LANES), F32), pltpu.SemaphoreType.DMA],
    )
    return pl.pallas_call(
        _dispatch_kernel,
        out_shape=jax.ShapeDtypeStruct((n_slots * TOK_ROWS, LANES), F32),
        grid_spec=grid_spec,
        compiler_params=_cparams("arbitrary"),
        name="dispatch",
    )(zrow, dest, h2)


def _experts_kernel(be_ref, bi_ref, nu_ref, x_ref, wg_ref, bg_ref, wu_ref, bu_ref, wd_ref, bd_ref, y_ref,
                    wgb_ref, wub_ref, wdb_ref):
    i = pl.program_id(0)

    @pl.when((i == 0) | (be_ref[i] != be_ref[jnp.maximum(i - 1, 0)]))
    def _():
        wgb_ref[...] = wg_ref[...].astype(BF16)
        wub_ref[...] = wu_ref[...].astype(BF16)
        wdb_ref[...] = wd_ref[...].astype(BF16)

    @pl.when(i < nu_ref[0])
    def _():
        x = _load_token_tiles(x_ref).astype(BF16)
        g = jnp.minimum(jnp.dot(x, wgb_ref[...], preferred_element_type=F32) + bg_ref[...], SWIGLU_LIMIT)
        u = jnp.clip(jnp.dot(x, wub_ref[...], preferred_element_type=F32) + bu_ref[...], -SWIGLU_LIMIT, SWIGLU_LIMIT)
        a = g * jax.nn.sigmoid(SWIGLU_ALPHA * g)
        mid = ((u + 1.0) * a).astype(BF16)
        _store_token_tiles(y_ref, jnp.dot(mid, wdb_ref[...], preferred_element_type=F32) + bd_ref[...])

    @pl.when(pl.program_id(0) >= nu_ref[0])
    def _():
        y_ref[...] = jnp.zeros_like(y_ref)


def _experts(block_e, block_i, n_used, xp, wg, bg, wu, bu, wd, bd):
    d, f = wg.shape[1], wg.shape[2]
    nb = xp.shape[0] // (MOE_BLOCK * TOK_ROWS)
    wspec = lambda a, b: pl.BlockSpec((None, a, b), lambda i, be, bi, nu: (be[i], 0, 0))
    slots = pl.BlockSpec((MOE_BLOCK * TOK_ROWS, LANES), lambda i, be, bi, nu: (bi[i], 0))
    grid_spec = pltpu.PrefetchScalarGridSpec(
        num_scalar_prefetch=3,
        grid=(nb,),
        in_specs=[slots, wspec(d, f), wspec(1, f), wspec(d, f), wspec(1, f), wspec(f, d), wspec(1, d)],
        out_specs=pl.BlockSpec((MOE_BLOCK * TOK_ROWS, LANES), lambda i, be, bi, nu: (i, 0)),
        scratch_shapes=[pltpu.VMEM((d, f), BF16), pltpu.VMEM((d, f), BF16), pltpu.VMEM((f, d), BF16)],
    )
    return pl.pallas_call(
        _experts_kernel,
        out_shape=jax.ShapeDtypeStruct(xp.shape, F32),
        grid_spec=grid_spec,
        compiler_params=_cparams("arbitrary"),
        name="experts",
    )(block_e, block_i, n_used, xp, wg, bg, wu, bu, wd, bd)


def _combine_kernel(dest_ref, dest_next_ref, gate_ref, x1_ref, gf_ref, g2_ref, b2_ref, yp_ref, o_ref, buf_ref, sems):
    t = x1_ref.shape[0]
    step = pl.program_id(0)
    slot = step % 2

    def start_gather(dst_ref, into):
        def body(i, _):
            for k in range(TOP_K):
                pltpu.make_async_copy(_token_rows(yp_ref, dst_ref[k, i]), _token_rows(buf_ref.at[into, k], i),
                                      sems.at[into]).start(priority=k % 2)
            return 0

        lax.fori_loop(0, t, body, 0, unroll=DMA_ISSUE_UNROLL)

    @pl.when(step == 0)
    def _():
        start_gather(dest_ref, 0)

    @pl.when(step + 1 < pl.num_programs(0))
    def _():
        start_gather(dest_next_ref, 1 - slot)

    for k in range(TOP_K):
        _token_copies_wait(yp_ref, buf_ref.at[slot, k], sems.at[slot], t)

    gates = jnp.concatenate([gate_ref[...], jnp.zeros((LANES - TOP_K, t), F32)], axis=0).T
    f = gates[:, 0:1] * _load_token_tiles(buf_ref.at[slot, 0])
    for k in range(1, TOP_K):
        f = f + gates[:, k:k + 1] * _load_token_tiles(buf_ref.at[slot, k])
    o_ref[...] = _layer_norm(DEEPNORM_ALPHA * x1_ref[...] + (1.0 + gf_ref[...]) * f, g2_ref[...], b2_ref[...])


def _combine(dest, gates, x1, gf, g2, b2, yp):
    n, d = x1.shape
    t = ROW_TILE
    nt = n // t
    if gf.shape[0] == n:
        mod_spec = pl.BlockSpec((t, d), lambda i: (i, 0))
    else:
        tiles_per_seq = nt // gf.shape[0]
        gf = gf.reshape(gf.shape[0], 1, d)
        mod_spec = pl.BlockSpec((None, 1, d), lambda i: (i // tiles_per_seq, 0, 0))
    return pl.pallas_call(
        _combine_kernel,
        out_shape=jax.ShapeDtypeStruct((n, d), F32),
        grid=(nt,),
        in_specs=[pl.BlockSpec((TOP_K, t), lambda i: (0, i), memory_space=pltpu.SMEM),
                  pl.BlockSpec((TOP_K, t), lambda i: (0, jnp.minimum(i + 1, nt - 1)), memory_space=pltpu.SMEM),
                  pl.BlockSpec((TOP_K, t), lambda i: (0, i)),
                  pl.BlockSpec((t, d), lambda i: (i, 0)), mod_spec,
                  pl.BlockSpec((1, d), lambda i: (0, 0)), pl.BlockSpec((1, d), lambda i: (0, 0)),
                  pl.BlockSpec(memory_space=pl.ANY)],
        out_specs=pl.BlockSpec((t, d), lambda i: (i, 0)),
        scratch_shapes=[pltpu.VMEM((2, TOP_K, t * TOK_ROWS, LANES), F32), pltpu.SemaphoreType.DMA((2,))],
        compiler_params=_cparams("arbitrary"),
        name="combine",
    )(dest, dest, gates, x1, gf, g2, b2, yp)


def _moe(h2, tope, rank, counts, gates, x1, gf, g2, b2, experts_w):
    n = tope.shape[1]
    nb = (n * TOP_K + N_EXPERTS * (MOE_BLOCK - 1) + MOE_BLOCK - 1) // MOE_BLOCK
    pcounts = (counts + MOE_BLOCK - 1) // MOE_BLOCK * MOE_BLOCK
    pend = jnp.cumsum(pcounts)
    pstart = pend - pcounts
    eids = jnp.arange(N_EXPERTS, dtype=I32).reshape(N_EXPERTS, 1, 1)
    dest = rank + jnp.sum(jnp.where(tope[None] == eids, pstart.reshape(N_EXPERTS, 1, 1), 0), axis=0).astype(I32)
    n_used = (pend[-1] // MOE_BLOCK).astype(I32)
    zrow = jnp.concatenate([jnp.where(counts > 0, pend - MOE_BLOCK, -1), n_used.reshape(1)]).astype(I32)
    blk = jnp.arange(nb, dtype=I32)
    block_i = jnp.minimum(blk, n_used - 1)
    block_e = jnp.minimum(jnp.sum(pend[None, :] <= (block_i * MOE_BLOCK)[:, None], axis=1), N_EXPERTS - 1).astype(I32)
    xp = _dispatch(zrow, dest, h2, nb * MOE_BLOCK)
    yp = _experts(block_e, block_i, n_used.reshape(1), xp, *experts_w)
    return _combine(dest, gates, x1, gf, g2, b2, yp)


def kernel(x_prompt, x_sample, c_prompt, c_sample, cache_k, cache_v, cache_kidx, w_in, a_ln_g, a_ln_b, a_ws, a_bs,
           w_o, b_o, w_c, b_c, ln1_g, ln1_b, ln2_g, ln2_b, w_router, b_router, w_gate, b_gate, w_up, b_up,
           w_down, b_down):
    bp, s, d = x_prompt.shape
    bs, ts, _ = x_sample.shape
    past = cache_k.shape[2]
    np_, ns = bp * s, bs * ts

    wi = w_in[0]
    w_in_p = jnp.zeros((d, IN_COLS_PAD), F32)
    w_in_p = w_in_p.at[:, :OFF_IK].set(wi[:, :OFF_IK])
    w_in_p = w_in_p.at[:, OFF_IK:OFF_IK + IDX_DIM].set(wi[:, OFF_IK:OFF_IK + IDX_DIM])
    w_in_p = w_in_p.at[:, OFF_IK + IDX_DIM:OFF_IK + 2 * IDX_DIM].set(wi[:, OFF_IK:OFF_IK + IDX_DIM])
    w_in_p = w_in_p.at[:, OFF_IW:OFF_IW + N_IDX_HEADS].set(wi[:, OFF_IK + IDX_DIM:OFF_IK + IDX_DIM + N_IDX_HEADS])
    w_in_p = w_in_p.astype(BF16)
    lng, lnb = a_ln_g[0].reshape(1, A_WIDTH), a_ln_b[0].reshape(1, A_WIDTH)
    wtril = jnp.tril(a_ws[0])
    wm_p = wtril.astype(BF16)
    bm_p = jnp.broadcast_to(a_bs[0][:, :, None], (A_GROUPS, A_CHUNK, A_GROUP_DIM)).astype(F32)
    rep = A_CHUNK // ts
    wm_s = jnp.einsum("ab,gij->gaibj", jnp.eye(rep, dtype=F32), wtril[:, :ts, :ts]).reshape(
        A_GROUPS, A_CHUNK, A_CHUNK).astype(BF16)
    bm_s = jnp.broadcast_to(jnp.tile(a_bs[0][:, :ts], (1, rep))[:, :, None], (A_GROUPS, A_CHUNK, A_GROUP_DIM)).astype(F32)
    w_o_b = w_o[0].astype(BF16)
    b_o_r = b_o[0].reshape(1, d)
    wr_t = w_router[0].T
    wrh = wr_t.astype(BF16)
    wrl = (wr_t - wrh.astype(F32)).astype(BF16)
    br = jnp.broadcast_to(b_router[0][:, None], (N_EXPERTS, POST_TILE)).astype(F32)
    experts_w = (w_gate[0], b_gate[0][:, None, :], w_up[0], b_up[0][:, None, :], w_down[0], b_down[0][:, None, :])
    g1, b1 = ln1_g[0].reshape(1, d), ln1_b[0].reshape(1, d)
    g2, b2 = ln2_g[0].reshape(1, d), ln2_b[0].reshape(1, d)

    mods = _cond_mods(jnp.concatenate([c_prompt, c_sample], axis=0), w_c[0], b_c[0]).reshape(bp + bs, 6, d)
    mods_p = [mods[:bp, i] for i in range(6)]
    mods_s = [jnp.repeat(mods[bp:, i], ts, axis=0) for i in range(6)]

    xp2 = x_prompt.reshape(np_, d)
    ya, q, kft, kb, _, vft, vt, iq, ikft, ikb, iwt, _ = _project(
        xp2, mods_p[1], mods_p[0], w_in_p, lng, lnb, wm_p, bm_p, bp)
    yb = _attend_prompt(q, iq, iwt, kb, vt, ikb, bp, s, min(TOPK_MAX, s // 4))
    x1, h2, tope, gates, rank, cnt = _post(xp2, ya, yb, mods_p[2], mods_p[4], mods_p[3], w_o_b, b_o_r, g1, b1, wrh, wrl, br)
    y_p = _moe(h2, tope, rank, cnt[:, 0].astype(I32), gates, x1, mods_p[5], g2, b2, experts_w)
    heads_last = lambda a: jnp.transpose(a.reshape(1, bp, N_HEADS, HEAD_DIM, s), (0, 1, 4, 2, 3))
    out_p = (y_p.reshape(bp, s, d), heads_last(kft), heads_last(vft),
             jnp.transpose(ikft.reshape(1, bp, IDX_DIM, s), (0, 1, 3, 2)))

    xs2 = x_sample.reshape(ns, d)
    ya, q, kft, _, vf, vft, _, iq, ikft, _, iwt, va = _project(
        xs2, mods_s[1], mods_s[0], w_in_p, lng, lnb, wm_s, bm_s, 1)
    kf, ikf = kft[0].T, ikft[0].T
    per_seq_t = lambda a: jnp.pad(jnp.transpose(a[0].reshape(a.shape[1], bs, ts), (1, 0, 2)),
                                  ((0, 0), (0, 0), (0, LANES - ts)))
    iw_rows = jnp.broadcast_to(
        jnp.transpose(iwt.reshape(N_IDX_HEADS, bs, ts), (1, 0, 2)).reshape(bs, N_IDX_HEADS * ts, 1),
        (bs, N_IDX_HEADS * ts, LANES))
    cache_kt = jnp.transpose(cache_k[0], (0, 2, 3, 1)).reshape(bs, ATT_WIDTH, past)
    cache_vt = jnp.transpose(cache_v[0], (0, 2, 3, 1)).reshape(bs, ATT_WIDTH, past)
    cache_it = jnp.transpose(cache_kidx[0], (0, 2, 1))
    yb = _attend_sample(q, iq, iw_rows, per_seq_t(kft), per_seq_t(vft), per_seq_t(ikft), cache_kt, cache_vt, cache_it,
                        min(TOPK_MAX, (past + ts) // 4))
    x1, h2, tope, gates, rank, cnt = _post(xs2, ya, yb, mods_s[2], mods_s[4], mods_s[3], w_o_b, b_o_r, g1, b1, wrh, wrl, br)
    y_s = _moe(h2, tope, rank, cnt[:, 0].astype(I32), gates, x1, mods_s[5], g2, b2, experts_w)

    return (out_p[0], y_s.reshape(bs, ts, d), out_p[1], out_p[2], out_p[3],
            kf.reshape(1, bs, ts, N_HEADS, HEAD_DIM), vf.reshape(1, bs, ts, N_HEADS, HEAD_DIM),
            ikf.reshape(1, bs, ts, IDX_DIM), va.reshape(1, bs, ts, A_WIDTH))
```

```python
import functools

import jax
import jax.numpy as jnp
from jax import lax
from jax.experimental import pallas as pl
from jax.experimental.pallas import tpu as pltpu

F32 = jnp.float32
BF16 = jnp.bfloat16
I32 = jnp.int32

D_MODEL = 1024
CHUNK_SHIFT = 6
A_GROUPS = 4
A_GROUP_DIM = 128
A_WIDTH = A_GROUPS * A_GROUP_DIM
A_CHUNK = 128
N_HEADS = 8
HEAD_DIM = 64
ATT_WIDTH = N_HEADS * HEAD_DIM
N_IDX_HEADS = 8
IDX_DIM = 64
TOPK_MAX = 256
ATTN_SCALE = HEAD_DIM ** -0.5
VT_HEAD_ROWS = HEAD_DIM + 16
VT_ROWS = N_HEADS * VT_HEAD_ROWS
IDX_W_SCALE = (N_IDX_HEADS ** -0.5) * (IDX_DIM ** -0.5)
N_EXPERTS = 32
TOP_K = 4
SWIGLU_LIMIT = 7.0
SWIGLU_ALPHA = 1.702
DEEPNORM_ALPHA = 2.0 ** 0.25
LN_EPS = 1e-5

LANES = 128
SUBLANES = 8
VMEM_LIMIT_BYTES = 56 * 1024 * 1024

ROW_TILE = 256
PROJECT_TILE = 512
POST_TILE = 512
MOE_BLOCK = 512
ATT_TILE = 256
SCORE_GROUP = 4
ATT_GROUP = 4
DMA_ISSUE_UNROLL = 8
STATIC_TILE_UNROLL = 4

OFF_AU, OFF_AV, OFF_Q, OFF_K, OFF_V, OFF_IQ = 0, 512, 1024, 1536, 2048, 2560
OFF_IK = 3072
OFF_IW = 3200
IN_COLS_PAD = 3328

MASKED_DIST = 3.0e32
F32_MAX = 3.4028234663852886e38
INT32_MIN = -(2 ** 31)
KEY_NEG_INF = INT32_MIN + 0x7FFFFF

_NT = (((1,), (1,)), ((), ()))


def _cparams(*sem):
    return pltpu.CompilerParams(dimension_semantics=sem, vmem_limit_bytes=VMEM_LIMIT_BYTES)


def _mods_kernel(c_ref, w_ref, b_ref, o_ref):
    c = c_ref[...]
    s = c * jax.nn.sigmoid(c)
    o_ref[...] = jnp.dot(s.astype(BF16), w_ref[...].astype(BF16), preferred_element_type=F32) + b_ref[...]


def _cond_mods(c, w_c, b_c):
    nb, d = c.shape
    n_out = w_c.shape[1]
    return pl.pallas_call(
        _mods_kernel,
        out_shape=jax.ShapeDtypeStruct((nb, n_out), F32),
        grid=(n_out // d,),
        in_specs=[pl.BlockSpec((nb, d), lambda j: (0, 0)),
                  pl.BlockSpec((d, d), lambda j: (0, j)),
                  pl.BlockSpec((1, d), lambda j: (0, j))],
        out_specs=pl.BlockSpec((nb, d), lambda j: (0, j)),
        compiler_params=_cparams("arbitrary"),
        name="mods",
    )(c, w_c, b_c.reshape(1, n_out))


def _gelu(x):
    return 0.5 * x * (1.0 + lax.erf(x * 0.7071067811865476))


def _project_kernel(x_ref, sc_ref, sh_ref, w_ref, lng_ref, lnb_ref, wm_ref, bm_ref,
                    ya_ref, q_ref, kft_ref, kb_ref, vf_ref, vft_ref, vt_ref, iq_ref, ikft_ref, ikb_ref, iwt_ref,
                    va_ref):
    t = x_ref.shape[0]
    h = (x_ref[...] * (1.0 + sc_ref[...]) + sh_ref[...]).astype(BF16)

    def proj(c0, n):
        return jnp.dot(h, w_ref[:, c0:c0 + n], preferred_element_type=F32)

    u = _gelu(proj(OFF_AU, A_WIDTH))
    gv = _gelu(proj(OFF_AV, A_WIDTH))
    for g in range(A_GROUPS):
        lo, hi = g * A_GROUP_DIM, (g + 1) * A_GROUP_DIM
        xg = gv[:, lo:hi]
        mu = jnp.mean(xg, axis=-1, keepdims=True)
        xc = xg - mu
        var = jnp.mean(xc * xc, axis=-1, keepdims=True)
        vg = xc * lax.rsqrt(var + LN_EPS) * lng_ref[:, lo:hi] + lnb_ref[:, lo:hi]
        va_ref[:, lo:hi] = vg
        vgb = vg.astype(BF16)
        for c in range(t // A_CHUNK):
            r0, r1 = c * A_CHUNK, (c + 1) * A_CHUNK
            mixed = jnp.dot(wm_ref[g], vgb[r0:r1, :], preferred_element_type=F32) + bm_ref[g]
            ya_ref[r0:r1, lo:hi] = (u[r0:r1, lo:hi] * mixed).astype(BF16)

    q_ref[...] = proj(OFF_Q, ATT_WIDTH).astype(BF16)
    k = proj(OFF_K, ATT_WIDTH)
    kft_ref[...] = k.T
    kb_ref[...] = k.astype(BF16)
    v = proj(OFF_V, ATT_WIDTH)
    vf_ref[...] = v
    v_t32 = v.T
    vft_ref[...] = v_t32
    v_t = v_t32.astype(BF16)
    ones = jnp.ones((VT_HEAD_ROWS - HEAD_DIM, ATT_TILE), BF16)
    for c in range(t // ATT_TILE):
        cols = slice(c * ATT_TILE, (c + 1) * ATT_TILE)
        vt_ref[c] = jnp.concatenate(
            [blk for h in range(N_HEADS) for blk in (v_t[h * HEAD_DIM:(h + 1) * HEAD_DIM, cols], ones)], axis=0)
    iq_ref[...] = proj(OFF_IQ, N_IDX_HEADS * IDX_DIM).astype(BF16)
    ik2 = proj(OFF_IK, LANES)
    ikft_ref[...] = ik2.T[:IDX_DIM, :]
    ikb_ref[...] = ik2.astype(BF16)
    iw = proj(OFF_IW, LANES) * IDX_W_SCALE
    iwt_ref[...] = iw.T[:N_IDX_HEADS, :]


def _project(x, sc, sh, w_in_p, lng, lnb, wm, bm, n_out_seq):
    n, d = x.shape
    t = PROJECT_TILE
    nt = n // t
    tiles_per_out = nt // n_out_seq
    trans = lambda rows: pl.BlockSpec((None, rows, t), lambda i: (i // tiles_per_out, 0, i % tiles_per_out))
    if sc.shape[0] == n:
        mod_spec = pl.BlockSpec((t, d), lambda i: (i, 0))
    else:
        tiles_per_seq = nt // sc.shape[0]
        sc = sc.reshape(sc.shape[0], 1, d)
        sh = sh.reshape(sh.shape[0], 1, d)
        mod_spec = pl.BlockSpec((None, 1, d), lambda i: (i // tiles_per_seq, 0, 0))
    row = lambda w: pl.BlockSpec((t, w), lambda i: (i, 0))
    const2 = lambda a: pl.BlockSpec(a.shape, lambda i: (0, 0))
    const3 = lambda a: pl.BlockSpec(a.shape, lambda i: (0, 0, 0))
    out_shape = (
        jax.ShapeDtypeStruct((n, A_WIDTH), BF16),
        jax.ShapeDtypeStruct((n, ATT_WIDTH), BF16),
        jax.ShapeDtypeStruct((n_out_seq, ATT_WIDTH, n // n_out_seq), F32),
        jax.ShapeDtypeStruct((n, ATT_WIDTH), BF16),
        jax.ShapeDtypeStruct((n, ATT_WIDTH), F32),
        jax.ShapeDtypeStruct((n_out_seq, ATT_WIDTH, n // n_out_seq), F32),
        jax.ShapeDtypeStruct((n // ATT_TILE, VT_ROWS, ATT_TILE), BF16),
        jax.ShapeDtypeStruct((n, ATT_WIDTH), BF16),
        jax.ShapeDtypeStruct((n_out_seq, IDX_DIM, n // n_out_seq), F32),
        jax.ShapeDtypeStruct((n, LANES), BF16),
        jax.ShapeDtypeStruct((N_IDX_HEADS, n), F32),
        jax.ShapeDtypeStruct((n, A_WIDTH), F32),
    )
    out_specs = (row(A_WIDTH), row(ATT_WIDTH), trans(ATT_WIDTH), row(ATT_WIDTH), row(ATT_WIDTH), trans(ATT_WIDTH),
                 pl.BlockSpec((t // ATT_TILE, VT_ROWS, ATT_TILE), lambda i: (i, 0, 0)),
                 row(ATT_WIDTH), trans(IDX_DIM), row(LANES),
                 pl.BlockSpec((N_IDX_HEADS, t), lambda i: (0, i)),
                 row(A_WIDTH))
    return pl.pallas_call(
        _project_kernel,
        out_shape=out_shape,
        grid=(nt,),
        in_specs=[row(d), mod_spec, mod_spec, const2(w_in_p), const2(lng), const2(lnb), const3(wm), const3(bm)],
        out_specs=out_specs,
        compiler_params=_cparams("arbitrary"),
        name="project",
    )(x, sc, sh, w_in_p, lng, lnb, wm, bm)


DIGIT_BITS = 8
N_DIGITS = 32 // DIGIT_BITS
DIGIT_ABOVE = 512.0
DIGIT_BELOW = -1.0
PACKED_ROWS = 16


def _static_unroll(trips):
    return STATIC_TILE_UNROLL if isinstance(trips, int) else 1


def _tile_loop(nkt, body, init):
    if isinstance(nkt, int):
        return lax.fori_loop(0, nkt, body, init, unroll=STATIC_TILE_UNROLL)
    g = STATIC_TILE_UNROLL
    groups = nkt // g

    def group_body(i, c):
        for k in range(g):
            c = body(g * i + k, c)
        return c

    return lax.fori_loop(g * groups, nkt, body, lax.fori_loop(0, groups, group_body, init))


def _mono_key(x):
    b = lax.bitcast_convert_type(x, I32)
    return jnp.where(b >= 0, b, b ^ jnp.int32(0x7FFFFFFF))


def _digit_plane(key, phase):
    shift = 32 - DIGIT_BITS * (phase + 1)
    d = lax.shift_right_arithmetic(key, jnp.int32(shift)) if shift else key
    d = d + (1 << (DIGIT_BITS - 1)) if phase == 0 else d & ((1 << DIGIT_BITS) - 1)
    return d.astype(F32).astype(BF16)


def _count_plane(plane_ref, nkt, cand, strict):
    _, tk, w = plane_ref.shape
    cb = cand.astype(BF16)
    one, zero = jnp.ones((), BF16), jnp.zeros((), BF16)

    def body(kt, cnt):
        e = plane_ref[kt]
        accs = [jnp.zeros((PACKED_ROWS, w), BF16) for _ in range(4)]
        for r in range(tk // PACKED_ROWS):
            blk = e[r * PACKED_ROWS:(r + 1) * PACKED_ROWS, :]
            accs[r % 4] = accs[r % 4] + jnp.where((blk > cb) if strict else (blk >= cb), one, zero)
        return cnt + ((accs[0] + accs[1]) + (accs[2] + accs[3])).astype(F32)

    cnt = _tile_loop(nkt, body, jnp.zeros((PACKED_ROWS, w), F32))
    return jnp.sum(cnt, axis=0, keepdims=True)


def _search_digit(plane_ref, nkt, topk):
    w = plane_ref.shape[2]

    def bit_body(i, d):
        cand = d + lax.shift_left(jnp.int32(1), jnp.int32(DIGIT_BITS - 1) - i).astype(F32)
        cnt = _count_plane(plane_ref, nkt, cand, strict=False)
        return jnp.where(cnt >= float(topk), cand, d)

    return lax.fori_loop(0, DIGIT_BITS, bit_body, jnp.zeros((1, w), F32))


def _topk_select(keys_ref, plane_ref, nkt, topk):
    _, tk, w = plane_ref.shape
    d = _search_digit(plane_ref, nkt, topk)
    for phase in range(1, N_DIGITS):
        db = d.astype(BF16)

        def refine(kt, _, phase=phase, db=db):
            e = plane_ref[kt]
            decided = jnp.where(e > db, jnp.asarray(DIGIT_ABOVE, BF16), jnp.asarray(DIGIT_BELOW, BF16))
            plane_ref[kt] = jnp.where(e == db, _digit_plane(keys_ref[kt], phase), decided)
            return 0

        _tile_loop(nkt, refine, 0)
        d = _search_digit(plane_ref, nkt, topk)

    cnt_ge = _count_plane(plane_ref, nkt, d, strict=False)

    @pl.when(jnp.max(cnt_ge) > float(topk))
    def _():
        need = float(topk) - _count_plane(plane_ref, nkt, d, strict=True)
        r = lax.broadcasted_iota(I32, (tk, tk), 0)
        c = lax.broadcasted_iota(I32, (tk, tk), 1)
        before = jnp.where(c < r, 1.0, 0.0).astype(BF16)

        def body(kt, seen):
            e = plane_ref[kt].astype(F32)
            eq = e == d
            eqf = jnp.where(eq, 1.0, 0.0)
            prior = jnp.dot(before, eqf.astype(BF16), preferred_element_type=F32) + seen
            plane_ref[kt] = jnp.where(eq & (prior >= need), DIGIT_BELOW, e).astype(BF16)
            return seen + jnp.sum(eqf.reshape(tk // SUBLANES, SUBLANES, w), axis=0).sum(axis=0, keepdims=True)

        lax.fori_loop(0, nkt, body, jnp.zeros((1, w), F32))

    return d


def _selected(keys_ref, plane_ref, kt, d):
    return (plane_ref[kt].astype(F32) >= d) & (keys_ref[kt] > jnp.int32(KEY_NEG_INF))


def _half_mask(x_pair, head):
    lane = lax.broadcasted_iota(I32, x_pair.shape, 1)
    keep = (lane >= HEAD_DIM) if head % 2 else (lane < HEAD_DIM)
    return jnp.where(keep, x_pair, jnp.zeros_like(x_pair))


def _attend_prompt_kernel(q_ref, iq_ref, iwt_ref, k_ref, vt_ref, ik_ref, o_ref, keys_ref, plane_ref, *head_refs, topk):
    qh_refs, acc_refs, lt_refs = (head_refs[i * N_HEADS:(i + 1) * N_HEADS] for i in range(3))
    tq = q_ref.shape[0]
    tk = keys_ref.shape[1]
    j = pl.program_id(1)
    nkt = j + 1
    q0 = j * tq
    row = lax.broadcasted_iota(I32, (tk, tq), 0)
    lane = lax.broadcasted_iota(I32, (tk, tq), 1)
    qpos = q0 + lane

    iq = iq_ref[...]
    iqm = [_half_mask(iq[:, (h // 2) * LANES:(h // 2 + 1) * LANES], h) for h in range(N_IDX_HEADS)]
    iw = iwt_ref[...]

    def score_tiles(kt, n, diagonal):
        k0 = pl.multiple_of(kt * tk, tk)
        ikt = ik_ref[pl.ds(k0, n * tk), :]
        s = jnp.zeros((n * tk, tq), F32)
        for h in range(N_IDX_HEADS):
            r = lax.dot_general(ikt, iqm[h], _NT, preferred_element_type=F32)
            s = s + jnp.maximum(r, 0.0) * iw[h:h + 1, :]
        if diagonal:
            adm = lax.shift_right_logical(k0 + row, CHUNK_SHIFT) <= lax.shift_right_logical(qpos, CHUNK_SHIFT)
            s = jnp.where(adm, s, -jnp.inf)
        for c in range(n):
            key = _mono_key(s[c * tk:(c + 1) * tk, :])
            keys_ref[kt + c] = key
            plane_ref[kt + c] = _digit_plane(key, 0)
        return 0

    assert tk == tq
    groups = j // SCORE_GROUP
    lax.fori_loop(0, groups, lambda i, _: score_tiles(SCORE_GROUP * i, SCORE_GROUP, False), 0)
    lax.fori_loop(SCORE_GROUP * groups, j, lambda kt, _: score_tiles(kt, 1, False), 0)
    score_tiles(j, 1, True)

    d_last = _topk_select(keys_ref, plane_ref, nkt, topk)

    def masked_distance(kt):
        dist = jnp.abs(qpos - (kt * tk + row)).astype(F32)
        return jnp.where(_selected(keys_ref, plane_ref, kt, d_last), dist, MASKED_DIST)

    qfull = q_ref[...]
    for h in range(N_HEADS):
        pair = h // 2
        qh_refs[h][...] = _half_mask(qfull[:, pair * LANES:(pair + 1) * LANES], h) * jnp.asarray(ATTN_SCALE, BF16)
        acc_refs[h][...] = jnp.zeros_like(acc_refs[h])

    def att_tiles(kt, n, carry):
        m_all, l_all = carry
        k0 = pl.multiple_of(kt * tk, tk)
        dist = jnp.concatenate([masked_distance(kt + c) for c in range(n)], axis=0)
        ms = []
        for h in range(N_HEADS):
            pair = h // 2
            slope = 2.0 ** (-8.0 * (h + 1) / N_HEADS)
            kp = k_ref[pl.ds(k0, n * tk), pair * LANES:(pair + 1) * LANES]
            lt = lax.dot_general(kp, qh_refs[h][...], _NT, preferred_element_type=F32) - slope * dist
            lt_refs[h][0:n * tk, :] = lt.astype(BF16)
            tile_max = jnp.max(lt, axis=0, keepdims=True).astype(BF16).astype(F32)
            ms.append(jnp.maximum(m_all[h:h + 1, :], tile_max))
        ls = []
        for h in range(N_HEADS):
            alpha = jnp.exp(m_all[h:h + 1, :] - ms[h])
            mb = ms[h].astype(BF16)
            acc = alpha * acc_refs[h][...]
            l = alpha * l_all[h:h + 1, :]
            for c in range(n):
                p = jnp.exp(lt_refs[h][c * tk:(c + 1) * tk, :] - mb)
                pv = jnp.dot(vt_ref[kt + c, h * VT_HEAD_ROWS:(h + 1) * VT_HEAD_ROWS, :], p,
                             preferred_element_type=F32)
                acc = acc + pv[:HEAD_DIM, :]
                l = l + pv[HEAD_DIM:HEAD_DIM + 1, :]
            acc_refs[h][...] = acc
            ls.append(l)
        return jnp.concatenate(ms, axis=0), jnp.concatenate(ls, axis=0)

    groups = nkt // ATT_GROUP
    carry = (jnp.full((N_HEADS, tq), -jnp.inf, F32), jnp.zeros((N_HEADS, tq), F32))
    carry = lax.fori_loop(0, groups, lambda i, c: att_tiles(ATT_GROUP * i, ATT_GROUP, c), carry)
    _, l_all = lax.fori_loop(ATT_GROUP * groups, nkt, lambda kt, c: att_tiles(kt, 1, c), carry)
    out_t = jnp.concatenate([acc_refs[h][...] / l_all[h:h + 1, :] for h in range(N_HEADS)], axis=0)
    o_ref[...] = out_t.T.astype(BF16)


def _attend_prompt(q, iq, iwt, kb, vt, ikb, n_seq, seq_len, topk):
    n = q.shape[0]
    t = ATT_TILE
    nq = seq_len // t
    once = pl.Buffered(1)
    return pl.pallas_call(
        functools.partial(_attend_prompt_kernel, topk=topk),
        out_shape=jax.ShapeDtypeStruct((n, ATT_WIDTH), BF16),
        grid=(n_seq, nq),
        in_specs=[pl.BlockSpec((t, ATT_WIDTH), lambda b, j: (b * nq + j, 0)),
                  pl.BlockSpec((t, ATT_WIDTH), lambda b, j: (b * nq + j, 0)),
                  pl.BlockSpec((N_IDX_HEADS, t), lambda b, j: (0, b * nq + j)),
                  pl.BlockSpec((seq_len, ATT_WIDTH), lambda b, j: (b, 0), pipeline_mode=once),
                  pl.BlockSpec((nq, VT_ROWS, t), lambda b, j: (b, 0, 0), pipeline_mode=once),
                  pl.BlockSpec((seq_len, LANES), lambda b, j: (b, 0), pipeline_mode=once)],
        out_specs=pl.BlockSpec((t, ATT_WIDTH), lambda b, j: (b * nq + j, 0)),
        scratch_shapes=([pltpu.VMEM((nq, t, t), I32), pltpu.VMEM((nq, t, t), BF16)]
                        + [pltpu.VMEM((t, LANES), BF16)] * N_HEADS
                        + [pltpu.VMEM((HEAD_DIM, t), F32)] * N_HEADS + [pltpu.VMEM((ATT_GROUP * t, t), BF16)] * N_HEADS),
        compiler_params=_cparams("arbitrary", "arbitrary"),
        name="attend_prompt",
    )(q, iq, iwt, kb, vt, ikb)


SAMPLE_KEY_TILE = 512


def _wide(a, width):
    return a if width == LANES else jnp.concatenate([a] * (width // LANES), axis=1)


def _row_sums(x_bf16):
    return jnp.dot(x_bf16, jnp.ones((x_bf16.shape[1], LANES), BF16), preferred_element_type=F32)


def _count_rows(planes, cand, strict):
    cb = cand.astype(BF16)
    one, zero = jnp.ones((), BF16), jnp.zeros((), BF16)
    accs = [jnp.zeros(cand.shape, BF16) for _ in range(4)]
    i = 0
    for ref in planes:
        e = ref[...]
        for c in range(e.shape[1] // LANES):
            blk = e[:, c * LANES:(c + 1) * LANES]
            accs[i % 4] = accs[i % 4] + jnp.where((blk > cb) if strict else (blk >= cb), one, zero)
            i += 1
    assert i <= 256
    return _row_sums((accs[0] + accs[1]) + (accs[2] + accs[3]))


def _topk_select_rows(keys, planes, topk):
    rows = planes[0].shape[0]

    def search():
        def bit_body(i, d):
            cand = d + lax.shift_left(jnp.int32(1), jnp.int32(DIGIT_BITS - 1) - i).astype(F32)
            return jnp.where(_count_rows(planes, cand, strict=False) >= float(topk), cand, d)

        return lax.fori_loop(0, DIGIT_BITS, bit_body, jnp.zeros((rows, LANES), F32))

    d = search()
    for phase in range(1, N_DIGITS):
        db = d.astype(BF16)
        for kref, pref in zip(keys, planes):
            e = pref[...]
            dw = _wide(db, e.shape[1])
            decided = jnp.where(e > dw, jnp.asarray(DIGIT_ABOVE, BF16), jnp.asarray(DIGIT_BELOW, BF16))
            pref[...] = jnp.where(e == dw, _digit_plane(kref[...], phase), decided)
        d = search()

    cnt_ge = _count_rows(planes, d, strict=False)

    @pl.when(jnp.max(cnt_ge) > float(topk))
    def _():
        need = float(topk) - _count_rows(planes, d, strict=True)
        seen = jnp.zeros((rows, LANES), F32)
        for pref in planes:
            width = pref.shape[1]
            e = pref[...].astype(F32)
            eq = e == _wide(d, width)
            eqb = jnp.where(eq, 1.0, 0.0).astype(BF16)
            r = lax.broadcasted_iota(I32, (width, width), 0)
            c = lax.broadcasted_iota(I32, (width, width), 1)
            before = jnp.where(r < c, 1.0, 0.0).astype(BF16)
            prior = jnp.dot(eqb, before, preferred_element_type=F32) + _wide(seen, width)
            pref[...] = jnp.where(eq & (prior >= _wide(need, width)), DIGIT_BELOW, e).astype(BF16)
            seen = seen + _row_sums(eqb)

    return d


def _attend_sample_kernel(q_ref, iq_ref, iwr_ref, knt_ref, vnt_ref, iknt_ref, ckt_ref, cvt_ref, cit_ref, o_ref,
                          keys_ref, plane_ref, keys_new_ref, plane_new_ref, lt_ref, lt_new_ref, *, topk, past):
    tq = q_ref.shape[0]
    nct, _, tkc = keys_ref.shape
    rows = N_HEADS * tq
    row = lax.broadcasted_iota(I32, (rows, LANES), 0)
    qpos = past + (row & (tq - 1))
    slope = lax.bitcast_convert_type(
        lax.shift_left(126 - lax.shift_right_logical(row, tq.bit_length() - 1), 23), F32)
    qpos16 = past + lax.broadcasted_iota(I32, (tq, LANES), 0)

    iq = iq_ref[...]
    q = q_ref[...]
    iq_rows = jnp.concatenate([iq[:, h * IDX_DIM:(h + 1) * IDX_DIM] for h in range(N_IDX_HEADS)], axis=0)
    q_rows = jnp.concatenate(
        [_half_mask_wide(q, h) for h in range(N_HEADS)], axis=0) * jnp.asarray(ATTN_SCALE, BF16)
    iw_rows = iwr_ref[...]

    tiles = [(kt * tkc, tkc, cit_ref.at[:, kt * tkc:(kt + 1) * tkc], ckt_ref.at[:, kt * tkc:(kt + 1) * tkc],
              cvt_ref.at[:, kt * tkc:(kt + 1) * tkc], keys_ref.at[kt], plane_ref.at[kt], lt_ref.at[kt])
             for kt in range(nct)]
    tiles.append((past, LANES, iknt_ref, knt_ref, vnt_ref, keys_new_ref, plane_new_ref, lt_new_ref))

    def key_positions(k0, width, n_rows):
        return k0 + lax.broadcasted_iota(I32, (n_rows, width), 1)

    for k0, width, ikt, _, _, kref, pref, _ in tiles:
        s = jnp.dot(iq_rows, ikt[...].astype(BF16), preferred_element_type=F32)
        s = jnp.maximum(s, 0.0) * _wide(iw_rows, width)
        score = s[0:tq, :]
        for h in range(1, N_IDX_HEADS):
            score = score + s[h * tq:(h + 1) * tq, :]
        kpos = key_positions(k0, width, tq)
        adm = lax.shift_right_logical(kpos, CHUNK_SHIFT) <= lax.shift_right_logical(_wide(qpos16, width), CHUNK_SHIFT)
        if k0 == past:
            adm = adm & (kpos < past + tq)
        key = _mono_key(jnp.where(adm, score, -jnp.inf))
        kref[...] = key
        pref[...] = _digit_plane(key, 0)

    d_last = _topk_select_rows([t[5] for t in tiles], [t[6] for t in tiles], topk)

    m_part = jnp.full((rows, LANES), -jnp.inf, F32)
    for k0, width, _, kt_ref, _, kref, pref, ltref in tiles:
        sel = (pref[...].astype(F32) >= _wide(d_last, width)) & (kref[...] > jnp.int32(KEY_NEG_INF))
        sel = jnp.concatenate([jnp.where(sel, 1.0, 0.0)] * N_HEADS, axis=0) > 0.5
        dist = jnp.abs(_wide(qpos, width) - key_positions(k0, width, rows)).astype(F32)
        lt = (jnp.dot(q_rows, kt_ref[...].astype(BF16), preferred_element_type=F32)
              - _wide(slope, width) * jnp.where(sel, dist, MASKED_DIST))
        ltref[...] = lt
        for c in range(width // LANES):
            m_part = jnp.maximum(m_part, lt[:, c * LANES:(c + 1) * LANES])
    m = jnp.broadcast_to(jnp.max(m_part, axis=1, keepdims=True), (rows, LANES))

    acc = jnp.zeros((rows, ATT_WIDTH), F32)
    l_part = jnp.zeros((rows, LANES), F32)
    for _, width, _, _, vt_ref, _, _, ltref in tiles:
        p = jnp.exp(ltref[...] - _wide(m, width))
        for c in range(width // LANES):
            l_part = l_part + p[:, c * LANES:(c + 1) * LANES]
        acc = acc + lax.dot_general(p.astype(BF16), vt_ref[...].astype(BF16), _NT, preferred_element_type=F32)
    out = acc / jnp.sum(l_part, axis=1, keepdims=True)

    out_lane_head = lax.shift_right_logical(lax.broadcasted_iota(I32, (tq, ATT_WIDTH), 1), 6)
    y = jnp.zeros((tq, ATT_WIDTH), F32)
    for h in range(N_HEADS):
        y = y + jnp.where(out_lane_head == h, out[h * tq:(h + 1) * tq, :], 0.0)
    o_ref[...] = y.astype(BF16)


def _half_mask_wide(x, head):
    lane = lax.broadcasted_iota(I32, x.shape, 1)
    keep = lax.shift_right_logical(lane, 6) == head
    return jnp.where(keep, x, jnp.zeros_like(x))


def _attend_sample(q, iq, iw_rows, knt, vnt, iknt, cache_kt, cache_vt, cache_it, topk):
    n_seq, _, past = cache_kt.shape
    tq = q.shape[0] // n_seq
    tkc = SAMPLE_KEY_TILE
    rows = N_HEADS * tq
    assert rows == LANES and past % tkc == 0
    new = lambda width: pl.BlockSpec((tq, width), lambda b: (b, 0))
    per_seq = lambda a: pl.BlockSpec((None,) + a.shape[1:], lambda b: (b, 0, 0))
    return pl.pallas_call(
        functools.partial(_attend_sample_kernel, topk=topk, past=past),
        out_shape=jax.ShapeDtypeStruct((n_seq * tq, ATT_WIDTH), BF16),
        grid=(n_seq,),
        in_specs=[new(ATT_WIDTH), new(ATT_WIDTH), per_seq(iw_rows), per_seq(knt), per_seq(vnt), per_seq(iknt),
                  per_seq(cache_kt), per_seq(cache_vt), per_seq(cache_it)],
        out_specs=new(ATT_WIDTH),
        scratch_shapes=[pltpu.VMEM((past // tkc, tq, tkc), I32), pltpu.VMEM((past // tkc, tq, tkc), BF16),
                        pltpu.VMEM((tq, LANES), I32), pltpu.VMEM((tq, LANES), BF16),
                        pltpu.VMEM((past // tkc, rows, tkc), F32), pltpu.VMEM((rows, LANES), F32)],
        compiler_params=_cparams("arbitrary"),
        name="attend_sample",
    )(q, iq, iw_rows, knt, vnt, iknt, cache_kt, cache_vt, cache_it)


TOK_ROWS = D_MODEL // LANES


def _store_token_tiles(ref, x):
    t = x.shape[0]
    for c in range(TOK_ROWS):
        ref[pl.ds(c, t, stride=TOK_ROWS), :] = x[:, c * LANES:(c + 1) * LANES]


def _load_token_tiles(ref):
    t = ref.shape[0] // TOK_ROWS
    return jnp.concatenate([ref[pl.ds(c, t, stride=TOK_ROWS), :] for c in range(TOK_ROWS)], axis=1)


def _layer_norm(x, g, b):
    mu = jnp.mean(x, axis=-1, keepdims=True)
    xc = x - mu
    var = jnp.mean(xc * xc, axis=-1, keepdims=True)
    return xc * lax.rsqrt(var + LN_EPS) * g + b


def _post_kernel(x_ref, ya_ref, yb_ref, ga_ref, scf_ref, shf_ref, wo_ref, bo_ref, g1_ref, b1_ref,
                 wrh_ref, wrl_ref, br_ref,
                 x1_ref, h2_ref, tope_ref, gate_ref, rank_ref, cnt_ref):
    t = x_ref.shape[0]
    y = (jnp.dot(ya_ref[...], wo_ref[:A_WIDTH, :], preferred_element_type=F32)
         + jnp.dot(yb_ref[...], wo_ref[A_WIDTH:, :], preferred_element_type=F32) + bo_ref[...])
    x1 = _layer_norm(DEEPNORM_ALPHA * x_ref[...] + (1.0 + ga_ref[...]) * y, g1_ref[...], b1_ref[...])
    x1_ref[...] = x1
    h2 = x1 * (1.0 + scf_ref[...]) + shf_ref[...]
    _store_token_tiles(h2_ref, h2)

    hh = h2.astype(BF16)
    hl = (h2 - hh.astype(F32)).astype(BF16)
    logits = (lax.dot_general(wrh_ref[...], hh, _NT, preferred_element_type=F32)
              + lax.dot_general(wrh_ref[...], hl, _NT, preferred_element_type=F32)
              + lax.dot_general(wrl_ref[...], hh, _NT, preferred_element_type=F32) + br_ref[...])
    erow = lax.broadcasted_iota(I32, (N_EXPERTS, t), 0)
    vals, idxs = [], []
    for _ in range(TOP_K):
        v = jnp.max(logits, axis=0, keepdims=True)
        i = jnp.min(jnp.where(logits == v, erow, N_EXPERTS), axis=0, keepdims=True)
        vals.append(v)
        idxs.append(i)
        logits = jnp.where(erow == i, -jnp.inf, logits)
    ex = [jnp.exp(v - vals[0]) for v in vals]
    den = ex[0] + ex[1] + ex[2] + ex[3]
    gate_ref[...] = jnp.concatenate([e / den for e in ex], axis=0)
    tope_ref[...] = jnp.concatenate(idxs, axis=0)

    @pl.when(pl.program_id(0) == 0)
    def _():
        cnt_ref[...] = jnp.zeros_like(cnt_ref)

    hit = jnp.zeros((N_EXPERTS, t), F32)
    for i in idxs:
        hit = hit + jnp.where(erow == i, 1.0, 0.0)
    hitb = hit.astype(BF16)
    r = lax.broadcasted_iota(I32, (t, t), 0)
    c = lax.broadcasted_iota(I32, (t, t), 1)
    earlier = jnp.where(r < c, 1.0, 0.0).astype(BF16)
    before = jnp.dot(hitb, earlier, preferred_element_type=F32) + cnt_ref[...]
    total = jnp.dot(hitb, jnp.ones((t, t), BF16), preferred_element_type=F32)
    rank_ref[...] = jnp.concatenate(
        [jnp.sum(jnp.where(erow == i, before, 0.0), axis=0, keepdims=True) for i in idxs], axis=0).astype(I32)
    cnt_ref[...] += total


def _post(x, ya, yb, ga, scf, shf, w_o, b_o, g1, b1, wrh, wrl, br):
    n, d = x.shape
    t = POST_TILE
    nt = n // t
    if ga.shape[0] == n:
        mod_spec = pl.BlockSpec((t, d), lambda i: (i, 0))
    else:
        tiles_per_seq = nt // ga.shape[0]
        ga, scf, shf = (a.reshape(a.shape[0], 1, d) for a in (ga, scf, shf))
        mod_spec = pl.BlockSpec((None, 1, d), lambda i: (i // tiles_per_seq, 0, 0))
    row = lambda w: pl.BlockSpec((t, w), lambda i: (i, 0))
    col = lambda r: pl.BlockSpec((r, t), lambda i: (0, i))
    const = lambda a: pl.BlockSpec(a.shape, lambda i: (0, 0))
    return pl.pallas_call(
        _post_kernel,
        out_shape=(jax.ShapeDtypeStruct((n, d), F32), jax.ShapeDtypeStruct((n * TOK_ROWS, LANES), F32),
                   jax.ShapeDtypeStruct((TOP_K, n), I32), jax.ShapeDtypeStruct((TOP_K, n), F32),
                   jax.ShapeDtypeStruct((TOP_K, n), I32), jax.ShapeDtypeStruct((N_EXPERTS, t), F32)),
        grid=(nt,),
        in_specs=[row(d), row(A_WIDTH), row(ATT_WIDTH), mod_spec, mod_spec, mod_spec,
                  const(w_o), const(b_o), const(g1), const(b1), const(wrh), const(wrl), const(br)],
        out_specs=(row(d), pl.BlockSpec((t * TOK_ROWS, LANES), lambda i: (i, 0)), col(TOP_K), col(TOP_K), col(TOP_K),
                   pl.BlockSpec((N_EXPERTS, t), lambda i: (0, 0))),
        compiler_params=_cparams("arbitrary"),
        name="post",
    )(x, ya, yb, ga, scf, shf, w_o, b_o, g1, b1, wrh, wrl, br)


def _token_rows(ref, r):
    return ref.at[pl.ds(pl.multiple_of(r * TOK_ROWS, TOK_ROWS), TOK_ROWS)]


def _token_copies_wait(hbm_ref, vmem_ref, sem, n_tokens):
    rows = n_tokens * TOK_ROWS
    pltpu.make_async_copy(hbm_ref.at[pl.ds(0, rows)], vmem_ref.at[pl.ds(0, rows)], sem).wait()


def _dispatch_kernel(zrow_ref, dest_ref, h_ref, xp_ref, zero_ref, sem):
    t = h_ref.shape[0] // TOK_ROWS

    block_rows = MOE_BLOCK * TOK_ROWS

    def zero_block(slot0):
        z0 = pl.multiple_of(slot0 * TOK_ROWS, block_rows)
        cp = pltpu.make_async_copy(zero_ref, xp_ref.at[pl.ds(z0, block_rows)], sem)
        cp.start()
        cp.wait()

    @pl.when(pl.program_id(0) == 0)
    def _():
        zero_ref[...] = jnp.zeros_like(zero_ref)
        for e in range(N_EXPERTS):
            @pl.when(zrow_ref[e] >= 0)
            def _():
                zero_block(zrow_ref[e])

        def unused(b, _):
            zero_block(b * MOE_BLOCK)
            return 0

        lax.fori_loop(zrow_ref[N_EXPERTS], xp_ref.shape[0] // block_rows, unused, 0)

    def body(i, _):
        for k in range(TOP_K):
            pltpu.make_async_copy(_token_rows(h_ref, i), _token_rows(xp_ref, dest_ref[k, i]), sem).start(priority=k % 2)
        return 0

    lax.fori_loop(0, t, body, 0, unroll=DMA_ISSUE_UNROLL)
    for k in range(TOP_K):
        _token_copies_wait(xp_ref, h_ref, sem, t)


def _dispatch(zrow, dest, h2, n_slots):
    t = ROW_TILE
    n = h2.shape[0] // TOK_ROWS
    grid_spec = pltpu.PrefetchScalarGridSpec(
        num_scalar_prefetch=1,
        grid=(n // t,),
        in_specs=[pl.BlockSpec((TOP_K, t), lambda i, z: (0, i), memory_space=pltpu.SMEM),
                  pl.BlockSpec((t * TOK_ROWS, LANES), lambda i, z: (i, 0))],
        out_specs=pl.BlockSpec(memory_space=pl.ANY),
        scratch_shapes=[pltpu.VMEM((MOE_BLOCK * TOK_ROWS, LANES), F32), pltpu.SemaphoreType.DMA],
    )
    return pl.pallas_call(
        _dispatch_kernel,
        out_shape=jax.ShapeDtypeStruct((n_slots * TOK_ROWS, LANES), F32),
        grid_spec=grid_spec,
        compiler_params=_cparams("arbitrary"),
        name="dispatch",
    )(zrow, dest, h2)


def _experts_kernel(be_ref, bi_ref, nu_ref, x_ref, wg_ref, bg_ref, wu_ref, bu_ref, wd_ref, bd_ref, y_ref,
                    wgb_ref, wub_ref, wdb_ref):
    i = pl.program_id(0)

    @pl.when((i == 0) | (be_ref[i] != be_ref[jnp.maximum(i - 1, 0)]))
    def _():
        wgb_ref[...] = wg_ref[...].astype(BF16)
        wub_ref[...] = wu_ref[...].astype(BF16)
        wdb_ref[...] = wd_ref[...].astype(BF16)

    @pl.when(i < nu_ref[0])
    def _():
        x = _load_token_tiles(x_ref).astype(BF16)
        g = jnp.minimum(jnp.dot(x, wgb_ref[...], preferred_element_type=F32) + bg_ref[...], SWIGLU_LIMIT)
        u = jnp.clip(jnp.dot(x, wub_ref[...], preferred_element_type=F32) + bu_ref[...], -SWIGLU_LIMIT, SWIGLU_LIMIT)
        a = g * jax.nn.sigmoid(SWIGLU_ALPHA * g)
        mid = ((u + 1.0) * a).astype(BF16)
        _store_token_tiles(y_ref, jnp.dot(mid, wdb_ref[...], preferred_element_type=F32) + bd_ref[...])

    @pl.when(pl.program_id(0) >= nu_ref[0])
    def _():
        y_ref[...] = jnp.zeros_like(y_ref)


def _experts(block_e, block_i, n_used, xp, wg, bg, wu, bu, wd, bd):
    d, f = wg.shape[1], wg.shape[2]
    nb = xp.shape[0] // (MOE_BLOCK * TOK_ROWS)
    wspec = lambda a, b: pl.BlockSpec((None, a, b), lambda i, be, bi, nu: (be[i], 0, 0))
    slots = pl.BlockSpec((MOE_BLOCK * TOK_ROWS, LANES), lambda i, be, bi, nu: (bi[i], 0))
    grid_spec = pltpu.PrefetchScalarGridSpec(
        num_scalar_prefetch=3,
        grid=(nb,),
        in_specs=[slots, wspec(d, f), wspec(1, f), wspec(d, f), wspec(1, f), wspec(f, d), wspec(1, d)],
        out_specs=pl.BlockSpec((MOE_BLOCK * TOK_ROWS, LANES), lambda i, be, bi, nu: (i, 0)),
        scratch_shapes=[pltpu.VMEM((d, f), BF16), pltpu.VMEM((d, f), BF16), pltpu.VMEM((f, d), BF16)],
    )
    return pl.pallas_call(
        _experts_kernel,
        out_shape=jax.ShapeDtypeStruct(xp.shape, F32),
        grid_spec=grid_spec,
        compiler_params=_cparams("arbitrary"),
        name="experts",
    )(block_e, block_i, n_used, xp, wg, bg, wu, bu, wd, bd)


def _combine_kernel(dest_ref, dest_next_ref, gate_ref, x1_ref, gf_ref, g2_ref, b2_ref, yp_ref, o_ref, buf_ref, sems):
    t = x1_ref.shape[0]
    step = pl.program_id(0)
    slot = step % 2

    def start_gather(dst_ref, into):
        def body(i, _):
            for k in range(TOP_K):
                pltpu.make_async_copy(_token_rows(yp_ref, dst_ref[k, i]), _token_rows(buf_ref.at[into, k], i),
                                      sems.at[into]).start(priority=k % 2)
            return 0

        lax.fori_loop(0, t, body, 0, unroll=DMA_ISSUE_UNROLL)

    @pl.when(step == 0)
    def _():
        start_gather(dest_ref, 0)

    @pl.when(step + 1 < pl.num_programs(0))
    def _():
        start_gather(dest_next_ref, 1 - slot)

    for k in range(TOP_K):
        _token_copies_wait(yp_ref, buf_ref.at[slot, k], sems.at[slot], t)

    gates = jnp.concatenate([gate_ref[...], jnp.zeros((LANES - TOP_K, t), F32)], axis=0).T
    f = gates[:, 0:1] * _load_token_tiles(buf_ref.at[slot, 0])
    for k in range(1, TOP_K):
        f = f + gates[:, k:k + 1] * _load_token_tiles(buf_ref.at[slot, k])
    o_ref[...] = _layer_norm(DEEPNORM_ALPHA * x1_ref[...] + (1.0 + gf_ref[...]) * f, g2_ref[...], b2_ref[...])


def _combine(dest, gates, x1, gf, g2, b2, yp):
    n, d = x1.shape
    t = ROW_TILE
    nt = n // t
    if gf.shape[0] == n:
        mod_spec = pl.BlockSpec((t, d), lambda i: (i, 0))
    else:
        tiles_per_seq = nt // gf.shape[0]
        gf = gf.reshape(gf.shape[0], 1, d)
        mod_spec = pl.BlockSpec((None, 1, d), lambda i: (i // tiles_per_seq, 0, 0))
    return pl.pallas_call(
        _combine_kernel,
        out_shape=jax.ShapeDtypeStruct((n, d), F32),
        grid=(nt,),
        in_specs=[pl.BlockSpec((TOP_K, t), lambda i: (0, i), memory_space=pltpu.SMEM),
                  pl.BlockSpec((TOP_K, t), lambda i: (0, jnp.minimum(i + 1, nt - 1)), memory_space=pltpu.SMEM),
                  pl.BlockSpec((TOP_K, t), lambda i: (0, i)),
                  pl.BlockSpec((t, d), lambda i: (i, 0)), mod_spec,
                  pl.BlockSpec((1, d), lambda i: (0, 0)), pl.BlockSpec((1, d), lambda i: (0, 0)),
                  pl.BlockSpec(memory_space=pl.ANY)],
        out_specs=pl.BlockSpec((t, d), lambda i: (i, 0)),
        scratch_shapes=[pltpu.VMEM((2, TOP_K, t * TOK_ROWS, LANES), F32), pltpu.SemaphoreType.DMA((2,))],
        compiler_params=_cparams("arbitrary"),
        name="combine",
    )(dest, dest, gates, x1, gf, g2, b2, yp)


def _moe(h2, tope, rank, counts, gates, x1, gf, g2, b2, experts_w):
    n = tope.shape[1]
    nb = (n * TOP_K + N_EXPERTS * (MOE_BLOCK - 1) + MOE_BLOCK - 1) // MOE_BLOCK
    pcounts = (counts + MOE_BLOCK - 1) // MOE_BLOCK * MOE_BLOCK
    pend = jnp.cumsum(pcounts)
    pstart = pend - pcounts
    eids = jnp.arange(N_EXPERTS, dtype=I32).reshape(N_EXPERTS, 1, 1)
    dest = rank + jnp.sum(jnp.where(tope[None] == eids, pstart.reshape(N_EXPERTS, 1, 1), 0), axis=0).astype(I32)
    n_used = (pend[-1] // MOE_BLOCK).astype(I32)
    zrow = jnp.concatenate([jnp.where(counts > 0, pend - MOE_BLOCK, -1), n_used.reshape(1)]).astype(I32)
    blk = jnp.arange(nb, dtype=I32)
    block_i = jnp.minimum(blk, n_used - 1)
    block_e = jnp.minimum(jnp.sum(pend[None, :] <= (block_i * MOE_BLOCK)[:, None], axis=1), N_EXPERTS - 1).astype(I32)
    xp = _dispatch(zrow, dest, h2, nb * MOE_BLOCK)
    yp = _experts(block_e, block_i, n_used.reshape(1), xp, *experts_w)
    return _combine(dest, gates, x1, gf, g2, b2, yp)


def kernel(x_prompt, x_sample, c_prompt, c_sample, cache_k, cache_v, cache_kidx, w_in, a_ln_g, a_ln_b, a_ws, a_bs,
           w_o, b_o, w_c, b_c, ln1_g, ln1_b, ln2_g, ln2_b, w_router, b_router, w_gate, b_gate, w_up, b_up,
           w_down, b_down):
    bp, s, d = x_prompt.shape
    bs, ts, _ = x_sample.shape
    past = cache_k.shape[2]
    np_, ns = bp * s, bs * ts

    wi = w_in[0]
    w_in_p = jnp.zeros((d, IN_COLS_PAD), F32)
    w_in_p = w_in_p.at[:, :OFF_IK].set(wi[:, :OFF_IK])
    w_in_p = w_in_p.at[:, OFF_IK:OFF_IK + IDX_DIM].set(wi[:, OFF_IK:OFF_IK + IDX_DIM])
    w_in_p = w_in_p.at[:, OFF_IK + IDX_DIM:OFF_IK + 2 * IDX_DIM].set(wi[:, OFF_IK:OFF_IK + IDX_DIM])
    w_in_p = w_in_p.at[:, OFF_IW:OFF_IW + N_IDX_HEADS].set(wi[:, OFF_IK + IDX_DIM:OFF_IK + IDX_DIM + N_IDX_HEADS])
    w_in_p = w_in_p.astype(BF16)
    lng, lnb = a_ln_g[0].reshape(1, A_WIDTH), a_ln_b[0].reshape(1, A_WIDTH)
    wtril = jnp.tril(a_ws[0])
    wm_p = wtril.astype(BF16)
    bm_p = jnp.broadcast_to(a_bs[0][:, :, None], (A_GROUPS, A_CHUNK, A_GROUP_DIM)).astype(F32)
    rep = A_CHUNK // ts
    wm_s = jnp.einsum("ab,gij->gaibj", jnp.eye(rep, dtype=F32), wtril[:, :ts, :ts]).reshape(
        A_GROUPS, A_CHUNK, A_CHUNK).astype(BF16)
    bm_s = jnp.broadcast_to(jnp.tile(a_bs[0][:, :ts], (1, rep))[:, :, None], (A_GROUPS, A_CHUNK, A_GROUP_DIM)).astype(F32)
    w_o_b = w_o[0].astype(BF16)
    b_o_r = b_o[0].reshape(1, d)
    wr_t = w_router[0].T
    wrh = wr_t.astype(BF16)
    wrl = (wr_t - wrh.astype(F32)).astype(BF16)
    br = jnp.broadcast_to(b_router[0][:, None], (N_EXPERTS, POST_TILE)).astype(F32)
    experts_w = (w_gate[0], b_gate[0][:, None, :], w_up[0], b_up[0][:, None, :], w_down[0], b_down[0][:, None, :])
    g1, b1 = ln1_g[0].reshape(1, d), ln1_b[0].reshape(1, d)
    g2, b2 = ln2_g[0].reshape(1, d), ln2_b[0].reshape(1, d)

    mods = _cond_mods(jnp.concatenate([c_prompt, c_sample], axis=0), w_c[0], b_c[0]).reshape(bp + bs, 6, d)
    mods_p = [mods[:bp, i] for i in range(6)]
    mods_s = [jnp.repeat(mods[bp:, i], ts, axis=0) for i in range(6)]

    xp2 = x_prompt.reshape(np_, d)
    ya, q, kft, kb, _, vft, vt, iq, ikft, ikb, iwt, _ = _project(
        xp2, mods_p[1], mods_p[0], w_in_p, lng, lnb, wm_p, bm_p, bp)
    yb = _attend_prompt(q, iq, iwt, kb, vt, ikb, bp, s, min(TOPK_MAX, s // 4))
    x1, h2, tope, gates, rank, cnt = _post(xp2, ya, yb, mods_p[2], mods_p[4], mods_p[3], w_o_b, b_o_r, g1, b1, wrh, wrl, br)
    y_p = _moe(h2, tope, rank, cnt[:, 0].astype(I32), gates, x1, mods_p[5], g2, b2, experts_w)
    heads_last = lambda a: jnp.transpose(a.reshape(1, bp, N_HEADS, HEAD_DIM, s), (0, 1, 4, 2, 3))
    out_p = (y_p.reshape(bp, s, d), heads_last(kft), heads_last(vft),
             jnp.transpose(ikft.reshape(1, bp, IDX_DIM, s), (0, 1, 3, 2)))

    xs2 = x_sample.reshape(ns, d)
    ya, q, kft, _, vf, vft, _, iq, ikft, _, iwt, va = _project(
        xs2, mods_s[1], mods_s[0], w_in_p, lng, lnb, wm_s, bm_s, 1)
    kf, ikf = kft[0].T, ikft[0].T
    per_seq_t = lambda a: jnp.pad(jnp.transpose(a[0].reshape(a.shape[1], bs, ts), (1, 0, 2)),
                                  ((0, 0), (0, 0), (0, LANES - ts)))
    iw_rows = jnp.broadcast_to(
        jnp.transpose(iwt.reshape(N_IDX_HEADS, bs, ts), (1, 0, 2)).reshape(bs, N_IDX_HEADS * ts, 1),
        (bs, N_IDX_HEADS * ts, LANES))
    cache_kt = jnp.transpose(cache_k[0], (0, 2, 3, 1)).reshape(bs, ATT_WIDTH, past)
    cache_vt = jnp.transpose(cache_v[0], (0, 2, 3, 1)).reshape(bs, ATT_WIDTH, past)
    cache_it = jnp.transpose(cache_kidx[0], (0, 2, 1))
    yb = _attend_sample(q, iq, iw_rows, per_seq_t(kft), per_seq_t(vft), per_seq_t(ikft), cache_kt, cache_vt, cache_it,
                        min(TOPK_MAX, (past + ts) // 4))
    x1, h2, tope, gates, rank, cnt = _post(xs2, ya, yb, mods_s[2], mods_s[4], mods_s[3], w_o_b, b_o_r, g1, b1, wrh, wrl, br)
    y_s = _moe(h2, tope, rank, cnt[:, 0].astype(I32), gates, x1, mods_s[5], g2, b2, experts_w)

    return (out_p[0], y_s.reshape(bs, ts, d), out_p[1], out_p[2], out_p[3],
            kf.reshape(1, bs, ts, N_HEADS, HEAD_DIM), vf.reshape(1, bs, ts, N_HEADS, HEAD_DIM),
            ikf.reshape(1, bs, ts, IDX_DIM), va.reshape(1, bs, ts, A_WIDTH))
```

```python
import functools

import jax
import jax.numpy as jnp
from jax import lax
from jax.experimental import pallas as pl
from jax.experimental.pallas import tpu as pltpu

F32 = jnp.float32
BF16 = jnp.bfloat16
I32 = jnp.int32

D_MODEL = 1024
CHUNK_SHIFT = 6
A_GROUPS = 4
A_GROUP_DIM = 128
A_WIDTH = A_GROUPS * A_GROUP_DIM
A_CHUNK = 128
N_HEADS = 8
HEAD_DIM = 64
HEAD_SHIFT = 6
ATT_WIDTH = N_HEADS * HEAD_DIM
N_IDX_HEADS = 8
IDX_DIM = 64
TOPK_MAX = 256
ATTN_SCALE = HEAD_DIM ** -0.5
VT_HEAD_ROWS = HEAD_DIM + 16
VT_ROWS = N_HEADS * VT_HEAD_ROWS
IDX_W_SCALE = (N_IDX_HEADS ** -0.5) * (IDX_DIM ** -0.5)
N_EXPERTS = 32
TOP_K = 4
SWIGLU_LIMIT = 7.0
SWIGLU_ALPHA = 1.702
DEEPNORM_ALPHA = 2.0 ** 0.25
LN_EPS = 1e-5

LANES = 128
SUBLANES = 8
VMEM_LIMIT_BYTES = 56 * 1024 * 1024

ROW_TILE = 256
PROJECT_TILE = 512
POST_TILE = 512
MOE_BLOCK = 512
ATT_TILE = 256
SCORE_GROUP = 4
ATT_GROUP = 4
DMA_ISSUE_UNROLL = 8
STATIC_TILE_UNROLL = 4

OFF_AU, OFF_AV, OFF_Q, OFF_K, OFF_V, OFF_IQ = 0, 512, 1024, 1536, 2048, 2560
OFF_IK = 3072
OFF_IW = 3200
IN_COLS_PAD = 3328

MASKED_DIST = 3.0e32
INT32_MIN = -(2 ** 31)
KEY_NEG_INF = INT32_MIN + 0x7FFFFF

_NT = (((1,), (1,)), ((), ()))


def _cparams(*sem):
    return pltpu.CompilerParams(dimension_semantics=sem, vmem_limit_bytes=VMEM_LIMIT_BYTES)


def _mods_kernel(c_ref, w_ref, b_ref, o_ref):
    c = c_ref[...]
    s = c * jax.nn.sigmoid(c)
    o_ref[...] = jnp.dot(s.astype(BF16), w_ref[...].astype(BF16), preferred_element_type=F32) + b_ref[...]


def _cond_mods(c, w_c, b_c):
    nb, d = c.shape
    n_out = w_c.shape[1]
    return pl.pallas_call(
        _mods_kernel,
        out_shape=jax.ShapeDtypeStruct((nb, n_out), F32),
        grid=(n_out // d,),
        in_specs=[pl.BlockSpec((nb, d), lambda j: (0, 0)),
                  pl.BlockSpec((d, d), lambda j: (0, j)),
                  pl.BlockSpec((1, d), lambda j: (0, j))],
        out_specs=pl.BlockSpec((nb, d), lambda j: (0, j)),
        compiler_params=_cparams("arbitrary"),
        name="mods",
    )(c, w_c, b_c.reshape(1, n_out))


def _gelu(x):
    return 0.5 * x * (1.0 + lax.erf(x * 0.7071067811865476))


def _project_kernel(x_ref, sc_ref, sh_ref, w_ref, lng_ref, lnb_ref, wm_ref, bm_ref,
                    ya_ref, q_ref, kft_ref, kb_ref, vf_ref, vft_ref, vt_ref, iq_ref, ikft_ref, ikb_ref, iwt_ref,
                    va_ref):
    t = x_ref.shape[0]
    h = (x_ref[...] * (1.0 + sc_ref[...]) + sh_ref[...]).astype(BF16)

    def proj(c0, n):
        return jnp.dot(h, w_ref[:, c0:c0 + n], preferred_element_type=F32)

    u = _gelu(proj(OFF_AU, A_WIDTH))
    gv = _gelu(proj(OFF_AV, A_WIDTH))
    for g in range(A_GROUPS):
        lo, hi = g * A_GROUP_DIM, (g + 1) * A_GROUP_DIM
        xg = gv[:, lo:hi]
        mu = jnp.mean(xg, axis=-1, keepdims=True)
        xc = xg - mu
        var = jnp.mean(xc * xc, axis=-1, keepdims=True)
        vg = xc * lax.rsqrt(var + LN_EPS) * lng_ref[:, lo:hi] + lnb_ref[:, lo:hi]
        va_ref[:, lo:hi] = vg
        vgb = vg.astype(BF16)
        for c in range(t // A_CHUNK):
            r0, r1 = c * A_CHUNK, (c + 1) * A_CHUNK
            mixed = jnp.dot(wm_ref[g], vgb[r0:r1, :], preferred_element_type=F32) + bm_ref[g]
            ya_ref[r0:r1, lo:hi] = (u[r0:r1, lo:hi] * mixed).astype(BF16)

    q_ref[...] = proj(OFF_Q, ATT_WIDTH).astype(BF16)
    k = proj(OFF_K, ATT_WIDTH)
    kft_ref[...] = k.T
    kb_ref[...] = k.astype(BF16)
    v = proj(OFF_V, ATT_WIDTH)
    vf_ref[...] = v
    v_t32 = v.T
    vft_ref[...] = v_t32
    v_t = v_t32.astype(BF16)
    ones = jnp.ones((VT_HEAD_ROWS - HEAD_DIM, ATT_TILE), BF16)
    for c in range(t // ATT_TILE):
        cols = slice(c * ATT_TILE, (c + 1) * ATT_TILE)
        vt_ref[c] = jnp.concatenate(
            [blk for h in range(N_HEADS) for blk in (v_t[h * HEAD_DIM:(h + 1) * HEAD_DIM, cols], ones)], axis=0)
    iq_ref[...] = proj(OFF_IQ, N_IDX_HEADS * IDX_DIM).astype(BF16)
    ik2 = proj(OFF_IK, LANES)
    ikft_ref[...] = ik2.T[:IDX_DIM, :]
    ikb_ref[...] = ik2.astype(BF16)
    iw = proj(OFF_IW, LANES) * IDX_W_SCALE
    iwt_ref[...] = iw.T[:N_IDX_HEADS, :]


def _project(x, sc, sh, w_in_p, lng, lnb, wm, bm, n_out_seq):
    n, d = x.shape
    t = PROJECT_TILE
    nt = n // t
    tiles_per_out = nt // n_out_seq
    trans = lambda rows: pl.BlockSpec((None, rows, t), lambda i: (i // tiles_per_out, 0, i % tiles_per_out))
    if sc.shape[0] == n:
        mod_spec = pl.BlockSpec((t, d), lambda i: (i, 0))
    else:
        tiles_per_seq = nt // sc.shape[0]
        sc = sc.reshape(sc.shape[0], 1, d)
        sh = sh.reshape(sh.shape[0], 1, d)
        mod_spec = pl.BlockSpec((None, 1, d), lambda i: (i // tiles_per_seq, 0, 0))
    row = lambda w: pl.BlockSpec((t, w), lambda i: (i, 0))
    const2 = lambda a: pl.BlockSpec(a.shape, lambda i: (0, 0))
    const3 = lambda a: pl.BlockSpec(a.shape, lambda i: (0, 0, 0))
    out_shape = (
        jax.ShapeDtypeStruct((n, A_WIDTH), BF16),
        jax.ShapeDtypeStruct((n, ATT_WIDTH), BF16),
        jax.ShapeDtypeStruct((n_out_seq, ATT_WIDTH, n // n_out_seq), F32),
        jax.ShapeDtypeStruct((n, ATT_WIDTH), BF16),
        jax.ShapeDtypeStruct((n, ATT_WIDTH), F32),
        jax.ShapeDtypeStruct((n_out_seq, ATT_WIDTH, n // n_out_seq), F32),
        jax.ShapeDtypeStruct((n // ATT_TILE, VT_ROWS, ATT_TILE), BF16),
        jax.ShapeDtypeStruct((n, ATT_WIDTH), BF16),
        jax.ShapeDtypeStruct((n_out_seq, IDX_DIM, n // n_out_seq), F32),
        jax.ShapeDtypeStruct((n, LANES), BF16),
        jax.ShapeDtypeStruct((N_IDX_HEADS, n), F32),
        jax.ShapeDtypeStruct((n, A_WIDTH), F32),
    )
    out_specs = (row(A_WIDTH), row(ATT_WIDTH), trans(ATT_WIDTH), row(ATT_WIDTH), row(ATT_WIDTH), trans(ATT_WIDTH),
                 pl.BlockSpec((t // ATT_TILE, VT_ROWS, ATT_TILE), lambda i: (i, 0, 0)),
                 row(ATT_WIDTH), trans(IDX_DIM), row(LANES),
                 pl.BlockSpec((N_IDX_HEADS, t), lambda i: (0, i)),
                 row(A_WIDTH))
    return pl.pallas_call(
        _project_kernel,
        out_shape=out_shape,
        grid=(nt,),
        in_specs=[row(d), mod_spec, mod_spec, const2(w_in_p), const2(lng), const2(lnb), const3(wm), const3(bm)],
        out_specs=out_specs,
        compiler_params=_cparams("arbitrary"),
        name="project",
    )(x, sc, sh, w_in_p, lng, lnb, wm, bm)


DIGIT_BITS = 8
N_DIGITS = 32 // DIGIT_BITS
DIGIT_ABOVE = 512.0
DIGIT_BELOW = -1.0
PACKED_ROWS = 16


def _tile_loop(nkt, body, init):
    if isinstance(nkt, int):
        return lax.fori_loop(0, nkt, body, init, unroll=STATIC_TILE_UNROLL)
    g = STATIC_TILE_UNROLL
    groups = nkt // g

    def group_body(i, c):
        for k in range(g):
            c = body(g * i + k, c)
        return c

    return lax.fori_loop(g * groups, nkt, body, lax.fori_loop(0, groups, group_body, init))


def _mono_key(x):
    b = lax.bitcast_convert_type(x, I32)
    return jnp.where(b >= 0, b, b ^ jnp.int32(0x7FFFFFFF))


def _digit_plane(key, phase):
    shift = 32 - DIGIT_BITS * (phase + 1)
    d = lax.shift_right_arithmetic(key, jnp.int32(shift)) if shift else key
    d = d + (1 << (DIGIT_BITS - 1)) if phase == 0 else d & ((1 << DIGIT_BITS) - 1)
    return d.astype(F32).astype(BF16)


def _count_plane(plane_ref, nkt, cand, strict):
    _, tk, w = plane_ref.shape
    cb = cand.astype(BF16)
    one, zero = jnp.ones((), BF16), jnp.zeros((), BF16)

    def body(kt, cnt):
        e = plane_ref[kt]
        accs = [jnp.zeros((PACKED_ROWS, w), BF16) for _ in range(4)]
        for r in range(tk // PACKED_ROWS):
            blk = e[r * PACKED_ROWS:(r + 1) * PACKED_ROWS, :]
            accs[r % 4] = accs[r % 4] + jnp.where((blk > cb) if strict else (blk >= cb), one, zero)
        return cnt + ((accs[0] + accs[1]) + (accs[2] + accs[3])).astype(F32)

    cnt = _tile_loop(nkt, body, jnp.zeros((PACKED_ROWS, w), F32))
    return jnp.sum(cnt, axis=0, keepdims=True)


def _search_digit(plane_ref, nkt, topk):
    w = plane_ref.shape[2]

    def bit_body(i, d):
        cand = d + lax.shift_left(jnp.int32(1), jnp.int32(DIGIT_BITS - 1) - i).astype(F32)
        cnt = _count_plane(plane_ref, nkt, cand, strict=False)
        return jnp.where(cnt >= float(topk), cand, d)

    return lax.fori_loop(0, DIGIT_BITS, bit_body, jnp.zeros((1, w), F32))


def _topk_select(keys_ref, plane_ref, nkt, topk):
    _, tk, w = plane_ref.shape
    d = _search_digit(plane_ref, nkt, topk)
    for phase in range(1, N_DIGITS):
        db = d.astype(BF16)

        def refine(kt, _, phase=phase, db=db):
            e = plane_ref[kt]
            decided = jnp.where(e > db, jnp.asarray(DIGIT_ABOVE, BF16), jnp.asarray(DIGIT_BELOW, BF16))
            plane_ref[kt] = jnp.where(e == db, _digit_plane(keys_ref[kt], phase), decided)
            return 0

        _tile_loop(nkt, refine, 0)
        d = _search_digit(plane_ref, nkt, topk)

    cnt_ge = _count_plane(plane_ref, nkt, d, strict=False)

    @pl.when(jnp.max(cnt_ge) > float(topk))
    def _():
        need = float(topk) - _count_plane(plane_ref, nkt, d, strict=True)
        r = lax.broadcasted_iota(I32, (tk, tk), 0)
        c = lax.broadcasted_iota(I32, (tk, tk), 1)
        before = jnp.where(c < r, 1.0, 0.0).astype(BF16)

        def body(kt, seen):
            e = plane_ref[kt].astype(F32)
            eq = e == d
            eqf = jnp.where(eq, 1.0, 0.0)
            prior = jnp.dot(before, eqf.astype(BF16), preferred_element_type=F32) + seen
            plane_ref[kt] = jnp.where(eq & (prior >= need), DIGIT_BELOW, e).astype(BF16)
            return seen + jnp.sum(eqf.reshape(tk // SUBLANES, SUBLANES, w), axis=0).sum(axis=0, keepdims=True)

        lax.fori_loop(0, nkt, body, jnp.zeros((1, w), F32))

    return d


def _selected(keys_ref, plane_ref, kt, d):
    return (plane_ref[kt].astype(F32) >= d) & (keys_ref[kt] > jnp.int32(KEY_NEG_INF))


def _half_mask(x_pair, head):
    lane = lax.broadcasted_iota(I32, x_pair.shape, 1)
    keep = (lane >= HEAD_DIM) if head % 2 else (lane < HEAD_DIM)
    return jnp.where(keep, x_pair, jnp.zeros_like(x_pair))


def _attend_prompt_kernel(q_ref, iq_ref, iwt_ref, k_ref, vt_ref, ik_ref, o_ref, keys_ref, plane_ref, *head_refs, topk):
    qh_refs, acc_refs, lt_refs = (head_refs[i * N_HEADS:(i + 1) * N_HEADS] for i in range(3))
    tq = q_ref.shape[0]
    tk = keys_ref.shape[1]
    j = pl.program_id(1)
    nkt = j + 1
    q0 = j * tq
    row = lax.broadcasted_iota(I32, (tk, tq), 0)
    lane = lax.broadcasted_iota(I32, (tk, tq), 1)
    qpos = q0 + lane

    iq = iq_ref[...]
    iqm = [_half_mask(iq[:, (h // 2) * LANES:(h // 2 + 1) * LANES], h) for h in range(N_IDX_HEADS)]
    iw = iwt_ref[...]

    def score_tiles(kt, n, diagonal):
        k0 = pl.multiple_of(kt * tk, tk)
        ikt = ik_ref[pl.ds(k0, n * tk), :]
        s = jnp.zeros((n * tk, tq), F32)
        for h in range(N_IDX_HEADS):
            r = lax.dot_general(ikt, iqm[h], _NT, preferred_element_type=F32)
            s = s + jnp.maximum(r, 0.0) * iw[h:h + 1, :]
        if diagonal:
            adm = lax.shift_right_logical(k0 + row, CHUNK_SHIFT) <= lax.shift_right_logical(qpos, CHUNK_SHIFT)
            s = jnp.where(adm, s, -jnp.inf)
        for c in range(n):
            key = _mono_key(s[c * tk:(c + 1) * tk, :])
            keys_ref[kt + c] = key
            plane_ref[kt + c] = _digit_plane(key, 0)
        return 0

    assert tk == tq
    groups = j // SCORE_GROUP
    lax.fori_loop(0, groups, lambda i, _: score_tiles(SCORE_GROUP * i, SCORE_GROUP, False), 0)
    lax.fori_loop(SCORE_GROUP * groups, j, lambda kt, _: score_tiles(kt, 1, False), 0)
    score_tiles(j, 1, True)

    d_last = _topk_select(keys_ref, plane_ref, nkt, topk)

    def masked_distance(kt):
        dist = jnp.abs(qpos - (kt * tk + row)).astype(F32)
        return jnp.where(_selected(keys_ref, plane_ref, kt, d_last), dist, MASKED_DIST)

    qfull = q_ref[...]
    for h in range(N_HEADS):
        pair = h // 2
        qh_refs[h][...] = _half_mask(qfull[:, pair * LANES:(pair + 1) * LANES], h) * jnp.asarray(ATTN_SCALE, BF16)
        acc_refs[h][...] = jnp.zeros_like(acc_refs[h])

    def att_tiles(kt, n, carry):
        m_all, l_all = carry
        k0 = pl.multiple_of(kt * tk, tk)
        dist = jnp.concatenate([masked_distance(kt + c) for c in range(n)], axis=0)
        ms = []
        for h in range(N_HEADS):
            pair = h // 2
            slope = 2.0 ** (-8.0 * (h + 1) / N_HEADS)
            kp = k_ref[pl.ds(k0, n * tk), pair * LANES:(pair + 1) * LANES]
            lt = lax.dot_general(kp, qh_refs[h][...], _NT, preferred_element_type=F32) - slope * dist
            lt_refs[h][0:n * tk, :] = lt.astype(BF16)
            tile_max = jnp.max(lt, axis=0, keepdims=True).astype(BF16).astype(F32)
            ms.append(jnp.maximum(m_all[h:h + 1, :], tile_max))
        ls = []
        for h in range(N_HEADS):
            alpha = jnp.exp(m_all[h:h + 1, :] - ms[h])
            mb = ms[h].astype(BF16)
            acc = alpha * acc_refs[h][...]
            l = alpha * l_all[h:h + 1, :]
            for c in range(n):
                p = jnp.exp(lt_refs[h][c * tk:(c + 1) * tk, :] - mb)
                pv = jnp.dot(vt_ref[kt + c, h * VT_HEAD_ROWS:(h + 1) * VT_HEAD_ROWS, :], p,
                             preferred_element_type=F32)
                acc = acc + pv[:HEAD_DIM, :]
                l = l + pv[HEAD_DIM:HEAD_DIM + 1, :]
            acc_refs[h][...] = acc
            ls.append(l)
        return jnp.concatenate(ms, axis=0), jnp.concatenate(ls, axis=0)

    groups = nkt // ATT_GROUP
    carry = (jnp.full((N_HEADS, tq), -jnp.inf, F32), jnp.zeros((N_HEADS, tq), F32))
    carry = lax.fori_loop(0, groups, lambda i, c: att_tiles(ATT_GROUP * i, ATT_GROUP, c), carry)
    _, l_all = lax.fori_loop(ATT_GROUP * groups, nkt, lambda kt, c: att_tiles(kt, 1, c), carry)
    out_t = jnp.concatenate([acc_refs[h][...] / l_all[h:h + 1, :] for h in range(N_HEADS)], axis=0)
    o_ref[...] = out_t.T.astype(BF16)


def _attend_prompt(q, iq, iwt, kb, vt, ikb, n_seq, seq_len, topk):
    n = q.shape[0]
    t = ATT_TILE
    nq = seq_len // t
    once = pl.Buffered(1)
    return pl.pallas_call(
        functools.partial(_attend_prompt_kernel, topk=topk),
        out_shape=jax.ShapeDtypeStruct((n, ATT_WIDTH), BF16),
        grid=(n_seq, nq),
        in_specs=[pl.BlockSpec((t, ATT_WIDTH), lambda b, j: (b * nq + j, 0)),
                  pl.BlockSpec((t, ATT_WIDTH), lambda b, j: (b * nq + j, 0)),
                  pl.BlockSpec((N_IDX_HEADS, t), lambda b, j: (0, b * nq + j)),
                  pl.BlockSpec((seq_len, ATT_WIDTH), lambda b, j: (b, 0), pipeline_mode=once),
                  pl.BlockSpec((nq, VT_ROWS, t), lambda b, j: (b, 0, 0), pipeline_mode=once),
                  pl.BlockSpec((seq_len, LANES), lambda b, j: (b, 0), pipeline_mode=once)],
        out_specs=pl.BlockSpec((t, ATT_WIDTH), lambda b, j: (b * nq + j, 0)),
        scratch_shapes=([pltpu.VMEM((nq, t, t), I32), pltpu.VMEM((nq, t, t), BF16)]
                        + [pltpu.VMEM((t, LANES), BF16)] * N_HEADS
                        + [pltpu.VMEM((HEAD_DIM, t), F32)] * N_HEADS + [pltpu.VMEM((ATT_GROUP * t, t), BF16)] * N_HEADS),
        compiler_params=_cparams("arbitrary", "arbitrary"),
        name="attend_prompt",
    )(q, iq, iwt, kb, vt, ikb)


SAMPLE_KEY_TILE = 512


def _wide(a, width):
    return a if width == LANES else jnp.concatenate([a] * (width // LANES), axis=1)


def _row_sums(x_bf16):
    return jnp.dot(x_bf16, jnp.ones((x_bf16.shape[1], LANES), BF16), preferred_element_type=F32)


def _count_rows(planes, cand, strict):
    cb = cand.astype(BF16)
    one, zero = jnp.ones((), BF16), jnp.zeros((), BF16)
    accs = [jnp.zeros(cand.shape, BF16) for _ in range(4)]
    i = 0
    for ref in planes:
        e = ref[...]
        for c in range(e.shape[1] // LANES):
            blk = e[:, c * LANES:(c + 1) * LANES]
            accs[i % 4] = accs[i % 4] + jnp.where((blk > cb) if strict else (blk >= cb), one, zero)
            i += 1
    assert i <= 256
    return _row_sums((accs[0] + accs[1]) + (accs[2] + accs[3]))


def _topk_select_rows(keys, planes, topk):
    rows = planes[0].shape[0]

    def search():
        def bit_body(i, d):
            cand = d + lax.shift_left(jnp.int32(1), jnp.int32(DIGIT_BITS - 1) - i).astype(F32)
            return jnp.where(_count_rows(planes, cand, strict=False) >= float(topk), cand, d)

        return lax.fori_loop(0, DIGIT_BITS, bit_body, jnp.zeros((rows, LANES), F32))

    d = search()
    for phase in range(1, N_DIGITS):
        db = d.astype(BF16)
        for kref, pref in zip(keys, planes):
            e = pref[...]
            dw = _wide(db, e.shape[1])
            decided = jnp.where(e > dw, jnp.asarray(DIGIT_ABOVE, BF16), jnp.asarray(DIGIT_BELOW, BF16))
            pref[...] = jnp.where(e == dw, _digit_plane(kref[...], phase), decided)
        d = search()

    cnt_ge = _count_rows(planes, d, strict=False)

    @pl.when(jnp.max(cnt_ge) > float(topk))
    def _():
        need = float(topk) - _count_rows(planes, d, strict=True)
        seen = jnp.zeros((rows, LANES), F32)
        for pref in planes:
            width = pref.shape[1]
            e = pref[...].astype(F32)
            eq = e == _wide(d, width)
            eqb = jnp.where(eq, 1.0, 0.0).astype(BF16)
            r = lax.broadcasted_iota(I32, (width, width), 0)
            c = lax.broadcasted_iota(I32, (width, width), 1)
            before = jnp.where(r < c, 1.0, 0.0).astype(BF16)
            prior = jnp.dot(eqb, before, preferred_element_type=F32) + _wide(seen, width)
            pref[...] = jnp.where(eq & (prior >= _wide(need, width)), DIGIT_BELOW, e).astype(BF16)
            seen = seen + _row_sums(eqb)

    return d


def _attend_sample_kernel(q_ref, iq_ref, iwr_ref, knt_ref, vnt_ref, iknt_ref, ckt_ref, cvt_ref, cit_ref, o_ref,
                          keys_ref, plane_ref, keys_new_ref, plane_new_ref, lt_ref, lt_new_ref, *, topk, past):
    tq = q_ref.shape[0]
    nct, _, tkc = keys_ref.shape
    rows = N_HEADS * tq
    row = lax.broadcasted_iota(I32, (rows, LANES), 0)
    qpos = past + (row & (tq - 1))
    slope = lax.bitcast_convert_type(
        lax.shift_left(126 - lax.shift_right_logical(row, tq.bit_length() - 1), 23), F32)
    qpos16 = past + lax.broadcasted_iota(I32, (tq, LANES), 0)

    iq = iq_ref[...]
    q = q_ref[...]
    iq_rows = jnp.concatenate([iq[:, h * IDX_DIM:(h + 1) * IDX_DIM] for h in range(N_IDX_HEADS)], axis=0)
    q_rows = jnp.concatenate(
        [_half_mask_wide(q, h) for h in range(N_HEADS)], axis=0) * jnp.asarray(ATTN_SCALE, BF16)
    iw_rows = iwr_ref[...]

    tiles = [(kt * tkc, tkc, cit_ref.at[:, kt * tkc:(kt + 1) * tkc], ckt_ref.at[:, kt * tkc:(kt + 1) * tkc],
              cvt_ref.at[:, kt * tkc:(kt + 1) * tkc], keys_ref.at[kt], plane_ref.at[kt], lt_ref.at[kt])
             for kt in range(nct)]
    tiles.append((past, LANES, iknt_ref, knt_ref, vnt_ref, keys_new_ref, plane_new_ref, lt_new_ref))

    def key_positions(k0, width, n_rows):
        return k0 + lax.broadcasted_iota(I32, (n_rows, width), 1)

    for k0, width, ikt, _, _, kref, pref, _ in tiles:
        s = jnp.dot(iq_rows, ikt[...].astype(BF16), preferred_element_type=F32)
        s = jnp.maximum(s, 0.0) * _wide(iw_rows, width)
        score = s[0:tq, :]
        for h in range(1, N_IDX_HEADS):
            score = score + s[h * tq:(h + 1) * tq, :]
        kpos = key_positions(k0, width, tq)
        adm = lax.shift_right_logical(kpos, CHUNK_SHIFT) <= lax.shift_right_logical(_wide(qpos16, width), CHUNK_SHIFT)
        if k0 == past:
            adm = adm & (kpos < past + tq)
        key = _mono_key(jnp.where(adm, score, -jnp.inf))
        kref[...] = key
        pref[...] = _digit_plane(key, 0)

    d_last = _topk_select_rows([t[5] for t in tiles], [t[6] for t in tiles], topk)

    m_part = jnp.full((rows, LANES), -jnp.inf, F32)
    for k0, width, _, kt_ref, _, kref, pref, ltref in tiles:
        sel = (pref[...].astype(F32) >= _wide(d_last, width)) & (kref[...] > jnp.int32(KEY_NEG_INF))
        sel = jnp.concatenate([jnp.where(sel, 1.0, 0.0)] * N_HEADS, axis=0) > 0.5
        dist = jnp.abs(_wide(qpos, width) - key_positions(k0, width, rows)).astype(F32)
        lt = (jnp.dot(q_rows, kt_ref[...].astype(BF16), preferred_element_type=F32)
              - _wide(slope, width) * jnp.where(sel, dist, MASKED_DIST))
        ltref[...] = lt
        for c in range(width // LANES):
            m_part = jnp.maximum(m_part, lt[:, c * LANES:(c + 1) * LANES])
    m = jnp.broadcast_to(jnp.max(m_part, axis=1, keepdims=True), (rows, LANES))

    acc = jnp.zeros((rows, ATT_WIDTH), F32)
    l_part = jnp.zeros((rows, LANES), F32)
    for _, width, _, _, vt_ref, _, _, ltref in tiles:
        p = jnp.exp(ltref[...] - _wide(m, width))
        for c in range(width // LANES):
            l_part = l_part + p[:, c * LANES:(c + 1) * LANES]
        acc = acc + lax.dot_general(p.astype(BF16), vt_ref[...].astype(BF16), _NT, preferred_element_type=F32)
    out = acc / jnp.sum(l_part, axis=1, keepdims=True)

    out_lane_head = lax.shift_right_logical(lax.broadcasted_iota(I32, (tq, ATT_WIDTH), 1), HEAD_SHIFT)
    y = jnp.zeros((tq, ATT_WIDTH), F32)
    for h in range(N_HEADS):
        y = y + jnp.where(out_lane_head == h, out[h * tq:(h + 1) * tq, :], 0.0)
    o_ref[...] = y.astype(BF16)


def _half_mask_wide(x, head):
    lane = lax.broadcasted_iota(I32, x.shape, 1)
    keep = lax.shift_right_logical(lane, HEAD_SHIFT) == head
    return jnp.where(keep, x, jnp.zeros_like(x))


def _attend_sample(q, iq, iw_rows, knt, vnt, iknt, cache_kt, cache_vt, cache_it, topk):
    n_seq, _, past = cache_kt.shape
    tq = q.shape[0] // n_seq
    tkc = SAMPLE_KEY_TILE
    rows = N_HEADS * tq
    assert rows == LANES and past % tkc == 0
    new = lambda width: pl.BlockSpec((tq, width), lambda b: (b, 0))
    per_seq = lambda a: pl.BlockSpec((None,) + a.shape[1:], lambda b: (b, 0, 0))
    return pl.pallas_call(
        functools.partial(_attend_sample_kernel, topk=topk, past=past),
        out_shape=jax.ShapeDtypeStruct((n_seq * tq, ATT_WIDTH), BF16),
        grid=(n_seq,),
        in_specs=[new(ATT_WIDTH), new(ATT_WIDTH), per_seq(iw_rows), per_seq(knt), per_seq(vnt), per_seq(iknt),
                  per_seq(cache_kt), per_seq(cache_vt), per_seq(cache_it)],
        out_specs=new(ATT_WIDTH),
        scratch_shapes=[pltpu.VMEM((past // tkc, tq, tkc), I32), pltpu.VMEM((past // tkc, tq, tkc), BF16),
                        pltpu.VMEM((tq, LANES), I32), pltpu.VMEM((tq, LANES), BF16),
                        pltpu.VMEM((past // tkc, rows, tkc), F32), pltpu.VMEM((rows, LANES), F32)],
        compiler_params=_cparams("arbitrary"),
        name="attend_sample",
    )(q, iq, iw_rows, knt, vnt, iknt, cache_kt, cache_vt, cache_it)


TOK_ROWS = D_MODEL // LANES


def _store_token_tiles(ref, x):
    t = x.shape[0]
    for c in range(TOK_ROWS):
        ref[pl.ds(c, t, stride=TOK_ROWS), :] = x[:, c * LANES:(c + 1) * LANES]


def _load_token_tiles(ref):
    t = ref.shape[0] // TOK_ROWS
    return jnp.concatenate([ref[pl.ds(c, t, stride=TOK_ROWS), :] for c in range(TOK_ROWS)], axis=1)


def _layer_norm(x, g, b):
    mu = jnp.mean(x, axis=-1, keepdims=True)
    xc = x - mu
    var = jnp.mean(xc * xc, axis=-1, keepdims=True)
    return xc * lax.rsqrt(var + LN_EPS) * g + b


def _post_kernel(x_ref, ya_ref, yb_ref, ga_ref, scf_ref, shf_ref, wo_ref, bo_ref, g1_ref, b1_ref,
                 wrh_ref, wrl_ref, br_ref,
                 x1_ref, h2_ref, tope_ref, gate_ref, rank_ref, cnt_ref):
    t = x_ref.shape[0]
    y = (jnp.dot(ya_ref[...], wo_ref[:A_WIDTH, :], preferred_element_type=F32)
         + jnp.dot(yb_ref[...], wo_ref[A_WIDTH:, :], preferred_element_type=F32) + bo_ref[...])
    x1 = _layer_norm(DEEPNORM_ALPHA * x_ref[...] + (1.0 + ga_ref[...]) * y, g1_ref[...], b1_ref[...])
    x1_ref[...] = x1
    h2 = x1 * (1.0 + scf_ref[...]) + shf_ref[...]
    _store_token_tiles(h2_ref, h2)

    hh = h2.astype(BF16)
    hl = (h2 - hh.astype(F32)).astype(BF16)
    logits = (lax.dot_general(wrh_ref[...], hh, _NT, preferred_element_type=F32)
              + lax.dot_general(wrh_ref[...], hl, _NT, preferred_element_type=F32)
              + lax.dot_general(wrl_ref[...], hh, _NT, preferred_element_type=F32) + br_ref[...])
    erow = lax.broadcasted_iota(I32, (N_EXPERTS, t), 0)
    vals, idxs = [], []
    for _ in range(TOP_K):
        v = jnp.max(logits, axis=0, keepdims=True)
        i = jnp.min(jnp.where(logits == v, erow, N_EXPERTS), axis=0, keepdims=True)
        vals.append(v)
        idxs.append(i)
        logits = jnp.where(erow == i, -jnp.inf, logits)
    ex = [jnp.exp(v - vals[0]) for v in vals]
    den = ex[0] + ex[1] + ex[2] + ex[3]
    gate_ref[...] = jnp.concatenate([e / den for e in ex], axis=0)
    tope_ref[...] = jnp.concatenate(idxs, axis=0)

    @pl.when(pl.program_id(0) == 0)
    def _():
        cnt_ref[...] = jnp.zeros_like(cnt_ref)

    hit = jnp.zeros((N_EXPERTS, t), F32)
    for i in idxs:
        hit = hit + jnp.where(erow == i, 1.0, 0.0)
    hitb = hit.astype(BF16)
    r = lax.broadcasted_iota(I32, (t, t), 0)
    c = lax.broadcasted_iota(I32, (t, t), 1)
    earlier = jnp.where(r < c, 1.0, 0.0).astype(BF16)
    before = jnp.dot(hitb, earlier, preferred_element_type=F32) + cnt_ref[...]
    total = jnp.dot(hitb, jnp.ones((t, t), BF16), preferred_element_type=F32)
    rank_ref[...] = jnp.concatenate(
        [jnp.sum(jnp.where(erow == i, before, 0.0), axis=0, keepdims=True) for i in idxs], axis=0).astype(I32)
    cnt_ref[...] += total


def _post(x, ya, yb, ga, scf, shf, w_o, b_o, g1, b1, wrh, wrl, br):
    n, d = x.shape
    t = POST_TILE
    nt = n // t
    if ga.shape[0] == n:
        mod_spec = pl.BlockSpec((t, d), lambda i: (i, 0))
    else:
        tiles_per_seq = nt // ga.shape[0]
        ga, scf, shf = (a.reshape(a.shape[0], 1, d) for a in (ga, scf, shf))
        mod_spec = pl.BlockSpec((None, 1, d), lambda i: (i // tiles_per_seq, 0, 0))
    row = lambda w: pl.BlockSpec((t, w), lambda i: (i, 0))
    col = lambda r: pl.BlockSpec((r, t), lambda i: (0, i))
    const = lambda a: pl.BlockSpec(a.shape, lambda i: (0, 0))
    return pl.pallas_call(
        _post_kernel,
        out_shape=(jax.ShapeDtypeStruct((n, d), F32), jax.ShapeDtypeStruct((n * TOK_ROWS, LANES), F32),
                   jax.ShapeDtypeStruct((TOP_K, n), I32), jax.ShapeDtypeStruct((TOP_K, n), F32),
                   jax.ShapeDtypeStruct((TOP_K, n), I32), jax.ShapeDtypeStruct((N_EXPERTS, t), F32)),
        grid=(nt,),
        in_specs=[row(d), row(A_WIDTH), row(ATT_WIDTH), mod_spec, mod_spec, mod_spec,
                  const(w_o), const(b_o), const(g1), const(b1), const(wrh), const(wrl), const(br)],
        out_specs=(row(d), pl.BlockSpec((t * TOK_ROWS, LANES), lambda i: (i, 0)), col(TOP_K), col(TOP_K), col(TOP_K),
                   pl.BlockSpec((N_EXPERTS, t), lambda i: (0, 0))),
        compiler_params=_cparams("arbitrary"),
        name="post",
    )(x, ya, yb, ga, scf, shf, w_o, b_o, g1, b1, wrh, wrl, br)


def _token_rows(ref, r):
    return ref.at[pl.ds(pl.multiple_of(r * TOK_ROWS, TOK_ROWS), TOK_ROWS)]


def _token_copies_wait(hbm_ref, vmem_ref, sem, n_tokens):
    rows = n_tokens * TOK_ROWS
    pltpu.make_async_copy(hbm_ref.at[pl.ds(0, rows)], vmem_ref.at[pl.ds(0, rows)], sem).wait()


def _dispatch_kernel(zrow_ref, dest_ref, h_ref, xp_ref, zero_ref, sem):
    t = h_ref.shape[0] // TOK_ROWS

    block_rows = MOE_BLOCK * TOK_ROWS

    def zero_block(slot0):
        z0 = pl.multiple_of(slot0 * TOK_ROWS, block_rows)
        cp = pltpu.make_async_copy(zero_ref, xp_ref.at[pl.ds(z0, block_rows)], sem)
        cp.start()
        cp.wait()

    @pl.when(pl.program_id(0) == 0)
    def _():
        zero_ref[...] = jnp.zeros_like(zero_ref)
        for e in range(N_EXPERTS):
            @pl.when(zrow_ref[e] >= 0)
            def _():
                zero_block(zrow_ref[e])

        def unused(b, _):
            zero_block(b * MOE_BLOCK)
            return 0

        lax.fori_loop(zrow_ref[N_EXPERTS], xp_ref.shape[0] // block_rows, unused, 0)

    def body(i, _):
        for k in range(TOP_K):
            pltpu.make_async_copy(_token_rows(h_ref, i), _token_rows(xp_ref, dest_ref[k, i]), sem).start(priority=k % 2)
        return 0

    lax.fori_loop(0, t, body, 0, unroll=DMA_ISSUE_UNROLL)
    for k in range(TOP_K):
        _token_copies_wait(xp_ref, h_ref, sem, t)


def _dispatch(zrow, dest, h2, n_slots):
    t = ROW_TILE
    n = h2.shape[0] // TOK_ROWS
    grid_spec = pltpu.PrefetchScalarGridSpec(
        num_scalar_prefetch=1,
        grid=(n // t,),
        in_specs=[pl.BlockSpec((TOP_K, t), lambda i, z: (0, i), memory_space=pltpu.SMEM),
                  pl.BlockSpec((t * TOK_ROWS, LANES), lambda i, z: (i, 0))],
        out_specs=pl.BlockSpec(memory_space=pl.ANY),
        scratch_shapes=[pltpu.VMEM((MOE_BLOCK * TOK_ROWS, LANES), F32), pltpu.SemaphoreType.DMA],
    )
    return pl.pallas_call(
        _dispatch_kernel,
        out_shape=jax.ShapeDtypeStruct((n_slots * TOK_ROWS, LANES), F32),
        grid_spec=grid_spec,
        compiler_params=_cparams("arbitrary"),
        name="dispatch",
    )(zrow, dest, h2)


def _experts_kernel(be_ref, bi_ref, nu_ref, x_ref, wg_ref, bg_ref, wu_ref, bu_ref, wd_ref, bd_ref, y_ref,
                    wgb_ref, wub_ref, wdb_ref):
    i = pl.program_id(0)

    @pl.when((i == 0) | (be_ref[i] != be_ref[jnp.maximum(i - 1, 0)]))
    def _():
        wgb_ref[...] = wg_ref[...].astype(BF16)
        wub_ref[...] = wu_ref[...].astype(BF16)
        wdb_ref[...] = wd_ref[...].astype(BF16)

    @pl.when(i < nu_ref[0])
    def _():
        x = _load_token_tiles(x_ref).astype(BF16)
        g = jnp.minimum(jnp.dot(x, wgb_ref[...], preferred_element_type=F32) + bg_ref[...], SWIGLU_LIMIT)
        u = jnp.clip(jnp.dot(x, wub_ref[...], preferred_element_type=F32) + bu_ref[...], -SWIGLU_LIMIT, SWIGLU_LIMIT)
        a = g * jax.nn.sigmoid(SWIGLU_ALPHA * g)
        mid = ((u + 1.0) * a).astype(BF16)
        _store_token_tiles(y_ref, jnp.dot(mid, wdb_ref[...], preferred_element_type=F32) + bd_ref[...])

    @pl.when(pl.program_id(0) >= nu_ref[0])
    def _():
        y_ref[...] = jnp.zeros_like(y_ref)


def _experts(block_e, block_i, n_used, xp, wg, bg, wu, bu, wd, bd):
    d, f = wg.shape[1], wg.shape[2]
    nb = xp.shape[0] // (MOE_BLOCK * TOK_ROWS)
    wspec = lambda a, b: pl.BlockSpec((None, a, b), lambda i, be, bi, nu: (be[i], 0, 0))
    slots = pl.BlockSpec((MOE_BLOCK * TOK_ROWS, LANES), lambda i, be, bi, nu: (bi[i], 0))
    grid_spec = pltpu.PrefetchScalarGridSpec(
        num_scalar_prefetch=3,
        grid=(nb,),
        in_specs=[slots, wspec(d, f), wspec(1, f), wspec(d, f), wspec(1, f), wspec(f, d), wspec(1, d)],
        out_specs=pl.BlockSpec((MOE_BLOCK * TOK_ROWS, LANES), lambda i, be, bi, nu: (i, 0)),
        scratch_shapes=[pltpu.VMEM((d, f), BF16), pltpu.VMEM((d, f), BF16), pltpu.VMEM((f, d), BF16)],
    )
    return pl.pallas_call(
        _experts_kernel,
        out_shape=jax.ShapeDtypeStruct(xp.shape, F32),
        grid_spec=grid_spec,
        compiler_params=_cparams("arbitrary"),
        name="experts",
    )(block_e, block_i, n_used, xp, wg, bg, wu, bu, wd, bd)


def _combine_kernel(dest_ref, dest_next_ref, gate_ref, x1_ref, gf_ref, g2_ref, b2_ref, yp_ref, o_ref, buf_ref, sems):
    t = x1_ref.shape[0]
    step = pl.program_id(0)
    slot = step % 2

    def start_gather(dst_ref, into):
        def body(i, _):
            for k in range(TOP_K):
                pltpu.make_async_copy(_token_rows(yp_ref, dst_ref[k, i]), _token_rows(buf_ref.at[into, k], i),
                                      sems.at[into]).start(priority=k % 2)
            return 0

        lax.fori_loop(0, t, body, 0, unroll=DMA_ISSUE_UNROLL)

    @pl.when(step == 0)
    def _():
        start_gather(dest_ref, 0)

    @pl.when(step + 1 < pl.num_programs(0))
    def _():
        start_gather(dest_next_ref, 1 - slot)

    for k in range(TOP_K):
        _token_copies_wait(yp_ref, buf_ref.at[slot, k], sems.at[slot], t)

    gates = jnp.concatenate([gate_ref[...], jnp.zeros((LANES - TOP_K, t), F32)], axis=0).T
    f = gates[:, 0:1] * _load_token_tiles(buf_ref.at[slot, 0])
    for k in range(1, TOP_K):
        f = f + gates[:, k:k + 1] * _load_token_tiles(buf_ref.at[slot, k])
    o_ref[...] = _layer_norm(DEEPNORM_ALPHA * x1_ref[...] + (1.0 + gf_ref[...]) * f, g2_ref[...], b2_ref[...])


def _combine(dest, gates, x1, gf, g2, b2, yp):
    n, d = x1.shape
    t = ROW_TILE
    nt = n // t
    if gf.shape[0] == n:
        mod_spec = pl.BlockSpec((t, d), lambda i: (i, 0))
    else:
        tiles_per_seq = nt // gf.shape[0]
        gf = gf.reshape(gf.shape[0], 1, d)
        mod_spec = pl.BlockSpec((None, 1, d), lambda i: (i // tiles_per_seq, 0, 0))
    return pl.pallas_call(
        _combine_kernel,
        out_shape=jax.ShapeDtypeStruct((n, d), F32),
        grid=(nt,),
        in_specs=[pl.BlockSpec((TOP_K, t), lambda i: (0, i), memory_space=pltpu.SMEM),
                  pl.BlockSpec((TOP_K, t), lambda i: (0, jnp.minimum(i + 1, nt - 1)), memory_space=pltpu.SMEM),
                  pl.BlockSpec((TOP_K, t), lambda i: (0, i)),
                  pl.BlockSpec((t, d), lambda i: (i, 0)), mod_spec,
                  pl.BlockSpec((1, d), lambda i: (0, 0)), pl.BlockSpec((1, d), lambda i: (0, 0)),
                  pl.BlockSpec(memory_space=pl.ANY)],
        out_specs=pl.BlockSpec((t, d), lambda i: (i, 0)),
        scratch_shapes=[pltpu.VMEM((2, TOP_K, t * TOK_ROWS, LANES), F32), pltpu.SemaphoreType.DMA((2,))],
        compiler_params=_cparams("arbitrary"),
        name="combine",
    )(dest, dest, gates, x1, gf, g2, b2, yp)


def _moe(h2, tope, rank, counts, gates, x1, gf, g2, b2, experts_w):
    n = tope.shape[1]
    nb = (n * TOP_K + N_EXPERTS * (MOE_BLOCK - 1) + MOE_BLOCK - 1) // MOE_BLOCK
    pcounts = (counts + MOE_BLOCK - 1) // MOE_BLOCK * MOE_BLOCK
    pend = jnp.cumsum(pcounts)
    pstart = pend - pcounts
    eids = jnp.arange(N_EXPERTS, dtype=I32).reshape(N_EXPERTS, 1, 1)
    dest = rank + jnp.sum(jnp.where(tope[None] == eids, pstart.reshape(N_EXPERTS, 1, 1), 0), axis=0).astype(I32)
    n_used = (pend[-1] // MOE_BLOCK).astype(I32)
    zrow = jnp.concatenate([jnp.where(counts > 0, pend - MOE_BLOCK, -1), n_used.reshape(1)]).astype(I32)
    blk = jnp.arange(nb, dtype=I32)
    block_i = jnp.minimum(blk, n_used - 1)
    block_e = jnp.minimum(jnp.sum(pend[None, :] <= (block_i * MOE_BLOCK)[:, None], axis=1), N_EXPERTS - 1).astype(I32)
    xp = _dispatch(zrow, dest, h2, nb * MOE_BLOCK)
    yp = _experts(block_e, block_i, n_used.reshape(1), xp, *experts_w)
    return _combine(dest, gates, x1, gf, g2, b2, yp)


def kernel(x_prompt, x_sample, c_prompt, c_sample, cache_k, cache_v, cache_kidx, w_in, a_ln_g, a_ln_b, a_ws, a_bs,
           w_o, b_o, w_c, b_c, ln1_g, ln1_b, ln2_g, ln2_b, w_router, b_router, w_gate, b_gate, w_up, b_up,
           w_down, b_down):
    bp, s, d = x_prompt.shape
    bs, ts, _ = x_sample.shape
    past = cache_k.shape[2]
    np_, ns = bp * s, bs * ts

    wi = w_in[0]
    w_in_p = jnp.zeros((d, IN_COLS_PAD), F32)
    w_in_p = w_in_p.at[:, :OFF_IK].set(wi[:, :OFF_IK])
    w_in_p = w_in_p.at[:, OFF_IK:OFF_IK + IDX_DIM].set(wi[:, OFF_IK:OFF_IK + IDX_DIM])
    w_in_p = w_in_p.at[:, OFF_IK + IDX_DIM:OFF_IK + 2 * IDX_DIM].set(wi[:, OFF_IK:OFF_IK + IDX_DIM])
    w_in_p = w_in_p.at[:, OFF_IW:OFF_IW + N_IDX_HEADS].set(wi[:, OFF_IK + IDX_DIM:OFF_IK + IDX_DIM + N_IDX_HEADS])
    w_in_p = w_in_p.astype(BF16)
    lng, lnb = a_ln_g[0].reshape(1, A_WIDTH), a_ln_b[0].reshape(1, A_WIDTH)
    wtril = jnp.tril(a_ws[0])
    wm_p = wtril.astype(BF16)
    bm_p = jnp.broadcast_to(a_bs[0][:, :, None], (A_GROUPS, A_CHUNK, A_GROUP_DIM)).astype(F32)
    rep = A_CHUNK // ts
    wm_s = jnp.einsum("ab,gij->gaibj", jnp.eye(rep, dtype=F32), wtril[:, :ts, :ts]).reshape(
        A_GROUPS, A_CHUNK, A_CHUNK).astype(BF16)
    bm_s = jnp.broadcast_to(jnp.tile(a_bs[0][:, :ts], (1, rep))[:, :, None], (A_GROUPS, A_CHUNK, A_GROUP_DIM)).astype(F32)
    w_o_b = w_o[0].astype(BF16)
    b_o_r = b_o[0].reshape(1, d)
    wr_t = w_router[0].T
    wrh = wr_t.astype(BF16)
    wrl = (wr_t - wrh.astype(F32)).astype(BF16)
    br = jnp.broadcast_to(b_router[0][:, None], (N_EXPERTS, POST_TILE)).astype(F32)
    experts_w = (w_gate[0], b_gate[0][:, None, :], w_up[0], b_up[0][:, None, :], w_down[0], b_down[0][:, None, :])
    g1, b1 = ln1_g[0].reshape(1, d), ln1_b[0].reshape(1, d)
    g2, b2 = ln2_g[0].reshape(1, d), ln2_b[0].reshape(1, d)

    mods = _cond_mods(jnp.concatenate([c_prompt, c_sample], axis=0), w_c[0], b_c[0]).reshape(bp + bs, 6, d)
    mods_p = [mods[:bp, i] for i in range(6)]
    mods_s = [jnp.repeat(mods[bp:, i], ts, axis=0) for i in range(6)]

    xp2 = x_prompt.reshape(np_, d)
    ya, q, kft, kb, _, vft, vt, iq, ikft, ikb, iwt, _ = _project(
        xp2, mods_p[1], mods_p[0], w_in_p, lng, lnb, wm_p, bm_p, bp)
    yb = _attend_prompt(q, iq, iwt, kb, vt, ikb, bp, s, min(TOPK_MAX, s // 4))
    x1, h2, tope, gates, rank, cnt = _post(xp2, ya, yb, mods_p[2], mods_p[4], mods_p[3], w_o_b, b_o_r, g1, b1, wrh, wrl, br)
    y_p = _moe(h2, tope, rank, cnt[:, 0].astype(I32), gates, x1, mods_p[5], g2, b2, experts_w)
    heads_last = lambda a: jnp.transpose(a.reshape(1, bp, N_HEADS, HEAD_DIM, s), (0, 1, 4, 2, 3))
    out_p = (y_p.reshape(bp, s, d), heads_last(kft), heads_last(vft),
             jnp.transpose(ikft.reshape(1, bp, IDX_DIM, s), (0, 1, 3, 2)))

    xs2 = x_sample.reshape(ns, d)
    ya, q, kft, _, vf, vft, _, iq, ikft, _, iwt, va = _project(
        xs2, mods_s[1], mods_s[0], w_in_p, lng, lnb, wm_s, bm_s, 1)
    kf, ikf = kft[0].T, ikft[0].T
    per_seq_t = lambda a: jnp.pad(jnp.transpose(a[0].reshape(a.shape[1], bs, ts), (1, 0, 2)),
                                  ((0, 0), (0, 0), (0, LANES - ts)))
    iw_rows = jnp.broadcast_to(
        jnp.transpose(iwt.reshape(N_IDX_HEADS, bs, ts), (1, 0, 2)).reshape(bs, N_IDX_HEADS * ts, 1),
        (bs, N_IDX_HEADS * ts, LANES))
    cache_kt = jnp.transpose(cache_k[0], (0, 2, 3, 1)).reshape(bs, ATT_WIDTH, past)
    cache_vt = jnp.transpose(cache_v[0], (0, 2, 3, 1)).reshape(bs, ATT_WIDTH, past)
    cache_it = jnp.transpose(cache_kidx[0], (0, 2, 1))
    yb = _attend_sample(q, iq, iw_rows, per_seq_t(kft), per_seq_t(vft), per_seq_t(ikft), cache_kt, cache_vt, cache_it,
                        min(TOPK_MAX, (past + ts) // 4))
    x1, h2, tope, gates, rank, cnt = _post(xs2, ya, yb, mods_s[2], mods_s[4], mods_s[3], w_o_b, b_o_r, g1, b1, wrh, wrl, br)
    y_s = _moe(h2, tope, rank, cnt[:, 0].astype(I32), gates, x1, mods_s[5], g2, b2, experts_w)

    return (out_p[0], y_s.reshape(bs, ts, d), out_p[1], out_p[2], out_p[3],
            kf.reshape(1, bs, ts, N_HEADS, HEAD_DIM), vf.reshape(1, bs, ts, N_HEADS, HEAD_DIM),
            ikf.reshape(1, bs, ts, IDX_DIM), va.reshape(1, bs, ts, A_WIDTH))
```

```python
import functools

import jax
import jax.numpy as jnp
from jax import lax
from jax.experimental import pallas as pl
from jax.experimental.pallas import tpu as pltpu

F32 = jnp.float32
BF16 = jnp.bfloat16
I32 = jnp.int32

D_MODEL = 1024
CHUNK_SHIFT = 6
A_GROUPS = 4
A_GROUP_DIM = 128
A_WIDTH = A_GROUPS * A_GROUP_DIM
A_CHUNK = 128
N_HEADS = 8
HEAD_DIM = 64
HEAD_SHIFT = 6
ATT_WIDTH = N_HEADS * HEAD_DIM
N_IDX_HEADS = 8
IDX_DIM = 64
TOPK_MAX = 256
ATTN_SCALE = HEAD_DIM ** -0.5
VT_HEAD_ROWS = HEAD_DIM + 16
VT_ROWS = N_HEADS * VT_HEAD_ROWS
IDX_W_SCALE = (N_IDX_HEADS ** -0.5) * (IDX_DIM ** -0.5)
N_EXPERTS = 32
TOP_K = 4
SWIGLU_LIMIT = 7.0
SWIGLU_ALPHA = 1.702
DEEPNORM_ALPHA = 2.0 ** 0.25
LN_EPS = 1e-5

LANES = 128
SUBLANES = 8
VMEM_LIMIT_BYTES = 56 * 1024 * 1024

ROW_TILE = 256
PROJECT_TILE = 512
POST_TILE = 512
MOE_BLOCK = 512
MOE_BLOCK_SMALL = 128
ATT_TILE = 256
SCORE_GROUP = 4
ATT_GROUP = 4
DMA_ISSUE_UNROLL = 8
STATIC_TILE_UNROLL = 4

OFF_AU, OFF_AV, OFF_Q, OFF_K, OFF_V, OFF_IQ = 0, 512, 1024, 1536, 2048, 2560
OFF_IK = 3072
OFF_IW = 3200
IN_COLS_PAD = 3328

MASKED_DIST = 3.0e32
INT32_MIN = -(2 ** 31)
KEY_NEG_INF = INT32_MIN + 0x7FFFFF

_NT = (((1,), (1,)), ((), ()))


def _cparams(*sem):
    return pltpu.CompilerParams(dimension_semantics=sem, vmem_limit_bytes=VMEM_LIMIT_BYTES)


def _mods_kernel(c_ref, w_ref, b_ref, o_ref):
    c = c_ref[...]
    s = c * jax.nn.sigmoid(c)
    o_ref[...] = jnp.dot(s.astype(BF16), w_ref[...].astype(BF16), preferred_element_type=F32) + b_ref[...]


def _cond_mods(c, w_c, b_c):
    nb, d = c.shape
    n_out = w_c.shape[1]
    return pl.pallas_call(
        _mods_kernel,
        out_shape=jax.ShapeDtypeStruct((nb, n_out), F32),
        grid=(n_out // d,),
        in_specs=[pl.BlockSpec((nb, d), lambda j: (0, 0)),
                  pl.BlockSpec((d, d), lambda j: (0, j)),
                  pl.BlockSpec((1, d), lambda j: (0, j))],
        out_specs=pl.BlockSpec((nb, d), lambda j: (0, j)),
        compiler_params=_cparams("arbitrary"),
        name="mods",
    )(c, w_c, b_c.reshape(1, n_out))


def _gelu(x):
    return 0.5 * x * (1.0 + lax.erf(x * 0.7071067811865476))


def _project_kernel(x_ref, sc_ref, sh_ref, w_ref, lng_ref, lnb_ref, wm_ref, bm_ref,
                    ya_ref, q_ref, kft_ref, kb_ref, vf_ref, vft_ref, vt_ref, iq_ref, ikft_ref, ikb_ref, iwt_ref,
                    va_ref):
    t = x_ref.shape[0]
    h = (x_ref[...] * (1.0 + sc_ref[...]) + sh_ref[...]).astype(BF16)

    def proj(c0, n):
        return jnp.dot(h, w_ref[:, c0:c0 + n], preferred_element_type=F32)

    u = _gelu(proj(OFF_AU, A_WIDTH))
    gv = _gelu(proj(OFF_AV, A_WIDTH))
    for g in range(A_GROUPS):
        lo, hi = g * A_GROUP_DIM, (g + 1) * A_GROUP_DIM
        xg = gv[:, lo:hi]
        mu = jnp.mean(xg, axis=-1, keepdims=True)
        xc = xg - mu
        var = jnp.mean(xc * xc, axis=-1, keepdims=True)
        vg = xc * lax.rsqrt(var + LN_EPS) * lng_ref[:, lo:hi] + lnb_ref[:, lo:hi]
        va_ref[:, lo:hi] = vg
        vgb = vg.astype(BF16)
        for c in range(t // A_CHUNK):
            r0, r1 = c * A_CHUNK, (c + 1) * A_CHUNK
            mixed = jnp.dot(wm_ref[g], vgb[r0:r1, :], preferred_element_type=F32) + bm_ref[g]
            ya_ref[r0:r1, lo:hi] = (u[r0:r1, lo:hi] * mixed).astype(BF16)

    q_ref[...] = proj(OFF_Q, ATT_WIDTH).astype(BF16)
    k = proj(OFF_K, ATT_WIDTH)
    kft_ref[...] = k.T
    kb_ref[...] = k.astype(BF16)
    v = proj(OFF_V, ATT_WIDTH)
    vf_ref[...] = v
    v_t32 = v.T
    vft_ref[...] = v_t32
    v_t = v_t32.astype(BF16)
    ones = jnp.ones((VT_HEAD_ROWS - HEAD_DIM, ATT_TILE), BF16)
    for c in range(t // ATT_TILE):
        cols = slice(c * ATT_TILE, (c + 1) * ATT_TILE)
        vt_ref[c] = jnp.concatenate(
            [blk for h in range(N_HEADS) for blk in (v_t[h * HEAD_DIM:(h + 1) * HEAD_DIM, cols], ones)], axis=0)
    iq_ref[...] = proj(OFF_IQ, N_IDX_HEADS * IDX_DIM).astype(BF16)
    ik2 = proj(OFF_IK, LANES)
    ikft_ref[...] = ik2.T[:IDX_DIM, :]
    ikb_ref[...] = ik2.astype(BF16)
    iw = proj(OFF_IW, LANES) * IDX_W_SCALE
    iwt_ref[...] = iw.T[:N_IDX_HEADS, :]


def _project(x, sc, sh, w_in_p, lng, lnb, wm, bm, n_out_seq):
    n, d = x.shape
    t = PROJECT_TILE
    nt = n // t
    tiles_per_out = nt // n_out_seq
    trans = lambda rows: pl.BlockSpec((None, rows, t), lambda i: (i // tiles_per_out, 0, i % tiles_per_out))
    if sc.shape[0] == n:
        mod_spec = pl.BlockSpec((t, d), lambda i: (i, 0))
    else:
        tiles_per_seq = nt // sc.shape[0]
        sc = sc.reshape(sc.shape[0], 1, d)
        sh = sh.reshape(sh.shape[0], 1, d)
        mod_spec = pl.BlockSpec((None, 1, d), lambda i: (i // tiles_per_seq, 0, 0))
    row = lambda w: pl.BlockSpec((t, w), lambda i: (i, 0))
    const2 = lambda a: pl.BlockSpec(a.shape, lambda i: (0, 0))
    const3 = lambda a: pl.BlockSpec(a.shape, lambda i: (0, 0, 0))
    out_shape = (
        jax.ShapeDtypeStruct((n, A_WIDTH), BF16),
        jax.ShapeDtypeStruct((n, ATT_WIDTH), BF16),
        jax.ShapeDtypeStruct((n_out_seq, ATT_WIDTH, n // n_out_seq), F32),
        jax.ShapeDtypeStruct((n, ATT_WIDTH), BF16),
        jax.ShapeDtypeStruct((n, ATT_WIDTH), F32),
        jax.ShapeDtypeStruct((n_out_seq, ATT_WIDTH, n // n_out_seq), F32),
        jax.ShapeDtypeStruct((n // ATT_TILE, VT_ROWS, ATT_TILE), BF16),
        jax.ShapeDtypeStruct((n, ATT_WIDTH), BF16),
        jax.ShapeDtypeStruct((n_out_seq, IDX_DIM, n // n_out_seq), F32),
        jax.ShapeDtypeStruct((n, LANES), BF16),
        jax.ShapeDtypeStruct((N_IDX_HEADS, n), F32),
        jax.ShapeDtypeStruct((n, A_WIDTH), F32),
    )
    out_specs = (row(A_WIDTH), row(ATT_WIDTH), trans(ATT_WIDTH), row(ATT_WIDTH), row(ATT_WIDTH), trans(ATT_WIDTH),
                 pl.BlockSpec((t // ATT_TILE, VT_ROWS, ATT_TILE), lambda i: (i, 0, 0)),
                 row(ATT_WIDTH), trans(IDX_DIM), row(LANES),
                 pl.BlockSpec((N_IDX_HEADS, t), lambda i: (0, i)),
                 row(A_WIDTH))
    return pl.pallas_call(
        _project_kernel,
        out_shape=out_shape,
        grid=(nt,),
        in_specs=[row(d), mod_spec, mod_spec, const2(w_in_p), const2(lng), const2(lnb), const3(wm), const3(bm)],
        out_specs=out_specs,
        compiler_params=_cparams("arbitrary"),
        name="project",
    )(x, sc, sh, w_in_p, lng, lnb, wm, bm)


DIGIT_BITS = 8
N_DIGITS = 32 // DIGIT_BITS
DIGIT_ABOVE = 512.0
DIGIT_BELOW = -1.0
PACKED_ROWS = 16


def _tile_loop(nkt, body, init):
    if isinstance(nkt, int):
        return lax.fori_loop(0, nkt, body, init, unroll=STATIC_TILE_UNROLL)
    g = STATIC_TILE_UNROLL
    groups = nkt // g

    def group_body(i, c):
        for k in range(g):
            c = body(g * i + k, c)
        return c

    return lax.fori_loop(g * groups, nkt, body, lax.fori_loop(0, groups, group_body, init))


def _mono_key(x):
    b = lax.bitcast_convert_type(x, I32)
    return jnp.where(b >= 0, b, b ^ jnp.int32(0x7FFFFFFF))


def _digit_plane(key, phase):
    shift = 32 - DIGIT_BITS * (phase + 1)
    d = lax.shift_right_arithmetic(key, jnp.int32(shift)) if shift else key
    d = d + (1 << (DIGIT_BITS - 1)) if phase == 0 else d & ((1 << DIGIT_BITS) - 1)
    return d.astype(F32).astype(BF16)


def _count_plane(plane_ref, nkt, cand, strict):
    _, tk, w = plane_ref.shape
    cb = cand.astype(BF16)
    one, zero = jnp.ones((), BF16), jnp.zeros((), BF16)

    def body(kt, cnt):
        e = plane_ref[kt]
        accs = [jnp.zeros((PACKED_ROWS, w), BF16) for _ in range(4)]
        for r in range(tk // PACKED_ROWS):
            blk = e[r * PACKED_ROWS:(r + 1) * PACKED_ROWS, :]
            accs[r % 4] = accs[r % 4] + jnp.where((blk > cb) if strict else (blk >= cb), one, zero)
        return cnt + ((accs[0] + accs[1]) + (accs[2] + accs[3])).astype(F32)

    cnt = _tile_loop(nkt, body, jnp.zeros((PACKED_ROWS, w), F32))
    return jnp.sum(cnt, axis=0, keepdims=True)


def _search_digit(plane_ref, nkt, topk):
    w = plane_ref.shape[2]

    def bit_body(i, d):
        cand = d + lax.shift_left(jnp.int32(1), jnp.int32(DIGIT_BITS - 1) - i).astype(F32)
        cnt = _count_plane(plane_ref, nkt, cand, strict=False)
        return jnp.where(cnt >= float(topk), cand, d)

    return lax.fori_loop(0, DIGIT_BITS, bit_body, jnp.zeros((1, w), F32))


def _topk_select(keys_ref, plane_ref, nkt, topk):
    _, tk, w = plane_ref.shape
    d = _search_digit(plane_ref, nkt, topk)
    for phase in range(1, N_DIGITS):
        db = d.astype(BF16)

        def refine(kt, _, phase=phase, db=db):
            e = plane_ref[kt]
            decided = jnp.where(e > db, jnp.asarray(DIGIT_ABOVE, BF16), jnp.asarray(DIGIT_BELOW, BF16))
            plane_ref[kt] = jnp.where(e == db, _digit_plane(keys_ref[kt], phase), decided)
            return 0

        _tile_loop(nkt, refine, 0)
        d = _search_digit(plane_ref, nkt, topk)

    cnt_ge = _count_plane(plane_ref, nkt, d, strict=False)

    @pl.when(jnp.max(cnt_ge) > float(topk))
    def _():
        need = float(topk) - _count_plane(plane_ref, nkt, d, strict=True)
        r = lax.broadcasted_iota(I32, (tk, tk), 0)
        c = lax.broadcasted_iota(I32, (tk, tk), 1)
        before = jnp.where(c < r, 1.0, 0.0).astype(BF16)

        def body(kt, seen):
            e = plane_ref[kt].astype(F32)
            eq = e == d
            eqf = jnp.where(eq, 1.0, 0.0)
            prior = jnp.dot(before, eqf.astype(BF16), preferred_element_type=F32) + seen
            plane_ref[kt] = jnp.where(eq & (prior >= need), DIGIT_BELOW, e).astype(BF16)
            return seen + jnp.sum(eqf.reshape(tk // SUBLANES, SUBLANES, w), axis=0).sum(axis=0, keepdims=True)

        lax.fori_loop(0, nkt, body, jnp.zeros((1, w), F32))

    return d


def _selected(keys_ref, plane_ref, kt, d):
    return (plane_ref[kt].astype(F32) >= d) & (keys_ref[kt] > jnp.int32(KEY_NEG_INF))


def _half_mask(x_pair, head):
    lane = lax.broadcasted_iota(I32, x_pair.shape, 1)
    keep = (lane >= HEAD_DIM) if head % 2 else (lane < HEAD_DIM)
    return jnp.where(keep, x_pair, jnp.zeros_like(x_pair))


def _attend_prompt_kernel(q_ref, iq_ref, iwt_ref, k_ref, vt_ref, ik_ref, o_ref, keys_ref, plane_ref, *head_refs, topk):
    qh_refs, acc_refs, lt_refs = (head_refs[i * N_HEADS:(i + 1) * N_HEADS] for i in range(3))
    tq = q_ref.shape[0]
    tk = keys_ref.shape[1]
    j = pl.program_id(1)
    nkt = j + 1
    q0 = j * tq
    row = lax.broadcasted_iota(I32, (tk, tq), 0)
    lane = lax.broadcasted_iota(I32, (tk, tq), 1)
    qpos = q0 + lane

    iq = iq_ref[...]
    iqm = [_half_mask(iq[:, (h // 2) * LANES:(h // 2 + 1) * LANES], h) for h in range(N_IDX_HEADS)]
    iw = iwt_ref[...]

    def score_tiles(kt, n, diagonal):
        k0 = pl.multiple_of(kt * tk, tk)
        ikt = ik_ref[pl.ds(k0, n * tk), :]
        s = jnp.zeros((n * tk, tq), F32)
        for h in range(N_IDX_HEADS):
            r = lax.dot_general(ikt, iqm[h], _NT, preferred_element_type=F32)
            s = s + jnp.maximum(r, 0.0) * iw[h:h + 1, :]
        if diagonal:
            adm = lax.shift_right_logical(k0 + row, CHUNK_SHIFT) <= lax.shift_right_logical(qpos, CHUNK_SHIFT)
            s = jnp.where(adm, s, -jnp.inf)
        for c in range(n):
            key = _mono_key(s[c * tk:(c + 1) * tk, :])
            keys_ref[kt + c] = key
            plane_ref[kt + c] = _digit_plane(key, 0)
        return 0

    assert tk == tq
    groups = j // SCORE_GROUP
    lax.fori_loop(0, groups, lambda i, _: score_tiles(SCORE_GROUP * i, SCORE_GROUP, False), 0)
    lax.fori_loop(SCORE_GROUP * groups, j, lambda kt, _: score_tiles(kt, 1, False), 0)
    score_tiles(j, 1, True)

    d_last = _topk_select(keys_ref, plane_ref, nkt, topk)

    def masked_distance(kt):
        dist = jnp.abs(qpos - (kt * tk + row)).astype(F32)
        return jnp.where(_selected(keys_ref, plane_ref, kt, d_last), dist, MASKED_DIST)

    qfull = q_ref[...]
    for h in range(N_HEADS):
        pair = h // 2
        qh_refs[h][...] = _half_mask(qfull[:, pair * LANES:(pair + 1) * LANES], h) * jnp.asarray(ATTN_SCALE, BF16)
        acc_refs[h][...] = jnp.zeros_like(acc_refs[h])

    def att_tiles(kt, n, carry):
        m_all, l_all = carry
        k0 = pl.multiple_of(kt * tk, tk)
        dist = jnp.concatenate([masked_distance(kt + c) for c in range(n)], axis=0)
        ms = []
        for h in range(N_HEADS):
            pair = h // 2
            slope = 2.0 ** (-8.0 * (h + 1) / N_HEADS)
            kp = k_ref[pl.ds(k0, n * tk), pair * LANES:(pair + 1) * LANES]
            lt = lax.dot_general(kp, qh_refs[h][...], _NT, preferred_element_type=F32) - slope * dist
            lt_refs[h][0:n * tk, :] = lt.astype(BF16)
            tile_max = jnp.max(lt, axis=0, keepdims=True).astype(BF16).astype(F32)
            ms.append(jnp.maximum(m_all[h:h + 1, :], tile_max))
        ls = []
        for h in range(N_HEADS):
            alpha = jnp.exp(m_all[h:h + 1, :] - ms[h])
            mb = ms[h].astype(BF16)
            acc = alpha * acc_refs[h][...]
            l = alpha * l_all[h:h + 1, :]
            for c in range(n):
                p = jnp.exp(lt_refs[h][c * tk:(c + 1) * tk, :] - mb)
                pv = jnp.dot(vt_ref[kt + c, h * VT_HEAD_ROWS:(h + 1) * VT_HEAD_ROWS, :], p,
                             preferred_element_type=F32)
                acc = acc + pv[:HEAD_DIM, :]
                l = l + pv[HEAD_DIM:HEAD_DIM + 1, :]
            acc_refs[h][...] = acc
            ls.append(l)
        return jnp.concatenate(ms, axis=0), jnp.concatenate(ls, axis=0)

    groups = nkt // ATT_GROUP
    carry = (jnp.full((N_HEADS, tq), -jnp.inf, F32), jnp.zeros((N_HEADS, tq), F32))
    carry = lax.fori_loop(0, groups, lambda i, c: att_tiles(ATT_GROUP * i, ATT_GROUP, c), carry)
    _, l_all = lax.fori_loop(ATT_GROUP * groups, nkt, lambda kt, c: att_tiles(kt, 1, c), carry)
    out_t = jnp.concatenate([acc_refs[h][...] / l_all[h:h + 1, :] for h in range(N_HEADS)], axis=0)
    o_ref[...] = out_t.T.astype(BF16)


def _attend_prompt(q, iq, iwt, kb, vt, ikb, n_seq, seq_len, topk):
    n = q.shape[0]
    t = ATT_TILE
    nq = seq_len // t
    once = pl.Buffered(1)
    return pl.pallas_call(
        functools.partial(_attend_prompt_kernel, topk=topk),
        out_shape=jax.ShapeDtypeStruct((n, ATT_WIDTH), BF16),
        grid=(n_seq, nq),
        in_specs=[pl.BlockSpec((t, ATT_WIDTH), lambda b, j: (b * nq + j, 0)),
                  pl.BlockSpec((t, ATT_WIDTH), lambda b, j: (b * nq + j, 0)),
                  pl.BlockSpec((N_IDX_HEADS, t), lambda b, j: (0, b * nq + j)),
                  pl.BlockSpec((seq_len, ATT_WIDTH), lambda b, j: (b, 0), pipeline_mode=once),
                  pl.BlockSpec((nq, VT_ROWS, t), lambda b, j: (b, 0, 0), pipeline_mode=once),
                  pl.BlockSpec((seq_len, LANES), lambda b, j: (b, 0), pipeline_mode=once)],
        out_specs=pl.BlockSpec((t, ATT_WIDTH), lambda b, j: (b * nq + j, 0)),
        scratch_shapes=([pltpu.VMEM((nq, t, t), I32), pltpu.VMEM((nq, t, t), BF16)]
                        + [pltpu.VMEM((t, LANES), BF16)] * N_HEADS
                        + [pltpu.VMEM((HEAD_DIM, t), F32)] * N_HEADS + [pltpu.VMEM((ATT_GROUP * t, t), BF16)] * N_HEADS),
        compiler_params=_cparams("arbitrary", "arbitrary"),
        name="attend_prompt",
    )(q, iq, iwt, kb, vt, ikb)


SAMPLE_KEY_TILE = 512


def _wide(a, width):
    return a if width == LANES else jnp.concatenate([a] * (width // LANES), axis=1)


def _row_sums(x_bf16):
    return jnp.dot(x_bf16, jnp.ones((x_bf16.shape[1], LANES), BF16), preferred_element_type=F32)


def _count_rows(planes, cand, strict):
    cb = cand.astype(BF16)
    one, zero = jnp.ones((), BF16), jnp.zeros((), BF16)
    accs = [jnp.zeros(cand.shape, BF16) for _ in range(4)]
    i = 0
    for ref in planes:
        e = ref[...]
        for c in range(e.shape[1] // LANES):
            blk = e[:, c * LANES:(c + 1) * LANES]
            accs[i % 4] = accs[i % 4] + jnp.where((blk > cb) if strict else (blk >= cb), one, zero)
            i += 1
    assert i <= 256
    return _row_sums((accs[0] + accs[1]) + (accs[2] + accs[3]))


def _topk_select_rows(keys, planes, topk):
    rows = planes[0].shape[0]

    def search():
        def bit_body(i, d):
            cand = d + lax.shift_left(jnp.int32(1), jnp.int32(DIGIT_BITS - 1) - i).astype(F32)
            return jnp.where(_count_rows(planes, cand, strict=False) >= float(topk), cand, d)

        return lax.fori_loop(0, DIGIT_BITS, bit_body, jnp.zeros((rows, LANES), F32))

    d = search()
    for phase in range(1, N_DIGITS):
        db = d.astype(BF16)
        for kref, pref in zip(keys, planes):
            e = pref[...]
            dw = _wide(db, e.shape[1])
            decided = jnp.where(e > dw, jnp.asarray(DIGIT_ABOVE, BF16), jnp.asarray(DIGIT_BELOW, BF16))
            pref[...] = jnp.where(e == dw, _digit_plane(kref[...], phase), decided)
        d = search()

    cnt_ge = _count_rows(planes, d, strict=False)

    @pl.when(jnp.max(cnt_ge) > float(topk))
    def _():
        need = float(topk) - _count_rows(planes, d, strict=True)
        seen = jnp.zeros((rows, LANES), F32)
        for pref in planes:
            width = pref.shape[1]
            e = pref[...].astype(F32)
            eq = e == _wide(d, width)
            eqb = jnp.where(eq, 1.0, 0.0).astype(BF16)
            r = lax.broadcasted_iota(I32, (width, width), 0)
            c = lax.broadcasted_iota(I32, (width, width), 1)
            before = jnp.where(r < c, 1.0, 0.0).astype(BF16)
            prior = jnp.dot(eqb, before, preferred_element_type=F32) + _wide(seen, width)
            pref[...] = jnp.where(eq & (prior >= _wide(need, width)), DIGIT_BELOW, e).astype(BF16)
            seen = seen + _row_sums(eqb)

    return d


def _attend_sample_kernel(q_ref, iq_ref, iwr_ref, knt_ref, vnt_ref, iknt_ref, ckt_ref, cvt_ref, cit_ref, o_ref,
                          keys_ref, plane_ref, keys_new_ref, plane_new_ref, lt_ref, lt_new_ref, *, topk, past):
    tq = q_ref.shape[0]
    nct, _, tkc = keys_ref.shape
    rows = N_HEADS * tq
    row = lax.broadcasted_iota(I32, (rows, LANES), 0)
    qpos = past + (row & (tq - 1))
    slope = lax.bitcast_convert_type(
        lax.shift_left(126 - lax.shift_right_logical(row, tq.bit_length() - 1), 23), F32)
    qpos16 = past + lax.broadcasted_iota(I32, (tq, LANES), 0)

    iq = iq_ref[...]
    q = q_ref[...]
    iq_rows = jnp.concatenate([iq[:, h * IDX_DIM:(h + 1) * IDX_DIM] for h in range(N_IDX_HEADS)], axis=0)
    q_rows = jnp.concatenate(
        [_half_mask_wide(q, h) for h in range(N_HEADS)], axis=0) * jnp.asarray(ATTN_SCALE, BF16)
    iw_rows = iwr_ref[...]

    tiles = [(kt * tkc, tkc, cit_ref.at[:, kt * tkc:(kt + 1) * tkc], ckt_ref.at[:, kt * tkc:(kt + 1) * tkc],
              cvt_ref.at[:, kt * tkc:(kt + 1) * tkc], keys_ref.at[kt], plane_ref.at[kt], lt_ref.at[kt])
             for kt in range(nct)]
    tiles.append((past, LANES, iknt_ref, knt_ref, vnt_ref, keys_new_ref, plane_new_ref, lt_new_ref))

    def key_positions(k0, width, n_rows):
        return k0 + lax.broadcasted_iota(I32, (n_rows, width), 1)

    for k0, width, ikt, _, _, kref, pref, _ in tiles:
        s = jnp.dot(iq_rows, ikt[...].astype(BF16), preferred_element_type=F32)
        s = jnp.maximum(s, 0.0) * _wide(iw_rows, width)
        score = s[0:tq, :]
        for h in range(1, N_IDX_HEADS):
            score = score + s[h * tq:(h + 1) * tq, :]
        kpos = key_positions(k0, width, tq)
        adm = lax.shift_right_logical(kpos, CHUNK_SHIFT) <= lax.shift_right_logical(_wide(qpos16, width), CHUNK_SHIFT)
        if k0 == past:
            adm = adm & (kpos < past + tq)
        key = _mono_key(jnp.where(adm, score, -jnp.inf))
        kref[...] = key
        pref[...] = _digit_plane(key, 0)

    d_last = _topk_select_rows([t[5] for t in tiles], [t[6] for t in tiles], topk)

    m_part = jnp.full((rows, LANES), -jnp.inf, F32)
    for k0, width, _, kt_ref, _, kref, pref, ltref in tiles:
        sel = (pref[...].astype(F32) >= _wide(d_last, width)) & (kref[...] > jnp.int32(KEY_NEG_INF))
        sel = jnp.concatenate([jnp.where(sel, 1.0, 0.0)] * N_HEADS, axis=0) > 0.5
        dist = jnp.abs(_wide(qpos, width) - key_positions(k0, width, rows)).astype(F32)
        lt = (jnp.dot(q_rows, kt_ref[...].astype(BF16), preferred_element_type=F32)
              - _wide(slope, width) * jnp.where(sel, dist, MASKED_DIST))
        ltref[...] = lt
        for c in range(width // LANES):
            m_part = jnp.maximum(m_part, lt[:, c * LANES:(c + 1) * LANES])
    m = jnp.broadcast_to(jnp.max(m_part, axis=1, keepdims=True), (rows, LANES))

    acc = jnp.zeros((rows, ATT_WIDTH), F32)
    l_part = jnp.zeros((rows, LANES), F32)
    for _, width, _, _, vt_ref, _, _, ltref in tiles:
        p = jnp.exp(ltref[...] - _wide(m, width))
        for c in range(width // LANES):
            l_part = l_part + p[:, c * LANES:(c + 1) * LANES]
        acc = acc + lax.dot_general(p.astype(BF16), vt_ref[...].astype(BF16), _NT, preferred_element_type=F32)
    out = acc / jnp.sum(l_part, axis=1, keepdims=True)

    out_lane_head = lax.shift_right_logical(lax.broadcasted_iota(I32, (tq, ATT_WIDTH), 1), HEAD_SHIFT)
    y = jnp.zeros((tq, ATT_WIDTH), F32)
    for h in range(N_HEADS):
        y = y + jnp.where(out_lane_head == h, out[h * tq:(h + 1) * tq, :], 0.0)
    o_ref[...] = y.astype(BF16)


def _half_mask_wide(x, head):
    lane = lax.broadcasted_iota(I32, x.shape, 1)
    keep = lax.shift_right_logical(lane, HEAD_SHIFT) == head
    return jnp.where(keep, x, jnp.zeros_like(x))


def _attend_sample(q, iq, iw_rows, knt, vnt, iknt, cache_kt, cache_vt, cache_it, topk):
    n_seq, _, past = cache_kt.shape
    tq = q.shape[0] // n_seq
    tkc = SAMPLE_KEY_TILE
    rows = N_HEADS * tq
    assert rows == LANES and past % tkc == 0
    new = lambda width: pl.BlockSpec((tq, width), lambda b: (b, 0))
    per_seq = lambda a: pl.BlockSpec((None,) + a.shape[1:], lambda b: (b, 0, 0))
    return pl.pallas_call(
        functools.partial(_attend_sample_kernel, topk=topk, past=past),
        out_shape=jax.ShapeDtypeStruct((n_seq * tq, ATT_WIDTH), BF16),
        grid=(n_seq,),
        in_specs=[new(ATT_WIDTH), new(ATT_WIDTH), per_seq(iw_rows), per_seq(knt), per_seq(vnt), per_seq(iknt),
                  per_seq(cache_kt), per_seq(cache_vt), per_seq(cache_it)],
        out_specs=new(ATT_WIDTH),
        scratch_shapes=[pltpu.VMEM((past // tkc, tq, tkc), I32), pltpu.VMEM((past // tkc, tq, tkc), BF16),
                        pltpu.VMEM((tq, LANES), I32), pltpu.VMEM((tq, LANES), BF16),
                        pltpu.VMEM((past // tkc, rows, tkc), F32), pltpu.VMEM((rows, LANES), F32)],
        compiler_params=_cparams("arbitrary"),
        name="attend_sample",
    )(q, iq, iw_rows, knt, vnt, iknt, cache_kt, cache_vt, cache_it)


TOK_ROWS = D_MODEL // LANES


def _store_token_tiles(ref, x):
    t = x.shape[0]
    for c in range(TOK_ROWS):
        ref[pl.ds(c, t, stride=TOK_ROWS), :] = x[:, c * LANES:(c + 1) * LANES]


def _load_token_tiles(ref):
    t = ref.shape[0] // TOK_ROWS
    return jnp.concatenate([ref[pl.ds(c, t, stride=TOK_ROWS), :] for c in range(TOK_ROWS)], axis=1)


def _layer_norm(x, g, b):
    mu = jnp.mean(x, axis=-1, keepdims=True)
    xc = x - mu
    var = jnp.mean(xc * xc, axis=-1, keepdims=True)
    return xc * lax.rsqrt(var + LN_EPS) * g + b


def _post_kernel(x_ref, ya_ref, yb_ref, ga_ref, scf_ref, shf_ref, wo_ref, bo_ref, g1_ref, b1_ref,
                 wrh_ref, wrl_ref, br_ref,
                 x1_ref, h2_ref, tope_ref, gate_ref, rank_ref, cnt_ref):
    t = x_ref.shape[0]
    y = (jnp.dot(ya_ref[...], wo_ref[:A_WIDTH, :], preferred_element_type=F32)
         + jnp.dot(yb_ref[...], wo_ref[A_WIDTH:, :], preferred_element_type=F32) + bo_ref[...])
    x1 = _layer_norm(DEEPNORM_ALPHA * x_ref[...] + (1.0 + ga_ref[...]) * y, g1_ref[...], b1_ref[...])
    x1_ref[...] = x1
    h2 = x1 * (1.0 + scf_ref[...]) + shf_ref[...]
    _store_token_tiles(h2_ref, h2)

    hh = h2.astype(BF16)
    hl = (h2 - hh.astype(F32)).astype(BF16)
    logits = (lax.dot_general(wrh_ref[...], hh, _NT, preferred_element_type=F32)
              + lax.dot_general(wrh_ref[...], hl, _NT, preferred_element_type=F32)
              + lax.dot_general(wrl_ref[...], hh, _NT, preferred_element_type=F32) + br_ref[...])
    erow = lax.broadcasted_iota(I32, (N_EXPERTS, t), 0)
    vals, idxs = [], []
    for _ in range(TOP_K):
        v = jnp.max(logits, axis=0, keepdims=True)
        i = jnp.min(jnp.where(logits == v, erow, N_EXPERTS), axis=0, keepdims=True)
        vals.append(v)
        idxs.append(i)
        logits = jnp.where(erow == i, -jnp.inf, logits)
    ex = [jnp.exp(v - vals[0]) for v in vals]
    den = ex[0] + ex[1] + ex[2] + ex[3]
    gate_ref[...] = jnp.concatenate([e / den for e in ex], axis=0)
    tope_ref[...] = jnp.concatenate(idxs, axis=0)

    @pl.when(pl.program_id(0) == 0)
    def _():
        cnt_ref[...] = jnp.zeros_like(cnt_ref)

    hit = jnp.zeros((N_EXPERTS, t), F32)
    for i in idxs:
        hit = hit + jnp.where(erow == i, 1.0, 0.0)
    hitb = hit.astype(BF16)
    r = lax.broadcasted_iota(I32, (t, t), 0)
    c = lax.broadcasted_iota(I32, (t, t), 1)
    earlier = jnp.where(r < c, 1.0, 0.0).astype(BF16)
    before = jnp.dot(hitb, earlier, preferred_element_type=F32) + cnt_ref[...]
    total = jnp.dot(hitb, jnp.ones((t, t), BF16), preferred_element_type=F32)
    rank_ref[...] = jnp.concatenate(
        [jnp.sum(jnp.where(erow == i, before, 0.0), axis=0, keepdims=True) for i in idxs], axis=0).astype(I32)
    cnt_ref[...] += total


def _post(x, ya, yb, ga, scf, shf, w_o, b_o, g1, b1, wrh, wrl, br):
    n, d = x.shape
    t = POST_TILE
    nt = n // t
    if ga.shape[0] == n:
        mod_spec = pl.BlockSpec((t, d), lambda i: (i, 0))
    else:
        tiles_per_seq = nt // ga.shape[0]
        ga, scf, shf = (a.reshape(a.shape[0], 1, d) for a in (ga, scf, shf))
        mod_spec = pl.BlockSpec((None, 1, d), lambda i: (i // tiles_per_seq, 0, 0))
    row = lambda w: pl.BlockSpec((t, w), lambda i: (i, 0))
    col = lambda r: pl.BlockSpec((r, t), lambda i: (0, i))
    const = lambda a: pl.BlockSpec(a.shape, lambda i: (0, 0))
    return pl.pallas_call(
        _post_kernel,
        out_shape=(jax.ShapeDtypeStruct((n, d), F32), jax.ShapeDtypeStruct((n * TOK_ROWS, LANES), F32),
                   jax.ShapeDtypeStruct((TOP_K, n), I32), jax.ShapeDtypeStruct((TOP_K, n), F32),
                   jax.ShapeDtypeStruct((TOP_K, n), I32), jax.ShapeDtypeStruct((N_EXPERTS, t), F32)),
        grid=(nt,),
        in_specs=[row(d), row(A_WIDTH), row(ATT_WIDTH), mod_spec, mod_spec, mod_spec,
                  const(w_o), const(b_o), const(g1), const(b1), const(wrh), const(wrl), const(br)],
        out_specs=(row(d), pl.BlockSpec((t * TOK_ROWS, LANES), lambda i: (i, 0)), col(TOP_K), col(TOP_K), col(TOP_K),
                   pl.BlockSpec((N_EXPERTS, t), lambda i: (0, 0))),
        compiler_params=_cparams("arbitrary"),
        name="post",
    )(x, ya, yb, ga, scf, shf, w_o, b_o, g1, b1, wrh, wrl, br)


def _token_rows(ref, r):
    return ref.at[pl.ds(pl.multiple_of(r * TOK_ROWS, TOK_ROWS), TOK_ROWS)]


def _token_copies_wait(hbm_ref, vmem_ref, sem, n_tokens):
    rows = n_tokens * TOK_ROWS
    pltpu.make_async_copy(hbm_ref.at[pl.ds(0, rows)], vmem_ref.at[pl.ds(0, rows)], sem).wait()


def _dispatch_kernel(zrow_ref, dest_ref, h_ref, xp_ref, zero_ref, sem):
    t = h_ref.shape[0] // TOK_ROWS
    block_rows = zero_ref.shape[0]

    def zero_block(slot0):
        z0 = pl.multiple_of(slot0 * TOK_ROWS, block_rows)
        cp = pltpu.make_async_copy(zero_ref, xp_ref.at[pl.ds(z0, block_rows)], sem)
        cp.start()
        cp.wait()

    @pl.when(pl.program_id(0) == 0)
    def _():
        zero_ref[...] = jnp.zeros_like(zero_ref)
        for e in range(N_EXPERTS):
            @pl.when(zrow_ref[e] >= 0)
            def _():
                zero_block(zrow_ref[e])

        def unused(b, _):
            zero_block(b * (block_rows // TOK_ROWS))
            return 0

        lax.fori_loop(zrow_ref[N_EXPERTS], xp_ref.shape[0] // block_rows, unused, 0)

    def body(i, _):
        for k in range(TOP_K):
            pltpu.make_async_copy(_token_rows(h_ref, i), _token_rows(xp_ref, dest_ref[k, i]), sem).start(priority=k % 2)
        return 0

    lax.fori_loop(0, t, body, 0, unroll=DMA_ISSUE_UNROLL)
    for k in range(TOP_K):
        _token_copies_wait(xp_ref, h_ref, sem, t)


def _dispatch(zrow, dest, h2, n_slots, block):
    t = ROW_TILE
    n = h2.shape[0] // TOK_ROWS
    grid_spec = pltpu.PrefetchScalarGridSpec(
        num_scalar_prefetch=1,
        grid=(n // t,),
        in_specs=[pl.BlockSpec((TOP_K, t), lambda i, z: (0, i), memory_space=pltpu.SMEM),
                  pl.BlockSpec((t * TOK_ROWS, LANES), lambda i, z: (i, 0))],
        out_specs=pl.BlockSpec(memory_space=pl.ANY),
        scratch_shapes=[pltpu.VMEM((block * TOK_ROWS, LANES), F32), pltpu.SemaphoreType.DMA],
    )
    return pl.pallas_call(
        _dispatch_kernel,
        out_shape=jax.ShapeDtypeStruct((n_slots * TOK_ROWS, LANES), F32),
        grid_spec=grid_spec,
        compiler_params=_cparams("arbitrary"),
        name="dispatch",
    )(zrow, dest, h2)


def _experts_kernel(be_ref, bi_ref, nu_ref, x_ref, wg_ref, bg_ref, wu_ref, bu_ref, wd_ref, bd_ref, y_ref,
                    wgb_ref, wub_ref, wdb_ref):
    i = pl.program_id(0)

    @pl.when((i == 0) | (be_ref[i] != be_ref[jnp.maximum(i - 1, 0)]))
    def _():
        wgb_ref[...] = wg_ref[...].astype(BF16)
        wub_ref[...] = wu_ref[...].astype(BF16)
        wdb_ref[...] = wd_ref[...].astype(BF16)

    @pl.when(i < nu_ref[0])
    def _():
        x = _load_token_tiles(x_ref).astype(BF16)
        g = jnp.minimum(jnp.dot(x, wgb_ref[...], preferred_element_type=F32) + bg_ref[...], SWIGLU_LIMIT)
        u = jnp.clip(jnp.dot(x, wub_ref[...], preferred_element_type=F32) + bu_ref[...], -SWIGLU_LIMIT, SWIGLU_LIMIT)
        a = g * jax.nn.sigmoid(SWIGLU_ALPHA * g)
        mid = ((u + 1.0) * a).astype(BF16)
        _store_token_tiles(y_ref, jnp.dot(mid, wdb_ref[...], preferred_element_type=F32) + bd_ref[...])

    @pl.when(pl.program_id(0) >= nu_ref[0])
    def _():
        y_ref[...] = jnp.zeros_like(y_ref)


def _experts(block_e, block_i, n_used, xp, block, wg, bg, wu, bu, wd, bd):
    d, f = wg.shape[1], wg.shape[2]
    nb = xp.shape[0] // (block * TOK_ROWS)
    wspec = lambda a, b: pl.BlockSpec((None, a, b), lambda i, be, bi, nu: (be[i], 0, 0))
    slots = pl.BlockSpec((block * TOK_ROWS, LANES), lambda i, be, bi, nu: (bi[i], 0))
    grid_spec = pltpu.PrefetchScalarGridSpec(
        num_scalar_prefetch=3,
        grid=(nb,),
        in_specs=[slots, wspec(d, f), wspec(1, f), wspec(d, f), wspec(1, f), wspec(f, d), wspec(1, d)],
        out_specs=pl.BlockSpec((block * TOK_ROWS, LANES), lambda i, be, bi, nu: (i, 0)),
        scratch_shapes=[pltpu.VMEM((d, f), BF16), pltpu.VMEM((d, f), BF16), pltpu.VMEM((f, d), BF16)],
    )
    return pl.pallas_call(
        _experts_kernel,
        out_shape=jax.ShapeDtypeStruct(xp.shape, F32),
        grid_spec=grid_spec,
        compiler_params=_cparams("arbitrary"),
        name="experts",
    )(block_e, block_i, n_used, xp, wg, bg, wu, bu, wd, bd)


def _combine_kernel(dest_ref, dest_next_ref, gate_ref, x1_ref, gf_ref, g2_ref, b2_ref, yp_ref, o_ref, buf_ref, sems):
    t = x1_ref.shape[0]
    step = pl.program_id(0)
    slot = step % 2

    def start_gather(dst_ref, into):
        def body(i, _):
            for k in range(TOP_K):
                pltpu.make_async_copy(_token_rows(yp_ref, dst_ref[k, i]), _token_rows(buf_ref.at[into, k], i),
                                      sems.at[into]).start(priority=k % 2)
            return 0

        lax.fori_loop(0, t, body, 0, unroll=DMA_ISSUE_UNROLL)

    @pl.when(step == 0)
    def _():
        start_gather(dest_ref, 0)

    @pl.when(step + 1 < pl.num_programs(0))
    def _():
        start_gather(dest_next_ref, 1 - slot)

    for k in range(TOP_K):
        _token_copies_wait(yp_ref, buf_ref.at[slot, k], sems.at[slot], t)

    gates = jnp.concatenate([gate_ref[...], jnp.zeros((LANES - TOP_K, t), F32)], axis=0).T
    f = gates[:, 0:1] * _load_token_tiles(buf_ref.at[slot, 0])
    for k in range(1, TOP_K):
        f = f + gates[:, k:k + 1] * _load_token_tiles(buf_ref.at[slot, k])
    o_ref[...] = _layer_norm(DEEPNORM_ALPHA * x1_ref[...] + (1.0 + gf_ref[...]) * f, g2_ref[...], b2_ref[...])


def _combine(dest, gates, x1, gf, g2, b2, yp):
    n, d = x1.shape
    t = ROW_TILE
    nt = n // t
    if gf.shape[0] == n:
        mod_spec = pl.BlockSpec((t, d), lambda i: (i, 0))
    else:
        tiles_per_seq = nt // gf.shape[0]
        gf = gf.reshape(gf.shape[0], 1, d)
        mod_spec = pl.BlockSpec((None, 1, d), lambda i: (i // tiles_per_seq, 0, 0))
    return pl.pallas_call(
        _combine_kernel,
        out_shape=jax.ShapeDtypeStruct((n, d), F32),
        grid=(nt,),
        in_specs=[pl.BlockSpec((TOP_K, t), lambda i: (0, i), memory_space=pltpu.SMEM),
                  pl.BlockSpec((TOP_K, t), lambda i: (0, jnp.minimum(i + 1, nt - 1)), memory_space=pltpu.SMEM),
                  pl.BlockSpec((TOP_K, t), lambda i: (0, i)),
                  pl.BlockSpec((t, d), lambda i: (i, 0)), mod_spec,
                  pl.BlockSpec((1, d), lambda i: (0, 0)), pl.BlockSpec((1, d), lambda i: (0, 0)),
                  pl.BlockSpec(memory_space=pl.ANY)],
        out_specs=pl.BlockSpec((t, d), lambda i: (i, 0)),
        scratch_shapes=[pltpu.VMEM((2, TOP_K, t * TOK_ROWS, LANES), F32), pltpu.SemaphoreType.DMA((2,))],
        compiler_params=_cparams("arbitrary"),
        name="combine",
    )(dest, dest, gates, x1, gf, g2, b2, yp)


def _moe(h2, tope, rank, counts, gates, x1, gf, g2, b2, experts_w):
    n = tope.shape[1]
    block = MOE_BLOCK if n * TOP_K // N_EXPERTS >= MOE_BLOCK else MOE_BLOCK_SMALL
    nb = (n * TOP_K + N_EXPERTS * (block - 1) + block - 1) // block
    pcounts = (counts + block - 1) // block * block
    pend = jnp.cumsum(pcounts)
    pstart = pend - pcounts
    eids = jnp.arange(N_EXPERTS, dtype=I32).reshape(N_EXPERTS, 1, 1)
    dest = rank + jnp.sum(jnp.where(tope[None] == eids, pstart.reshape(N_EXPERTS, 1, 1), 0), axis=0).astype(I32)
    n_used = (pend[-1] // block).astype(I32)
    zrow = jnp.concatenate([jnp.where(counts > 0, pend - block, -1), n_used.reshape(1)]).astype(I32)
    blk = jnp.arange(nb, dtype=I32)
    block_i = jnp.minimum(blk, n_used - 1)
    block_e = jnp.minimum(jnp.sum(pend[None, :] <= (block_i * block)[:, None], axis=1), N_EXPERTS - 1).astype(I32)
    xp = _dispatch(zrow, dest, h2, nb * block, block)
    yp = _experts(block_e, block_i, n_used.reshape(1), xp, block, *experts_w)
    return _combine(dest, gates, x1, gf, g2, b2, yp)


def kernel(x_prompt, x_sample, c_prompt, c_sample, cache_k, cache_v, cache_kidx, w_in, a_ln_g, a_ln_b, a_ws, a_bs,
           w_o, b_o, w_c, b_c, ln1_g, ln1_b, ln2_g, ln2_b, w_router, b_router, w_gate, b_gate, w_up, b_up,
           w_down, b_down):
    bp, s, d = x_prompt.shape
    bs, ts, _ = x_sample.shape
    past = cache_k.shape[2]
    np_, ns = bp * s, bs * ts

    wi = w_in[0]
    w_in_p = jnp.zeros((d, IN_COLS_PAD), F32)
    w_in_p = w_in_p.at[:, :OFF_IK].set(wi[:, :OFF_IK])
    w_in_p = w_in_p.at[:, OFF_IK:OFF_IK + IDX_DIM].set(wi[:, OFF_IK:OFF_IK + IDX_DIM])
    w_in_p = w_in_p.at[:, OFF_IK + IDX_DIM:OFF_IK + 2 * IDX_DIM].set(wi[:, OFF_IK:OFF_IK + IDX_DIM])
    w_in_p = w_in_p.at[:, OFF_IW:OFF_IW + N_IDX_HEADS].set(wi[:, OFF_IK + IDX_DIM:OFF_IK + IDX_DIM + N_IDX_HEADS])
    w_in_p = w_in_p.astype(BF16)
    lng, lnb = a_ln_g[0].reshape(1, A_WIDTH), a_ln_b[0].reshape(1, A_WIDTH)
    wtril = jnp.tril(a_ws[0])
    wm_p = wtril.astype(BF16)
    bm_p = jnp.broadcast_to(a_bs[0][:, :, None], (A_GROUPS, A_CHUNK, A_GROUP_DIM)).astype(F32)
    rep = A_CHUNK // ts
    wm_s = jnp.einsum("ab,gij->gaibj", jnp.eye(rep, dtype=F32), wtril[:, :ts, :ts]).reshape(
        A_GROUPS, A_CHUNK, A_CHUNK).astype(BF16)
    bm_s = jnp.broadcast_to(jnp.tile(a_bs[0][:, :ts], (1, rep))[:, :, None], (A_GROUPS, A_CHUNK, A_GROUP_DIM)).astype(F32)
    w_o_b = w_o[0].astype(BF16)
    b_o_r = b_o[0].reshape(1, d)
    wr_t = w_router[0].T
    wrh = wr_t.astype(BF16)
    wrl = (wr_t - wrh.astype(F32)).astype(BF16)
    br = jnp.broadcast_to(b_router[0][:, None], (N_EXPERTS, POST_TILE)).astype(F32)
    experts_w = (w_gate[0], b_gate[0][:, None, :], w_up[0], b_up[0][:, None, :], w_down[0], b_down[0][:, None, :])
    g1, b1 = ln1_g[0].reshape(1, d), ln1_b[0].reshape(1, d)
    g2, b2 = ln2_g[0].reshape(1, d), ln2_b[0].reshape(1, d)

    mods = _cond_mods(jnp.concatenate([c_prompt, c_sample], axis=0), w_c[0], b_c[0]).reshape(bp + bs, 6, d)
    mods_p = [mods[:bp, i] for i in range(6)]
    mods_s = [jnp.repeat(mods[bp:, i], ts, axis=0) for i in range(6)]

    xp2 = x_prompt.reshape(np_, d)
    ya, q, kft, kb, _, vft, vt, iq, ikft, ikb, iwt, _ = _project(
        xp2, mods_p[1], mods_p[0], w_in_p, lng, lnb, wm_p, bm_p, bp)
    yb = _attend_prompt(q, iq, iwt, kb, vt, ikb, bp, s, min(TOPK_MAX, s // 4))
    x1, h2, tope, gates, rank, cnt = _post(xp2, ya, yb, mods_p[2], mods_p[4], mods_p[3], w_o_b, b_o_r, g1, b1, wrh, wrl, br)
    y_p = _moe(h2, tope, rank, cnt[:, 0].astype(I32), gates, x1, mods_p[5], g2, b2, experts_w)
    heads_last = lambda a: jnp.transpose(a.reshape(1, bp, N_HEADS, HEAD_DIM, s), (0, 1, 4, 2, 3))
    out_p = (y_p.reshape(bp, s, d), heads_last(kft), heads_last(vft),
             jnp.transpose(ikft.reshape(1, bp, IDX_DIM, s), (0, 1, 3, 2)))

    xs2 = x_sample.reshape(ns, d)
    ya, q, kft, _, vf, vft, _, iq, ikft, _, iwt, va = _project(
        xs2, mods_s[1], mods_s[0], w_in_p, lng, lnb, wm_s, bm_s, 1)
    kf, ikf = kft[0].T, ikft[0].T
    per_seq_t = lambda a: jnp.pad(jnp.transpose(a[0].reshape(a.shape[1], bs, ts), (1, 0, 2)),
                                  ((0, 0), (0, 0), (0, LANES - ts)))
    iw_rows = jnp.broadcast_to(
        jnp.transpose(iwt.reshape(N_IDX_HEADS, bs, ts), (1, 0, 2)).reshape(bs, N_IDX_HEADS * ts, 1),
        (bs, N_IDX_HEADS * ts, LANES))
    cache_kt = jnp.transpose(cache_k[0], (0, 2, 3, 1)).reshape(bs, ATT_WIDTH, past)
    cache_vt = jnp.transpose(cache_v[0], (0, 2, 3, 1)).reshape(bs, ATT_WIDTH, past)
    cache_it = jnp.transpose(cache_kidx[0], (0, 2, 1))
    yb = _attend_sample(q, iq, iw_rows, per_seq_t(kft), per_seq_t(vft), per_seq_t(ikft), cache_kt, cache_vt, cache_it,
                        min(TOPK_MAX, (past + ts) // 4))
    x1, h2, tope, gates, rank, cnt = _post(xs2, ya, yb, mods_s[2], mods_s[4], mods_s[3], w_o_b, b_o_r, g1, b1, wrh, wrl, br)
    y_s = _moe(h2, tope, rank, cnt[:, 0].astype(I32), gates, x1, mods_s[5], g2, b2, experts_w)

    return (out_p[0], y_s.reshape(bs, ts, d), out_p[1], out_p[2], out_p[3],
            kf.reshape(1, bs, ts, N_HEADS, HEAD_DIM), vf.reshape(1, bs, ts, N_HEADS, HEAD_DIM),
            ikf.reshape(1, bs, ts, IDX_DIM), va.reshape(1, bs, ts, A_WIDTH))
```

```python
import functools

import jax
import jax.numpy as jnp
from jax import lax
from jax.experimental import pallas as pl
from jax.experimental.pallas import tpu as pltpu

F32 = jnp.float32
BF16 = jnp.bfloat16
I32 = jnp.int32

D_MODEL = 1024
CHUNK_SHIFT = 6
A_GROUPS = 4
A_GROUP_DIM = 128
A_WIDTH = A_GROUPS * A_GROUP_DIM
A_CHUNK = 128
N_HEADS = 8
HEAD_DIM = 64
HEAD_SHIFT = 6
ATT_WIDTH = N_HEADS * HEAD_DIM
N_IDX_HEADS = 8
IDX_DIM = 64
TOPK_MAX = 256
ATTN_SCALE = HEAD_DIM ** -0.5
VT_HEAD_ROWS = HEAD_DIM + 16
VT_ROWS = N_HEADS * VT_HEAD_ROWS
IDX_W_SCALE = (N_IDX_HEADS ** -0.5) * (IDX_DIM ** -0.5)
N_EXPERTS = 32
TOP_K = 4
SWIGLU_LIMIT = 7.0
SWIGLU_ALPHA = 1.702
DEEPNORM_ALPHA = 2.0 ** 0.25
LN_EPS = 1e-5

LANES = 128
SUBLANES = 8
VMEM_LIMIT_BYTES = 56 * 1024 * 1024

ROW_TILE = 256
PROJECT_TILE = 512
POST_TILE = 512
MOE_BLOCK = 512
MOE_BLOCK_SMALL = 128
ATT_TILE = 256
SCORE_GROUP = 4
ATT_GROUP = 4
DMA_ISSUE_UNROLL = 8
STATIC_TILE_UNROLL = 4

OFF_AU, OFF_AV, OFF_Q, OFF_K, OFF_V, OFF_IQ = 0, 512, 1024, 1536, 2048, 2560
OFF_IK = 3072
OFF_IW = 3200
IN_COLS_PAD = 3328

MASKED_DIST = 3.0e32
INT32_MIN = -(2 ** 31)
KEY_NEG_INF = INT32_MIN + 0x7FFFFF

_NT = (((1,), (1,)), ((), ()))


def _cparams(*sem):
    return pltpu.CompilerParams(dimension_semantics=sem, vmem_limit_bytes=VMEM_LIMIT_BYTES)


def _mods_kernel(c_ref, w_ref, b_ref, o_ref):
    c = c_ref[...]
    s = c * jax.nn.sigmoid(c)
    o_ref[...] = jnp.dot(s.astype(BF16), w_ref[...].astype(BF16), preferred_element_type=F32) + b_ref[...]


def _cond_mods(c, w_c, b_c):
    nb, d = c.shape
    n_out = w_c.shape[1]
    return pl.pallas_call(
        _mods_kernel,
        out_shape=jax.ShapeDtypeStruct((nb, n_out), F32),
        grid=(n_out // d,),
        in_specs=[pl.BlockSpec((nb, d), lambda j: (0, 0)),
                  pl.BlockSpec((d, d), lambda j: (0, j)),
                  pl.BlockSpec((1, d), lambda j: (0, j))],
        out_specs=pl.BlockSpec((nb, d), lambda j: (0, j)),
        compiler_params=_cparams("arbitrary"),
        name="mods",
    )(c, w_c, b_c.reshape(1, n_out))


def _gelu(x):
    return 0.5 * x * (1.0 + lax.erf(x * 0.7071067811865476))


def _project_kernel(x_ref, sc_ref, sh_ref, w_ref, lng_ref, lnb_ref, wm_ref, bm_ref,
                    ya_ref, q_ref, kft_ref, kb_ref, vf_ref, vft_ref, vt_ref, iq_ref, ikft_ref, ikb_ref, iwt_ref,
                    va_ref):
    t = x_ref.shape[0]
    h = (x_ref[...] * (1.0 + sc_ref[...]) + sh_ref[...]).astype(BF16)

    def proj(c0, n):
        return jnp.dot(h, w_ref[:, c0:c0 + n], preferred_element_type=F32)

    u = _gelu(proj(OFF_AU, A_WIDTH))
    gv = _gelu(proj(OFF_AV, A_WIDTH))
    for g in range(A_GROUPS):
        lo, hi = g * A_GROUP_DIM, (g + 1) * A_GROUP_DIM
        xg = gv[:, lo:hi]
        mu = jnp.mean(xg, axis=-1, keepdims=True)
        xc = xg - mu
        var = jnp.mean(xc * xc, axis=-1, keepdims=True)
        vg = xc * lax.rsqrt(var + LN_EPS) * lng_ref[:, lo:hi] + lnb_ref[:, lo:hi]
        va_ref[:, lo:hi] = vg
        vgb = vg.astype(BF16)
        for c in range(t // A_CHUNK):
            r0, r1 = c * A_CHUNK, (c + 1) * A_CHUNK
            mixed = jnp.dot(wm_ref[g], vgb[r0:r1, :], preferred_element_type=F32) + bm_ref[g]
            ya_ref[r0:r1, lo:hi] = (u[r0:r1, lo:hi] * mixed).astype(BF16)

    q_ref[...] = proj(OFF_Q, ATT_WIDTH).astype(BF16)
    k = proj(OFF_K, ATT_WIDTH)
    kft_ref[...] = k.T
    kb_ref[...] = k.astype(BF16)
    v = proj(OFF_V, ATT_WIDTH)
    vf_ref[...] = v
    v_t32 = v.T
    vft_ref[...] = v_t32
    v_t = v_t32.astype(BF16)
    ones = jnp.ones((VT_HEAD_ROWS - HEAD_DIM, ATT_TILE), BF16)
    for c in range(t // ATT_TILE):
        cols = slice(c * ATT_TILE, (c + 1) * ATT_TILE)
        vt_ref[c] = jnp.concatenate(
            [blk for h in range(N_HEADS) for blk in (v_t[h * HEAD_DIM:(h + 1) * HEAD_DIM, cols], ones)], axis=0)
    iq_ref[...] = proj(OFF_IQ, N_IDX_HEADS * IDX_DIM).astype(BF16)
    ik2 = proj(OFF_IK, LANES)
    ikft_ref[...] = ik2.T[:IDX_DIM, :]
    ikb_ref[...] = ik2.astype(BF16)
    iw = proj(OFF_IW, LANES) * IDX_W_SCALE
    iwt_ref[...] = iw.T[:N_IDX_HEADS, :]


def _project(x, sc, sh, w_in_p, lng, lnb, wm, bm, n_out_seq):
    n, d = x.shape
    t = PROJECT_TILE
    nt = n // t
    tiles_per_out = nt // n_out_seq
    trans = lambda rows: pl.BlockSpec((None, rows, t), lambda i: (i // tiles_per_out, 0, i % tiles_per_out))
    if sc.shape[0] == n:
        mod_spec = pl.BlockSpec((t, d), lambda i: (i, 0))
    else:
        tiles_per_seq = nt // sc.shape[0]
        sc = sc.reshape(sc.shape[0], 1, d)
        sh = sh.reshape(sh.shape[0], 1, d)
        mod_spec = pl.BlockSpec((None, 1, d), lambda i: (i // tiles_per_seq, 0, 0))
    row = lambda w: pl.BlockSpec((t, w), lambda i: (i, 0))
    const2 = lambda a: pl.BlockSpec(a.shape, lambda i: (0, 0))
    const3 = lambda a: pl.BlockSpec(a.shape, lambda i: (0, 0, 0))
    out_shape = (
        jax.ShapeDtypeStruct((n, A_WIDTH), BF16),
        jax.ShapeDtypeStruct((n, ATT_WIDTH), BF16),
        jax.ShapeDtypeStruct((n_out_seq, ATT_WIDTH, n // n_out_seq), F32),
        jax.ShapeDtypeStruct((n, ATT_WIDTH), BF16),
        jax.ShapeDtypeStruct((n, ATT_WIDTH), F32),
        jax.ShapeDtypeStruct((n_out_seq, ATT_WIDTH, n // n_out_seq), F32),
        jax.ShapeDtypeStruct((n // ATT_TILE, VT_ROWS, ATT_TILE), BF16),
        jax.ShapeDtypeStruct((n, ATT_WIDTH), BF16),
        jax.ShapeDtypeStruct((n_out_seq, IDX_DIM, n // n_out_seq), F32),
        jax.ShapeDtypeStruct((n, LANES), BF16),
        jax.ShapeDtypeStruct((N_IDX_HEADS, n), F32),
        jax.ShapeDtypeStruct((n, A_WIDTH), F32),
    )
    out_specs = (row(A_WIDTH), row(ATT_WIDTH), trans(ATT_WIDTH), row(ATT_WIDTH), row(ATT_WIDTH), trans(ATT_WIDTH),
                 pl.BlockSpec((t // ATT_TILE, VT_ROWS, ATT_TILE), lambda i: (i, 0, 0)),
                 row(ATT_WIDTH), trans(IDX_DIM), row(LANES),
                 pl.BlockSpec((N_IDX_HEADS, t), lambda i: (0, i)),
                 row(A_WIDTH))
    return pl.pallas_call(
        _project_kernel,
        out_shape=out_shape,
        grid=(nt,),
        in_specs=[row(d), mod_spec, mod_spec, const2(w_in_p), const2(lng), const2(lnb), const3(wm), const3(bm)],
        out_specs=out_specs,
        compiler_params=_cparams("arbitrary"),
        name="project",
    )(x, sc, sh, w_in_p, lng, lnb, wm, bm)


DIGIT_BITS = 8
N_DIGITS = 32 // DIGIT_BITS
DIGIT_ABOVE = 512.0
DIGIT_BELOW = -1.0
PACKED_ROWS = 16


def _tile_loop(nkt, body, init):
    if isinstance(nkt, int):
        return lax.fori_loop(0, nkt, body, init, unroll=STATIC_TILE_UNROLL)
    g = STATIC_TILE_UNROLL
    groups = nkt // g

    def group_body(i, c):
        for k in range(g):
            c = body(g * i + k, c)
        return c

    return lax.fori_loop(g * groups, nkt, body, lax.fori_loop(0, groups, group_body, init))


def _mono_key(x):
    b = lax.bitcast_convert_type(x, I32)
    return jnp.where(b >= 0, b, b ^ jnp.int32(0x7FFFFFFF))


def _digit_plane(key, phase):
    shift = 32 - DIGIT_BITS * (phase + 1)
    d = lax.shift_right_arithmetic(key, jnp.int32(shift)) if shift else key
    d = d + (1 << (DIGIT_BITS - 1)) if phase == 0 else d & ((1 << DIGIT_BITS) - 1)
    return d.astype(F32).astype(BF16)


def _count_plane(plane_ref, nkt, cand, strict):
    _, tk, w = plane_ref.shape
    cb = cand.astype(BF16)
    one, zero = jnp.ones((), BF16), jnp.zeros((), BF16)

    def body(kt, cnt):
        e = plane_ref[kt]
        accs = [jnp.zeros((PACKED_ROWS, w), BF16) for _ in range(4)]
        for r in range(tk // PACKED_ROWS):
            blk = e[r * PACKED_ROWS:(r + 1) * PACKED_ROWS, :]
            accs[r % 4] = accs[r % 4] + jnp.where((blk > cb) if strict else (blk >= cb), one, zero)
        return cnt + ((accs[0] + accs[1]) + (accs[2] + accs[3])).astype(F32)

    cnt = _tile_loop(nkt, body, jnp.zeros((PACKED_ROWS, w), F32))
    return jnp.sum(cnt, axis=0, keepdims=True)


def _search_digit(plane_ref, nkt, topk, count_at_zero):
    w = plane_ref.shape[2]

    def bit_body(i, carry):
        d, count_at_d = carry
        cand = d + lax.shift_left(jnp.int32(1), jnp.int32(DIGIT_BITS - 1) - i).astype(F32)
        cnt = _count_plane(plane_ref, nkt, cand, strict=False)
        enough = cnt >= float(topk)
        return jnp.where(enough, cand, d), jnp.where(enough, cnt, count_at_d)

    return lax.fori_loop(0, DIGIT_BITS, bit_body, (jnp.zeros((1, w), F32), count_at_zero))


def _topk_select(keys_ref, plane_ref, nkt, topk):
    _, tk, w = plane_ref.shape
    d, cnt_ge = _search_digit(plane_ref, nkt, topk, jnp.full((1, w), nkt * tk, I32).astype(F32))
    for phase in range(1, N_DIGITS):
        db = d.astype(BF16)

        def refine(kt, _, phase=phase, db=db):
            e = plane_ref[kt]
            decided = jnp.where(e > db, jnp.asarray(DIGIT_ABOVE, BF16), jnp.asarray(DIGIT_BELOW, BF16))
            plane_ref[kt] = jnp.where(e == db, _digit_plane(keys_ref[kt], phase), decided)
            return 0

        _tile_loop(nkt, refine, 0)
        d, cnt_ge = _search_digit(plane_ref, nkt, topk, cnt_ge)

    @pl.when(jnp.max(cnt_ge) > float(topk))
    def _():
        need = float(topk) - _count_plane(plane_ref, nkt, d, strict=True)
        r = lax.broadcasted_iota(I32, (tk, tk), 0)
        c = lax.broadcasted_iota(I32, (tk, tk), 1)
        before = jnp.where(c < r, 1.0, 0.0).astype(BF16)

        def body(kt, seen):
            e = plane_ref[kt].astype(F32)
            eq = e == d
            eqf = jnp.where(eq, 1.0, 0.0)
            prior = jnp.dot(before, eqf.astype(BF16), preferred_element_type=F32) + seen
            plane_ref[kt] = jnp.where(eq & (prior >= need), DIGIT_BELOW, e).astype(BF16)
            return seen + jnp.sum(eqf.reshape(tk // SUBLANES, SUBLANES, w), axis=0).sum(axis=0, keepdims=True)

        lax.fori_loop(0, nkt, body, jnp.zeros((1, w), F32))

    return d


def _selected(keys_ref, plane_ref, kt, d):
    return (plane_ref[kt].astype(F32) >= d) & (keys_ref[kt] > jnp.int32(KEY_NEG_INF))


def _half_mask(x_pair, head):
    lane = lax.broadcasted_iota(I32, x_pair.shape, 1)
    keep = (lane >= HEAD_DIM) if head % 2 else (lane < HEAD_DIM)
    return jnp.where(keep, x_pair, jnp.zeros_like(x_pair))


def _attend_prompt_kernel(q_ref, iq_ref, iwt_ref, k_ref, vt_ref, ik_ref, o_ref, keys_ref, plane_ref, *head_refs, topk):
    qh_refs, acc_refs, lt_refs = (head_refs[i * N_HEADS:(i + 1) * N_HEADS] for i in range(3))
    tq = q_ref.shape[0]
    tk = keys_ref.shape[1]
    j = pl.program_id(1)
    nkt = j + 1
    q0 = j * tq
    row = lax.broadcasted_iota(I32, (tk, tq), 0)
    lane = lax.broadcasted_iota(I32, (tk, tq), 1)
    qpos = q0 + lane

    iq = iq_ref[...]
    iqm = [_half_mask(iq[:, (h // 2) * LANES:(h // 2 + 1) * LANES], h) for h in range(N_IDX_HEADS)]
    iw = iwt_ref[...]

    def score_tiles(kt, n, diagonal):
        k0 = pl.multiple_of(kt * tk, tk)
        ikt = ik_ref[pl.ds(k0, n * tk), :]
        s = jnp.zeros((n * tk, tq), F32)
        for h in range(N_IDX_HEADS):
            r = lax.dot_general(ikt, iqm[h], _NT, preferred_element_type=F32)
            s = s + jnp.maximum(r, 0.0) * iw[h:h + 1, :]
        if diagonal:
            adm = lax.shift_right_logical(k0 + row, CHUNK_SHIFT) <= lax.shift_right_logical(qpos, CHUNK_SHIFT)
            s = jnp.where(adm, s, -jnp.inf)
        for c in range(n):
            key = _mono_key(s[c * tk:(c + 1) * tk, :])
            keys_ref[kt + c] = key
            plane_ref[kt + c] = _digit_plane(key, 0)
        return 0

    assert tk == tq
    groups = j // SCORE_GROUP
    lax.fori_loop(0, groups, lambda i, _: score_tiles(SCORE_GROUP * i, SCORE_GROUP, False), 0)
    lax.fori_loop(SCORE_GROUP * groups, j, lambda kt, _: score_tiles(kt, 1, False), 0)
    score_tiles(j, 1, True)

    d_last = _topk_select(keys_ref, plane_ref, nkt, topk)

    def masked_distance(kt):
        dist = jnp.abs(qpos - (kt * tk + row)).astype(F32)
        return jnp.where(_selected(keys_ref, plane_ref, kt, d_last), dist, MASKED_DIST)

    qfull = q_ref[...]
    for h in range(N_HEADS):
        pair = h // 2
        qh_refs[h][...] = _half_mask(qfull[:, pair * LANES:(pair + 1) * LANES], h) * jnp.asarray(ATTN_SCALE, BF16)
        acc_refs[h][...] = jnp.zeros_like(acc_refs[h])

    def att_tiles(kt, n, carry):
        m_all, l_all = carry
        k0 = pl.multiple_of(kt * tk, tk)
        dist = jnp.concatenate([masked_distance(kt + c) for c in range(n)], axis=0)
        ms = []
        for h in range(N_HEADS):
            pair = h // 2
            slope = 2.0 ** (-8.0 * (h + 1) / N_HEADS)
            kp = k_ref[pl.ds(k0, n * tk), pair * LANES:(pair + 1) * LANES]
            lt = lax.dot_general(kp, qh_refs[h][...], _NT, preferred_element_type=F32) - slope * dist
            lt_refs[h][0:n * tk, :] = lt.astype(BF16)
            tile_max = jnp.max(lt, axis=0, keepdims=True).astype(BF16).astype(F32)
            ms.append(jnp.maximum(m_all[h:h + 1, :], tile_max))
        ls = []
        for h in range(N_HEADS):
            alpha = jnp.exp(m_all[h:h + 1, :] - ms[h])
            mb = ms[h].astype(BF16)
            acc = alpha * acc_refs[h][...]
            l = alpha * l_all[h:h + 1, :]
            for c in range(n):
                p = jnp.exp(lt_refs[h][c * tk:(c + 1) * tk, :] - mb)
                pv = jnp.dot(vt_ref[kt + c, h * VT_HEAD_ROWS:(h + 1) * VT_HEAD_ROWS, :], p,
                             preferred_element_type=F32)
                acc = acc + pv[:HEAD_DIM, :]
                l = l + pv[HEAD_DIM:HEAD_DIM + 1, :]
            acc_refs[h][...] = acc
            ls.append(l)
        return jnp.concatenate(ms, axis=0), jnp.concatenate(ls, axis=0)

    groups = nkt // ATT_GROUP
    carry = (jnp.full((N_HEADS, tq), -jnp.inf, F32), jnp.zeros((N_HEADS, tq), F32))
    carry = lax.fori_loop(0, groups, lambda i, c: att_tiles(ATT_GROUP * i, ATT_GROUP, c), carry)
    _, l_all = lax.fori_loop(ATT_GROUP * groups, nkt, lambda kt, c: att_tiles(kt, 1, c), carry)
    out_t = jnp.concatenate([acc_refs[h][...] / l_all[h:h + 1, :] for h in range(N_HEADS)], axis=0)
    o_ref[...] = out_t.T.astype(BF16)


def _attend_prompt(q, iq, iwt, kb, vt, ikb, n_seq, seq_len, topk):
    n = q.shape[0]
    t = ATT_TILE
    nq = seq_len // t
    once = pl.Buffered(1)
    return pl.pallas_call(
        functools.partial(_attend_prompt_kernel, topk=topk),
        out_shape=jax.ShapeDtypeStruct((n, ATT_WIDTH), BF16),
        grid=(n_seq, nq),
        in_specs=[pl.BlockSpec((t, ATT_WIDTH), lambda b, j: (b * nq + j, 0)),
                  pl.BlockSpec((t, ATT_WIDTH), lambda b, j: (b * nq + j, 0)),
                  pl.BlockSpec((N_IDX_HEADS, t), lambda b, j: (0, b * nq + j)),
                  pl.BlockSpec((seq_len, ATT_WIDTH), lambda b, j: (b, 0), pipeline_mode=once),
                  pl.BlockSpec((nq, VT_ROWS, t), lambda b, j: (b, 0, 0), pipeline_mode=once),
                  pl.BlockSpec((seq_len, LANES), lambda b, j: (b, 0), pipeline_mode=once)],
        out_specs=pl.BlockSpec((t, ATT_WIDTH), lambda b, j: (b * nq + j, 0)),
        scratch_shapes=([pltpu.VMEM((nq, t, t), I32), pltpu.VMEM((nq, t, t), BF16)]
                        + [pltpu.VMEM((t, LANES), BF16)] * N_HEADS
                        + [pltpu.VMEM((HEAD_DIM, t), F32)] * N_HEADS + [pltpu.VMEM((ATT_GROUP * t, t), BF16)] * N_HEADS),
        compiler_params=_cparams("arbitrary", "arbitrary"),
        name="attend_prompt",
    )(q, iq, iwt, kb, vt, ikb)


SAMPLE_KEY_TILE = 512


def _wide(a, width):
    return a if width == LANES else jnp.concatenate([a] * (width // LANES), axis=1)


def _row_sums(x_bf16):
    return jnp.dot(x_bf16, jnp.ones((x_bf16.shape[1], LANES), BF16), preferred_element_type=F32)


def _count_rows(planes, cand, strict):
    cb = cand.astype(BF16)
    one, zero = jnp.ones((), BF16), jnp.zeros((), BF16)
    accs = [jnp.zeros(cand.shape, BF16) for _ in range(4)]
    i = 0
    for ref in planes:
        e = ref[...]
        for c in range(e.shape[1] // LANES):
            blk = e[:, c * LANES:(c + 1) * LANES]
            accs[i % 4] = accs[i % 4] + jnp.where((blk > cb) if strict else (blk >= cb), one, zero)
            i += 1
    assert i <= 256
    return _row_sums((accs[0] + accs[1]) + (accs[2] + accs[3]))


def _topk_select_rows(keys, planes, topk):
    rows = planes[0].shape[0]

    def search():
        def bit_body(i, d):
            cand = d + lax.shift_left(jnp.int32(1), jnp.int32(DIGIT_BITS - 1) - i).astype(F32)
            return jnp.where(_count_rows(planes, cand, strict=False) >= float(topk), cand, d)

        return lax.fori_loop(0, DIGIT_BITS, bit_body, jnp.zeros((rows, LANES), F32))

    d = search()
    for phase in range(1, N_DIGITS):
        db = d.astype(BF16)
        for kref, pref in zip(keys, planes):
            e = pref[...]
            dw = _wide(db, e.shape[1])
            decided = jnp.where(e > dw, jnp.asarray(DIGIT_ABOVE, BF16), jnp.asarray(DIGIT_BELOW, BF16))
            pref[...] = jnp.where(e == dw, _digit_plane(kref[...], phase), decided)
        d = search()

    cnt_ge = _count_rows(planes, d, strict=False)

    @pl.when(jnp.max(cnt_ge) > float(topk))
    def _():
        need = float(topk) - _count_rows(planes, d, strict=True)
        seen = jnp.zeros((rows, LANES), F32)
        for pref in planes:
            width = pref.shape[1]
            e = pref[...].astype(F32)
            eq = e == _wide(d, width)
            eqb = jnp.where(eq, 1.0, 0.0).astype(BF16)
            r = lax.broadcasted_iota(I32, (width, width), 0)
            c = lax.broadcasted_iota(I32, (width, width), 1)
            before = jnp.where(r < c, 1.0, 0.0).astype(BF16)
            prior = jnp.dot(eqb, before, preferred_element_type=F32) + _wide(seen, width)
            pref[...] = jnp.where(eq & (prior >= _wide(need, width)), DIGIT_BELOW, e).astype(BF16)
            seen = seen + _row_sums(eqb)

    return d


def _attend_sample_kernel(q_ref, iq_ref, iwr_ref, knt_ref, vnt_ref, iknt_ref, ckt_ref, cvt_ref, cit_ref, o_ref,
                          keys_ref, plane_ref, keys_new_ref, plane_new_ref, lt_ref, lt_new_ref, *, topk, past):
    tq = q_ref.shape[0]
    nct, _, tkc = keys_ref.shape
    rows = N_HEADS * tq
    row = lax.broadcasted_iota(I32, (rows, LANES), 0)
    qpos = past + (row & (tq - 1))
    slope = lax.bitcast_convert_type(
        lax.shift_left(126 - lax.shift_right_logical(row, tq.bit_length() - 1), 23), F32)
    qpos16 = past + lax.broadcasted_iota(I32, (tq, LANES), 0)

    iq = iq_ref[...]
    q = q_ref[...]
    iq_rows = jnp.concatenate([iq[:, h * IDX_DIM:(h + 1) * IDX_DIM] for h in range(N_IDX_HEADS)], axis=0)
    q_rows = jnp.concatenate(
        [_half_mask_wide(q, h) for h in range(N_HEADS)], axis=0) * jnp.asarray(ATTN_SCALE, BF16)
    iw_rows = iwr_ref[...]

    tiles = [(kt * tkc, tkc, cit_ref.at[:, kt * tkc:(kt + 1) * tkc], ckt_ref.at[:, kt * tkc:(kt + 1) * tkc],
              cvt_ref.at[:, kt * tkc:(kt + 1) * tkc], keys_ref.at[kt], plane_ref.at[kt], lt_ref.at[kt])
             for kt in range(nct)]
    tiles.append((past, LANES, iknt_ref, knt_ref, vnt_ref, keys_new_ref, plane_new_ref, lt_new_ref))

    def key_positions(k0, width, n_rows):
        return k0 + lax.broadcasted_iota(I32, (n_rows, width), 1)

    for k0, width, ikt, _, _, kref, pref, _ in tiles:
        s = jnp.dot(iq_rows, ikt[...].astype(BF16), preferred_element_type=F32)
        s = jnp.maximum(s, 0.0) * _wide(iw_rows, width)
        score = s[0:tq, :]
        for h in range(1, N_IDX_HEADS):
            score = score + s[h * tq:(h + 1) * tq, :]
        kpos = key_positions(k0, width, tq)
        adm = lax.shift_right_logical(kpos, CHUNK_SHIFT) <= lax.shift_right_logical(_wide(qpos16, width), CHUNK_SHIFT)
        if k0 == past:
            adm = adm & (kpos < past + tq)
        key = _mono_key(jnp.where(adm, score, -jnp.inf))
        kref[...] = key
        pref[...] = _digit_plane(key, 0)

    d_last = _topk_select_rows([t[5] for t in tiles], [t[6] for t in tiles], topk)

    m_part = jnp.full((rows, LANES), -jnp.inf, F32)
    for k0, width, _, kt_ref, _, kref, pref, ltref in tiles:
        sel = (pref[...].astype(F32) >= _wide(d_last, width)) & (kref[...] > jnp.int32(KEY_NEG_INF))
        sel = jnp.concatenate([jnp.where(sel, 1.0, 0.0)] * N_HEADS, axis=0) > 0.5
        dist = jnp.abs(_wide(qpos, width) - key_positions(k0, width, rows)).astype(F32)
        lt = (jnp.dot(q_rows, kt_ref[...].astype(BF16), preferred_element_type=F32)
              - _wide(slope, width) * jnp.where(sel, dist, MASKED_DIST))
        ltref[...] = lt
        for c in range(width // LANES):
            m_part = jnp.maximum(m_part, lt[:, c * LANES:(c + 1) * LANES])
    m = jnp.broadcast_to(jnp.max(m_part, axis=1, keepdims=True), (rows, LANES))

    acc = jnp.zeros((rows, ATT_WIDTH), F32)
    l_part = jnp.zeros((rows, LANES), F32)
    for _, width, _, _, vt_ref, _, _, ltref in tiles:
        p = jnp.exp(ltref[...] - _wide(m, width))
        for c in range(width // LANES):
            l_part = l_part + p[:, c * LANES:(c + 1) * LANES]
        acc = acc + lax.dot_general(p.astype(BF16), vt_ref[...].astype(BF16), _NT, preferred_element_type=F32)
    out = acc / jnp.sum(l_part, axis=1, keepdims=True)

    out_lane_head = lax.shift_right_logical(lax.broadcasted_iota(I32, (tq, ATT_WIDTH), 1), HEAD_SHIFT)
    y = jnp.zeros((tq, ATT_WIDTH), F32)
    for h in range(N_HEADS):
        y = y + jnp.where(out_lane_head == h, out[h * tq:(h + 1) * tq, :], 0.0)
    o_ref[...] = y.astype(BF16)


def _half_mask_wide(x, head):
    lane = lax.broadcasted_iota(I32, x.shape, 1)
    keep = lax.shift_right_logical(lane, HEAD_SHIFT) == head
    return jnp.where(keep, x, jnp.zeros_like(x))


def _attend_sample(q, iq, iw_rows, knt, vnt, iknt, cache_kt, cache_vt, cache_it, topk):
    n_seq, _, past = cache_kt.shape
    tq = q.shape[0] // n_seq
    tkc = SAMPLE_KEY_TILE
    rows = N_HEADS * tq
    assert rows == LANES and past % tkc == 0
    new = lambda width: pl.BlockSpec((tq, width), lambda b: (b, 0))
    per_seq = lambda a: pl.BlockSpec((None,) + a.shape[1:], lambda b: (b, 0, 0))
    return pl.pallas_call(
        functools.partial(_attend_sample_kernel, topk=topk, past=past),
        out_shape=jax.ShapeDtypeStruct((n_seq * tq, ATT_WIDTH), BF16),
        grid=(n_seq,),
        in_specs=[new(ATT_WIDTH), new(ATT_WIDTH), per_seq(iw_rows), per_seq(knt), per_seq(vnt), per_seq(iknt),
                  per_seq(cache_kt), per_seq(cache_vt), per_seq(cache_it)],
        out_specs=new(ATT_WIDTH),
        scratch_shapes=[pltpu.VMEM((past // tkc, tq, tkc), I32), pltpu.VMEM((past // tkc, tq, tkc), BF16),
                        pltpu.VMEM((tq, LANES), I32), pltpu.VMEM((tq, LANES), BF16),
                        pltpu.VMEM((past // tkc, rows, tkc), F32), pltpu.VMEM((rows, LANES), F32)],
        compiler_params=_cparams("arbitrary"),
        name="attend_sample",
    )(q, iq, iw_rows, knt, vnt, iknt, cache_kt, cache_vt, cache_it)


TOK_ROWS = D_MODEL // LANES


def _store_token_tiles(ref, x):
    t = x.shape[0]
    for c in range(TOK_ROWS):
        ref[pl.ds(c, t, stride=TOK_ROWS), :] = x[:, c * LANES:(c + 1) * LANES]


def _load_token_tiles(ref):
    t = ref.shape[0] // TOK_ROWS
    return jnp.concatenate([ref[pl.ds(c, t, stride=TOK_ROWS), :] for c in range(TOK_ROWS)], axis=1)


def _layer_norm(x, g, b):
    mu = jnp.mean(x, axis=-1, keepdims=True)
    xc = x - mu
    var = jnp.mean(xc * xc, axis=-1, keepdims=True)
    return xc * lax.rsqrt(var + LN_EPS) * g + b


def _post_kernel(x_ref, ya_ref, yb_ref, ga_ref, scf_ref, shf_ref, wo_ref, bo_ref, g1_ref, b1_ref,
                 wrh_ref, wrl_ref, br_ref,
                 x1_ref, h2_ref, tope_ref, gate_ref, rank_ref, cnt_ref):
    t = x_ref.shape[0]
    y = (jnp.dot(ya_ref[...], wo_ref[:A_WIDTH, :], preferred_element_type=F32)
         + jnp.dot(yb_ref[...], wo_ref[A_WIDTH:, :], preferred_element_type=F32) + bo_ref[...])
    x1 = _layer_norm(DEEPNORM_ALPHA * x_ref[...] + (1.0 + ga_ref[...]) * y, g1_ref[...], b1_ref[...])
    x1_ref[...] = x1
    h2 = x1 * (1.0 + scf_ref[...]) + shf_ref[...]
    _store_token_tiles(h2_ref, h2)

    hh = h2.astype(BF16)
    hl = (h2 - hh.astype(F32)).astype(BF16)
    logits = (lax.dot_general(wrh_ref[...], hh, _NT, preferred_element_type=F32)
              + lax.dot_general(wrh_ref[...], hl, _NT, preferred_element_type=F32)
              + lax.dot_general(wrl_ref[...], hh, _NT, preferred_element_type=F32) + br_ref[...])
    erow = lax.broadcasted_iota(I32, (N_EXPERTS, t), 0)
    vals, idxs = [], []
    for _ in range(TOP_K):
        v = jnp.max(logits, axis=0, keepdims=True)
        i = jnp.min(jnp.where(logits == v, erow, N_EXPERTS), axis=0, keepdims=True)
        vals.append(v)
        idxs.append(i)
        logits = jnp.where(erow == i, -jnp.inf, logits)
    ex = [jnp.exp(v - vals[0]) for v in vals]
    den = ex[0] + ex[1] + ex[2] + ex[3]
    gate_ref[...] = jnp.concatenate([e / den for e in ex], axis=0)
    tope_ref[...] = jnp.concatenate(idxs, axis=0)

    @pl.when(pl.program_id(0) == 0)
    def _():
        cnt_ref[...] = jnp.zeros_like(cnt_ref)

    hit = jnp.zeros((N_EXPERTS, t), F32)
    for i in idxs:
        hit = hit + jnp.where(erow == i, 1.0, 0.0)
    hitb = hit.astype(BF16)
    r = lax.broadcasted_iota(I32, (t, t), 0)
    c = lax.broadcasted_iota(I32, (t, t), 1)
    earlier = jnp.where(r < c, 1.0, 0.0).astype(BF16)
    before = jnp.dot(hitb, earlier, preferred_element_type=F32) + cnt_ref[...]
    total = jnp.dot(hitb, jnp.ones((t, t), BF16), preferred_element_type=F32)
    rank_ref[...] = jnp.concatenate(
        [jnp.sum(jnp.where(erow == i, before, 0.0), axis=0, keepdims=True) for i in idxs], axis=0).astype(I32)
    cnt_ref[...] += total


def _post(x, ya, yb, ga, scf, shf, w_o, b_o, g1, b1, wrh, wrl, br):
    n, d = x.shape
    t = POST_TILE
    nt = n // t
    if ga.shape[0] == n:
        mod_spec = pl.BlockSpec((t, d), lambda i: (i, 0))
    else:
        tiles_per_seq = nt // ga.shape[0]
        ga, scf, shf = (a.reshape(a.shape[0], 1, d) for a in (ga, scf, shf))
        mod_spec = pl.BlockSpec((None, 1, d), lambda i: (i // tiles_per_seq, 0, 0))
    row = lambda w: pl.BlockSpec((t, w), lambda i: (i, 0))
    col = lambda r: pl.BlockSpec((r, t), lambda i: (0, i))
    const = lambda a: pl.BlockSpec(a.shape, lambda i: (0, 0))
    return pl.pallas_call(
        _post_kernel,
        out_shape=(jax.ShapeDtypeStruct((n, d), F32), jax.ShapeDtypeStruct((n * TOK_ROWS, LANES), F32),
                   jax.ShapeDtypeStruct((TOP_K, n), I32), jax.ShapeDtypeStruct((TOP_K, n), F32),
                   jax.ShapeDtypeStruct((TOP_K, n), I32), jax.ShapeDtypeStruct((N_EXPERTS, t), F32)),
        grid=(nt,),
        in_specs=[row(d), row(A_WIDTH), row(ATT_WIDTH), mod_spec, mod_spec, mod_spec,
                  const(w_o), const(b_o), const(g1), const(b1), const(wrh), const(wrl), const(br)],
        out_specs=(row(d), pl.BlockSpec((t * TOK_ROWS, LANES), lambda i: (i, 0)), col(TOP_K), col(TOP_K), col(TOP_K),
                   pl.BlockSpec((N_EXPERTS, t), lambda i: (0, 0))),
        compiler_params=_cparams("arbitrary"),
        name="post",
    )(x, ya, yb, ga, scf, shf, w_o, b_o, g1, b1, wrh, wrl, br)


def _token_rows(ref, r):
    return ref.at[pl.ds(pl.multiple_of(r * TOK_ROWS, TOK_ROWS), TOK_ROWS)]


def _token_copies_wait(hbm_ref, vmem_ref, sem, n_tokens):
    rows = n_tokens * TOK_ROWS
    pltpu.make_async_copy(hbm_ref.at[pl.ds(0, rows)], vmem_ref.at[pl.ds(0, rows)], sem).wait()


def _dispatch_kernel(zrow_ref, dest_ref, h_ref, xp_ref, zero_ref, sem):
    t = h_ref.shape[0] // TOK_ROWS
    block_rows = zero_ref.shape[0]

    def zero_block(slot0):
        z0 = pl.multiple_of(slot0 * TOK_ROWS, block_rows)
        cp = pltpu.make_async_copy(zero_ref, xp_ref.at[pl.ds(z0, block_rows)], sem)
        cp.start()
        cp.wait()

    @pl.when(pl.program_id(0) == 0)
    def _():
        zero_ref[...] = jnp.zeros_like(zero_ref)
        for e in range(N_EXPERTS):
            @pl.when(zrow_ref[e] >= 0)
            def _():
                zero_block(zrow_ref[e])

        def unused(b, _):
            zero_block(b * (block_rows // TOK_ROWS))
            return 0

        lax.fori_loop(zrow_ref[N_EXPERTS], xp_ref.shape[0] // block_rows, unused, 0)

    def body(i, _):
        for k in range(TOP_K):
            pltpu.make_async_copy(_token_rows(h_ref, i), _token_rows(xp_ref, dest_ref[k, i]), sem).start(priority=k % 2)
        return 0

    lax.fori_loop(0, t, body, 0, unroll=DMA_ISSUE_UNROLL)
    for k in range(TOP_K):
        _token_copies_wait(xp_ref, h_ref, sem, t)


def _dispatch(zrow, dest, h2, n_slots, block):
    t = ROW_TILE
    n = h2.shape[0] // TOK_ROWS
    grid_spec = pltpu.PrefetchScalarGridSpec(
        num_scalar_prefetch=1,
        grid=(n // t,),
        in_specs=[pl.BlockSpec((TOP_K, t), lambda i, z: (0, i), memory_space=pltpu.SMEM),
                  pl.BlockSpec((t * TOK_ROWS, LANES), lambda i, z: (i, 0))],
        out_specs=pl.BlockSpec(memory_space=pl.ANY),
        scratch_shapes=[pltpu.VMEM((block * TOK_ROWS, LANES), F32), pltpu.SemaphoreType.DMA],
    )
    return pl.pallas_call(
        _dispatch_kernel,
        out_shape=jax.ShapeDtypeStruct((n_slots * TOK_ROWS, LANES), F32),
        grid_spec=grid_spec,
        compiler_params=_cparams("arbitrary"),
        name="dispatch",
    )(zrow, dest, h2)


def _experts_kernel(be_ref, bi_ref, nu_ref, x_ref, wg_ref, bg_ref, wu_ref, bu_ref, wd_ref, bd_ref, y_ref,
                    wgb_ref, wub_ref, wdb_ref):
    i = pl.program_id(0)

    @pl.when((i == 0) | (be_ref[i] != be_ref[jnp.maximum(i - 1, 0)]))
    def _():
        wgb_ref[...] = wg_ref[...].astype(BF16)
        wub_ref[...] = wu_ref[...].astype(BF16)
        wdb_ref[...] = wd_ref[...].astype(BF16)

    @pl.when(i < nu_ref[0])
    def _():
        x = _load_token_tiles(x_ref).astype(BF16)
        g = jnp.minimum(jnp.dot(x, wgb_ref[...], preferred_element_type=F32) + bg_ref[...], SWIGLU_LIMIT)
        u = jnp.clip(jnp.dot(x, wub_ref[...], preferred_element_type=F32) + bu_ref[...], -SWIGLU_LIMIT, SWIGLU_LIMIT)
        a = g * jax.nn.sigmoid(SWIGLU_ALPHA * g)
        mid = ((u + 1.0) * a).astype(BF16)
        _store_token_tiles(y_ref, jnp.dot(mid, wdb_ref[...], preferred_element_type=F32) + bd_ref[...])

    @pl.when(pl.program_id(0) >= nu_ref[0])
    def _():
        y_ref[...] = jnp.zeros_like(y_ref)


def _experts(block_e, block_i, n_used, xp, block, wg, bg, wu, bu, wd, bd):
    d, f = wg.shape[1], wg.shape[2]
    nb = xp.shape[0] // (block * TOK_ROWS)
    wspec = lambda a, b: pl.BlockSpec((None, a, b), lambda i, be, bi, nu: (be[i], 0, 0))
    slots = pl.BlockSpec((block * TOK_ROWS, LANES), lambda i, be, bi, nu: (bi[i], 0))
    grid_spec = pltpu.PrefetchScalarGridSpec(
        num_scalar_prefetch=3,
        grid=(nb,),
        in_specs=[slots, wspec(d, f), wspec(1, f), wspec(d, f), wspec(1, f), wspec(f, d), wspec(1, d)],
        out_specs=pl.BlockSpec((block * TOK_ROWS, LANES), lambda i, be, bi, nu: (i, 0)),
        scratch_shapes=[pltpu.VMEM((d, f), BF16), pltpu.VMEM((d, f), BF16), pltpu.VMEM((f, d), BF16)],
    )
    return pl.pallas_call(
        _experts_kernel,
        out_shape=jax.ShapeDtypeStruct(xp.shape, F32),
        grid_spec=grid_spec,
        compiler_params=_cparams("arbitrary"),
        name="experts",
    )(block_e, block_i, n_used, xp, wg, bg, wu, bu, wd, bd)


def _combine_kernel(dest_ref, dest_next_ref, gate_ref, x1_ref, gf_ref, g2_ref, b2_ref, yp_ref, o_ref, buf_ref, sems):
    t = x1_ref.shape[0]
    step = pl.program_id(0)
    slot = step % 2

    def start_gather(dst_ref, into):
        def body(i, _):
            for k in range(TOP_K):
                pltpu.make_async_copy(_token_rows(yp_ref, dst_ref[k, i]), _token_rows(buf_ref.at[into, k], i),
                                      sems.at[into]).start(priority=k % 2)
            return 0

        lax.fori_loop(0, t, body, 0, unroll=DMA_ISSUE_UNROLL)

    @pl.when(step == 0)
    def _():
        start_gather(dest_ref, 0)

    @pl.when(step + 1 < pl.num_programs(0))
    def _():
        start_gather(dest_next_ref, 1 - slot)

    for k in range(TOP_K):
        _token_copies_wait(yp_ref, buf_ref.at[slot, k], sems.at[slot], t)

    gates = jnp.concatenate([gate_ref[...], jnp.zeros((LANES - TOP_K, t), F32)], axis=0).T
    f = gates[:, 0:1] * _load_token_tiles(buf_ref.at[slot, 0])
    for k in range(1, TOP_K):
        f = f + gates[:, k:k + 1] * _load_token_tiles(buf_ref.at[slot, k])
    o_ref[...] = _layer_norm(DEEPNORM_ALPHA * x1_ref[...] + (1.0 + gf_ref[...]) * f, g2_ref[...], b2_ref[...])


def _combine(dest, gates, x1, gf, g2, b2, yp):
    n, d = x1.shape
    t = ROW_TILE
    nt = n // t
    if gf.shape[0] == n:
        mod_spec = pl.BlockSpec((t, d), lambda i: (i, 0))
    else:
        tiles_per_seq = nt // gf.shape[0]
        gf = gf.reshape(gf.shape[0], 1, d)
        mod_spec = pl.BlockSpec((None, 1, d), lambda i: (i // tiles_per_seq, 0, 0))
    return pl.pallas_call(
        _combine_kernel,
        out_shape=jax.ShapeDtypeStruct((n, d), F32),
        grid=(nt,),
        in_specs=[pl.BlockSpec((TOP_K, t), lambda i: (0, i), memory_space=pltpu.SMEM),
                  pl.BlockSpec((TOP_K, t), lambda i: (0, jnp.minimum(i + 1, nt - 1)), memory_space=pltpu.SMEM),
                  pl.BlockSpec((TOP_K, t), lambda i: (0, i)),
                  pl.BlockSpec((t, d), lambda i: (i, 0)), mod_spec,
                  pl.BlockSpec((1, d), lambda i: (0, 0)), pl.BlockSpec((1, d), lambda i: (0, 0)),
                  pl.BlockSpec(memory_space=pl.ANY)],
        out_specs=pl.BlockSpec((t, d), lambda i: (i, 0)),
        scratch_shapes=[pltpu.VMEM((2, TOP_K, t * TOK_ROWS, LANES), F32), pltpu.SemaphoreType.DMA((2,))],
        compiler_params=_cparams("arbitrary"),
        name="combine",
    )(dest, dest, gates, x1, gf, g2, b2, yp)


def _moe(h2, tope, rank, counts, gates, x1, gf, g2, b2, experts_w):
    n = tope.shape[1]
    block = MOE_BLOCK if n * TOP_K // N_EXPERTS >= MOE_BLOCK else MOE_BLOCK_SMALL
    nb = (n * TOP_K + N_EXPERTS * (block - 1) + block - 1) // block
    pcounts = (counts + block - 1) // block * block
    pend = jnp.cumsum(pcounts)
    pstart = pend - pcounts
    eids = jnp.arange(N_EXPERTS, dtype=I32).reshape(N_EXPERTS, 1, 1)
    dest = rank + jnp.sum(jnp.where(tope[None] == eids, pstart.reshape(N_EXPERTS, 1, 1), 0), axis=0).astype(I32)
    n_used = (pend[-1] // block).astype(I32)
    zrow = jnp.concatenate([jnp.where(counts > 0, pend - block, -1), n_used.reshape(1)]).astype(I32)
    blk = jnp.arange(nb, dtype=I32)
    block_i = jnp.minimum(blk, n_used - 1)
    block_e = jnp.minimum(jnp.sum(pend[None, :] <= (block_i * block)[:, None], axis=1), N_EXPERTS - 1).astype(I32)
    xp = _dispatch(zrow, dest, h2, nb * block, block)
    yp = _experts(block_e, block_i, n_used.reshape(1), xp, block, *experts_w)
    return _combine(dest, gates, x1, gf, g2, b2, yp)


def kernel(x_prompt, x_sample, c_prompt, c_sample, cache_k, cache_v, cache_kidx, w_in, a_ln_g, a_ln_b, a_ws, a_bs,
           w_o, b_o, w_c, b_c, ln1_g, ln1_b, ln2_g, ln2_b, w_router, b_router, w_gate, b_gate, w_up, b_up,
           w_down, b_down):
    bp, s, d = x_prompt.shape
    bs, ts, _ = x_sample.shape
    past = cache_k.shape[2]
    np_, ns = bp * s, bs * ts

    wi = w_in[0]
    w_in_p = jnp.zeros((d, IN_COLS_PAD), F32)
    w_in_p = w_in_p.at[:, :OFF_IK].set(wi[:, :OFF_IK])
    w_in_p = w_in_p.at[:, OFF_IK:OFF_IK + IDX_DIM].set(wi[:, OFF_IK:OFF_IK + IDX_DIM])
    w_in_p = w_in_p.at[:, OFF_IK + IDX_DIM:OFF_IK + 2 * IDX_DIM].set(wi[:, OFF_IK:OFF_IK + IDX_DIM])
    w_in_p = w_in_p.at[:, OFF_IW:OFF_IW + N_IDX_HEADS].set(wi[:, OFF_IK + IDX_DIM:OFF_IK + IDX_DIM + N_IDX_HEADS])
    w_in_p = w_in_p.astype(BF16)
    lng, lnb = a_ln_g[0].reshape(1, A_WIDTH), a_ln_b[0].reshape(1, A_WIDTH)
    wtril = jnp.tril(a_ws[0])
    wm_p = wtril.astype(BF16)
    bm_p = jnp.broadcast_to(a_bs[0][:, :, None], (A_GROUPS, A_CHUNK, A_GROUP_DIM)).astype(F32)
    rep = A_CHUNK // ts
    wm_s = jnp.einsum("ab,gij->gaibj", jnp.eye(rep, dtype=F32), wtril[:, :ts, :ts]).reshape(
        A_GROUPS, A_CHUNK, A_CHUNK).astype(BF16)
    bm_s = jnp.broadcast_to(jnp.tile(a_bs[0][:, :ts], (1, rep))[:, :, None], (A_GROUPS, A_CHUNK, A_GROUP_DIM)).astype(F32)
    w_o_b = w_o[0].astype(BF16)
    b_o_r = b_o[0].reshape(1, d)
    wr_t = w_router[0].T
    wrh = wr_t.astype(BF16)
    wrl = (wr_t - wrh.astype(F32)).astype(BF16)
    br = jnp.broadcast_to(b_router[0][:, None], (N_EXPERTS, POST_TILE)).astype(F32)
    experts_w = (w_gate[0], b_gate[0][:, None, :], w_up[0], b_up[0][:, None, :], w_down[0], b_down[0][:, None, :])
    g1, b1 = ln1_g[0].reshape(1, d), ln1_b[0].reshape(1, d)
    g2, b2 = ln2_g[0].reshape(1, d), ln2_b[0].reshape(1, d)

    mods = _cond_mods(jnp.concatenate([c_prompt, c_sample], axis=0), w_c[0], b_c[0]).reshape(bp + bs, 6, d)
    mods_p = [mods[:bp, i] for i in range(6)]
    mods_s = [jnp.repeat(mods[bp:, i], ts, axis=0) for i in range(6)]

    xp2 = x_prompt.reshape(np_, d)
    ya, q, kft, kb, _, vft, vt, iq, ikft, ikb, iwt, _ = _project(
        xp2, mods_p[1], mods_p[0], w_in_p, lng, lnb, wm_p, bm_p, bp)
    yb = _attend_prompt(q, iq, iwt, kb, vt, ikb, bp, s, min(TOPK_MAX, s // 4))
    x1, h2, tope, gates, rank, cnt = _post(xp2, ya, yb, mods_p[2], mods_p[4], mods_p[3], w_o_b, b_o_r, g1, b1, wrh, wrl, br)
    y_p = _moe(h2, tope, rank, cnt[:, 0].astype(I32), gates, x1, mods_p[5], g2, b2, experts_w)
    heads_last = lambda a: jnp.transpose(a.reshape(1, bp, N_HEADS, HEAD_DIM, s), (0, 1, 4, 2, 3))
    out_p = (y_p.reshape(bp, s, d), heads_last(kft), heads_last(vft),
             jnp.transpose(ikft.reshape(1, bp, IDX_DIM, s), (0, 1, 3, 2)))

    xs2 = x_sample.reshape(ns, d)
    ya, q, kft, _, vf, vft, _, iq, ikft, _, iwt, va = _project(
        xs2, mods_s[1], mods_s[0], w_in_p, lng, lnb, wm_s, bm_s, 1)
    kf, ikf = kft[0].T, ikft[0].T
    per_seq_t = lambda a: jnp.pad(jnp.transpose(a[0].reshape(a.shape[1], bs, ts), (1, 0, 2)),
                                  ((0, 0), (0, 0), (0, LANES - ts)))
    iw_rows = jnp.broadcast_to(
        jnp.transpose(iwt.reshape(N_IDX_HEADS, bs, ts), (1, 0, 2)).reshape(bs, N_IDX_HEADS * ts, 1),
        (bs, N_IDX_HEADS * ts, LANES))
    cache_kt = jnp.transpose(cache_k[0], (0, 2, 3, 1)).reshape(bs, ATT_WIDTH, past)
    cache_vt = jnp.transpose(cache_v[0], (0, 2, 3, 1)).reshape(bs, ATT_WIDTH, past)
    cache_it = jnp.transpose(cache_kidx[0], (0, 2, 1))
    yb = _attend_sample(q, iq, iw_rows, per_seq_t(kft), per_seq_t(vft), per_seq_t(ikft), cache_kt, cache_vt, cache_it,
                        min(TOPK_MAX, (past + ts) // 4))
    x1, h2, tope, gates, rank, cnt = _post(xs2, ya, yb, mods_s[2], mods_s[4], mods_s[3], w_o_b, b_o_r, g1, b1, wrh, wrl, br)
    y_s = _moe(h2, tope, rank, cnt[:, 0].astype(I32), gates, x1, mods_s[5], g2, b2, experts_w)

    return (out_p[0], y_s.reshape(bs, ts, d), out_p[1], out_p[2], out_p[3],
            kf.reshape(1, bs, ts, N_HEADS, HEAD_DIM), vf.reshape(1, bs, ts, N_HEADS, HEAD_DIM),
            ikf.reshape(1, bs, ts, IDX_DIM), va.reshape(1, bs, ts, A_WIDTH))
```
